```python
import functools
import jax, jax.numpy as jnp
from jax import lax
import numpy as np

D_MODEL = 1024
BATCH = 32
SEQ = 2048
DEPTH = 1
DEC_BATCH = 32
DEC_SEQ = 32
PAST_LEN = 4096

CHUNK = 64
Q_BLOCK = 128
EPS = 1e-6
ROPE_THETA = 10000.0
A_WIDTH = D_MODEL // 2
A_HD = 64
A_HEADS = A_WIDTH // A_HD
A_KV = 2
IDX_HEADS = 4
IDX_DIM = 64
INDEX_TOPK = 256
B_WIDTH = D_MODEL - A_WIDTH
B_HEADS = 4
B_DV = B_WIDTH // B_HEADS
B_DK = B_DV // 2
GATE_RANK = 16
GATE_TAU = 16.0
N_EXPERTS = 32
TOP_K = 4
D_FF = D_MODEL
SWIGLU_LIMIT = 7.0
SWIGLU_ALPHA = 1.702
MOE_BLOCK = 128
IN_WIDTHS = (A_HEADS * A_HD, A_KV * A_HD, A_KV * A_HD, IDX_HEADS * IDX_DIM, IDX_DIM, IDX_HEADS,
             B_HEADS * B_DK, B_HEADS * B_DK, B_HEADS * B_DV, GATE_RANK, B_WIDTH)
D_IN = sum(IN_WIDTHS)

kernel_name = 'hybrid_dsa_gla_moe_stream_step'


def rms_norm(x, g):
    xf = x.astype(jnp.float32)
    y = xf * lax.rsqrt(jnp.mean(xf * xf, axis=-1, keepdims=True) + EPS)
    return (y * g.astype(jnp.float32)).astype(x.dtype)


def adaln_params(c, w, b, n):
    mod = jax.nn.silu(c) @ w + b
    return jnp.split(mod[:, None, :], n, axis=-1)


def modulate(h, shift, scale):
    return h * (1 + scale) + shift


def rope(x, pos):
    half = x.shape[-1] // 2
    inv_freq = ROPE_THETA ** (-jnp.arange(half, dtype=jnp.float32) / half)
    ang = pos.astype(jnp.float32)[:, None] * inv_freq[None, :]
    cos = jnp.cos(ang)[None, :, None, :]
    sin = jnp.sin(ang)[None, :, None, :]
    xf = x.astype(jnp.float32)
    x1, x2 = xf[..., :half], xf[..., half:]
    return jnp.concatenate([x1 * cos - x2 * sin, x2 * cos + x1 * sin], axis=-1).astype(x.dtype)


def project_mixer_inputs(h, w_in, w_gate, b_gate, pos):
    b, t, _ = h.shape
    offs, acc = [], 0
    for w in IN_WIDTHS[:-1]:
        acc += w
        offs.append(acc)
    qa, ka, va, qi, ki, wi, qg, kg, vg, lr, rg = jnp.split(h @ w_in, offs, axis=-1)
    qa = rope(qa.reshape(b, t, A_HEADS, A_HD), pos)
    ka = rope(ka.reshape(b, t, A_KV, A_HD), pos)
    va = va.reshape(b, t, A_KV, A_HD)
    qi = rope(qi.reshape(b, t, IDX_HEADS, IDX_DIM), pos)
    ki = rope(ki.reshape(b, t, 1, IDX_DIM), pos)[:, :, 0]
    wi = wi * IDX_HEADS ** -0.5
    qg = qg.reshape(b, t, B_HEADS, B_DK) * B_DK ** -0.5
    kg = kg.reshape(b, t, B_HEADS, B_DK)
    vg = vg.reshape(b, t, B_HEADS, B_DV)
    gg = (jax.nn.log_sigmoid((lr @ w_gate + b_gate).astype(jnp.float32)) / GATE_TAU).reshape(b, t, B_HEADS, B_DK)
    return qa, ka, va, qi, ki, wi, qg, kg, vg, gg, rg


def dsa_attend(q, qi, wi, key_lim, k_all, v_all, ki_all, top_k):
    b, nq, nh, hd = q.shape
    n_keys = k_all.shape[1]
    rel = jax.nn.relu(jnp.einsum('bqhd,bld->bqhl', qi, ki_all).astype(jnp.float32))
    iscore = jnp.einsum('bqh,bqhl->bql', wi.astype(jnp.float32), rel)
    admissible = jnp.arange(n_keys)[None, :] < key_lim[:, None]
    iscore = jnp.where(admissible[None], iscore, -jnp.inf)
    _, sel = lax.top_k(iscore, top_k)
    valid = sel < key_lim[None, :, None]
    bidx = jnp.arange(b)[:, None, None]
    k_sel = k_all[bidx, sel]
    v_sel = v_all[bidx, sel]
    qh = q.reshape(b, nq, A_KV, nh // A_KV, hd)
    logits = jnp.einsum('bqgrd,bqkgd->bqgrk', qh, k_sel).astype(jnp.float32) * hd ** -0.5
    logits = jnp.where(valid[:, :, None, None, :], logits, -jnp.inf)
    p = jax.nn.softmax(logits, axis=-1)
    o = jnp.einsum('bqgrk,bqkgd->bqgrd', p.astype(v_sel.dtype), v_sel)
    return o.reshape(b, nq, nh, hd)


def gla_chunk(state, q, k, v, g):
    c = q.shape[1]
    qf, kf, vf = q.astype(jnp.float32), k.astype(jnp.float32), v.astype(jnp.float32)
    bcum = jnp.cumsum(g, axis=1)
    causal = jnp.arange(c)[:, None] >= jnp.arange(c)[None, :]
    diff = bcum[:, :, None] - bcum[:, None, :]
    decay = jnp.exp(jnp.where(causal[None, :, :, None, None], diff, -jnp.inf))
    scores = jnp.einsum('bthd,bshd,btshd->bhts', qf, kf, decay)
    o = (jnp.einsum('bhts,bshv->bthv', scores, vf)
         + jnp.einsum('bthd,bhdv->bthv', qf * jnp.exp(bcum), state))
    b_last = bcum[:, -1]
    new_state = (jnp.exp(b_last)[..., None] * state
                 + jnp.einsum('bshd,bshv->bhdv', kf * jnp.exp(b_last[:, None] - bcum), vf))
    return new_state, o


def merge_groups(oa, ob, rg, g_out, w_out):
    b, t = oa.shape[:2]
    ob = rms_norm(ob, g_out).reshape(b, t, B_WIDTH).astype(rg.dtype) * jax.nn.silu(rg)
    return jnp.concatenate([oa.reshape(b, t, A_WIDTH).astype(rg.dtype), ob], axis=-1) @ w_out


def mix_prompt(w_in, w_gate, b_gate, g_out, w_out, h):
    b, s, _ = h.shape
    pos = jnp.arange(s)
    qa, ka, va, qi, ki, wi, qg, kg, vg, gg, rg = project_mixer_inputs(h, w_in, w_gate, b_gate, pos)
    n_qb = s // Q_BLOCK
    key_lim = (pos // CHUNK + 1) * CHUNK
    top_k = min(INDEX_TOPK, s // 4)

    def qblocks(a):
        return a.reshape(b, n_qb, Q_BLOCK, *a.shape[2:]).swapaxes(0, 1)

    oa = lax.map(lambda args: dsa_attend(*args, ka, va, ki, top_k),
                 (qblocks(qa), qblocks(qi), qblocks(wi), key_lim.reshape(n_qb, Q_BLOCK)))
    oa = oa.swapaxes(0, 1).reshape(b, s, A_HEADS, A_HD)
    n_c = s // CHUNK

    def chunks(a):
        return a.reshape(b, n_c, CHUNK, *a.shape[2:]).swapaxes(0, 1)

    state0 = jnp.zeros((b, B_HEADS, B_DK, B_DV), jnp.float32)
    state, ob = lax.scan(lambda st, xs_: gla_chunk(st, *xs_), state0,
                         (chunks(qg), chunks(kg), chunks(vg), chunks(gg)))
    ob = ob.swapaxes(0, 1).reshape(b, s, B_HEADS, B_DV)
    y = merge_groups(oa, ob, rg, g_out, w_out)
    return y, (ka, va, ki, state.astype(h.dtype))


def mix_sample(cache_k, cache_v, cache_kidx, state, w_in, w_gate, b_gate, g_out, w_out, h):
    b, t, _ = h.shape
    past = cache_k.shape[1]
    pos = past + jnp.arange(t)
    qa, ka, va, qi, ki, wi, qg, kg, vg, gg, rg = project_mixer_inputs(h, w_in, w_gate, b_gate, pos)
    k_all = jnp.concatenate([cache_k.astype(ka.dtype), ka], axis=1)
    v_all = jnp.concatenate([cache_v.astype(va.dtype), va], axis=1)
    ki_all = jnp.concatenate([cache_kidx.astype(ki.dtype), ki], axis=1)
    n_keys = past + t
    key_lim = jnp.full((t,), n_keys, jnp.int32)
    oa = dsa_attend(qa, qi, wi, key_lim, k_all, v_all, ki_all, min(INDEX_TOPK, n_keys // 4))
    new_state, ob = gla_chunk(state.astype(jnp.float32), qg, kg, vg, gg)
    y = merge_groups(oa, ob, rg, g_out, w_out)
    return y, (ka, va, ki, new_state.astype(state.dtype))


def expert_ffn(xb, w_gu, b_gu, w_dn, b_dn):
    gu = xb @ w_gu + b_gu
    gate = jnp.minimum(gu[..., :D_FF], SWIGLU_LIMIT)
    up = jnp.clip(gu[..., D_FF:], -SWIGLU_LIMIT, SWIGLU_LIMIT)
    glu = gate * jax.nn.sigmoid(gate * SWIGLU_ALPHA)
    return ((up + 1) * glu) @ w_dn + b_dn


def moe_ffn(h, w_router, b_router, w_gu, b_gu, w_dn, b_dn):
    b, t, d = h.shape
    hf = h.reshape(-1, d)
    n_tok = hf.shape[0]
    logits = (hf @ w_router + b_router).astype(jnp.float32)
    top_v, top_e = lax.top_k(logits, TOP_K)
    gates = jax.nn.softmax(top_v, axis=-1)
    n_asg = n_tok * TOP_K
    n_blocks = -(-n_asg // MOE_BLOCK) + N_EXPERTS
    n_slots = n_blocks * MOE_BLOCK
    flat_e = top_e.reshape(-1)
    order = jnp.argsort(flat_e)
    e_sorted = flat_e[order]
    tok_sorted = order // TOP_K
    g_sorted = gates.reshape(-1)[order]
    counts = jnp.bincount(flat_e, length=N_EXPERTS)
    padded = (counts + MOE_BLOCK - 1) // MOE_BLOCK * MOE_BLOCK
    pad_end = jnp.cumsum(padded)
    pad_start = pad_end - padded
    start = jnp.cumsum(counts) - counts
    dest = pad_start[e_sorted] + jnp.arange(n_asg) - start[e_sorted]
    slot_tok = jnp.full((n_slots,), n_tok, jnp.int32).at[dest].set(tok_sorted.astype(jnp.int32))
    slot_gate = jnp.zeros((n_slots,), jnp.float32).at[dest].set(g_sorted)
    block_expert = jnp.minimum(
        jnp.searchsorted(pad_end, jnp.arange(n_blocks) * MOE_BLOCK, side='right'), N_EXPERTS - 1)
    h_pad = jnp.concatenate([hf, jnp.zeros((1, d), hf.dtype)], axis=0)
    xs = h_pad[slot_tok].reshape(n_blocks, MOE_BLOCK, d)
    ys = lax.map(lambda args: expert_ffn(args[0], w_gu[args[1]], b_gu[args[1]], w_dn[args[1]], b_dn[args[1]]),
                 (xs, block_expert)).reshape(n_slots, d)
    out = jax.ops.segment_sum(ys * slot_gate[:, None].astype(ys.dtype), slot_tok,
                              num_segments=n_tok + 1)[:n_tok]
    return out.reshape(b, t, d)


def trunk_layer(x, c, mixer, w_mod, b_mod, g_mix, g_ffn, w_router, b_router, w_gu, b_gu, w_dn, b_dn):
    sh1, sc1, gt1, sh2, sc2, gt2 = adaln_params(c, w_mod, b_mod, 6)
    y, new_state = mixer(modulate(rms_norm(x, g_mix), sh1, sc1))
    x = x + gt1 * y
    x = x + gt2 * moe_ffn(modulate(rms_norm(x, g_ffn), sh2, sc2), w_router, b_router, w_gu, b_gu, w_dn, b_dn)
    return x, new_state


def final_norm(x, c, w_mod_final, b_mod_final, g_final):
    shift, scale = adaln_params(c, w_mod_final, b_mod_final, 2)
    return modulate(rms_norm(x, g_final), shift, scale)


def setup_inputs(seed: int = 0) -> dict:
    key = jax.random.key(seed)
    ks = jax.random.split(key, 32)

    def nrm(k, shape, scale):
        return jax.random.normal(k, shape, jnp.float32) * scale

    d_mix = A_WIDTH + B_WIDTH
    return {
        'x_prompt': nrm(ks[0], (BATCH, SEQ, D_MODEL), 1.0),
        'x_sample': nrm(ks[1], (DEC_BATCH, DEC_SEQ, D_MODEL), 1.0),
        'cache_k': nrm(ks[2], (DEPTH, DEC_BATCH, PAST_LEN, A_KV, A_HD), 1.0),
        'cache_v': nrm(ks[3], (DEPTH, DEC_BATCH, PAST_LEN, A_KV, A_HD), 1.0),
        'cache_kidx': nrm(ks[4], (DEPTH, DEC_BATCH, PAST_LEN, IDX_DIM), 1.0),
        'state_gla': nrm(ks[5], (DEPTH, DEC_BATCH, B_HEADS, B_DK, B_DV), 0.3),
        'c_prompt': nrm(ks[6], (BATCH, D_MODEL), 1.0),
        'c_sample': nrm(ks[7], (DEC_BATCH, D_MODEL), 1.0),
        'w_mod': nrm(ks[8], (DEPTH, D_MODEL, 6 * D_MODEL), 0.5 * D_MODEL ** -0.5),
        'b_mod': nrm(ks[9], (DEPTH, 6 * D_MODEL), 0.02),
        'g_mix': 1.0 + nrm(ks[10], (DEPTH, D_MODEL), 0.02),
        'g_ffn': 1.0 + nrm(ks[11], (DEPTH, D_MODEL), 0.02),
        'w_in': nrm(ks[12], (DEPTH, D_MODEL, D_IN), D_MODEL ** -0.5),
        'gla_w_gate': nrm(ks[13], (DEPTH, GATE_RANK, B_HEADS * B_DK), GATE_RANK ** -0.5),
        'gla_b_gate': nrm(ks[14], (DEPTH, B_HEADS * B_DK), 0.1),
        'gla_g_out': 1.0 + nrm(ks[15], (DEPTH, B_DV), 0.02),
        'w_out': nrm(ks[16], (DEPTH, d_mix, D_MODEL), d_mix ** -0.5),
        'w_router': nrm(ks[17], (DEPTH, D_MODEL, N_EXPERTS), D_MODEL ** -0.5),
        'b_router': nrm(ks[18], (DEPTH, N_EXPERTS), 0.01),
        'w_gate_up': nrm(ks[19], (DEPTH, N_EXPERTS, D_MODEL, 2 * D_FF), D_MODEL ** -0.5),
        'b_gate_up': nrm(ks[20], (DEPTH, N_EXPERTS, 2 * D_FF), 0.01),
        'w_down': nrm(ks[21], (DEPTH, N_EXPERTS, D_FF, D_MODEL), D_FF ** -0.5),
        'b_down': nrm(ks[22], (DEPTH, N_EXPERTS, D_MODEL), 0.01),
        'w_mod_final': nrm(ks[23], (D_MODEL, 2 * D_MODEL), 0.5 * D_MODEL ** -0.5),
        'b_mod_final': nrm(ks[24], (2 * D_MODEL,), 0.02),
        'g_final': 1.0 + nrm(ks[25], (D_MODEL,), 0.02),
    }


def reference(x_prompt, x_sample, cache_k, cache_v, cache_kidx, state_gla, c_prompt, c_sample,
              w_mod, b_mod, g_mix, g_ffn, w_in, gla_w_gate, gla_b_gate, gla_g_out, w_out,
              w_router, b_router, w_gate_up, b_gate_up, w_down, b_down,
              w_mod_final, b_mod_final, g_final):
    xp, xs = x_prompt, x_sample
    kp_l, vp_l, kip_l, sp_l = [], [], [], []
    ks_l, vs_l, kis_l, ss_l = [], [], [], []
    for l in range(DEPTH):
        mix_w = (w_in[l], gla_w_gate[l], gla_b_gate[l], gla_g_out[l], w_out[l])
        blk_w = (w_mod[l], b_mod[l], g_mix[l], g_ffn[l], w_router[l], b_router[l],
                 w_gate_up[l], b_gate_up[l], w_down[l], b_down[l])
        xp, (kp, vp, kip, sp) = trunk_layer(xp, c_prompt, functools.partial(mix_prompt, *mix_w), *blk_w)
        xs, (kn, vn, kin, sn) = trunk_layer(
            xs, c_sample,
            functools.partial(mix_sample, cache_k[l], cache_v[l], cache_kidx[l], state_gla[l], *mix_w),
            *blk_w)
        kp_l.append(kp); vp_l.append(vp); kip_l.append(kip); sp_l.append(sp)
        ks_l.append(kn); vs_l.append(vn); kis_l.append(kin); ss_l.append(sn)
    y_prompt = final_norm(xp, c_prompt, w_mod_final, b_mod_final, g_final)
    y_sample = final_norm(xs, c_sample, w_mod_final, b_mod_final, g_final)
    return (y_prompt, y_sample,
            jnp.stack(kp_l), jnp.stack(vp_l), jnp.stack(kip_l), jnp.stack(sp_l),
            jnp.stack(ks_l), jnp.stack(vs_l), jnp.stack(kis_l), jnp.stack(ss_l))
```

```python
import functools

import jax
import jax.numpy as jnp
from jax import lax
from jax.experimental import pallas as pl
from jax.experimental.pallas import tpu as pltpu

F32 = jnp.float32
BF16 = jnp.bfloat16
I32 = jnp.int32
HI = lax.Precision.HIGHEST

CHUNK = 64
CHUNK_SHIFT = 6
EPS = 1e-6
ROPE_THETA = 10000.0
A_HD = 64
A_HEADS = 8
A_KV = 2
IDX_HEADS = 4
IDX_DIM = 64
INDEX_TOPK = 256
B_HEADS = 4
B_DK = 64
B_DV = 128
GATE_RANK = 16
GATE_TAU = 16.0
N_EXPERTS = 32
TOP_K = 4
SWIGLU_LIMIT = 7.0
SWIGLU_ALPHA = 1.702

LANES = 128
GLA_SUB = 16
GLA_EXP_CLAMP = 80.0
SLOT_BLOCK = 512
VMEM_LIMIT = 56 * 1024 * 1024
INT_MIN = -2147483648
NEG_BIG = -1e30

C_QA, C_KA, C_VA, C_QI, C_MISC, C_QG, C_KG, C_VG, C_RG, C_END = 0, 512, 640, 768, 1024, 1152, 1408, 1664, 2176, 2688
MISC_WI = 64
MISC_LR = 68


def _params(*sem):
    return pltpu.CompilerParams(dimension_semantics=sem, vmem_limit_bytes=VMEM_LIMIT)


def _nt(a, b):
    return lax.dot_general(a, b, (((1,), (1,)), ((), ())), preferred_element_type=F32)


def _tn(a, b):
    return lax.dot_general(a, b, (((0,), (0,)), ((), ())), preferred_element_type=F32)


def _rms(x):
    return x * lax.rsqrt(jnp.mean(x * x, axis=-1, keepdims=True) + EPS)


def _silu(x):
    return x / (1.0 + jnp.exp(-x))


def _adaln_kernel(c_ref, w_ref, b_ref, o_ref):
    a = _silu(c_ref[...])
    o_ref[...] = jnp.dot(a, w_ref[...], preferred_element_type=F32, precision=HI) + b_ref[...]


def _adaln(c, w, b):
    r, d = c.shape
    n = w.shape[1]
    tn = 512
    return pl.pallas_call(
        _adaln_kernel,
        grid=(n // tn,),
        in_specs=[pl.BlockSpec((r, d), lambda j: (0, 0)),
                  pl.BlockSpec((d, tn), lambda j: (0, j)),
                  pl.BlockSpec((1, tn), lambda j: (0, j))],
        out_specs=pl.BlockSpec((r, tn), lambda j: (0, j)),
        out_shape=jax.ShapeDtypeStruct((r, n), F32),
        name="adaln",
        compiler_params=_params("arbitrary"),
    )(c, w, b.reshape(1, n))


def _premix_kernel(x_ref, mod_ref, g_ref, w_ref, wg_ref, bg_ref, cos_ref, sin_ref,
                   qa_ref, ka_ref, va_ref, qi_ref, ki_ref, misc_ref, qg_ref, kg_ref, vg_ref, gg_ref, rg_ref):
    x = x_ref[0]
    mod = mod_ref[0]
    h = _rms(x) * g_ref[...] * (1.0 + mod[1:2]) + mod[0:1]
    proj = jnp.dot(h.astype(BF16), w_ref[...], preferred_element_type=F32)
    tm = x.shape[0]
    cos = cos_ref[...]
    sin = sin_ref[...]
    lane = lax.broadcasted_iota(I32, (tm, LANES), 1)
    lower_half = (lane & (A_HD - 1)) < (A_HD // 2)

    def rope(xc):
        rot = jnp.where(lower_half, pltpu.roll(xc, LANES - A_HD // 2, 1), pltpu.roll(xc, A_HD // 2, 1))
        return xc * cos + rot * sin

    def chunk(c0, j=0):
        return proj[:, c0 + j * LANES:c0 + (j + 1) * LANES]

    for j in range((C_KA - C_QA) // LANES):
        qa_ref[0, :, j * LANES:(j + 1) * LANES] = (rope(chunk(C_QA, j)) * (A_HD ** -0.5)).astype(BF16)
    ka_ref[0] = rope(chunk(C_KA))
    va_ref[0] = chunk(C_VA)
    for j in range((C_MISC - C_QI) // LANES):
        qi_ref[0, :, j * LANES:(j + 1) * LANES] = rope(chunk(C_QI, j)).astype(BF16)
    m = chunk(C_MISC)
    mr = rope(m)
    ki_ref[0] = mr[:, :IDX_DIM]
    misc_ref[0] = jnp.where(lane < IDX_DIM, mr, m * (IDX_HEADS ** -0.5))
    xg = jnp.dot(m, wg_ref[...], preferred_element_type=F32, precision=HI) + bg_ref[...]
    gg_ref[0] = (jnp.minimum(xg, 0.0) - jnp.log(1.0 + jnp.exp(-jnp.abs(xg)))) * (1.0 / GATE_TAU)
    qg_ref[0] = proj[:, C_QG:C_KG] * (B_DK ** -0.5)
    kg_ref[0] = proj[:, C_KG:C_VG]
    vg_ref[0] = proj[:, C_VG:C_RG]
    rg_ref[0] = proj[:, C_RG:C_END]


def _premix(x, mod, g_mix, w_r, wg_pad, bg, cos_t, sin_t, tm):
    b, t, d = x.shape
    widths = [(512, BF16), (128, F32), (128, F32), (256, BF16), (64, F32), (128, F32),
              (256, F32), (256, F32), (512, F32), (256, F32), (512, F32)]
    return pl.pallas_call(
        _premix_kernel,
        grid=(b, t // tm),
        in_specs=[pl.BlockSpec((1, tm, d), lambda i, j: (i, j, 0)),
                  pl.BlockSpec((1, 6, d), lambda i, j: (i, 0, 0)),
                  pl.BlockSpec((1, d), lambda i, j: (0, 0)),
                  pl.BlockSpec((d, C_END), lambda i, j: (0, 0)),
                  pl.BlockSpec((LANES, 256), lambda i, j: (0, 0)),
                  pl.BlockSpec((1, 256), lambda i, j: (0, 0)),
                  pl.BlockSpec((tm, LANES), lambda i, j: (j, 0)),
                  pl.BlockSpec((tm, LANES), lambda i, j: (j, 0))],
        out_specs=[pl.BlockSpec((1, tm, w), lambda i, j: (i, j, 0)) for w, _ in widths],
        out_shape=[jax.ShapeDtypeStruct((b, t, w), dt) for w, dt in widths],
        name="premix",
        compiler_params=_params("arbitrary", "arbitrary"),
    )(x, mod, g_mix, w_r, wg_pad, bg, cos_t, sin_t)


def _attn_kernel(q_ref, qi_ref, misc_ref, k_ref, v_ref, ki_ref, o_ref, *, tq, n_pad, top_k, causal, n_keys):
    q = q_ref[0]
    qi = qi_ref[0]
    misc = misc_ref[0]
    kib = ki_ref[0].astype(BF16)
    isc = jnp.zeros((tq, n_pad), F32)
    for h in range(IDX_HEADS):
        s = _nt(qi[:, h * IDX_DIM:(h + 1) * IDX_DIM], kib)
        isc = isc + misc[:, MISC_WI + h:MISC_WI + h + 1] * jnp.maximum(s, 0.0)
    kpos = lax.broadcasted_iota(I32, (tq, n_pad), 1)
    if causal:
        row = lax.broadcasted_iota(I32, (tq, 1), 0) + pl.program_id(1) * tq
        key_lim = (lax.shift_right_logical(row, CHUNK_SHIFT) + 1) * CHUNK
    else:
        key_lim = jnp.full((tq, 1), n_keys, I32)
    isc = jnp.where(isc == 0.0, 0.0, isc)
    bits = pltpu.bitcast(isc, I32)
    key = jnp.where(bits < 0, bits ^ 0x7FFFFFFF, bits)
    key = jnp.where(kpos < key_lim, key, INT_MIN)
    kk = jnp.minimum(key_lim, top_k)

    def search(i, ans):
        cand = ans | lax.shift_left(jnp.int32(1), 31 - i)
        cnt = jnp.sum((key >= (cand ^ INT_MIN)).astype(I32), axis=1, keepdims=True)
        return jnp.where(cnt >= kk, cand, ans)

    thr = lax.fori_loop(0, 32, search, jnp.zeros((tq, 1), I32)) ^ INT_MIN
    gt = key > thr
    need = (kk - jnp.sum(gt.astype(I32), axis=1, keepdims=True)).astype(F32)
    r = lax.broadcasted_iota(I32, (LANES, LANES), 0)
    c = lax.broadcasted_iota(I32, (LANES, LANES), 1)
    upper = jnp.where(r < c, 1.0, 0.0).astype(BF16)
    carry = jnp.zeros((tq, 1), F32)
    bias_blocks = []
    for j in range(n_pad // LANES):
        sl = slice(j * LANES, (j + 1) * LANES)
        tie = jnp.where(key[:, sl] == thr, 1.0, 0.0)
        before = jnp.dot(tie.astype(BF16), upper, preferred_element_type=F32) + carry
        take_tie = jnp.where(before < need, tie, 0.0)
        bias_blocks.append(jnp.where(gt[:, sl], 0.0, jnp.where(take_tie > 0.0, 0.0, NEG_BIG)))
        carry = carry + jnp.sum(tie, axis=1, keepdims=True)
    bias = jnp.concatenate(bias_blocks, axis=1)

    k = k_ref[0]
    v = v_ref[0]
    outs = []
    rep = A_HEADS // A_KV
    for g in range(A_KV):
        kg = k[:, g * A_HD:(g + 1) * A_HD].astype(BF16)
        vg = v[:, g * A_HD:(g + 1) * A_HD].astype(BF16)
        for rr in range(rep):
            hh = g * rep + rr
            s = _nt(q[:, hh * A_HD:(hh + 1) * A_HD], kg) + bias
            p = jnp.exp(s - jnp.max(s, axis=1, keepdims=True))
            den = jnp.sum(p, axis=1, keepdims=True)
            outs.append(jnp.dot(p.astype(BF16), vg, preferred_element_type=F32) / den)
    o_ref[0] = jnp.concatenate(outs, axis=1).astype(BF16)


def _attn(q, qi, misc, k_all, v_all, ki_all, *, tq, causal, n_keys):
    b, t, _ = q.shape
    n_pad = k_all.shape[1]
    top_k = min(INDEX_TOPK, n_keys // 4)
    kern = functools.partial(_attn_kernel, tq=tq, n_pad=n_pad, top_k=top_k, causal=causal, n_keys=n_keys)
    return pl.pallas_call(
        kern,
        grid=(b, t // tq),
        in_specs=[pl.BlockSpec((1, tq, 512), lambda i, j: (i, j, 0)),
                  pl.BlockSpec((1, tq, 256), lambda i, j: (i, j, 0)),
                  pl.BlockSpec((1, tq, LANES), lambda i, j: (i, j, 0)),
                  pl.BlockSpec((1, n_pad, LANES), lambda i, j: (i, 0, 0)),
                  pl.BlockSpec((1, n_pad, LANES), lambda i, j: (i, 0, 0)),
                  pl.BlockSpec((1, n_pad, IDX_DIM), lambda i, j: (i, 0, 0))],
        out_specs=pl.BlockSpec((1, tq, 512), lambda i, j: (i, j, 0)),
        out_shape=jax.ShapeDtypeStruct((b, t, 512), BF16),
        name="attn",
        compiler_params=_params("arbitrary", "arbitrary"),
    )(q, qi, misc, k_all, v_all, ki_all)


def _gla_kernel(q_ref, k_ref, v_ref, g_ref, rg_ref, s0_ref, go_ref, ob_ref, st_ref, st_scr, *, cs, nc):
    j = pl.program_id(1)

    @pl.when(j == 0)
    def _():
        st_scr[...] = s0_ref[0]

    r = lax.broadcasted_iota(I32, (cs, cs), 0)
    c = lax.broadcasted_iota(I32, (cs, cs), 1)
    causal = r >= c
    tri = jnp.where(causal, 1.0, 0.0)
    go = go_ref[...]
    for ci in range(nc):
        sl = slice(ci * cs, (ci + 1) * cs)
        q = q_ref[0, sl, :]
        k = k_ref[0, sl, :]
        v = v_ref[0, sl, :]
        rg = rg_ref[0, sl, :]
        bcum = jnp.dot(tri, g_ref[0, sl, :], preferred_element_type=F32, precision=HI)
        blast = bcum[cs - 1:cs, :]
        qdec = (q * jnp.exp(bcum)).astype(BF16)
        kdec = (k * jnp.exp(blast - bcum)).astype(BF16)
        eb = jnp.exp(blast)
        a_rows = [[] for _ in range(B_HEADS)]
        for i in range(cs // GLA_SUB):
            rs = slice(i * GLA_SUB, (i + 1) * GLA_SUB)
            ref = bcum[i * GLA_SUB:i * GLA_SUB + 1, :]
            qt = (q[rs, :] * jnp.exp(bcum[rs, :] - ref)).astype(BF16)
            kt = (k * jnp.exp(jnp.minimum(ref - bcum, GLA_EXP_CLAMP))).astype(BF16)
            for h in range(B_HEADS):
                hs = slice(h * B_DK, (h + 1) * B_DK)
                a_rows[h].append(_nt(qt[:, hs], kt[:, hs]))
        for h in range(B_HEADS):
            hs = slice(h * B_DK, (h + 1) * B_DK)
            vs = slice(h * B_DV, (h + 1) * B_DV)
            a = jnp.where(causal, jnp.concatenate(a_rows[h], axis=0), 0.0).astype(BF16)
            vh = v[:, vs].astype(BF16)
            st = st_scr[h]
            o = jnp.dot(a, vh, preferred_element_type=F32) + _nt(qdec[:, hs], st.astype(BF16))
            st_scr[h] = st * eb[:, hs] + _tn(vh, kdec[:, hs])
            y = _rms(o) * go
            ob_ref[0, sl, vs] = (y * _silu(rg[:, vs])).astype(BF16)

    @pl.when(j == pl.num_programs(1) - 1)
    def _():
        st_ref[0] = st_scr[...]


def _gla(qg, kg, vg, gg, rg, state_t, g_out, *, cs, nc):
    b, t, _ = qg.shape
    tt = cs * nc
    kern = functools.partial(_gla_kernel, cs=cs, nc=nc)

    def tok(w):
        return pl.BlockSpec((1, tt, w), lambda i, j: (i, j, 0))

    st_spec = pl.BlockSpec((1, B_HEADS, B_DV, B_DK), lambda i, j: (i, 0, 0, 0))
    return pl.pallas_call(
        kern,
        grid=(b, t // tt),
        in_specs=[tok(256), tok(256), tok(512), tok(256), tok(512), st_spec,
                  pl.BlockSpec((1, B_DV), lambda i, j: (0, 0))],
        out_specs=[tok(512), st_spec],
        out_shape=[jax.ShapeDtypeStruct((b, t, 512), BF16),
                   jax.ShapeDtypeStruct((b, B_HEADS, B_DV, B_DK), F32)],
        scratch_shapes=[pltpu.VMEM((B_HEADS, B_DV, B_DK), F32)],
        name="gla",
        compiler_params=_params("arbitrary", "arbitrary"),
    )(qg, kg, vg, gg, rg, state_t, g_out)


def _merge_kernel(oa_ref, ob_ref, x_ref, mod_ref, wo_ref, gf_ref, wr_ref, br_ref, cnt0_ref,
                  x1_ref, h2_ref, ri_ref, rgate_ref, cnt_ref, carry_scr):
    @pl.when((pl.program_id(0) == 0) & (pl.program_id(1) == 0))
    def _():
        carry_scr[...] = cnt0_ref[...]

    mod = mod_ref[0]
    cat = jnp.concatenate([oa_ref[0], ob_ref[0]], axis=1)
    x1 = x_ref[0] + mod[2:3] * jnp.dot(cat, wo_ref[...], preferred_element_type=F32)
    x1_ref[0] = x1
    h2 = _rms(x1) * gf_ref[...] * (1.0 + mod[4:5]) + mod[3:4]
    h2_ref[0] = h2
    tm = x1.shape[0]
    lane = lax.broadcasted_iota(I32, (tm, LANES), 1)
    left = jnp.dot(h2, wr_ref[...], preferred_element_type=F32, precision=HI) + br_ref[...]
    idx, val = [], []
    for _ in range(TOP_K):
        m = jnp.max(left, axis=1, keepdims=True)
        e = jnp.min(jnp.where(left == m, lane, LANES), axis=1, keepdims=True)
        idx.append(e)
        val.append(m)
        left = jnp.where(lane == e, -jnp.inf, left)
    ex = [jnp.exp(vv - val[0]) for vv in val]
    den = ex[0] + ex[1] + ex[2] + ex[3]
    onehot = jnp.zeros((tm, LANES), F32)
    for e in idx:
        onehot = onehot + jnp.where(lane == e, 1.0, 0.0)
    r = lax.broadcasted_iota(I32, (tm, tm), 0)
    c = lax.broadcasted_iota(I32, (tm, tm), 1)
    earlier = jnp.where(r > c, 1.0, 0.0).astype(BF16)
    before = jnp.dot(earlier, onehot.astype(BF16), preferred_element_type=F32) + carry_scr[...]
    ri = jnp.zeros((tm, LANES), I32)
    rgate = jnp.zeros((tm, LANES), F32)
    for kq in range(TOP_K):
        rank = jnp.sum(jnp.where(lane == idx[kq], before, 0.0), axis=1, keepdims=True).astype(I32)
        ri = jnp.where(lane == kq, idx[kq], ri)
        ri = jnp.where(lane == TOP_K + kq, rank, ri)
        rgate = jnp.where(lane == kq, ex[kq] / den, rgate)
    ri_ref[0] = ri
    rgate_ref[0] = rgate
    carry_scr[...] = carry_scr[...] + jnp.sum(onehot, axis=0, keepdims=True)
    cnt_ref[...] = carry_scr[...]


def _merge(oa, ob, x, mod, w_out, g_ffn, wr_pad, br_pad, cnt0, tm):
    b, t, d = x.shape

    def tok(w):
        return pl.BlockSpec((1, tm, w), lambda i, j: (i, j, 0))

    def const(s):
        return pl.BlockSpec(s, lambda i, j: (0, 0))

    return pl.pallas_call(
        _merge_kernel,
        grid=(b, t // tm),
        in_specs=[tok(512), tok(512), tok(d), pl.BlockSpec((1, 6, d), lambda i, j: (i, 0, 0)),
                  const((d, d)), const((1, d)), const((d, LANES)), const((1, LANES)), const((1, LANES))],
        out_specs=[tok(d), tok(d), tok(LANES), tok(LANES), const((1, LANES))],
        out_shape=[jax.ShapeDtypeStruct((b, t, d), F32), jax.ShapeDtypeStruct((b, t, d), F32),
                   jax.ShapeDtypeStruct((b, t, LANES), I32), jax.ShapeDtypeStruct((b, t, LANES), F32),
                   jax.ShapeDtypeStruct((1, LANES), F32)],
        scratch_shapes=[pltpu.VMEM((1, LANES), F32)],
        name="merge",
        compiler_params=_params("arbitrary", "arbitrary"),
    )(oa, ob, x, mod, w_out, g_ffn, wr_pad, br_pad, cnt0)


def _dispatch_kernel(dest_ref, h_ref, xs_in_ref, xs_ref, sem, *, tm):
    del xs_in_ref

    def row_copy(r, slot):
        return pltpu.make_async_copy(h_ref.at[pl.ds(r, 1), :], xs_ref.at[pl.ds(slot, 1), :], sem)

    def issue(r, carry):
        for kq in range(TOP_K):
            row_copy(r, dest_ref[r * TOP_K + kq]).start()
        return carry

    lax.fori_loop(0, tm, issue, 0)

    def drain(r, carry):
        for kq in range(TOP_K):
            row_copy(0, 0).wait()
        return carry

    lax.fori_loop(0, tm, drain, 0)


def _dispatch(dest_flat, h2_flat, xs, tm):
    n, d = h2_flat.shape
    kern = functools.partial(_dispatch_kernel, tm=tm)
    return pl.pallas_call(
        kern,
        grid=(n // tm,),
        in_specs=[pl.BlockSpec((tm * TOP_K,), lambda i: (i,), memory_space=pltpu.SMEM),
                  pl.BlockSpec((tm, d), lambda i: (i, 0)),
                  pl.BlockSpec(memory_space=pl.ANY)],
        out_specs=pl.BlockSpec(memory_space=pl.ANY),
        out_shape=jax.ShapeDtypeStruct(xs.shape, xs.dtype),
        scratch_shapes=[pltpu.SemaphoreType.DMA(())],
        input_output_aliases={2: 0},
        name="dispatch",
        compiler_params=_params("arbitrary"),
    )(dest_flat, h2_flat, xs)


def _expert_kernel(be_ref, nu_ref, xs_ref, wgu_ref, bgu_ref, wdn_ref, bdn_ref, ys_ref):
    @pl.when(pl.program_id(0) < nu_ref[0])
    def _():
        f = wdn_ref.shape[1]
        gu = jnp.dot(xs_ref[...].astype(BF16), wgu_ref[0], preferred_element_type=F32) + bgu_ref[0]
        gate = jnp.minimum(gu[:, :f], SWIGLU_LIMIT)
        up = jnp.clip(gu[:, f:], -SWIGLU_LIMIT, SWIGLU_LIMIT)
        glu = gate / (1.0 + jnp.exp(-SWIGLU_ALPHA * gate))
        act = ((up + 1.0) * glu).astype(BF16)
        ys_ref[...] = jnp.dot(act, wdn_ref[0], preferred_element_type=F32) + bdn_ref[0]


def _experts(block_expert, n_used, xs, wgu, bgu, wdn, bdn):
    n_slots, d = xs.shape
    nb = n_slots // SLOT_BLOCK
    f = wdn.shape[1]

    def blk(i, be, nu):
        return (jnp.minimum(i, nu[0] - 1), 0)

    def wsel(i, be, nu):
        return (be[i], 0, 0)

    grid_spec = pltpu.PrefetchScalarGridSpec(
        num_scalar_prefetch=2,
        grid=(nb,),
        in_specs=[pl.BlockSpec((SLOT_BLOCK, d), blk),
                  pl.BlockSpec((1, d, 2 * f), wsel),
                  pl.BlockSpec((1, 1, 2 * f), wsel),
                  pl.BlockSpec((1, f, d), wsel),
                  pl.BlockSpec((1, 1, d), wsel)],
        out_specs=pl.BlockSpec((SLOT_BLOCK, d), blk),
    )
    return pl.pallas_call(
        _expert_kernel,
        grid_spec=grid_spec,
        out_shape=jax.ShapeDtypeStruct((n_slots, d), F32),
        name="experts",
        compiler_params=_params("arbitrary"),
    )(block_expert, n_used, xs, wgu, bgu, wdn, bdn)


def _combine_kernel(dest_ref, x1_ref, gate_ref, mod_ref, modf_ref, gfin_ref, ys_ref, y_ref, buf, sem, *, tm):
    def row_copy(kq, r, slot):
        return pltpu.make_async_copy(ys_ref.at[pl.ds(slot, 1), :], buf.at[kq, pl.ds(r, 1), :], sem)

    def issue(r, carry):
        for kq in range(TOP_K):
            row_copy(kq, r, dest_ref[r * TOP_K + kq]).start()
        return carry

    lax.fori_loop(0, tm, issue, 0)

    def drain(r, carry):
        for kq in range(TOP_K):
            row_copy(0, 0, 0).wait()
        return carry

    lax.fori_loop(0, tm, drain, 0)
    gates = gate_ref[0]
    moe = gates[:, 0:1] * buf[0]
    for kq in range(1, TOP_K):
        moe = moe + gates[:, kq:kq + 1] * buf[kq]
    xo = x1_ref[0] + mod_ref[0][5:6] * moe
    modf = modf_ref[0]
    y_ref[0] = _rms(xo) * gfin_ref[...] * (1.0 + modf[1:2]) + modf[0:1]


def _combine(dest_flat, x1, gates, mod, modf, g_final, ys, tm):
    b, t, d = x1.shape
    nt = t // tm
    kern = functools.partial(_combine_kernel, tm=tm)
    return pl.pallas_call(
        kern,
        grid=(b, nt),
        in_specs=[pl.BlockSpec((tm * TOP_K,), lambda i, j: (i * nt + j,), memory_space=pltpu.SMEM),
                  pl.BlockSpec((1, tm, d), lambda i, j: (i, j, 0)),
                  pl.BlockSpec((1, tm, LANES), lambda i, j: (i, j, 0)),
                  pl.BlockSpec((1, 6, d), lambda i, j: (i, 0, 0)),
                  pl.BlockSpec((1, 2, d), lambda i, j: (i, 0, 0)),
                  pl.BlockSpec((1, d), lambda i, j: (0, 0)),
                  pl.BlockSpec(memory_space=pl.ANY)],
        out_specs=pl.BlockSpec((1, tm, d), lambda i, j: (i, j, 0)),
        out_shape=jax.ShapeDtypeStruct((b, t, d), F32),
        scratch_shapes=[pltpu.VMEM((TOP_K, tm, d), F32), pltpu.SemaphoreType.DMA(())],
        name="combine",
        compiler_params=_params("arbitrary", "arbitrary"),
    )(dest_flat, x1, gates, mod, modf, g_final, ys)


def _rope_tables(pos):
    half = A_HD // 2
    inv_freq = ROPE_THETA ** (-jnp.arange(half, dtype=F32) / half)
    ang = pos.astype(F32)[:, None] * inv_freq[None, :]
    cos = jnp.cos(ang)
    sin = jnp.sin(ang)
    reps = LANES // A_HD
    return jnp.tile(jnp.concatenate([cos, cos], axis=1), (1, reps)), jnp.tile(jnp.concatenate([-sin, sin], axis=1), (1, reps))


def _reorder_w_in(w_in):
    offs = [0]
    for w in (512, 128, 128, 256, 64, 4, 256, 256, 512, 16, 512):
        offs.append(offs[-1] + w)
    seg = [w_in[:, offs[i]:offs[i + 1]] for i in range(11)]
    qa, ka, va, qi, ki, wi, qg, kg, vg, lr, rg = seg
    pad = jnp.zeros((w_in.shape[0], LANES - IDX_DIM - IDX_HEADS - GATE_RANK), w_in.dtype)
    return jnp.concatenate([qa, ka, va, qi, ki, wi, lr, pad, qg, kg, vg, rg], axis=1).astype(BF16)


def _pad_keys(a, n_pad):
    return jnp.pad(a, ((0, 0), (0, n_pad - a.shape[1]), (0, 0)))


def kernel(x_prompt, x_sample, cache_k, cache_v, cache_kidx, state_gla, c_prompt, c_sample,
           w_mod, b_mod, g_mix, g_ffn, w_in, gla_w_gate, gla_b_gate, gla_g_out, w_out,
           w_router, b_router, w_gate_up, b_gate_up, w_down, b_down,
           w_mod_final, b_mod_final, g_final):
    depth = w_in.shape[0]
    assert depth == 1
    bp, sp, d = x_prompt.shape
    bs, ts, _ = x_sample.shape
    past = cache_k.shape[2]
    n_exp = w_router.shape[2]
    f = w_down.shape[2]

    c_all = jnp.concatenate([c_prompt, c_sample], axis=0)
    w_r = _reorder_w_in(w_in[0])
    wg_pad = jnp.zeros((LANES, B_HEADS * B_DK), F32).at[MISC_LR:MISC_LR + GATE_RANK].set(gla_w_gate[0])
    bg = gla_b_gate[0].reshape(1, -1)
    wr_pad = jnp.zeros((d, LANES), F32).at[:, :n_exp].set(w_router[0])
    br_pad = jnp.full((1, LANES), NEG_BIG, F32).at[0, :n_exp].set(b_router[0])
    w_out_b = w_out[0].astype(BF16)
    wgu_b = w_gate_up[0].astype(BF16)
    wdn_b = w_down[0].astype(BF16)
    bgu = b_gate_up[0].reshape(n_exp, 1, 2 * f)
    bdn = b_down[0].reshape(n_exp, 1, d)
    g_out = gla_g_out[0].reshape(1, B_DV)

    mod_all = _adaln(c_all, w_mod[0], b_mod[0]).reshape(bp + bs, 6, d)
    modf_all = _adaln(c_all, w_mod_final, b_mod_final).reshape(bp + bs, 2, d)
    mod_p, mod_s = mod_all[:bp], mod_all[bp:]
    modf_p, modf_s = modf_all[:bp], modf_all[bp:]

    def mixer(x, mod, pos, tm, k_past, v_past, ki_past, state_t, causal, cs, nc, tq):
        b, t, _ = x.shape
        cos_t, sin_t = _rope_tables(pos)
        qa, ka, va, qi, ki, misc, qg, kg, vg, gg, rg = _premix(
            x, mod, g_mix[0].reshape(1, d), w_r, wg_pad, bg, cos_t, sin_t, tm)
        if k_past is None:
            k_all, v_all, ki_all = ka, va, ki
        else:
            k_all = jnp.concatenate([k_past, ka], axis=1)
            v_all = jnp.concatenate([v_past, va], axis=1)
            ki_all = jnp.concatenate([ki_past, ki], axis=1)
        n_keys = k_all.shape[1]
        n_pad = -(-n_keys // LANES) * LANES
        if n_pad != n_keys:
            k_all, v_all, ki_all = _pad_keys(k_all, n_pad), _pad_keys(v_all, n_pad), _pad_keys(ki_all, n_pad)
        oa = _attn(qa, qi, misc, k_all, v_all, ki_all, tq=tq, causal=causal, n_keys=n_keys)
        ob, st = _gla(qg, kg, vg, gg, rg, state_t, g_out, cs=cs, nc=nc)
        return oa, ob, (ka, va, ki, st)

    state0_p = jnp.zeros((bp, B_HEADS, B_DV, B_DK), F32)
    oa_p, ob_p, (ka_p, va_p, ki_p, st_p) = mixer(
        x_prompt, mod_p, jnp.arange(sp), min(256, sp), None, None, None, state0_p, True, CHUNK, 4, min(128, sp))
    state0_s = jnp.swapaxes(state_gla[0], -1, -2)
    oa_s, ob_s, (ka_s, va_s, ki_s, st_s) = mixer(
        x_sample, mod_s, past + jnp.arange(ts), ts,
        cache_k[0].reshape(bs, past, A_KV * A_HD), cache_v[0].reshape(bs, past, A_KV * A_HD), cache_kidx[0],
        state0_s, False, ts, 1, ts)

    cnt0 = jnp.zeros((1, LANES), F32)
    x1_p, h2_p, ri_p, gate_p, cnt1 = _merge(oa_p, ob_p, x_prompt, mod_p, w_out_b, g_ffn[0].reshape(1, d),
                                            wr_pad, br_pad, cnt0, min(256, sp))
    x1_s, h2_s, ri_s, gate_s, cnt2 = _merge(oa_s, ob_s, x_sample, mod_s, w_out_b, g_ffn[0].reshape(1, d),
                                            wr_pad, br_pad, cnt1, ts)
    counts = cnt2[0, :n_exp].astype(I32)
    padded = (counts + SLOT_BLOCK - 1) // SLOT_BLOCK * SLOT_BLOCK
    pad_end = jnp.cumsum(padded)
    pad_start = pad_end - padded
    n_asg = (bp * sp + bs * ts) * TOP_K
    nb = -(-n_asg // SLOT_BLOCK) + n_exp
    n_slots = nb * SLOT_BLOCK
    block_expert = jnp.minimum(
        jnp.searchsorted(pad_end, jnp.arange(nb, dtype=I32) * SLOT_BLOCK, side='right'), n_exp - 1).astype(I32)
    n_used = (pad_end[-1:] // SLOT_BLOCK).astype(I32)

    def dests(ri):
        return (pad_start[ri[..., :TOP_K]] + ri[..., TOP_K:2 * TOP_K]).reshape(-1)

    dest_p, dest_s = dests(ri_p), dests(ri_s)
    xs = jnp.zeros((n_slots, d), F32)
    xs = _dispatch(dest_p, h2_p.reshape(bp * sp, d), xs, min(256, sp))
    xs = _dispatch(dest_s, h2_s.reshape(bs * ts, d), xs, min(256, bs * ts))
    ys = _experts(block_expert, n_used, xs, wgu_b, bgu, wdn_b, bdn)
    y_p = _combine(dest_p, x1_p, gate_p, mod_p, modf_p, g_final.reshape(1, d), ys, min(256, sp))
    y_s = _combine(dest_s, x1_s, gate_s, mod_s, modf_s, g_final.reshape(1, d), ys, ts)

    def kv(a, b, t):
        return a.reshape(1, b, t, A_KV, A_HD)

    return (y_p, y_s,
            kv(ka_p, bp, sp), kv(va_p, bp, sp), ki_p[None], jnp.swapaxes(st_p, -1, -2)[None],
            kv(ka_s, bs, ts), kv(va_s, bs, ts), ki_s[None], jnp.swapaxes(st_s, -1, -2)[None])
```

```python
import functools

import jax
import jax.numpy as jnp
from jax import lax
from jax.experimental import pallas as pl
from jax.experimental.pallas import tpu as pltpu

F32 = jnp.float32
BF16 = jnp.bfloat16
I32 = jnp.int32
HI = lax.Precision.HIGHEST

CHUNK = 64
CHUNK_SHIFT = 6
EPS = 1e-6
ROPE_THETA = 10000.0
A_HD = 64
A_HEADS = 8
A_KV = 2
IDX_HEADS = 4
IDX_DIM = 64
INDEX_TOPK = 256
B_HEADS = 4
B_DK = 64
B_DV = 128
GATE_RANK = 16
GATE_TAU = 16.0
N_EXPERTS = 32
TOP_K = 4
SWIGLU_LIMIT = 7.0
SWIGLU_ALPHA = 1.702

LANES = 128
GLA_SUB = 16
GLA_EXP_CLAMP = 80.0
SLOT_BLOCK = 512
VMEM_LIMIT = 56 * 1024 * 1024
INT_MIN = -2147483648
NEG_BIG = -1e30

C_QA, C_KA, C_VA, C_QI, C_MISC, C_QG, C_KG, C_VG, C_RG, C_END = 0, 512, 640, 768, 1024, 1152, 1408, 1664, 2176, 2688
MISC_WI = 64
MISC_LR = 68


def _params(*sem):
    return pltpu.CompilerParams(dimension_semantics=sem, vmem_limit_bytes=VMEM_LIMIT)


def _nt(a, b):
    return lax.dot_general(a, b, (((1,), (1,)), ((), ())), preferred_element_type=F32)


def _tn(a, b):
    return lax.dot_general(a, b, (((0,), (0,)), ((), ())), preferred_element_type=F32)


def _rms(x):
    return x * lax.rsqrt(jnp.mean(x * x, axis=-1, keepdims=True) + EPS)


def _silu(x):
    return x / (1.0 + jnp.exp(-x))


def _value_with_ones(vb):
    ones = jnp.ones(vb.shape[:-1] + (A_HD,), vb.dtype)
    return jnp.concatenate([vb[..., :A_HD], ones, vb[..., A_HD:], ones], axis=-1)


def _adaln_kernel(c_ref, w_ref, b_ref, o_ref):
    a = _silu(c_ref[...])
    o_ref[...] = jnp.dot(a, w_ref[...], preferred_element_type=F32, precision=HI) + b_ref[...]


def _adaln(c, w, b):
    r, d = c.shape
    n = w.shape[1]
    tn = 512
    return pl.pallas_call(
        _adaln_kernel,
        grid=(n // tn,),
        in_specs=[pl.BlockSpec((r, d), lambda j: (0, 0)),
                  pl.BlockSpec((d, tn), lambda j: (0, j)),
                  pl.BlockSpec((1, tn), lambda j: (0, j))],
        out_specs=pl.BlockSpec((r, tn), lambda j: (0, j)),
        out_shape=jax.ShapeDtypeStruct((r, n), F32),
        name="adaln",
        compiler_params=_params("arbitrary"),
    )(c, w, b.reshape(1, n))


def _premix_kernel(x_ref, mod_ref, g_ref, w_ref, wg_ref, bg_ref, cos_ref, sin_ref,
                   qa_ref, ka_ref, va_ref, qi_ref, ki_ref, misc_ref, qg_ref, kg_ref, vg_ref, gg_ref, rg_ref,
                   kb_ref, vx_ref, kib_ref):
    x = x_ref[0]
    mod = mod_ref[0]
    h = _rms(x) * g_ref[...] * (1.0 + mod[1:2]) + mod[0:1]
    proj = jnp.dot(h.astype(BF16), w_ref[...], preferred_element_type=F32)
    tm = x.shape[0]
    cos = cos_ref[...]
    sin = sin_ref[...]
    lane = lax.broadcasted_iota(I32, (tm, LANES), 1)
    lower_half = (lane & (A_HD - 1)) < (A_HD // 2)

    def rope(xc):
        rot = jnp.where(lower_half, pltpu.roll(xc, LANES - A_HD // 2, 1), pltpu.roll(xc, A_HD // 2, 1))
        return xc * cos + rot * sin

    def chunk(c0, j=0):
        return proj[:, c0 + j * LANES:c0 + (j + 1) * LANES]

    for j in range((C_KA - C_QA) // LANES):
        qa_ref[0, :, j * LANES:(j + 1) * LANES] = (rope(chunk(C_QA, j)) * (A_HD ** -0.5)).astype(BF16)
    ka = rope(chunk(C_KA))
    va = chunk(C_VA)
    ka_ref[0] = ka
    va_ref[0] = va
    kb_ref[0] = ka.astype(BF16)
    vx_ref[0] = _value_with_ones(va.astype(BF16))
    for j in range((C_MISC - C_QI) // LANES):
        qi_ref[0, :, j * LANES:(j + 1) * LANES] = rope(chunk(C_QI, j)).astype(BF16)
    m = chunk(C_MISC)
    mr = rope(m)
    ki_ref[0] = mr[:, :IDX_DIM]
    kib_ref[0] = mr[:, :IDX_DIM].astype(BF16)
    misc_ref[0] = jnp.where(lane < IDX_DIM, mr, m * (IDX_HEADS ** -0.5))
    xg = jnp.dot(m, wg_ref[...], preferred_element_type=F32, precision=HI) + bg_ref[...]
    gg_ref[0] = (jnp.minimum(xg, 0.0) - jnp.log(1.0 + jnp.exp(-jnp.abs(xg)))) * (1.0 / GATE_TAU)
    qg_ref[0] = proj[:, C_QG:C_KG] * (B_DK ** -0.5)
    kg_ref[0] = proj[:, C_KG:C_VG]
    vg_ref[0] = proj[:, C_VG:C_RG]
    rg_ref[0] = proj[:, C_RG:C_END]


def _premix(x, mod, g_mix, w_r, wg_pad, bg, cos_t, sin_t, tm):
    b, t, d = x.shape
    widths = [(512, BF16), (128, F32), (128, F32), (256, BF16), (64, F32), (128, F32),
              (256, F32), (256, F32), (512, F32), (256, F32), (512, F32),
              (128, BF16), (2 * LANES, BF16), (IDX_DIM, BF16)]
    return pl.pallas_call(
        _premix_kernel,
        grid=(b, t // tm),
        in_specs=[pl.BlockSpec((1, tm, d), lambda i, j: (i, j, 0)),
                  pl.BlockSpec((1, 6, d), lambda i, j: (i, 0, 0)),
                  pl.BlockSpec((1, d), lambda i, j: (0, 0)),
                  pl.BlockSpec((d, C_END), lambda i, j: (0, 0)),
                  pl.BlockSpec((LANES, 256), lambda i, j: (0, 0)),
                  pl.BlockSpec((1, 256), lambda i, j: (0, 0)),
                  pl.BlockSpec((tm, LANES), lambda i, j: (j, 0)),
                  pl.BlockSpec((tm, LANES), lambda i, j: (j, 0))],
        out_specs=[pl.BlockSpec((1, tm, w), lambda i, j: (i, j, 0)) for w, _ in widths],
        out_shape=[jax.ShapeDtypeStruct((b, t, w), dt) for w, dt in widths],
        name="premix",
        compiler_params=_params("arbitrary", "arbitrary"),
    )(x, mod, g_mix, w_r, wg_pad, bg, cos_t, sin_t)


def _attn_kernel(q_ref, qi_ref, misc_ref, k_ref, vx_ref, ki_ref, o_ref, *,
                 tq, n_ctx, top_k, causal, n_keys, q_off, n_groups):
    kib = ki_ref[0]
    rows = tq // n_groups
    row0 = q_off + pl.program_id(1) * tq
    keys, kks = [], []
    for g in range(n_groups):
        rs = slice(g * rows, (g + 1) * rows)
        qi = qi_ref[0, rs, :]
        misc = misc_ref[0, rs, :]
        isc = jnp.zeros((rows, n_ctx), F32)
        for h in range(IDX_HEADS):
            s = _nt(qi[:, h * IDX_DIM:(h + 1) * IDX_DIM], kib)
            isc = isc + misc[:, MISC_WI + h:MISC_WI + h + 1] * jnp.maximum(s, 0.0)
        kpos = lax.broadcasted_iota(I32, (rows, n_ctx), 1)
        if causal:
            row = lax.broadcasted_iota(I32, (rows, 1), 0) + (row0 + g * rows)
            key_lim = (lax.shift_right_logical(row, CHUNK_SHIFT) + 1) * CHUNK
        else:
            key_lim = jnp.full((rows, 1), n_keys, I32)
        isc = jnp.where(isc == 0.0, 0.0, isc)
        bits = pltpu.bitcast(isc, I32)
        key = jnp.where(bits < 0, bits ^ 0x7FFFFFFF, bits)
        keys.append(jnp.where(kpos < key_lim, key, INT_MIN))
        kks.append(jnp.minimum(key_lim, top_k))

    def search(i, ans):
        bit = lax.shift_left(jnp.int32(1), 31 - i)
        out = []
        for g in range(n_groups):
            cand = ans[g] | bit
            cnt = jnp.sum((keys[g] >= (cand ^ INT_MIN)).astype(I32), axis=1, keepdims=True)
            out.append(jnp.where(cnt >= kks[g], cand, ans[g]))
        return tuple(out)

    ans = lax.fori_loop(0, 32, search, tuple(jnp.zeros((rows, 1), I32) for _ in range(n_groups)))
    r = lax.broadcasted_iota(I32, (LANES, LANES), 0)
    c = lax.broadcasted_iota(I32, (LANES, LANES), 1)
    upper = jnp.where(r < c, 1.0, 0.0).astype(BF16)
    bias_groups = []
    for g in range(n_groups):
        key = keys[g]
        thr = ans[g] ^ INT_MIN
        gt = key > thr
        need = (kks[g] - jnp.sum(gt.astype(I32), axis=1, keepdims=True)).astype(F32)
        carry = jnp.zeros((rows, 1), F32)
        blocks = []
        for j in range(n_ctx // LANES):
            sl = slice(j * LANES, (j + 1) * LANES)
            tie = jnp.where(key[:, sl] == thr, 1.0, 0.0)
            before = jnp.dot(tie.astype(BF16), upper, preferred_element_type=F32) + carry
            take_tie = jnp.where(before < need, tie, 0.0)
            blocks.append(jnp.where(gt[:, sl], 0.0, jnp.where(take_tie > 0.0, 0.0, NEG_BIG)))
            carry = carry + jnp.sum(tie, axis=1, keepdims=True)
        bias_groups.append(jnp.concatenate(blocks, axis=1))
    bias = jnp.concatenate(bias_groups, axis=0)

    q = q_ref[0]
    k = k_ref[0]
    vx = vx_ref[0]
    outs = []
    rep = A_HEADS // A_KV
    for g in range(A_KV):
        kg = k[:, g * A_HD:(g + 1) * A_HD]
        vxg = vx[:, g * LANES:(g + 1) * LANES]
        for rr in range(rep):
            hh = g * rep + rr
            s = _nt(q[:, hh * A_HD:(hh + 1) * A_HD], kg) + bias
            p = jnp.exp((s - jnp.max(s, axis=1, keepdims=True)).astype(BF16))
            ox = jnp.dot(p, vxg, preferred_element_type=F32)
            outs.append(ox[:, :A_HD] / ox[:, A_HD:A_HD + 1])
    o_ref[0] = jnp.concatenate(outs, axis=1).astype(BF16)


def _attn_call(q, qi, misc, kb, vx, kib, *, tq, q_off, n_tiles, n_ctx, causal, n_keys, n_groups):
    b = q.shape[0]
    t0 = q_off // tq
    top_k = min(INDEX_TOPK, n_keys // 4)
    kern = functools.partial(_attn_kernel, tq=tq, n_ctx=n_ctx, top_k=top_k, causal=causal, n_keys=n_keys,
                             q_off=q_off, n_groups=n_groups)

    def tok(w):
        return pl.BlockSpec((1, tq, w), lambda i, j: (i, t0 + j, 0))

    def ctx(w):
        return pl.BlockSpec((1, n_ctx, w), lambda i, j: (i, 0, 0))

    return pl.pallas_call(
        kern,
        grid=(b, n_tiles),
        in_specs=[tok(512), tok(256), tok(LANES), ctx(LANES), ctx(2 * LANES), ctx(IDX_DIM)],
        out_specs=pl.BlockSpec((1, tq, 512), lambda i, j: (i, j, 0)),
        out_shape=jax.ShapeDtypeStruct((b, n_tiles * tq, 512), BF16),
        name="attn",
        compiler_params=_params("arbitrary", "arbitrary"),
    )(q, qi, misc, kb, vx, kib)


def _attn_causal(q, qi, misc, kb, vx, kib, tq):
    t = q.shape[1]
    return jnp.stack([
        _attn_call(q, qi, misc, kb, vx, kib, tq=tq, q_off=c * tq, n_tiles=1, n_ctx=(c + 1) * tq,
                   causal=True, n_keys=t, n_groups=2)
        for c in range(t // tq)])


def _gla_kernel(q_ref, k_ref, v_ref, g_ref, rg_ref, s0_ref, go_ref, ob_ref, st_ref, st_scr, *, cs, nc):
    j = pl.program_id(1)

    @pl.when(j == 0)
    def _():
        st_scr[...] = s0_ref[0]

    r = lax.broadcasted_iota(I32, (cs, cs), 0)
    c = lax.broadcasted_iota(I32, (cs, cs), 1)
    causal = r >= c
    tri = jnp.where(causal, 1.0, 0.0)
    go = go_ref[...]
    for ci in range(nc):
        sl = slice(ci * cs, (ci + 1) * cs)
        q = q_ref[0, sl, :]
        k = k_ref[0, sl, :]
        v = v_ref[0, sl, :]
        rg = rg_ref[0, sl, :]
        bcum = jnp.dot(tri, g_ref[0, sl, :], preferred_element_type=F32, precision=HI)
        blast = bcum[cs - 1:cs, :]
        qdec = (q * jnp.exp(bcum)).astype(BF16)
        kdec = (k * jnp.exp(blast - bcum)).astype(BF16)
        eb = jnp.exp(blast)
        a_rows = [[] for _ in range(B_HEADS)]
        for i in range(cs // GLA_SUB):
            rs = slice(i * GLA_SUB, (i + 1) * GLA_SUB)
            ref = bcum[i * GLA_SUB:i * GLA_SUB + 1, :]
            qt = (q[rs, :] * jnp.exp(bcum[rs, :] - ref)).astype(BF16)
            kt = (k * jnp.exp(jnp.minimum(ref - bcum, GLA_EXP_CLAMP))).astype(BF16)
            for h in range(B_HEADS):
                hs = slice(h * B_DK, (h + 1) * B_DK)
                a_rows[h].append(_nt(qt[:, hs], kt[:, hs]))
        for h in range(B_HEADS):
            hs = slice(h * B_DK, (h + 1) * B_DK)
            vs = slice(h * B_DV, (h + 1) * B_DV)
            a = jnp.where(causal, jnp.concatenate(a_rows[h], axis=0), 0.0).astype(BF16)
            vh = v[:, vs].astype(BF16)
            st = st_scr[h]
            o = jnp.dot(a, vh, preferred_element_type=F32) + _nt(qdec[:, hs], st.astype(BF16))
            st_scr[h] = st * eb[:, hs] + _tn(vh, kdec[:, hs])
            y = _rms(o) * go
            ob_ref[0, sl, vs] = (y * _silu(rg[:, vs])).astype(BF16)

    @pl.when(j == pl.num_programs(1) - 1)
    def _():
        st_ref[0] = st_scr[...]


def _gla(qg, kg, vg, gg, rg, state_t, g_out, *, cs, nc):
    b, t, _ = qg.shape
    tt = cs * nc
    kern = functools.partial(_gla_kernel, cs=cs, nc=nc)

    def tok(w):
        return pl.BlockSpec((1, tt, w), lambda i, j: (i, j, 0))

    st_spec = pl.BlockSpec((1, B_HEADS, B_DV, B_DK), lambda i, j: (i, 0, 0, 0))
    return pl.pallas_call(
        kern,
        grid=(b, t // tt),
        in_specs=[tok(256), tok(256), tok(512), tok(256), tok(512), st_spec,
                  pl.BlockSpec((1, B_DV), lambda i, j: (0, 0))],
        out_specs=[tok(512), st_spec],
        out_shape=[jax.ShapeDtypeStruct((b, t, 512), BF16),
                   jax.ShapeDtypeStruct((b, B_HEADS, B_DV, B_DK), F32)],
        scratch_shapes=[pltpu.VMEM((B_HEADS, B_DV, B_DK), F32)],
        name="gla",
        compiler_params=_params("arbitrary", "arbitrary"),
    )(qg, kg, vg, gg, rg, state_t, g_out)


def _merge_kernel(oa_ref, ob_ref, x_ref, mod_ref, wo_ref, gf_ref, wr_ref, br_ref, cnt0_ref,
                  x1_ref, h2_ref, ri_ref, rgate_ref, cnt_ref, carry_scr):
    @pl.when((pl.program_id(0) == 0) & (pl.program_id(1) == 0))
    def _():
        carry_scr[...] = cnt0_ref[...]

    mod = mod_ref[0]
    cat = jnp.concatenate([oa_ref[0, 0], ob_ref[0]], axis=1)
    x1 = x_ref[0] + mod[2:3] * jnp.dot(cat, wo_ref[...], preferred_element_type=F32)
    x1_ref[0] = x1
    h2 = _rms(x1) * gf_ref[...] * (1.0 + mod[4:5]) + mod[3:4]
    h2_ref[0] = h2
    tm = x1.shape[0]
    lane = lax.broadcasted_iota(I32, (tm, LANES), 1)
    hi = h2.astype(BF16)
    lo = (h2 - hi.astype(F32)).astype(BF16)
    wr = wr_ref[...]
    a = jnp.dot(hi, wr, preferred_element_type=F32)
    left = (a[:, :LANES] + a[:, LANES:]) + jnp.dot(lo, wr[:, :LANES], preferred_element_type=F32) + br_ref[...]
    idx, val = [], []
    for _ in range(TOP_K):
        m = jnp.max(left, axis=1, keepdims=True)
        e = jnp.min(jnp.where(left == m, lane, LANES), axis=1, keepdims=True)
        idx.append(e)
        val.append(m)
        left = jnp.where(lane == e, -jnp.inf, left)
    ex = [jnp.exp(vv - val[0]) for vv in val]
    den = ex[0] + ex[1] + ex[2] + ex[3]
    onehot = jnp.zeros((tm, LANES), F32)
    for e in idx:
        onehot = onehot + jnp.where(lane == e, 1.0, 0.0)
    r = lax.broadcasted_iota(I32, (tm, tm), 0)
    c = lax.broadcasted_iota(I32, (tm, tm), 1)
    earlier = jnp.where(r > c, 1.0, 0.0).astype(BF16)
    before = jnp.dot(earlier, onehot.astype(BF16), preferred_element_type=F32) + carry_scr[...]
    ri = jnp.zeros((tm, LANES), I32)
    rgate = jnp.zeros((tm, LANES), F32)
    for kq in range(TOP_K):
        rank = jnp.sum(jnp.where(lane == idx[kq], before, 0.0), axis=1, keepdims=True).astype(I32)
        ri = jnp.where(lane == kq, idx[kq], ri)
        ri = jnp.where(lane == TOP_K + kq, rank, ri)
        rgate = jnp.where(lane == kq, ex[kq] / den, rgate)
    ri_ref[0] = ri
    rgate_ref[0] = rgate
    carry_scr[...] = carry_scr[...] + jnp.sum(onehot, axis=0, keepdims=True)
    cnt_ref[...] = carry_scr[...]


def _merge(oa, ob, x, mod, w_out, g_ffn, wr_pad, br_pad, cnt0, tm):
    b, t, d = x.shape

    def tok(w):
        return pl.BlockSpec((1, tm, w), lambda i, j: (i, j, 0))

    def const(s):
        return pl.BlockSpec(s, lambda i, j: (0, 0))

    return pl.pallas_call(
        _merge_kernel,
        grid=(b, t // tm),
        in_specs=[pl.BlockSpec((1, 1, tm, 512), lambda i, j: (j, i, 0, 0)),
                  tok(512), tok(d), pl.BlockSpec((1, 6, d), lambda i, j: (i, 0, 0)),
                  const((d, d)), const((1, d)), const((d, 2 * LANES)), const((1, LANES)), const((1, LANES))],
        out_specs=[tok(d), tok(d), tok(LANES), tok(LANES), const((1, LANES))],
        out_shape=[jax.ShapeDtypeStruct((b, t, d), F32), jax.ShapeDtypeStruct((b, t, d), F32),
                   jax.ShapeDtypeStruct((b, t, LANES), I32), jax.ShapeDtypeStruct((b, t, LANES), F32),
                   jax.ShapeDtypeStruct((1, LANES), F32)],
        scratch_shapes=[pltpu.VMEM((1, LANES), F32)],
        name="merge",
        compiler_params=_params("arbitrary", "arbitrary"),
    )(oa, ob, x, mod, w_out, g_ffn, wr_pad, br_pad, cnt0)


def _dispatch_kernel(pe_ref, hp_ref, dest_ref, h_ref, *rest, tm, zero_init):
    if zero_init:
        xs_ref, zbuf, sem, zsem = rest

        @pl.when(pl.program_id(0) == 0)
        def _():
            zbuf[...] = jnp.zeros_like(zbuf)

            def zero_copy(e):
                start = pl.multiple_of(pe_ref[e] - SLOT_BLOCK, SLOT_BLOCK)
                return pltpu.make_async_copy(zbuf, xs_ref.at[pl.ds(start, SLOT_BLOCK), :], zsem)

            for e in range(N_EXPERTS):
                @pl.when(hp_ref[e] > 0)
                def _():
                    zero_copy(e).start()
            for e in range(N_EXPERTS):
                @pl.when(hp_ref[e] > 0)
                def _():
                    zero_copy(e).wait()
    else:
        _, xs_ref, sem = rest

    def row_copy(r, slot):
        return pltpu.make_async_copy(h_ref.at[pl.ds(r, 1), :], xs_ref.at[pl.ds(slot, 1), :], sem)

    def issue(r, carry):
        for kq in range(TOP_K):
            row_copy(r, dest_ref[r * TOP_K + kq]).start()
        return carry

    lax.fori_loop(0, tm, issue, 0)

    def drain(r, carry):
        for kq in range(TOP_K):
            row_copy(0, 0).wait()
        return carry

    lax.fori_loop(0, tm, drain, 0)


def _dispatch(pad_end, has_pad, dest_flat, h2_flat, xs, n_slots, tm):
    n, d = h2_flat.shape
    zero_init = xs is None
    kern = functools.partial(_dispatch_kernel, tm=tm, zero_init=zero_init)
    in_specs = [pl.BlockSpec((tm * TOP_K,), lambda i, pe, hp: (i,), memory_space=pltpu.SMEM),
                pl.BlockSpec((tm, d), lambda i, pe, hp: (i, 0))]
    args = [pad_end, has_pad, dest_flat, h2_flat]
    scratch = [pltpu.SemaphoreType.DMA(())]
    aliases = {}
    if zero_init:
        scratch = [pltpu.VMEM((SLOT_BLOCK, d), F32), pltpu.SemaphoreType.DMA(()), pltpu.SemaphoreType.DMA(())]
    else:
        in_specs.append(pl.BlockSpec(memory_space=pl.ANY))
        args.append(xs)
        aliases = {4: 0}
    grid_spec = pltpu.PrefetchScalarGridSpec(
        num_scalar_prefetch=2,
        grid=(n // tm,),
        in_specs=in_specs,
        out_specs=pl.BlockSpec(memory_space=pl.ANY),
        scratch_shapes=scratch,
    )
    return pl.pallas_call(
        kern,
        grid_spec=grid_spec,
        out_shape=jax.ShapeDtypeStruct((n_slots, d), F32),
        input_output_aliases=aliases,
        name="dispatch",
        compiler_params=_params("arbitrary"),
    )(*args)


def _expert_kernel(be_ref, nu_ref, xs_ref, wgu_ref, bgu_ref, wdn_ref, bdn_ref, ys_ref):
    @pl.when(pl.program_id(0) < nu_ref[0])
    def _():
        f = wdn_ref.shape[1]
        gu = jnp.dot(xs_ref[...].astype(BF16), wgu_ref[0], preferred_element_type=F32) + bgu_ref[0]
        gate = jnp.minimum(gu[:, :f], SWIGLU_LIMIT)
        up = jnp.clip(gu[:, f:], -SWIGLU_LIMIT, SWIGLU_LIMIT)
        glu = gate / (1.0 + jnp.exp(-SWIGLU_ALPHA * gate))
        act = ((up + 1.0) * glu).astype(BF16)
        ys_ref[...] = jnp.dot(act, wdn_ref[0], preferred_element_type=F32) + bdn_ref[0]


def _experts(block_expert, n_used, xs, wgu, bgu, wdn, bdn):
    n_slots, d = xs.shape
    nb = n_slots // SLOT_BLOCK
    f = wdn.shape[1]

    def blk(i, be, nu):
        return (jnp.minimum(i, nu[0] - 1), 0)

    def wsel(i, be, nu):
        return (be[i], 0, 0)

    grid_spec = pltpu.PrefetchScalarGridSpec(
        num_scalar_prefetch=2,
        grid=(nb,),
        in_specs=[pl.BlockSpec((SLOT_BLOCK, d), blk),
                  pl.BlockSpec((1, d, 2 * f), wsel),
                  pl.BlockSpec((1, 1, 2 * f), wsel),
                  pl.BlockSpec((1, f, d), wsel),
                  pl.BlockSpec((1, 1, d), wsel)],
        out_specs=pl.BlockSpec((SLOT_BLOCK, d), blk),
    )
    return pl.pallas_call(
        _expert_kernel,
        grid_spec=grid_spec,
        out_shape=jax.ShapeDtypeStruct((n_slots, d), F32),
        name="experts",
        compiler_params=_params("arbitrary"),
    )(block_expert, n_used, xs, wgu, bgu, wdn, bdn)


def _combine_kernel(dest_ref, x1_ref, gate_ref, mod_ref, modf_ref, gfin_ref, ys_ref, y_ref, buf, sem, *, tm):
    def row_copy(kq, r, slot):
        return pltpu.make_async_copy(ys_ref.at[pl.ds(slot, 1), :], buf.at[kq, pl.ds(r, 1), :], sem)

    def issue(r, carry):
        for kq in range(TOP_K):
            row_copy(kq, r, dest_ref[r * TOP_K + kq]).start()
        return carry

    lax.fori_loop(0, tm, issue, 0)

    def drain(r, carry):
        for kq in range(TOP_K):
            row_copy(0, 0, 0).wait()
        return carry

    lax.fori_loop(0, tm, drain, 0)
    gates = gate_ref[0]
    moe = gates[:, 0:1] * buf[0]
    for kq in range(1, TOP_K):
        moe = moe + gates[:, kq:kq + 1] * buf[kq]
    xo = x1_ref[0] + mod_ref[0][5:6] * moe
    modf = modf_ref[0]
    y_ref[0] = _rms(xo) * gfin_ref[...] * (1.0 + modf[1:2]) + modf[0:1]


def _combine(dest_flat, x1, gates, mod, modf, g_final, ys, tm):
    b, t, d = x1.shape
    nt = t // tm
    kern = functools.partial(_combine_kernel, tm=tm)
    return pl.pallas_call(
        kern,
        grid=(b, nt),
        in_specs=[pl.BlockSpec((tm * TOP_K,), lambda i, j: (i * nt + j,), memory_space=pltpu.SMEM),
                  pl.BlockSpec((1, tm, d), lambda i, j: (i, j, 0)),
                  pl.BlockSpec((1, tm, LANES), lambda i, j: (i, j, 0)),
                  pl.BlockSpec((1, 6, d), lambda i, j: (i, 0, 0)),
                  pl.BlockSpec((1, 2, d), lambda i, j: (i, 0, 0)),
                  pl.BlockSpec((1, d), lambda i, j: (0, 0)),
                  pl.BlockSpec(memory_space=pl.ANY)],
        out_specs=pl.BlockSpec((1, tm, d), lambda i, j: (i, j, 0)),
        out_shape=jax.ShapeDtypeStruct((b, t, d), F32),
        scratch_shapes=[pltpu.VMEM((TOP_K, tm, d), F32), pltpu.SemaphoreType.DMA(())],
        name="combine",
        compiler_params=_params("arbitrary", "arbitrary"),
    )(dest_flat, x1, gates, mod, modf, g_final, ys)


def _rope_tables(pos):
    half = A_HD // 2
    inv_freq = ROPE_THETA ** (-jnp.arange(half, dtype=F32) / half)
    ang = pos.astype(F32)[:, None] * inv_freq[None, :]
    cos = jnp.cos(ang)
    sin = jnp.sin(ang)
    reps = LANES // A_HD
    return jnp.tile(jnp.concatenate([cos, cos], axis=1), (1, reps)), jnp.tile(jnp.concatenate([-sin, sin], axis=1), (1, reps))


def _reorder_w_in(w_in):
    offs = [0]
    for w in (512, 128, 128, 256, 64, 4, 256, 256, 512, 16, 512):
        offs.append(offs[-1] + w)
    seg = [w_in[:, offs[i]:offs[i + 1]] for i in range(11)]
    qa, ka, va, qi, ki, wi, qg, kg, vg, lr, rg = seg
    pad = jnp.zeros((w_in.shape[0], LANES - IDX_DIM - IDX_HEADS - GATE_RANK), w_in.dtype)
    return jnp.concatenate([qa, ka, va, qi, ki, wi, lr, pad, qg, kg, vg, rg], axis=1).astype(BF16)


def _pad_keys(a, n_pad):
    return jnp.pad(a, ((0, 0), (0, n_pad - a.shape[1]), (0, 0)))


def kernel(x_prompt, x_sample, cache_k, cache_v, cache_kidx, state_gla, c_prompt, c_sample,
           w_mod, b_mod, g_mix, g_ffn, w_in, gla_w_gate, gla_b_gate, gla_g_out, w_out,
           w_router, b_router, w_gate_up, b_gate_up, w_down, b_down,
           w_mod_final, b_mod_final, g_final):
    depth = w_in.shape[0]
    assert depth == 1
    bp, sp, d = x_prompt.shape
    bs, ts, _ = x_sample.shape
    past = cache_k.shape[2]
    n_exp = w_router.shape[2]
    f = w_down.shape[2]

    c_all = jnp.concatenate([c_prompt, c_sample], axis=0)
    w_r = _reorder_w_in(w_in[0])
    wg_pad = jnp.zeros((LANES, B_HEADS * B_DK), F32).at[MISC_LR:MISC_LR + GATE_RANK].set(gla_w_gate[0])
    bg = gla_b_gate[0].reshape(1, -1)
    wr_pad = jnp.zeros((d, LANES), F32).at[:, :n_exp].set(w_router[0])
    wr_hi = wr_pad.astype(BF16)
    wr_cat = jnp.concatenate([wr_hi, (wr_pad - wr_hi.astype(F32)).astype(BF16)], axis=1)
    br_pad = jnp.full((1, LANES), NEG_BIG, F32).at[0, :n_exp].set(b_router[0])
    w_out_b = w_out[0].astype(BF16)
    wgu_b = w_gate_up[0].astype(BF16)
    wdn_b = w_down[0].astype(BF16)
    bgu = b_gate_up[0].reshape(n_exp, 1, 2 * f)
    bdn = b_down[0].reshape(n_exp, 1, d)
    g_out = gla_g_out[0].reshape(1, B_DV)

    mod_all = _adaln(c_all, w_mod[0], b_mod[0]).reshape(bp + bs, 6, d)
    modf_all = _adaln(c_all, w_mod_final, b_mod_final).reshape(bp + bs, 2, d)
    mod_p, mod_s = mod_all[:bp], mod_all[bp:]
    modf_p, modf_s = modf_all[:bp], modf_all[bp:]

    def mixer(x, mod, pos, tm, past_kv, state_t, cs, nc, tq):
        cos_t, sin_t = _rope_tables(pos)
        qa, ka, va, qi, ki, misc, qg, kg, vg, gg, rg, kb, vx, kib = _premix(
            x, mod, g_mix[0].reshape(1, d), w_r, wg_pad, bg, cos_t, sin_t, tm)
        if past_kv is None:
            oa = _attn_causal(qa, qi, misc, kb, vx, kib, tq)
        else:
            k_past, v_past, ki_past = past_kv
            kb = jnp.concatenate([k_past.astype(BF16), kb], axis=1)
            vx = jnp.concatenate([_value_with_ones(v_past.astype(BF16)), vx], axis=1)
            kib = jnp.concatenate([ki_past.astype(BF16), kib], axis=1)
            n_keys = kb.shape[1]
            n_pad = -(-n_keys // LANES) * LANES
            if n_pad != n_keys:
                kb, vx, kib = _pad_keys(kb, n_pad), _pad_keys(vx, n_pad), _pad_keys(kib, n_pad)
            assert tq == x.shape[1]
            oa = _attn_call(qa, qi, misc, kb, vx, kib, tq=tq, q_off=0, n_tiles=1,
                            n_ctx=n_pad, causal=False, n_keys=n_keys, n_groups=1)[None]
        ob, st = _gla(qg, kg, vg, gg, rg, state_t, g_out, cs=cs, nc=nc)
        return oa, ob, (ka, va, ki, st)

    state0_p = jnp.zeros((bp, B_HEADS, B_DV, B_DK), F32)
    oa_p, ob_p, (ka_p, va_p, ki_p, st_p) = mixer(
        x_prompt, mod_p, jnp.arange(sp), min(256, sp), None, state0_p, CHUNK, 4, min(256, sp))
    state0_s = jnp.swapaxes(state_gla[0], -1, -2)
    past_kv = (cache_k[0].reshape(bs, past, A_KV * A_HD), cache_v[0].reshape(bs, past, A_KV * A_HD), cache_kidx[0])
    oa_s, ob_s, (ka_s, va_s, ki_s, st_s) = mixer(
        x_sample, mod_s, past + jnp.arange(ts), ts, past_kv, state0_s, ts, 1, ts)

    cnt0 = jnp.zeros((1, LANES), F32)
    x1_p, h2_p, ri_p, gate_p, cnt1 = _merge(oa_p, ob_p, x_prompt, mod_p, w_out_b, g_ffn[0].reshape(1, d),
                                            wr_cat, br_pad, cnt0, min(256, sp))
    x1_s, h2_s, ri_s, gate_s, cnt2 = _merge(oa_s, ob_s, x_sample, mod_s, w_out_b, g_ffn[0].reshape(1, d),
                                            wr_cat, br_pad, cnt1, ts)
    counts = cnt2[0, :n_exp].astype(I32)
    padded = (counts + SLOT_BLOCK - 1) // SLOT_BLOCK * SLOT_BLOCK
    pad_end = jnp.cumsum(padded)
    pad_start = pad_end - padded
    n_asg = (bp * sp + bs * ts) * TOP_K
    nb = -(-n_asg // SLOT_BLOCK) + n_exp
    n_slots = nb * SLOT_BLOCK
    block_start = jnp.arange(nb, dtype=I32) * SLOT_BLOCK
    block_expert = jnp.minimum(jnp.sum((pad_end[None, :] <= block_start[:, None]).astype(I32), axis=1), n_exp - 1)
    n_used = (pad_end[-1:] // SLOT_BLOCK).astype(I32)
    has_pad = (padded > counts).astype(I32)
    expert_ids = jnp.arange(n_exp, dtype=I32)

    def dests(ri):
        e, rank = ri[..., :TOP_K], ri[..., TOP_K:2 * TOP_K]
        start = jnp.sum(jnp.where(e[..., None] == expert_ids, pad_start.astype(I32), 0), axis=-1)
        return (start + rank).reshape(-1)

    dest_p, dest_s = dests(ri_p), dests(ri_s)
    pad_end_i = pad_end.astype(I32)
    xs = _dispatch(pad_end_i, has_pad, dest_p, h2_p.reshape(bp * sp, d), None, n_slots, min(256, sp))
    xs = _dispatch(pad_end_i, has_pad, dest_s, h2_s.reshape(bs * ts, d), xs, n_slots, min(256, bs * ts))
    ys = _experts(block_expert, n_used, xs, wgu_b, bgu, wdn_b, bdn)
    y_p = _combine(dest_p, x1_p, gate_p, mod_p, modf_p, g_final.reshape(1, d), ys, min(256, sp))
    y_s = _combine(dest_s, x1_s, gate_s, mod_s, modf_s, g_final.reshape(1, d), ys, ts)

    def kv(a, b, t):
        return a.reshape(1, b, t, A_KV, A_HD)

    return (y_p, y_s,
            kv(ka_p, bp, sp), kv(va_p, bp, sp), ki_p[None], jnp.swapaxes(st_p, -1, -2)[None],
            kv(ka_s, bs, ts), kv(va_s, bs, ts), ki_s[None], jnp.swapaxes(st_s, -1, -2)[None])
```

```python
import functools

import jax
import jax.numpy as jnp
from jax import lax
from jax.experimental import pallas as pl
from jax.experimental.pallas import tpu as pltpu

F32 = jnp.float32
BF16 = jnp.bfloat16
I32 = jnp.int32
HI = lax.Precision.HIGHEST

CHUNK = 64
CHUNK_SHIFT = 6
EPS = 1e-6
ROPE_THETA = 10000.0
A_HD = 64
A_HEADS = 8
A_KV = 2
IDX_HEADS = 4
IDX_DIM = 64
INDEX_TOPK = 256
B_HEADS = 4
B_DK = 64
B_DV = 128
GATE_RANK = 16
GATE_TAU = 16.0
N_EXPERTS = 32
TOP_K = 4
SWIGLU_LIMIT = 7.0
SWIGLU_ALPHA = 1.702

LANES = 128
SUBLANES = 8
ROW_UNROLL = 8
GLA_SUB = 16
GLA_EXP_CLAMP = 80.0
SLOT_BLOCK = 512
VMEM_LIMIT = 56 * 1024 * 1024
INT_MIN = -2147483648
NEG_BIG = -1e30

C_QA, C_KA, C_VA, C_QI, C_MISC, C_QG, C_KG, C_VG, C_RG, C_END = 0, 512, 640, 768, 1024, 1152, 1408, 1664, 2176, 2688
MISC_WI = 64
MISC_LR = 68


def _params(*sem):
    return pltpu.CompilerParams(dimension_semantics=sem, vmem_limit_bytes=VMEM_LIMIT)


def _nt(a, b):
    return lax.dot_general(a, b, (((1,), (1,)), ((), ())), preferred_element_type=F32)


def _tn(a, b):
    return lax.dot_general(a, b, (((0,), (0,)), ((), ())), preferred_element_type=F32)


def _rms(x):
    return x * lax.rsqrt(jnp.mean(x * x, axis=-1, keepdims=True) + EPS)


def _silu(x):
    return x / (1.0 + jnp.exp(-x))


def _store_row_tiles(ref, val):
    rows, width = val.shape
    assert width == SUBLANES * LANES
    for s in range(SUBLANES):
        ref[pl.ds(s, rows, stride=SUBLANES), :] = val[:, s * LANES:(s + 1) * LANES]


def _load_row_tiles(ref, rows):
    return jnp.concatenate([ref[pl.ds(s, rows, stride=SUBLANES), :] for s in range(SUBLANES)], axis=1)


def _value_with_ones(vb):
    ones = jnp.ones(vb.shape[:-1] + (A_HD,), vb.dtype)
    return jnp.concatenate([vb[..., :A_HD], ones, vb[..., A_HD:], ones], axis=-1)


def _adaln_kernel(c_ref, w_ref, b_ref, o_ref):
    a = _silu(c_ref[...])
    o_ref[...] = jnp.dot(a, w_ref[...], preferred_element_type=F32, precision=HI) + b_ref[...]


def _adaln(c, w, b):
    r, d = c.shape
    n = w.shape[1]
    tn = 512
    return pl.pallas_call(
        _adaln_kernel,
        grid=(n // tn,),
        in_specs=[pl.BlockSpec((r, d), lambda j: (0, 0)),
                  pl.BlockSpec((d, tn), lambda j: (0, j)),
                  pl.BlockSpec((1, tn), lambda j: (0, j))],
        out_specs=pl.BlockSpec((r, tn), lambda j: (0, j)),
        out_shape=jax.ShapeDtypeStruct((r, n), F32),
        name="adaln",
        compiler_params=_params("arbitrary"),
    )(c, w, b.reshape(1, n))


def _premix_kernel(x_ref, mod_ref, g_ref, w_ref, wg_ref, bg_ref, cos_ref, sin_ref,
                   qa_ref, ka_ref, va_ref, qi_ref, ki_ref, misc_ref, qg_ref, kg_ref, vg_ref, gg_ref, rg_ref,
                   kb_ref, vx_ref, kib_ref):
    x = x_ref[0]
    mod = mod_ref[0]
    h = _rms(x) * g_ref[...] * (1.0 + mod[1:2]) + mod[0:1]
    proj = jnp.dot(h.astype(BF16), w_ref[...], preferred_element_type=F32)
    tm = x.shape[0]
    cos = cos_ref[...]
    sin = sin_ref[...]
    lane = lax.broadcasted_iota(I32, (tm, LANES), 1)
    lower_half = (lane & (A_HD - 1)) < (A_HD // 2)

    def rope(xc):
        rot = jnp.where(lower_half, pltpu.roll(xc, LANES - A_HD // 2, 1), pltpu.roll(xc, A_HD // 2, 1))
        return xc * cos + rot * sin

    def chunk(c0, j=0):
        return proj[:, c0 + j * LANES:c0 + (j + 1) * LANES]

    for j in range((C_KA - C_QA) // LANES):
        qa_ref[0, :, j * LANES:(j + 1) * LANES] = (rope(chunk(C_QA, j)) * (A_HD ** -0.5)).astype(BF16)
    ka = rope(chunk(C_KA))
    va = chunk(C_VA)
    ka_ref[0] = ka
    va_ref[0] = va
    kb_ref[0] = ka.astype(BF16)
    vx_ref[0] = _value_with_ones(va.astype(BF16))
    for j in range((C_MISC - C_QI) // LANES):
        qi_ref[0, :, j * LANES:(j + 1) * LANES] = rope(chunk(C_QI, j)).astype(BF16)
    m = chunk(C_MISC)
    mr = rope(m)
    ki_ref[0] = mr[:, :IDX_DIM]
    kib_ref[0] = mr[:, :IDX_DIM].astype(BF16)
    misc_ref[0] = jnp.where(lane < IDX_DIM, mr, m * (IDX_HEADS ** -0.5))
    xg = jnp.dot(m, wg_ref[...], preferred_element_type=F32, precision=HI) + bg_ref[...]
    gg_ref[0] = (jnp.minimum(xg, 0.0) - jnp.log(1.0 + jnp.exp(-jnp.abs(xg)))) * (1.0 / GATE_TAU)
    qg_ref[0] = proj[:, C_QG:C_KG] * (B_DK ** -0.5)
    kg_ref[0] = proj[:, C_KG:C_VG]
    vg_ref[0] = proj[:, C_VG:C_RG]
    rg_ref[0] = proj[:, C_RG:C_END]


def _premix(x, mod, g_mix, w_r, wg_pad, bg, cos_t, sin_t, tm):
    b, t, d = x.shape
    widths = [(512, BF16), (128, F32), (128, F32), (256, BF16), (64, F32), (128, F32),
              (256, F32), (256, F32), (512, F32), (256, F32), (512, F32),
              (128, BF16), (2 * LANES, BF16), (IDX_DIM, BF16)]
    return pl.pallas_call(
        _premix_kernel,
        grid=(b, t // tm),
        in_specs=[pl.BlockSpec((1, tm, d), lambda i, j: (i, j, 0)),
                  pl.BlockSpec((1, 6, d), lambda i, j: (i, 0, 0)),
                  pl.BlockSpec((1, d), lambda i, j: (0, 0)),
                  pl.BlockSpec((d, C_END), lambda i, j: (0, 0)),
                  pl.BlockSpec((LANES, 256), lambda i, j: (0, 0)),
                  pl.BlockSpec((1, 256), lambda i, j: (0, 0)),
                  pl.BlockSpec((tm, LANES), lambda i, j: (j, 0)),
                  pl.BlockSpec((tm, LANES), lambda i, j: (j, 0))],
        out_specs=[pl.BlockSpec((1, tm, w), lambda i, j: (i, j, 0)) for w, _ in widths],
        out_shape=[jax.ShapeDtypeStruct((b, t, w), dt) for w, dt in widths],
        name="premix",
        compiler_params=_params("arbitrary", "arbitrary"),
    )(x, mod, g_mix, w_r, wg_pad, bg, cos_t, sin_t)


def _attn_kernel(q_ref, qi_ref, misc_ref, k_ref, vx_ref, ki_ref, o_ref, *,
                 tq, n_ctx, top_k, causal, n_keys, q_off, n_groups):
    kib = ki_ref[0]
    rows = tq // n_groups
    row0 = q_off + pl.program_id(1) * tq
    keys, kks = [], []
    for g in range(n_groups):
        rs = slice(g * rows, (g + 1) * rows)
        qi = qi_ref[0, rs, :]
        misc = misc_ref[0, rs, :]
        isc = jnp.zeros((rows, n_ctx), F32)
        for h in range(IDX_HEADS):
            s = _nt(qi[:, h * IDX_DIM:(h + 1) * IDX_DIM], kib)
            isc = isc + misc[:, MISC_WI + h:MISC_WI + h + 1] * jnp.maximum(s, 0.0)
        kpos = lax.broadcasted_iota(I32, (rows, n_ctx), 1)
        if causal:
            row = lax.broadcasted_iota(I32, (rows, 1), 0) + (row0 + g * rows)
            key_lim = (lax.shift_right_logical(row, CHUNK_SHIFT) + 1) * CHUNK
        else:
            key_lim = jnp.full((rows, 1), n_keys, I32)
        isc = jnp.where(isc == 0.0, 0.0, isc)
        bits = pltpu.bitcast(isc, I32)
        key = jnp.where(bits < 0, bits ^ 0x7FFFFFFF, bits)
        keys.append(jnp.where(kpos < key_lim, key, INT_MIN))
        kks.append(jnp.minimum(key_lim, top_k))

    def search(i, ans):
        bit = lax.shift_left(jnp.int32(1), 31 - i)
        out = []
        for g in range(n_groups):
            cand = ans[g] | bit
            cnt = jnp.sum((keys[g] >= (cand ^ INT_MIN)).astype(I32), axis=1, keepdims=True)
            out.append(jnp.where(cnt >= kks[g], cand, ans[g]))
        return tuple(out)

    ans = lax.fori_loop(0, 32, search, tuple(jnp.zeros((rows, 1), I32) for _ in range(n_groups)))
    r = lax.broadcasted_iota(I32, (LANES, LANES), 0)
    c = lax.broadcasted_iota(I32, (LANES, LANES), 1)
    upper = jnp.where(r < c, 1.0, 0.0).astype(BF16)
    bias_groups = []
    for g in range(n_groups):
        key = keys[g]
        thr = ans[g] ^ INT_MIN
        gt = key > thr
        need = (kks[g] - jnp.sum(gt.astype(I32), axis=1, keepdims=True)).astype(F32)
        carry = jnp.zeros((rows, 1), F32)
        blocks = []
        for j in range(n_ctx // LANES):
            sl = slice(j * LANES, (j + 1) * LANES)
            tie = jnp.where(key[:, sl] == thr, 1.0, 0.0)
            before = jnp.dot(tie.astype(BF16), upper, preferred_element_type=F32) + carry
            take_tie = jnp.where(before < need, tie, 0.0)
            blocks.append(jnp.where(gt[:, sl], 0.0, jnp.where(take_tie > 0.0, 0.0, NEG_BIG)))
            carry = carry + jnp.sum(tie, axis=1, keepdims=True)
        bias_groups.append(jnp.concatenate(blocks, axis=1))
    bias = jnp.concatenate(bias_groups, axis=0)

    q = q_ref[0]
    k = k_ref[0]
    vx = vx_ref[0]
    outs = []
    rep = A_HEADS // A_KV
    for g in range(A_KV):
        kg = k[:, g * A_HD:(g + 1) * A_HD]
        vxg = vx[:, g * LANES:(g + 1) * LANES]
        for rr in range(rep):
            hh = g * rep + rr
            s = _nt(q[:, hh * A_HD:(hh + 1) * A_HD], kg) + bias
            p = jnp.exp((s - jnp.max(s, axis=1, keepdims=True)).astype(BF16))
            ox = jnp.dot(p, vxg, preferred_element_type=F32)
            outs.append(ox[:, :A_HD] / ox[:, A_HD:A_HD + 1])
    o_ref[0] = jnp.concatenate(outs, axis=1).astype(BF16)


def _attn_call(q, qi, misc, kb, vx, kib, *, tq, q_off, n_tiles, n_ctx, causal, n_keys, n_groups):
    b = q.shape[0]
    t0 = q_off // tq
    top_k = min(INDEX_TOPK, n_keys // 4)
    kern = functools.partial(_attn_kernel, tq=tq, n_ctx=n_ctx, top_k=top_k, causal=causal, n_keys=n_keys,
                             q_off=q_off, n_groups=n_groups)

    def tok(w):
        return pl.BlockSpec((1, tq, w), lambda i, j: (i, t0 + j, 0))

    def ctx(w):
        return pl.BlockSpec((1, n_ctx, w), lambda i, j: (i, 0, 0))

    return pl.pallas_call(
        kern,
        grid=(b, n_tiles),
        in_specs=[tok(512), tok(256), tok(LANES), ctx(LANES), ctx(2 * LANES), ctx(IDX_DIM)],
        out_specs=pl.BlockSpec((1, tq, 512), lambda i, j: (i, j, 0)),
        out_shape=jax.ShapeDtypeStruct((b, n_tiles * tq, 512), BF16),
        name="attn",
        compiler_params=_params("arbitrary", "arbitrary"),
    )(q, qi, misc, kb, vx, kib)


def _attn_causal(q, qi, misc, kb, vx, kib, tq):
    t = q.shape[1]
    return jnp.stack([
        _attn_call(q, qi, misc, kb, vx, kib, tq=tq, q_off=c * tq, n_tiles=1, n_ctx=(c + 1) * tq,
                   causal=True, n_keys=t, n_groups=2)
        for c in range(t // tq)])


def _gla_kernel(q_ref, k_ref, v_ref, g_ref, rg_ref, s0_ref, go_ref, ob_ref, st_ref, st_scr, *, cs, nc):
    j = pl.program_id(1)

    @pl.when(j == 0)
    def _():
        st_scr[...] = s0_ref[0]

    r = lax.broadcasted_iota(I32, (cs, cs), 0)
    c = lax.broadcasted_iota(I32, (cs, cs), 1)
    causal = r >= c
    tri = jnp.where(causal, 1.0, 0.0)
    go = go_ref[...]
    for ci in range(nc):
        sl = slice(ci * cs, (ci + 1) * cs)
        q = q_ref[0, sl, :]
        k = k_ref[0, sl, :]
        v = v_ref[0, sl, :]
        rg = rg_ref[0, sl, :]
        bcum = jnp.dot(tri, g_ref[0, sl, :], preferred_element_type=F32, precision=HI)
        blast = bcum[cs - 1:cs, :]
        qdec = (q * jnp.exp(bcum)).astype(BF16)
        kdec = (k * jnp.exp(blast - bcum)).astype(BF16)
        eb = jnp.exp(blast)
        a_rows = [[] for _ in range(B_HEADS)]
        for i in range(cs // GLA_SUB):
            rs = slice(i * GLA_SUB, (i + 1) * GLA_SUB)
            ref = bcum[i * GLA_SUB:i * GLA_SUB + 1, :]
            qt = (q[rs, :] * jnp.exp(bcum[rs, :] - ref)).astype(BF16)
            kt = (k * jnp.exp(jnp.minimum(ref - bcum, GLA_EXP_CLAMP))).astype(BF16)
            for h in range(B_HEADS):
                hs = slice(h * B_DK, (h + 1) * B_DK)
                a_rows[h].append(_nt(qt[:, hs], kt[:, hs]))
        for h in range(B_HEADS):
            hs = slice(h * B_DK, (h + 1) * B_DK)
            vs = slice(h * B_DV, (h + 1) * B_DV)
            a = jnp.where(causal, jnp.concatenate(a_rows[h], axis=0), 0.0).astype(BF16)
            vh = v[:, vs].astype(BF16)
            st = st_scr[h]
            o = jnp.dot(a, vh, preferred_element_type=F32) + _nt(qdec[:, hs], st.astype(BF16))
            st_scr[h] = st * eb[:, hs] + _tn(vh, kdec[:, hs])
            y = _rms(o) * go
            ob_ref[0, sl, vs] = (y * _silu(rg[:, vs])).astype(BF16)

    @pl.when(j == pl.num_programs(1) - 1)
    def _():
        st_ref[0] = st_scr[...]


def _gla(qg, kg, vg, gg, rg, state_t, g_out, *, cs, nc):
    b, t, _ = qg.shape
    tt = cs * nc
    kern = functools.partial(_gla_kernel, cs=cs, nc=nc)

    def tok(w):
        return pl.BlockSpec((1, tt, w), lambda i, j: (i, j, 0))

    st_spec = pl.BlockSpec((1, B_HEADS, B_DV, B_DK), lambda i, j: (i, 0, 0, 0))
    return pl.pallas_call(
        kern,
        grid=(b, t // tt),
        in_specs=[tok(256), tok(256), tok(512), tok(256), tok(512), st_spec,
                  pl.BlockSpec((1, B_DV), lambda i, j: (0, 0))],
        out_specs=[tok(512), st_spec],
        out_shape=[jax.ShapeDtypeStruct((b, t, 512), BF16),
                   jax.ShapeDtypeStruct((b, B_HEADS, B_DV, B_DK), F32)],
        scratch_shapes=[pltpu.VMEM((B_HEADS, B_DV, B_DK), F32)],
        name="gla",
        compiler_params=_params("arbitrary", "arbitrary"),
    )(qg, kg, vg, gg, rg, state_t, g_out)


def _merge_kernel(oa_ref, ob_ref, x_ref, mod_ref, wo_ref, gf_ref, wr_ref, br_ref, cnt0_ref,
                  x1_ref, h2_ref, ri_ref, rgate_ref, cnt_ref, carry_scr):
    @pl.when((pl.program_id(0) == 0) & (pl.program_id(1) == 0))
    def _():
        carry_scr[...] = cnt0_ref[...]

    mod = mod_ref[0]
    cat = jnp.concatenate([oa_ref[0, 0], ob_ref[0]], axis=1)
    x1 = x_ref[0] + mod[2:3] * jnp.dot(cat, wo_ref[...], preferred_element_type=F32)
    x1_ref[0] = x1
    h2 = _rms(x1) * gf_ref[...] * (1.0 + mod[4:5]) + mod[3:4]
    _store_row_tiles(h2_ref.at[0], h2)
    tm = x1.shape[0]
    lane = lax.broadcasted_iota(I32, (tm, LANES), 1)
    hi = h2.astype(BF16)
    lo = (h2 - hi.astype(F32)).astype(BF16)
    wr = wr_ref[...]
    a = jnp.dot(hi, wr, preferred_element_type=F32)
    left = (a[:, :LANES] + a[:, LANES:]) + jnp.dot(lo, wr[:, :LANES], preferred_element_type=F32) + br_ref[...]
    idx, val = [], []
    for _ in range(TOP_K):
        m = jnp.max(left, axis=1, keepdims=True)
        e = jnp.min(jnp.where(left == m, lane, LANES), axis=1, keepdims=True)
        idx.append(e)
        val.append(m)
        left = jnp.where(lane == e, -jnp.inf, left)
    ex = [jnp.exp(vv - val[0]) for vv in val]
    den = ex[0] + ex[1] + ex[2] + ex[3]
    onehot = jnp.zeros((tm, LANES), F32)
    for e in idx:
        onehot = onehot + jnp.where(lane == e, 1.0, 0.0)
    r = lax.broadcasted_iota(I32, (tm, tm), 0)
    c = lax.broadcasted_iota(I32, (tm, tm), 1)
    earlier = jnp.where(r > c, 1.0, 0.0).astype(BF16)
    before = jnp.dot(earlier, onehot.astype(BF16), preferred_element_type=F32) + carry_scr[...]
    ri = jnp.zeros((tm, LANES), I32)
    rgate = jnp.zeros((tm, LANES), F32)
    for kq in range(TOP_K):
        rank = jnp.sum(jnp.where(lane == idx[kq], before, 0.0), axis=1, keepdims=True).astype(I32)
        ri = jnp.where(lane == kq, idx[kq], ri)
        ri = jnp.where(lane == TOP_K + kq, rank, ri)
        rgate = jnp.where(lane == kq, ex[kq] / den, rgate)
    ri_ref[0] = ri
    rgate_ref[0] = rgate
    carry_scr[...] = carry_scr[...] + jnp.sum(onehot, axis=0, keepdims=True)
    cnt_ref[...] = carry_scr[...]


def _merge(oa, ob, x, mod, w_out, g_ffn, wr_pad, br_pad, cnt0, tm):
    b, t, d = x.shape

    def tok(w):
        return pl.BlockSpec((1, tm, w), lambda i, j: (i, j, 0))

    def const(s):
        return pl.BlockSpec(s, lambda i, j: (0, 0))

    return pl.pallas_call(
        _merge_kernel,
        grid=(b, t // tm),
        in_specs=[pl.BlockSpec((1, 1, tm, 512), lambda i, j: (j, i, 0, 0)),
                  tok(512), tok(d), pl.BlockSpec((1, 6, d), lambda i, j: (i, 0, 0)),
                  const((d, d)), const((1, d)), const((d, 2 * LANES)), const((1, LANES)), const((1, LANES))],
        out_specs=[tok(d), pl.BlockSpec((1, tm * SUBLANES, LANES), lambda i, j: (i, j, 0)),
                   tok(LANES), tok(LANES), const((1, LANES))],
        out_shape=[jax.ShapeDtypeStruct((b, t, d), F32), jax.ShapeDtypeStruct((b, t * SUBLANES, LANES), F32),
                   jax.ShapeDtypeStruct((b, t, LANES), I32), jax.ShapeDtypeStruct((b, t, LANES), F32),
                   jax.ShapeDtypeStruct((1, LANES), F32)],
        scratch_shapes=[pltpu.VMEM((1, LANES), F32)],
        name="merge",
        compiler_params=_params("arbitrary", "arbitrary"),
    )(oa, ob, x, mod, w_out, g_ffn, wr_pad, br_pad, cnt0)


def _dispatch_kernel(pe_ref, hp_ref, dest_ref, h_ref, *rest, tm, zero_init):
    if zero_init:
        xs_ref, zbuf, sem, zsem = rest

        @pl.when(pl.program_id(0) == 0)
        def _():
            zbuf[...] = jnp.zeros_like(zbuf)

            def zero_copy(e):
                start = pl.multiple_of(pe_ref[e] - SLOT_BLOCK, SLOT_BLOCK)
                return pltpu.make_async_copy(zbuf, xs_ref.at[pl.ds(start, SLOT_BLOCK)], zsem)

            for e in range(N_EXPERTS):
                @pl.when(hp_ref[e] > 0)
                def _():
                    zero_copy(e).start()
            for e in range(N_EXPERTS):
                @pl.when(hp_ref[e] > 0)
                def _():
                    zero_copy(e).wait()
    else:
        _, xs_ref, sem = rest

    def row_copy(r, slot):
        return pltpu.make_async_copy(h_ref.at[r], xs_ref.at[slot], sem)

    def issue(i, carry):
        for j in range(ROW_UNROLL):
            for kq in range(TOP_K):
                row_copy(i * ROW_UNROLL + j, dest_ref[(i * ROW_UNROLL + j) * TOP_K + kq]).start(priority=kq % 2)
        return carry

    lax.fori_loop(0, tm // ROW_UNROLL, issue, 0)

    def drain(i, carry):
        for _ in range(ROW_UNROLL * TOP_K):
            row_copy(0, 0).wait()
        return carry

    lax.fori_loop(0, tm // ROW_UNROLL, drain, 0)


def _dispatch(pad_end, has_pad, dest_flat, h2_tiles, xs, n_slots, tm):
    n = h2_tiles.shape[0]
    tile = h2_tiles.shape[1:]
    zero_init = xs is None
    kern = functools.partial(_dispatch_kernel, tm=tm, zero_init=zero_init)
    in_specs = [pl.BlockSpec((tm * TOP_K,), lambda i, pe, hp: (i,), memory_space=pltpu.SMEM),
                pl.BlockSpec((tm,) + tile, lambda i, pe, hp: (i, 0, 0))]
    args = [pad_end, has_pad, dest_flat, h2_tiles]
    scratch = [pltpu.SemaphoreType.DMA(())]
    aliases = {}
    if zero_init:
        scratch = [pltpu.VMEM((SLOT_BLOCK,) + tile, F32), pltpu.SemaphoreType.DMA(()), pltpu.SemaphoreType.DMA(())]
    else:
        in_specs.append(pl.BlockSpec(memory_space=pl.ANY))
        args.append(xs)
        aliases = {4: 0}
    grid_spec = pltpu.PrefetchScalarGridSpec(
        num_scalar_prefetch=2,
        grid=(n // tm,),
        in_specs=in_specs,
        out_specs=pl.BlockSpec(memory_space=pl.ANY),
        scratch_shapes=scratch,
    )
    return pl.pallas_call(
        kern,
        grid_spec=grid_spec,
        out_shape=jax.ShapeDtypeStruct((n_slots,) + tile, F32),
        input_output_aliases=aliases,
        name="dispatch",
        compiler_params=_params("arbitrary"),
    )(*args)


def _expert_kernel(be_ref, nu_ref, xs_ref, wgu_ref, bgu_ref, wdn_ref, bdn_ref, ys_ref):
    @pl.when(pl.program_id(0) < nu_ref[0])
    def _():
        f = wdn_ref.shape[1]
        x = _load_row_tiles(xs_ref, SLOT_BLOCK).astype(BF16)
        gu = jnp.dot(x, wgu_ref[0], preferred_element_type=F32) + bgu_ref[0]
        gate = jnp.minimum(gu[:, :f], SWIGLU_LIMIT)
        up = jnp.clip(gu[:, f:], -SWIGLU_LIMIT, SWIGLU_LIMIT)
        glu = gate / (1.0 + jnp.exp(-SWIGLU_ALPHA * gate))
        act = ((up + 1.0) * glu).astype(BF16)
        _store_row_tiles(ys_ref, jnp.dot(act, wdn_ref[0], preferred_element_type=F32) + bdn_ref[0])


def _experts(block_expert, n_used, xs, wgu, bgu, wdn, bdn):
    n_slots = xs.shape[0] // SUBLANES
    nb = n_slots // SLOT_BLOCK
    f, d = wdn.shape[1:]
    rows = SLOT_BLOCK * SUBLANES

    def blk(i, be, nu):
        return (jnp.minimum(i, nu[0] - 1), 0)

    def wsel(i, be, nu):
        return (be[i], 0, 0)

    grid_spec = pltpu.PrefetchScalarGridSpec(
        num_scalar_prefetch=2,
        grid=(nb,),
        in_specs=[pl.BlockSpec((rows, LANES), blk),
                  pl.BlockSpec((1, d, 2 * f), wsel),
                  pl.BlockSpec((1, 1, 2 * f), wsel),
                  pl.BlockSpec((1, f, d), wsel),
                  pl.BlockSpec((1, 1, d), wsel)],
        out_specs=pl.BlockSpec((rows, LANES), blk),
    )
    return pl.pallas_call(
        _expert_kernel,
        grid_spec=grid_spec,
        out_shape=jax.ShapeDtypeStruct(xs.shape, F32),
        name="experts",
        compiler_params=_params("arbitrary"),
    )(block_expert, n_used, xs, wgu, bgu, wdn, bdn)


def _combine_kernel(dest_ref, x1_ref, gate_ref, mod_ref, modf_ref, gfin_ref, ys_ref, y_ref, buf, sem, *, tm):
    def row_copy(kq, r, slot):
        off = pl.multiple_of((kq * tm + r) * SUBLANES, SUBLANES)
        return pltpu.make_async_copy(ys_ref.at[slot], buf.at[pl.ds(off, SUBLANES), :], sem)

    def issue(i, carry):
        for j in range(ROW_UNROLL):
            for kq in range(TOP_K):
                row_copy(kq, i * ROW_UNROLL + j, dest_ref[(i * ROW_UNROLL + j) * TOP_K + kq]).start(priority=kq % 2)
        return carry

    lax.fori_loop(0, tm // ROW_UNROLL, issue, 0)

    def drain(i, carry):
        for _ in range(ROW_UNROLL * TOP_K):
            row_copy(0, 0, 0).wait()
        return carry

    lax.fori_loop(0, tm // ROW_UNROLL, drain, 0)
    gates = gate_ref[0]
    moe = None
    for kq in range(TOP_K):
        rows = _load_row_tiles(buf.at[pl.ds(kq * tm * SUBLANES, tm * SUBLANES), :], tm)
        term = gates[:, kq:kq + 1] * rows
        moe = term if moe is None else moe + term
    xo = x1_ref[0] + mod_ref[0][5:6] * moe
    modf = modf_ref[0]
    y_ref[0] = _rms(xo) * gfin_ref[...] * (1.0 + modf[1:2]) + modf[0:1]


def _combine(dest_flat, x1, gates, mod, modf, g_final, ys, tm):
    b, t, d = x1.shape
    nt = t // tm
    kern = functools.partial(_combine_kernel, tm=tm)
    return pl.pallas_call(
        kern,
        grid=(b, nt),
        in_specs=[pl.BlockSpec((tm * TOP_K,), lambda i, j: (i * nt + j,), memory_space=pltpu.SMEM),
                  pl.BlockSpec((1, tm, d), lambda i, j: (i, j, 0)),
                  pl.BlockSpec((1, tm, LANES), lambda i, j: (i, j, 0)),
                  pl.BlockSpec((1, 6, d), lambda i, j: (i, 0, 0)),
                  pl.BlockSpec((1, 2, d), lambda i, j: (i, 0, 0)),
                  pl.BlockSpec((1, d), lambda i, j: (0, 0)),
                  pl.BlockSpec(memory_space=pl.ANY)],
        out_specs=pl.BlockSpec((1, tm, d), lambda i, j: (i, j, 0)),
        out_shape=jax.ShapeDtypeStruct((b, t, d), F32),
        scratch_shapes=[pltpu.VMEM((TOP_K * tm * SUBLANES, LANES), F32), pltpu.SemaphoreType.DMA(())],
        name="combine",
        compiler_params=_params("arbitrary", "arbitrary"),
    )(dest_flat, x1, gates, mod, modf, g_final, ys)


def _rope_tables(pos):
    half = A_HD // 2
    inv_freq = ROPE_THETA ** (-jnp.arange(half, dtype=F32) / half)
    ang = pos.astype(F32)[:, None] * inv_freq[None, :]
    cos = jnp.cos(ang)
    sin = jnp.sin(ang)
    reps = LANES // A_HD
    return jnp.tile(jnp.concatenate([cos, cos], axis=1), (1, reps)), jnp.tile(jnp.concatenate([-sin, sin], axis=1), (1, reps))


def _reorder_w_in(w_in):
    offs = [0]
    for w in (512, 128, 128, 256, 64, 4, 256, 256, 512, 16, 512):
        offs.append(offs[-1] + w)
    seg = [w_in[:, offs[i]:offs[i + 1]] for i in range(11)]
    qa, ka, va, qi, ki, wi, qg, kg, vg, lr, rg = seg
    pad = jnp.zeros((w_in.shape[0], LANES - IDX_DIM - IDX_HEADS - GATE_RANK), w_in.dtype)
    return jnp.concatenate([qa, ka, va, qi, ki, wi, lr, pad, qg, kg, vg, rg], axis=1).astype(BF16)


def _pad_keys(a, n_pad):
    return jnp.pad(a, ((0, 0), (0, n_pad - a.shape[1]), (0, 0)))


def kernel(x_prompt, x_sample, cache_k, cache_v, cache_kidx, state_gla, c_prompt, c_sample,
           w_mod, b_mod, g_mix, g_ffn, w_in, gla_w_gate, gla_b_gate, gla_g_out, w_out,
           w_router, b_router, w_gate_up, b_gate_up, w_down, b_down,
           w_mod_final, b_mod_final, g_final):
    depth = w_in.shape[0]
    assert depth == 1
    bp, sp, d = x_prompt.shape
    bs, ts, _ = x_sample.shape
    past = cache_k.shape[2]
    n_exp = w_router.shape[2]
    f = w_down.shape[2]

    c_all = jnp.concatenate([c_prompt, c_sample], axis=0)
    w_r = _reorder_w_in(w_in[0])
    wg_pad = jnp.zeros((LANES, B_HEADS * B_DK), F32).at[MISC_LR:MISC_LR + GATE_RANK].set(gla_w_gate[0])
    bg = gla_b_gate[0].reshape(1, -1)
    wr_pad = jnp.zeros((d, LANES), F32).at[:, :n_exp].set(w_router[0])
    wr_hi = wr_pad.astype(BF16)
    wr_cat = jnp.concatenate([wr_hi, (wr_pad - wr_hi.astype(F32)).astype(BF16)], axis=1)
    br_pad = jnp.full((1, LANES), NEG_BIG, F32).at[0, :n_exp].set(b_router[0])
    w_out_b = w_out[0].astype(BF16)
    wgu_b = w_gate_up[0].astype(BF16)
    wdn_b = w_down[0].astype(BF16)
    bgu = b_gate_up[0].reshape(n_exp, 1, 2 * f)
    bdn = b_down[0].reshape(n_exp, 1, d)
    g_out = gla_g_out[0].reshape(1, B_DV)

    mod_all = _adaln(c_all, w_mod[0], b_mod[0]).reshape(bp + bs, 6, d)
    modf_all = _adaln(c_all, w_mod_final, b_mod_final).reshape(bp + bs, 2, d)
    mod_p, mod_s = mod_all[:bp], mod_all[bp:]
    modf_p, modf_s = modf_all[:bp], modf_all[bp:]

    def mixer(x, mod, pos, tm, past_kv, state_t, cs, nc, tq):
        cos_t, sin_t = _rope_tables(pos)
        qa, ka, va, qi, ki, misc, qg, kg, vg, gg, rg, kb, vx, kib = _premix(
            x, mod, g_mix[0].reshape(1, d), w_r, wg_pad, bg, cos_t, sin_t, tm)
        if past_kv is None:
            oa = _attn_causal(qa, qi, misc, kb, vx, kib, tq)
        else:
            k_past, v_past, ki_past = past_kv
            kb = jnp.concatenate([k_past.astype(BF16), kb], axis=1)
            vx = jnp.concatenate([_value_with_ones(v_past.astype(BF16)), vx], axis=1)
            kib = jnp.concatenate([ki_past.astype(BF16), kib], axis=1)
            n_keys = kb.shape[1]
            n_pad = -(-n_keys // LANES) * LANES
            if n_pad != n_keys:
                kb, vx, kib = _pad_keys(kb, n_pad), _pad_keys(vx, n_pad), _pad_keys(kib, n_pad)
            assert tq == x.shape[1]
            oa = _attn_call(qa, qi, misc, kb, vx, kib, tq=tq, q_off=0, n_tiles=1,
                            n_ctx=n_pad, causal=False, n_keys=n_keys, n_groups=1)[None]
        ob, st = _gla(qg, kg, vg, gg, rg, state_t, g_out, cs=cs, nc=nc)
        return oa, ob, (ka, va, ki, st)

    state0_p = jnp.zeros((bp, B_HEADS, B_DV, B_DK), F32)
    oa_p, ob_p, (ka_p, va_p, ki_p, st_p) = mixer(
        x_prompt, mod_p, jnp.arange(sp), min(256, sp), None, state0_p, CHUNK, 4, min(256, sp))
    state0_s = jnp.swapaxes(state_gla[0], -1, -2)
    past_kv = (cache_k[0].reshape(bs, past, A_KV * A_HD), cache_v[0].reshape(bs, past, A_KV * A_HD), cache_kidx[0])
    oa_s, ob_s, (ka_s, va_s, ki_s, st_s) = mixer(
        x_sample, mod_s, past + jnp.arange(ts), ts, past_kv, state0_s, ts, 1, ts)

    cnt0 = jnp.zeros((1, LANES), F32)
    x1_p, h2_p, ri_p, gate_p, cnt1 = _merge(oa_p, ob_p, x_prompt, mod_p, w_out_b, g_ffn[0].reshape(1, d),
                                            wr_cat, br_pad, cnt0, min(256, sp))
    x1_s, h2_s, ri_s, gate_s, cnt2 = _merge(oa_s, ob_s, x_sample, mod_s, w_out_b, g_ffn[0].reshape(1, d),
                                            wr_cat, br_pad, cnt1, ts)
    counts = cnt2[0, :n_exp].astype(I32)
    padded = (counts + SLOT_BLOCK - 1) // SLOT_BLOCK * SLOT_BLOCK
    pad_end = jnp.cumsum(padded)
    pad_start = pad_end - padded
    n_asg = (bp * sp + bs * ts) * TOP_K
    nb = -(-n_asg // SLOT_BLOCK) + n_exp
    n_slots = nb * SLOT_BLOCK
    block_start = jnp.arange(nb, dtype=I32) * SLOT_BLOCK
    block_expert = jnp.minimum(jnp.sum((pad_end[None, :] <= block_start[:, None]).astype(I32), axis=1), n_exp - 1)
    n_used = (pad_end[-1:] // SLOT_BLOCK).astype(I32)
    has_pad = (padded > counts).astype(I32)
    expert_ids = jnp.arange(n_exp, dtype=I32)

    def dests(ri):
        e, rank = ri[..., :TOP_K], ri[..., TOP_K:2 * TOP_K]
        start = jnp.sum(jnp.where(e[..., None] == expert_ids, pad_start.astype(I32), 0), axis=-1)
        return (start + rank).reshape(-1)

    dest_p, dest_s = dests(ri_p), dests(ri_s)
    pad_end_i = pad_end.astype(I32)
    tile = (SUBLANES, LANES)
    xs = _dispatch(pad_end_i, has_pad, dest_p, h2_p.reshape((bp * sp,) + tile), None, n_slots, min(256, sp))
    xs = _dispatch(pad_end_i, has_pad, dest_s, h2_s.reshape((bs * ts,) + tile), xs, n_slots, min(256, bs * ts))
    ys = _experts(block_expert, n_used, xs.reshape(n_slots * SUBLANES, LANES), wgu_b, bgu, wdn_b, bdn)
    ys = ys.reshape((n_slots,) + tile)
    y_p = _combine(dest_p, x1_p, gate_p, mod_p, modf_p, g_final.reshape(1, d), ys, min(256, sp))
    y_s = _combine(dest_s, x1_s, gate_s, mod_s, modf_s, g_final.reshape(1, d), ys, ts)

    def kv(a, b, t):
        return a.reshape(1, b, t, A_KV, A_HD)

    return (y_p, y_s,
            kv(ka_p, bp, sp), kv(va_p, bp, sp), ki_p[None], jnp.swapaxes(st_p, -1, -2)[None],
            kv(ka_s, bs, ts), kv(va_s, bs, ts), ki_s[None], jnp.swapaxes(st_s, -1, -2)[None])
```

```python
import functools

import jax
import jax.numpy as jnp
from jax import lax
from jax.experimental import pallas as pl
from jax.experimental.pallas import tpu as pltpu

F32 = jnp.float32
BF16 = jnp.bfloat16
I32 = jnp.int32
HI = lax.Precision.HIGHEST

CHUNK = 64
CHUNK_SHIFT = 6
EPS = 1e-6
ROPE_THETA = 10000.0
A_HD = 64
A_HEADS = 8
A_KV = 2
IDX_HEADS = 4
IDX_DIM = 64
INDEX_TOPK = 256
B_HEADS = 4
B_DK = 64
B_DV = 128
GATE_RANK = 16
GATE_TAU = 16.0
N_EXPERTS = 32
TOP_K = 4
SWIGLU_LIMIT = 7.0
SWIGLU_ALPHA = 1.702

LANES = 128
SUBLANES = 8
ROW_UNROLL = 8
GLA_SUB = 16
GLA_BATCH = 2
GLA_EXP_CLAMP = 80.0
SLOT_BLOCK = 512
VMEM_LIMIT = 56 * 1024 * 1024
INT_MIN = -2147483648
NEG_BIG = -1e30
SEARCH_TWO_BIT_MAX_KEYS = 768

C_QA, C_KA, C_VA, C_QI, C_MISC, C_QG, C_KG, C_VG, C_RG, C_END = 0, 512, 640, 768, 1024, 1152, 1408, 1664, 2176, 2688
MISC_WI = 64
MISC_LR = 68


def _params(*sem):
    return pltpu.CompilerParams(dimension_semantics=sem, vmem_limit_bytes=VMEM_LIMIT)


def _nt(a, b):
    return lax.dot_general(a, b, (((1,), (1,)), ((), ())), preferred_element_type=F32)


def _tn(a, b):
    return lax.dot_general(a, b, (((0,), (0,)), ((), ())), preferred_element_type=F32)


def _rms(x):
    return x * lax.rsqrt(jnp.mean(x * x, axis=-1, keepdims=True) + EPS)


def _silu(x):
    return x / (1.0 + jnp.exp(-x))


def _store_row_tiles(ref, val):
    rows, width = val.shape
    assert width == SUBLANES * LANES
    for s in range(SUBLANES):
        ref[pl.ds(s, rows, stride=SUBLANES), :] = val[:, s * LANES:(s + 1) * LANES]


def _load_row_tiles(ref, rows):
    return jnp.concatenate([ref[pl.ds(s, rows, stride=SUBLANES), :] for s in range(SUBLANES)], axis=1)


def _value_with_ones(vb):
    return jnp.concatenate([vb, jnp.ones_like(vb)], axis=-1)


def _adaln_kernel(c_ref, w_ref, b_ref, o_ref):
    a = _silu(c_ref[...])
    o_ref[...] = jnp.dot(a, w_ref[...], preferred_element_type=F32, precision=HI) + b_ref[...]


def _adaln(c, w, b):
    r, d = c.shape
    n = w.shape[1]
    tn = 512
    return pl.pallas_call(
        _adaln_kernel,
        grid=(n // tn,),
        in_specs=[pl.BlockSpec((r, d), lambda j: (0, 0)),
                  pl.BlockSpec((d, tn), lambda j: (0, j)),
                  pl.BlockSpec((1, tn), lambda j: (0, j))],
        out_specs=pl.BlockSpec((r, tn), lambda j: (0, j)),
        out_shape=jax.ShapeDtypeStruct((r, n), F32),
        name="adaln",
        compiler_params=_params("arbitrary"),
    )(c, w, b.reshape(1, n))


def _premix_kernel(x_ref, mod_ref, g_ref, w_ref, wg_ref, bg_ref, cos_ref, sin_ref,
                   qa_ref, ka_ref, va_ref, qi_ref, ki_ref, misc_ref, qg_ref, kg_ref, vg_ref, gg_ref, rg_ref,
                   kb_ref, vx_ref, kib_ref):
    x = x_ref[0]
    mod = mod_ref[0]
    h = _rms(x) * g_ref[...] * (1.0 + mod[1:2]) + mod[0:1]
    proj = jnp.dot(h.astype(BF16), w_ref[...], preferred_element_type=F32)
    tm = x.shape[0]
    cos = cos_ref[...]
    sin = sin_ref[...]
    lane = lax.broadcasted_iota(I32, (tm, LANES), 1)
    lower_half = (lane & (A_HD - 1)) < (A_HD // 2)

    def rope(xc):
        rot = jnp.where(lower_half, pltpu.roll(xc, LANES - A_HD // 2, 1), pltpu.roll(xc, A_HD // 2, 1))
        return xc * cos + rot * sin

    def chunk(c0, j=0):
        return proj[:, c0 + j * LANES:c0 + (j + 1) * LANES]

    for j in range((C_KA - C_QA) // LANES):
        qa_ref[0, :, j * LANES:(j + 1) * LANES] = (rope(chunk(C_QA, j)) * (A_HD ** -0.5)).astype(BF16)
    ka = rope(chunk(C_KA))
    va = chunk(C_VA)
    ka_ref[0] = ka
    va_ref[0] = va
    kb_ref[0] = ka.astype(BF16)
    vx_ref[0] = _value_with_ones(va.astype(BF16))
    for j in range((C_MISC - C_QI) // LANES):
        qi_ref[0, :, j * LANES:(j + 1) * LANES] = rope(chunk(C_QI, j)).astype(BF16)
    m = chunk(C_MISC)
    mr = rope(m)
    ki_ref[0] = mr[:, :IDX_DIM]
    kib_ref[0] = mr[:, :IDX_DIM].astype(BF16)
    misc_ref[0] = jnp.where(lane < IDX_DIM, mr, m * (IDX_HEADS ** -0.5))
    xg = jnp.dot(m, wg_ref[...], preferred_element_type=F32, precision=HI) + bg_ref[...]
    gg_ref[0] = (jnp.minimum(xg, 0.0) - jnp.log(1.0 + jnp.exp(-jnp.abs(xg)))) * (1.0 / GATE_TAU)
    qg_ref[0] = proj[:, C_QG:C_KG] * (B_DK ** -0.5)
    kg_ref[0] = proj[:, C_KG:C_VG]
    vg_ref[0] = proj[:, C_VG:C_RG]
    rg_ref[0] = proj[:, C_RG:C_END]


def _premix(x, mod, g_mix, w_r, wg_pad, bg, cos_t, sin_t, tm):
    b, t, d = x.shape
    widths = [(512, BF16), (128, F32), (128, F32), (256, BF16), (64, F32), (128, F32),
              (256, F32), (256, F32), (512, F32), (256, F32), (512, F32),
              (128, BF16), (2 * LANES, BF16), (IDX_DIM, BF16)]
    return pl.pallas_call(
        _premix_kernel,
        grid=(b, t // tm),
        in_specs=[pl.BlockSpec((1, tm, d), lambda i, j: (i, j, 0)),
                  pl.BlockSpec((1, 6, d), lambda i, j: (i, 0, 0)),
                  pl.BlockSpec((1, d), lambda i, j: (0, 0)),
                  pl.BlockSpec((d, C_END), lambda i, j: (0, 0)),
                  pl.BlockSpec((LANES, 256), lambda i, j: (0, 0)),
                  pl.BlockSpec((1, 256), lambda i, j: (0, 0)),
                  pl.BlockSpec((tm, LANES), lambda i, j: (j, 0)),
                  pl.BlockSpec((tm, LANES), lambda i, j: (j, 0))],
        out_specs=[pl.BlockSpec((1, tm, w), lambda i, j: (i, j, 0)) for w, _ in widths],
        out_shape=[jax.ShapeDtypeStruct((b, t, w), dt) for w, dt in widths],
        name="premix",
        compiler_params=_params("arbitrary", "arbitrary"),
    )(x, mod, g_mix, w_r, wg_pad, bg, cos_t, sin_t)


def _attn_kernel(*refs, tq, n_ctx, top_k, causal, n_keys, q_off, n_groups, n_past):
    if n_past:
        q_ref, qi_ref, misc_ref, k_ref, vx_ref, ki_ref, pk_ref, pv_ref, pki_ref, o_ref = refs
        pad = n_ctx - n_past - k_ref.shape[1]

        def with_past(past, new):
            parts = [past, new] + ([jnp.zeros((pad, new.shape[1]), BF16)] if pad else [])
            return jnp.concatenate(parts, axis=0)

        k = with_past(pk_ref[0].astype(BF16), k_ref[0])
        vx = with_past(_value_with_ones(pv_ref[0].astype(BF16)), vx_ref[0])
        kib = with_past(pki_ref[0].astype(BF16), ki_ref[0])
    else:
        q_ref, qi_ref, misc_ref, k_ref, vx_ref, ki_ref, o_ref = refs
        k = k_ref[0]
        vx = vx_ref[0]
        kib = ki_ref[0]
    rows = tq // n_groups
    row0 = q_off + pl.program_id(1) * tq
    keys, kks = [], []
    for g in range(n_groups):
        rs = slice(g * rows, (g + 1) * rows)
        qi = qi_ref[0, rs, :]
        misc = misc_ref[0, rs, :]
        isc = jnp.zeros((rows, n_ctx), F32)
        for h in range(IDX_HEADS):
            s = _nt(qi[:, h * IDX_DIM:(h + 1) * IDX_DIM], kib)
            isc = isc + misc[:, MISC_WI + h:MISC_WI + h + 1] * jnp.maximum(s, 0.0)
        kpos = lax.broadcasted_iota(I32, (rows, n_ctx), 1)
        if causal:
            row = lax.broadcasted_iota(I32, (rows, 1), 0) + (row0 + g * rows)
            key_lim = (lax.shift_right_logical(row, CHUNK_SHIFT) + 1) * CHUNK
        else:
            key_lim = jnp.full((rows, 1), n_keys, I32)
        isc = jnp.where(isc == 0.0, 0.0, isc)
        bits = pltpu.bitcast(isc, I32)
        key = jnp.where(bits < 0, bits ^ 0x7FFFFFFF, bits)
        keys.append(jnp.where(kpos < key_lim, key, INT_MIN))
        kks.append(jnp.minimum(key_lim, top_k).astype(F32))

    def count_at_least(g, cand):
        return jnp.sum(jnp.where(keys[g] >= (cand ^ INT_MIN), 1.0, 0.0), axis=1, keepdims=True)

    bits_per_pass = 2 if n_ctx <= SEARCH_TWO_BIT_MAX_KEYS else 1

    def search(i, ans):
        out = []
        for g in range(n_groups):
            if bits_per_pass == 1:
                cand = ans[g] | lax.shift_left(jnp.int32(1), 31 - i)
                out.append(jnp.where(count_at_least(g, cand) >= kks[g], cand, ans[g]))
            else:
                a1 = ans[g] | lax.shift_left(jnp.int32(1), 31 - 2 * i)
                a2 = ans[g] | lax.shift_left(jnp.int32(1), 30 - 2 * i)
                a3 = a1 | a2
                c1, c2, c3 = count_at_least(g, a1), count_at_least(g, a2), count_at_least(g, a3)
                kk = kks[g]
                out.append(jnp.where(c3 >= kk, a3, jnp.where(c1 >= kk, a1, jnp.where(c2 >= kk, a2, ans[g]))))
        return tuple(out)

    ans = lax.fori_loop(0, 32 // bits_per_pass, search,
                        tuple(jnp.zeros((rows, 1), I32) for _ in range(n_groups)))
    r = lax.broadcasted_iota(I32, (LANES, LANES), 0)
    c = lax.broadcasted_iota(I32, (LANES, LANES), 1)
    upper = jnp.where(r < c, 1.0, 0.0).astype(BF16)
    bias_groups = []
    for g in range(n_groups):
        key = keys[g]
        thr = ans[g] ^ INT_MIN
        gt = key > thr
        need = kks[g] - jnp.sum(jnp.where(gt, 1.0, 0.0), axis=1, keepdims=True)
        carry = jnp.zeros((rows, 1), F32)
        blocks = []
        for j in range(n_ctx // LANES):
            sl = slice(j * LANES, (j + 1) * LANES)
            tie = jnp.where(key[:, sl] == thr, 1.0, 0.0)
            before = jnp.dot(tie.astype(BF16), upper, preferred_element_type=F32) + carry
            take_tie = jnp.where(before < need, tie, 0.0)
            blocks.append(jnp.where(gt[:, sl], 0.0, jnp.where(take_tie > 0.0, 0.0, NEG_BIG)))
            carry = carry + jnp.sum(tie, axis=1, keepdims=True)
        bias_groups.append(jnp.concatenate(blocks, axis=1))
    bias = jnp.concatenate(bias_groups, axis=0)

    q = q_ref[0]
    outs = []
    rep = A_HEADS // A_KV
    for g in range(A_KV):
        kg = k[:, g * A_HD:(g + 1) * A_HD]
        for rr in range(rep):
            hh = g * rep + rr
            s = _nt(q[:, hh * A_HD:(hh + 1) * A_HD], kg) + bias
            p = jnp.exp((s - jnp.max(s, axis=1, keepdims=True)).astype(BF16))
            ox = jnp.dot(p, vx, preferred_element_type=F32)
            outs.append(ox[:, g * A_HD:(g + 1) * A_HD] / ox[:, LANES:LANES + 1])
    o_ref[0] = jnp.concatenate(outs, axis=1).astype(BF16)


def _attn_call(q, qi, misc, kb, vx, kib, past, *, tq, q_off, n_tiles, n_ctx, causal, n_keys, n_groups):
    b = q.shape[0]
    t0 = q_off // tq
    top_k = min(INDEX_TOPK, n_keys // 4)
    n_past = 0 if past is None else past[0].shape[1]
    n_own = n_ctx if past is None else kb.shape[1]
    kern = functools.partial(_attn_kernel, tq=tq, n_ctx=n_ctx, top_k=top_k, causal=causal, n_keys=n_keys,
                             q_off=q_off, n_groups=n_groups, n_past=n_past)

    def tok(w):
        return pl.BlockSpec((1, tq, w), lambda i, j: (i, t0 + j, 0))

    def ctx(rows, w):
        return pl.BlockSpec((1, rows, w), lambda i, j: (i, 0, 0))

    in_specs = [tok(512), tok(256), tok(LANES), ctx(n_own, LANES), ctx(n_own, 2 * LANES), ctx(n_own, IDX_DIM)]
    args = [q, qi, misc, kb, vx, kib]
    if past is not None:
        in_specs += [ctx(n_past, LANES), ctx(n_past, LANES), ctx(n_past, IDX_DIM)]
        args += list(past)
    return pl.pallas_call(
        kern,
        grid=(b, n_tiles),
        in_specs=in_specs,
        out_specs=pl.BlockSpec((1, tq, 512), lambda i, j: (i, j, 0)),
        out_shape=jax.ShapeDtypeStruct((b, n_tiles * tq, 512), BF16),
        name="attn",
        compiler_params=_params("arbitrary", "arbitrary"),
    )(*args)


def _attn_causal(q, qi, misc, kb, vx, kib, tq):
    t = q.shape[1]
    return jnp.stack([
        _attn_call(q, qi, misc, kb, vx, kib, None, tq=tq, q_off=c * tq, n_tiles=1, n_ctx=(c + 1) * tq,
                   causal=True, n_keys=t, n_groups=2)
        for c in range(t // tq)])


def _gla_kernel(q_ref, k_ref, v_ref, g_ref, rg_ref, s0_ref, go_ref, ob_ref, st_ref, st_scr, *, cs, nc, bb):
    j = pl.program_id(1)

    @pl.when(j == 0)
    def _():
        st_scr[...] = s0_ref[...]

    r = lax.broadcasted_iota(I32, (cs, cs), 0)
    c = lax.broadcasted_iota(I32, (cs, cs), 1)
    causal = r >= c
    tri = jnp.where(causal, 1.0, 0.0)
    go = go_ref[...]
    for ci in range(nc):
        for bi in range(bb):
            _gla_chunk(q_ref, k_ref, v_ref, g_ref, rg_ref, ob_ref, st_scr, bi, slice(ci * cs, (ci + 1) * cs),
                       causal, tri, go, cs)

    @pl.when(j == pl.num_programs(1) - 1)
    def _():
        st_ref[...] = st_scr[...]


def _gla_chunk(q_ref, k_ref, v_ref, g_ref, rg_ref, ob_ref, st_scr, bi, sl, causal, tri, go, cs):
    q = q_ref[bi, sl, :]
    k = k_ref[bi, sl, :]
    v = v_ref[bi, sl, :]
    rg = rg_ref[bi, sl, :]
    bcum = jnp.dot(tri, g_ref[bi, sl, :], preferred_element_type=F32, precision=HI)
    blast = bcum[cs - 1:cs, :]
    qdec = (q * jnp.exp(bcum)).astype(BF16)
    kdec = (k * jnp.exp(blast - bcum)).astype(BF16)
    eb = jnp.exp(blast)
    a_rows = [[] for _ in range(B_HEADS)]
    for i in range(cs // GLA_SUB):
        rs = slice(i * GLA_SUB, (i + 1) * GLA_SUB)
        ref = bcum[i * GLA_SUB:i * GLA_SUB + 1, :]
        qt = (q[rs, :] * jnp.exp(bcum[rs, :] - ref)).astype(BF16)
        kt = (k * jnp.exp(jnp.minimum(ref - bcum, GLA_EXP_CLAMP))).astype(BF16)
        for h in range(B_HEADS):
            hs = slice(h * B_DK, (h + 1) * B_DK)
            a_rows[h].append(_nt(qt[:, hs], kt[:, hs]))
    for h in range(B_HEADS):
        hs = slice(h * B_DK, (h + 1) * B_DK)
        vs = slice(h * B_DV, (h + 1) * B_DV)
        a = jnp.where(causal, jnp.concatenate(a_rows[h], axis=0), 0.0).astype(BF16)
        vh = v[:, vs].astype(BF16)
        st = st_scr[bi, h]
        o = jnp.dot(a, vh, preferred_element_type=F32) + _nt(qdec[:, hs], st.astype(BF16))
        st_scr[bi, h] = st * eb[:, hs] + _tn(vh, kdec[:, hs])
        y = _rms(o) * go
        ob_ref[bi, sl, vs] = (y * _silu(rg[:, vs])).astype(BF16)


def _gla(qg, kg, vg, gg, rg, state_t, g_out, *, cs, nc, bb):
    b, t, _ = qg.shape
    assert b % bb == 0
    tt = cs * nc
    kern = functools.partial(_gla_kernel, cs=cs, nc=nc, bb=bb)

    def tok(w):
        return pl.BlockSpec((bb, tt, w), lambda i, j: (i, j, 0))

    st_spec = pl.BlockSpec((bb, B_HEADS, B_DV, B_DK), lambda i, j: (i, 0, 0, 0))
    return pl.pallas_call(
        kern,
        grid=(b // bb, t // tt),
        in_specs=[tok(256), tok(256), tok(512), tok(256), tok(512), st_spec,
                  pl.BlockSpec((1, B_DV), lambda i, j: (0, 0))],
        out_specs=[tok(512), st_spec],
        out_shape=[jax.ShapeDtypeStruct((b, t, 512), BF16),
                   jax.ShapeDtypeStruct((b, B_HEADS, B_DV, B_DK), F32)],
        scratch_shapes=[pltpu.VMEM((bb, B_HEADS, B_DV, B_DK), F32)],
        name="gla",
        compiler_params=_params("arbitrary", "arbitrary"),
    )(qg, kg, vg, gg, rg, state_t, g_out)


def _merge_kernel(oa_ref, ob_ref, x_ref, mod_ref, wo_ref, gf_ref, wr_ref, br_ref, cnt0_ref,
                  x1_ref, h2_ref, ri_ref, rgate_ref, cnt_ref, carry_scr):
    @pl.when((pl.program_id(0) == 0) & (pl.program_id(1) == 0))
    def _():
        carry_scr[...] = cnt0_ref[...]

    mod = mod_ref[0]
    cat = jnp.concatenate([oa_ref[0, 0], ob_ref[0]], axis=1)
    x1 = x_ref[0] + mod[2:3] * jnp.dot(cat, wo_ref[...], preferred_element_type=F32)
    x1_ref[0] = x1
    h2 = _rms(x1) * gf_ref[...] * (1.0 + mod[4:5]) + mod[3:4]
    _store_row_tiles(h2_ref.at[0], h2)
    tm = x1.shape[0]
    lane = lax.broadcasted_iota(I32, (tm, LANES), 1)
    hi = h2.astype(BF16)
    lo = (h2 - hi.astype(F32)).astype(BF16)
    wr = wr_ref[...]
    a = jnp.dot(hi, wr, preferred_element_type=F32)
    left = (a[:, :LANES] + a[:, LANES:]) + jnp.dot(lo, wr[:, :LANES], preferred_element_type=F32) + br_ref[...]
    idx, val = [], []
    for _ in range(TOP_K):
        m = jnp.max(left, axis=1, keepdims=True)
        e = jnp.min(jnp.where(left == m, lane, LANES), axis=1, keepdims=True)
        idx.append(e)
        val.append(m)
        left = jnp.where(lane == e, -jnp.inf, left)
    ex = [jnp.exp(vv - val[0]) for vv in val]
    den = ex[0] + ex[1] + ex[2] + ex[3]
    onehot = jnp.zeros((tm, LANES), F32)
    for e in idx:
        onehot = onehot + jnp.where(lane == e, 1.0, 0.0)
    r = lax.broadcasted_iota(I32, (tm, tm), 0)
    c = lax.broadcasted_iota(I32, (tm, tm), 1)
    earlier = jnp.where(r > c, 1.0, 0.0).astype(BF16)
    before = jnp.dot(earlier, onehot.astype(BF16), preferred_element_type=F32) + carry_scr[...]
    ri = jnp.zeros((tm, LANES), I32)
    rgate = jnp.zeros((tm, LANES), F32)
    for kq in range(TOP_K):
        rank = jnp.sum(jnp.where(lane == idx[kq], before, 0.0), axis=1, keepdims=True).astype(I32)
        ri = jnp.where(lane == kq, idx[kq], ri)
        ri = jnp.where(lane == TOP_K + kq, rank, ri)
        rgate = jnp.where(lane == kq, ex[kq] / den, rgate)
    ri_ref[0] = ri
    rgate_ref[0] = rgate
    carry_scr[...] = carry_scr[...] + jnp.sum(onehot, axis=0, keepdims=True)
    cnt_ref[...] = carry_scr[...]


def _merge(oa, ob, x, mod, w_out, g_ffn, wr_pad, br_pad, cnt0, tm):
    b, t, d = x.shape

    def tok(w):
        return pl.BlockSpec((1, tm, w), lambda i, j: (i, j, 0))

    def const(s):
        return pl.BlockSpec(s, lambda i, j: (0, 0))

    return pl.pallas_call(
        _merge_kernel,
        grid=(b, t // tm),
        in_specs=[pl.BlockSpec((1, 1, tm, 512), lambda i, j: (j, i, 0, 0)),
                  tok(512), tok(d), pl.BlockSpec((1, 6, d), lambda i, j: (i, 0, 0)),
                  const((d, d)), const((1, d)), const((d, 2 * LANES)), const((1, LANES)), const((1, LANES))],
        out_specs=[tok(d), pl.BlockSpec((1, tm * SUBLANES, LANES), lambda i, j: (i, j, 0)),
                   tok(LANES), tok(LANES), const((1, LANES))],
        out_shape=[jax.ShapeDtypeStruct((b, t, d), F32), jax.ShapeDtypeStruct((b, t * SUBLANES, LANES), F32),
                   jax.ShapeDtypeStruct((b, t, LANES), I32), jax.ShapeDtypeStruct((b, t, LANES), F32),
                   jax.ShapeDtypeStruct((1, LANES), F32)],
        scratch_shapes=[pltpu.VMEM((1, LANES), F32)],
        name="merge",
        compiler_params=_params("arbitrary", "arbitrary"),
    )(oa, ob, x, mod, w_out, g_ffn, wr_pad, br_pad, cnt0)


def _dispatch_kernel(pe_ref, hp_ref, dest_ref, h_ref, *rest, tm, zero_init):
    if zero_init:
        xs_ref, zbuf, sem, zsem = rest

        @pl.when(pl.program_id(0) == 0)
        def _():
            zbuf[...] = jnp.zeros_like(zbuf)

            def zero_copy(e):
                start = pl.multiple_of(pe_ref[e] - SLOT_BLOCK, SLOT_BLOCK)
                return pltpu.make_async_copy(zbuf, xs_ref.at[pl.ds(start, SLOT_BLOCK)], zsem)

            for e in range(N_EXPERTS):
                @pl.when(hp_ref[e] > 0)
                def _():
                    zero_copy(e).start()
            for e in range(N_EXPERTS):
                @pl.when(hp_ref[e] > 0)
                def _():
                    zero_copy(e).wait()
    else:
        _, xs_ref, sem = rest

    def row_copy(r, slot):
        return pltpu.make_async_copy(h_ref.at[r], xs_ref.at[slot], sem)

    def issue(i, carry):
        for j in range(ROW_UNROLL):
            for kq in range(TOP_K):
                row_copy(i * ROW_UNROLL + j, dest_ref[(i * ROW_UNROLL + j) * TOP_K + kq]).start(priority=kq % 2)
        return carry

    lax.fori_loop(0, tm // ROW_UNROLL, issue, 0)

    def drain(i, carry):
        for _ in range(ROW_UNROLL * TOP_K):
            row_copy(0, 0).wait()
        return carry

    lax.fori_loop(0, tm // ROW_UNROLL, drain, 0)


def _dispatch(pad_end, has_pad, dest_flat, h2_tiles, xs, n_slots, tm):
    n = h2_tiles.shape[0]
    tile = h2_tiles.shape[1:]
    zero_init = xs is None
    kern = functools.partial(_dispatch_kernel, tm=tm, zero_init=zero_init)
    in_specs = [pl.BlockSpec((tm * TOP_K,), lambda i, pe, hp: (i,), memory_space=pltpu.SMEM),
                pl.BlockSpec((tm,) + tile, lambda i, pe, hp: (i, 0, 0))]
    args = [pad_end, has_pad, dest_flat, h2_tiles]
    scratch = [pltpu.SemaphoreType.DMA(())]
    aliases = {}
    if zero_init:
        scratch = [pltpu.VMEM((SLOT_BLOCK,) + tile, F32), pltpu.SemaphoreType.DMA(()), pltpu.SemaphoreType.DMA(())]
    else:
        in_specs.append(pl.BlockSpec(memory_space=pl.ANY))
        args.append(xs)
        aliases = {4: 0}
    grid_spec = pltpu.PrefetchScalarGridSpec(
        num_scalar_prefetch=2,
        grid=(n // tm,),
        in_specs=in_specs,
        out_specs=pl.BlockSpec(memory_space=pl.ANY),
        scratch_shapes=scratch,
    )
    return pl.pallas_call(
        kern,
        grid_spec=grid_spec,
        out_shape=jax.ShapeDtypeStruct((n_slots,) + tile, F32),
        input_output_aliases=aliases,
        name="dispatch",
        compiler_params=_params("arbitrary"),
    )(*args)


def _expert_kernel(be_ref, nu_ref, xs_ref, wgu_ref, bgu_ref, wdn_ref, bdn_ref, ys_ref):
    @pl.when(pl.program_id(0) < nu_ref[0])
    def _():
        f = wdn_ref.shape[1]
        x = _load_row_tiles(xs_ref, SLOT_BLOCK).astype(BF16)
        gu = jnp.dot(x, wgu_ref[0], preferred_element_type=F32) + bgu_ref[0]
        gate = jnp.minimum(gu[:, :f], SWIGLU_LIMIT)
        up = jnp.clip(gu[:, f:], -SWIGLU_LIMIT, SWIGLU_LIMIT)
        glu = gate / (1.0 + jnp.exp(-SWIGLU_ALPHA * gate))
        act = ((up + 1.0) * glu).astype(BF16)
        _store_row_tiles(ys_ref, jnp.dot(act, wdn_ref[0], preferred_element_type=F32) + bdn_ref[0])


def _experts(block_expert, n_used, xs, wgu, bgu, wdn, bdn):
    n_slots = xs.shape[0] // SUBLANES
    nb = n_slots // SLOT_BLOCK
    f, d = wdn.shape[1:]
    rows = SLOT_BLOCK * SUBLANES

    def blk(i, be, nu):
        return (jnp.minimum(i, nu[0] - 1), 0)

    def wsel(i, be, nu):
        return (be[i], 0, 0)

    grid_spec = pltpu.PrefetchScalarGridSpec(
        num_scalar_prefetch=2,
        grid=(nb,),
        in_specs=[pl.BlockSpec((rows, LANES), blk),
                  pl.BlockSpec((1, d, 2 * f), wsel),
                  pl.BlockSpec((1, 1, 2 * f), wsel),
                  pl.BlockSpec((1, f, d), wsel),
                  pl.BlockSpec((1, 1, d), wsel)],
        out_specs=pl.BlockSpec((rows, LANES), blk),
    )
    return pl.pallas_call(
        _expert_kernel,
        grid_spec=grid_spec,
        out_shape=jax.ShapeDtypeStruct(xs.shape, F32),
        name="experts",
        compiler_params=_params("arbitrary"),
    )(block_expert, n_used, xs, wgu, bgu, wdn, bdn)


def _combine_kernel(dest_ref, dnext_ref, x1_ref, gate_ref, mod_ref, modf_ref, gfin_ref, ys_ref, y_ref,
                    buf, sem, *, tm):
    nt = pl.num_programs(1)
    step = pl.program_id(0) * nt + pl.program_id(1)
    n_steps = pl.num_programs(0) * nt
    half = TOP_K * tm * SUBLANES
    cur = lax.rem(step, 2)

    def row_copy(which, kq, r, slot):
        off = pl.multiple_of(which * half + (kq * tm + r) * SUBLANES, SUBLANES)
        return pltpu.make_async_copy(ys_ref.at[slot], buf.at[pl.ds(off, SUBLANES), :], sem.at[which])

    def gather(slots_ref, which):
        def issue(i, carry):
            for j in range(ROW_UNROLL):
                for kq in range(TOP_K):
                    r = i * ROW_UNROLL + j
                    row_copy(which, kq, r, slots_ref[r * TOP_K + kq]).start(priority=kq % 2)
            return carry

        lax.fori_loop(0, tm // ROW_UNROLL, issue, 0)

    @pl.when(step == 0)
    def _():
        gather(dest_ref, cur)

    @pl.when(step + 1 < n_steps)
    def _():
        gather(dnext_ref, 1 - cur)

    def drain(i, carry):
        for _ in range(ROW_UNROLL * TOP_K):
            row_copy(cur, 0, 0, 0).wait()
        return carry

    lax.fori_loop(0, tm // ROW_UNROLL, drain, 0)
    gates = gate_ref[0]
    moe = None
    for kq in range(TOP_K):
        start = pl.multiple_of(cur * half + kq * tm * SUBLANES, SUBLANES)
        rows = _load_row_tiles(buf.at[pl.ds(start, tm * SUBLANES), :], tm)
        term = gates[:, kq:kq + 1] * rows
        moe = term if moe is None else moe + term
    xo = x1_ref[0] + mod_ref[0][5:6] * moe
    modf = modf_ref[0]
    y_ref[0] = _rms(xo) * gfin_ref[...] * (1.0 + modf[1:2]) + modf[0:1]


def _combine(dest_flat, x1, gates, mod, modf, g_final, ys, tm):
    b, t, d = x1.shape
    nt = t // tm
    last = b * nt - 1
    kern = functools.partial(_combine_kernel, tm=tm)
    return pl.pallas_call(
        kern,
        grid=(b, nt),
        in_specs=[pl.BlockSpec((tm * TOP_K,), lambda i, j: (i * nt + j,), memory_space=pltpu.SMEM),
                  pl.BlockSpec((tm * TOP_K,), lambda i, j: (jnp.minimum(i * nt + j + 1, last),),
                               memory_space=pltpu.SMEM),
                  pl.BlockSpec((1, tm, d), lambda i, j: (i, j, 0)),
                  pl.BlockSpec((1, tm, LANES), lambda i, j: (i, j, 0)),
                  pl.BlockSpec((1, 6, d), lambda i, j: (i, 0, 0)),
                  pl.BlockSpec((1, 2, d), lambda i, j: (i, 0, 0)),
                  pl.BlockSpec((1, d), lambda i, j: (0, 0)),
                  pl.BlockSpec(memory_space=pl.ANY)],
        out_specs=pl.BlockSpec((1, tm, d), lambda i, j: (i, j, 0)),
        out_shape=jax.ShapeDtypeStruct((b, t, d), F32),
        scratch_shapes=[pltpu.VMEM((2 * TOP_K * tm * SUBLANES, LANES), F32), pltpu.SemaphoreType.DMA((2,))],
        name="combine",
        compiler_params=_params("arbitrary", "arbitrary"),
    )(dest_flat, dest_flat, x1, gates, mod, modf, g_final, ys)


def _rope_tables(pos):
    half = A_HD // 2
    inv_freq = ROPE_THETA ** (-jnp.arange(half, dtype=F32) / half)
    ang = pos.astype(F32)[:, None] * inv_freq[None, :]
    cos = jnp.cos(ang)
    sin = jnp.sin(ang)
    reps = LANES // A_HD
    return jnp.tile(jnp.concatenate([cos, cos], axis=1), (1, reps)), jnp.tile(jnp.concatenate([-sin, sin], axis=1), (1, reps))


def _reorder_w_in(w_in):
    offs = [0]
    for w in (512, 128, 128, 256, 64, 4, 256, 256, 512, 16, 512):
        offs.append(offs[-1] + w)
    seg = [w_in[:, offs[i]:offs[i + 1]] for i in range(11)]
    qa, ka, va, qi, ki, wi, qg, kg, vg, lr, rg = seg
    pad = jnp.zeros((w_in.shape[0], LANES - IDX_DIM - IDX_HEADS - GATE_RANK), w_in.dtype)
    return jnp.concatenate([qa, ka, va, qi, ki, wi, lr, pad, qg, kg, vg, rg], axis=1).astype(BF16)


def kernel(x_prompt, x_sample, cache_k, cache_v, cache_kidx, state_gla, c_prompt, c_sample,
           w_mod, b_mod, g_mix, g_ffn, w_in, gla_w_gate, gla_b_gate, gla_g_out, w_out,
           w_router, b_router, w_gate_up, b_gate_up, w_down, b_down,
           w_mod_final, b_mod_final, g_final):
    depth = w_in.shape[0]
    assert depth == 1
    bp, sp, d = x_prompt.shape
    bs, ts, _ = x_sample.shape
    past = cache_k.shape[2]
    n_exp = w_router.shape[2]
    f = w_down.shape[2]

    c_all = jnp.concatenate([c_prompt, c_sample], axis=0)
    w_r = _reorder_w_in(w_in[0])
    wg_pad = jnp.zeros((LANES, B_HEADS * B_DK), F32).at[MISC_LR:MISC_LR + GATE_RANK].set(gla_w_gate[0])
    bg = gla_b_gate[0].reshape(1, -1)
    wr_pad = jnp.zeros((d, LANES), F32).at[:, :n_exp].set(w_router[0])
    wr_hi = wr_pad.astype(BF16)
    wr_cat = jnp.concatenate([wr_hi, (wr_pad - wr_hi.astype(F32)).astype(BF16)], axis=1)
    br_pad = jnp.full((1, LANES), NEG_BIG, F32).at[0, :n_exp].set(b_router[0])
    w_out_b = w_out[0].astype(BF16)
    wgu_b = w_gate_up[0].astype(BF16)
    wdn_b = w_down[0].astype(BF16)
    bgu = b_gate_up[0].reshape(n_exp, 1, 2 * f)
    bdn = b_down[0].reshape(n_exp, 1, d)
    g_out = gla_g_out[0].reshape(1, B_DV)

    mod_all = _adaln(c_all, w_mod[0], b_mod[0]).reshape(bp + bs, 6, d)
    modf_all = _adaln(c_all, w_mod_final, b_mod_final).reshape(bp + bs, 2, d)
    mod_p, mod_s = mod_all[:bp], mod_all[bp:]
    modf_p, modf_s = modf_all[:bp], modf_all[bp:]

    def mixer(x, mod, pos, tm, past_kv, state_t, cs, nc, tq):
        cos_t, sin_t = _rope_tables(pos)
        qa, ka, va, qi, ki, misc, qg, kg, vg, gg, rg, kb, vx, kib = _premix(
            x, mod, g_mix[0].reshape(1, d), w_r, wg_pad, bg, cos_t, sin_t, tm)
        if past_kv is None:
            oa = _attn_causal(qa, qi, misc, kb, vx, kib, tq)
        else:
            n_keys = past_kv[0].shape[1] + kb.shape[1]
            assert tq == x.shape[1]
            oa = _attn_call(qa, qi, misc, kb, vx, kib, past_kv, tq=tq, q_off=0, n_tiles=1,
                            n_ctx=-(-n_keys // LANES) * LANES, causal=False, n_keys=n_keys, n_groups=1)[None]
        ob, st = _gla(qg, kg, vg, gg, rg, state_t, g_out, cs=cs, nc=nc, bb=GLA_BATCH)
        return oa, ob, (ka, va, ki, st)

    state0_p = jnp.zeros((bp, B_HEADS, B_DV, B_DK), F32)
    oa_p, ob_p, (ka_p, va_p, ki_p, st_p) = mixer(
        x_prompt, mod_p, jnp.arange(sp), min(256, sp), None, state0_p, CHUNK, 4, min(256, sp))
    state0_s = jnp.swapaxes(state_gla[0], -1, -2)
    past_kv = (cache_k[0].reshape(bs, past, A_KV * A_HD), cache_v[0].reshape(bs, past, A_KV * A_HD), cache_kidx[0])
    oa_s, ob_s, (ka_s, va_s, ki_s, st_s) = mixer(
        x_sample, mod_s, past + jnp.arange(ts), ts, past_kv, state0_s, ts, 1, ts)

    cnt0 = jnp.zeros((1, LANES), F32)
    x1_p, h2_p, ri_p, gate_p, cnt1 = _merge(oa_p, ob_p, x_prompt, mod_p, w_out_b, g_ffn[0].reshape(1, d),
                                            wr_cat, br_pad, cnt0, min(256, sp))
    x1_s, h2_s, ri_s, gate_s, cnt2 = _merge(oa_s, ob_s, x_sample, mod_s, w_out_b, g_ffn[0].reshape(1, d),
                                            wr_cat, br_pad, cnt1, ts)
    counts = cnt2[0, :n_exp].astype(I32)
    padded = (counts + SLOT_BLOCK - 1) // SLOT_BLOCK * SLOT_BLOCK
    pad_end = jnp.cumsum(padded)
    pad_start = pad_end - padded
    n_asg = (bp * sp + bs * ts) * TOP_K
    nb = -(-n_asg // SLOT_BLOCK) + n_exp
    n_slots = nb * SLOT_BLOCK
    block_start = jnp.arange(nb, dtype=I32) * SLOT_BLOCK
    block_expert = jnp.minimum(jnp.sum((pad_end[None, :] <= block_start[:, None]).astype(I32), axis=1), n_exp - 1)
    n_used = (pad_end[-1:] // SLOT_BLOCK).astype(I32)
    has_pad = (padded > counts).astype(I32)
    expert_ids = jnp.arange(n_exp, dtype=I32)

    def dests(ri):
        e, rank = ri[..., :TOP_K], ri[..., TOP_K:2 * TOP_K]
        start = jnp.sum(jnp.where(e[..., None] == expert_ids, pad_start.astype(I32), 0), axis=-1)
        return (start + rank).reshape(-1)

    dest_p, dest_s = dests(ri_p), dests(ri_s)
    pad_end_i = pad_end.astype(I32)
    tile = (SUBLANES, LANES)
    xs = _dispatch(pad_end_i, has_pad, dest_p, h2_p.reshape((bp * sp,) + tile), None, n_slots, min(256, sp))
    xs = _dispatch(pad_end_i, has_pad, dest_s, h2_s.reshape((bs * ts,) + tile), xs, n_slots, min(256, bs * ts))
    ys = _experts(block_expert, n_used, xs.reshape(n_slots * SUBLANES, LANES), wgu_b, bgu, wdn_b, bdn)
    ys = ys.reshape((n_slots,) + tile)
    y_p = _combine(dest_p, x1_p, gate_p, mod_p, modf_p, g_final.reshape(1, d), ys, min(256, sp))
    y_s = _combine(dest_s, x1_s, gate_s, mod_s, modf_s, g_final.reshape(1, d), ys, ts)

    def kv(a, b, t):
        return a.reshape(1, b, t, A_KV, A_HD)

    return (y_p, y_s,
            kv(ka_p, bp, sp), kv(va_p, bp, sp), ki_p[None], jnp.swapaxes(st_p, -1, -2)[None],
            kv(ka_s, bs, ts), kv(va_s, bs, ts), ki_s[None], jnp.swapaxes(st_s, -1, -2)[None])
```

```python
import functools

import jax
import jax.numpy as jnp
from jax import lax
from jax.experimental import pallas as pl
from jax.experimental.pallas import tpu as pltpu

F32 = jnp.float32
BF16 = jnp.bfloat16
I32 = jnp.int32
HI = lax.Precision.HIGHEST

CHUNK = 64
CHUNK_SHIFT = 6
EPS = 1e-6
ROPE_THETA = 10000.0
A_HD = 64
A_HEADS = 8
A_KV = 2
IDX_HEADS = 4
IDX_DIM = 64
INDEX_TOPK = 256
B_HEADS = 4
B_DK = 64
B_DV = 128
GATE_RANK = 16
GATE_TAU = 16.0
N_EXPERTS = 32
TOP_K = 4
SWIGLU_LIMIT = 7.0
SWIGLU_ALPHA = 1.702

LANES = 128
SUBLANES = 8
ROW_UNROLL = 8
GLA_SUB = 16
GLA_BATCH = 2
GLA_EXP_CLAMP = 80.0
SLOT_BLOCK = 512
VMEM_LIMIT = 56 * 1024 * 1024
INT_MIN = -2147483648
NEG_BIG = -1e30
SEARCH_TWO_BIT_MAX_KEYS = 768

C_QA, C_KA, C_VA, C_QI, C_MISC, C_QG, C_KG, C_VG, C_RG, C_END = 0, 512, 640, 768, 1024, 1152, 1408, 1664, 2176, 2688
MISC_WI = 64
MISC_LR = 68


def _params(*sem):
    return pltpu.CompilerParams(dimension_semantics=sem, vmem_limit_bytes=VMEM_LIMIT)


def _nt(a, b):
    return lax.dot_general(a, b, (((1,), (1,)), ((), ())), preferred_element_type=F32)


def _tn(a, b):
    return lax.dot_general(a, b, (((0,), (0,)), ((), ())), preferred_element_type=F32)


def _rms(x):
    return x * lax.rsqrt(jnp.mean(x * x, axis=-1, keepdims=True) + EPS)


def _silu(x):
    return x / (1.0 + jnp.exp(-x))


def _store_row_tiles(ref, val):
    rows, width = val.shape
    assert width == SUBLANES * LANES
    for s in range(SUBLANES):
        ref[pl.ds(s, rows, stride=SUBLANES), :] = val[:, s * LANES:(s + 1) * LANES]


def _load_row_tiles(ref, rows):
    return jnp.concatenate([ref[pl.ds(s, rows, stride=SUBLANES), :] for s in range(SUBLANES)], axis=1)


def _value_with_ones(vb):
    return jnp.concatenate([vb, jnp.ones_like(vb)], axis=-1)


def _adaln_kernel(c_ref, w_ref, b_ref, o_ref):
    a = _silu(c_ref[...])
    o_ref[...] = jnp.dot(a, w_ref[...], preferred_element_type=F32, precision=HI) + b_ref[...]


def _adaln(c, w, b):
    r, d = c.shape
    n = w.shape[1]
    tn = 512
    return pl.pallas_call(
        _adaln_kernel,
        grid=(n // tn,),
        in_specs=[pl.BlockSpec((r, d), lambda j: (0, 0)),
                  pl.BlockSpec((d, tn), lambda j: (0, j)),
                  pl.BlockSpec((1, tn), lambda j: (0, j))],
        out_specs=pl.BlockSpec((r, tn), lambda j: (0, j)),
        out_shape=jax.ShapeDtypeStruct((r, n), F32),
        name="adaln",
        compiler_params=_params("arbitrary"),
    )(c, w, b.reshape(1, n))


def _premix_kernel(x_ref, mod_ref, g_ref, w_ref, wg_ref, bg_ref, cos_ref, sin_ref,
                   qa_ref, ka_ref, va_ref, qi_ref, ki_ref, misc_ref, qg_ref, kg_ref, vg_ref, gg_ref, rg_ref,
                   kb_ref, vx_ref, kib_ref):
    x = x_ref[0]
    mod = mod_ref[0]
    h = _rms(x) * g_ref[...] * (1.0 + mod[1:2]) + mod[0:1]
    proj = jnp.dot(h.astype(BF16), w_ref[...], preferred_element_type=F32)
    tm = x.shape[0]
    cos = cos_ref[...]
    sin = sin_ref[...]
    lane = lax.broadcasted_iota(I32, (tm, LANES), 1)
    lower_half = (lane & (A_HD - 1)) < (A_HD // 2)

    def rope(xc):
        rot = jnp.where(lower_half, pltpu.roll(xc, LANES - A_HD // 2, 1), pltpu.roll(xc, A_HD // 2, 1))
        return xc * cos + rot * sin

    def chunk(c0, j=0):
        return proj[:, c0 + j * LANES:c0 + (j + 1) * LANES]

    for j in range((C_KA - C_QA) // LANES):
        qa_ref[0, :, j * LANES:(j + 1) * LANES] = (rope(chunk(C_QA, j)) * (A_HD ** -0.5)).astype(BF16)
    ka = rope(chunk(C_KA))
    va = chunk(C_VA)
    ka_ref[0] = ka
    va_ref[0] = va
    kb_ref[0] = ka.astype(BF16)
    vx_ref[0] = _value_with_ones(va.astype(BF16))
    for j in range((C_MISC - C_QI) // LANES):
        qi_ref[0, :, j * LANES:(j + 1) * LANES] = rope(chunk(C_QI, j)).astype(BF16)
    m = chunk(C_MISC)
    mr = rope(m)
    ki_ref[0] = mr[:, :IDX_DIM]
    kib_ref[0] = mr[:, :IDX_DIM].astype(BF16)
    misc_ref[0] = jnp.where(lane < IDX_DIM, mr, m * (IDX_HEADS ** -0.5))
    xg = jnp.dot(m, wg_ref[...], preferred_element_type=F32, precision=HI) + bg_ref[...]
    gg_ref[0] = (jnp.minimum(xg, 0.0) - jnp.log(1.0 + jnp.exp(-jnp.abs(xg)))) * (1.0 / GATE_TAU)
    qg_ref[0] = proj[:, C_QG:C_KG] * (B_DK ** -0.5)
    kg_ref[0] = proj[:, C_KG:C_VG]
    vg_ref[0] = proj[:, C_VG:C_RG]
    rg_ref[0] = proj[:, C_RG:C_END]


def _premix(x, mod, g_mix, w_r, wg_pad, bg, cos_t, sin_t, tm):
    b, t, d = x.shape
    widths = [(512, BF16), (128, F32), (128, F32), (256, BF16), (64, F32), (128, F32),
              (256, F32), (256, F32), (512, F32), (256, F32), (512, F32),
              (128, BF16), (2 * LANES, BF16), (IDX_DIM, BF16)]
    return pl.pallas_call(
        _premix_kernel,
        grid=(b, t // tm),
        in_specs=[pl.BlockSpec((1, tm, d), lambda i, j: (i, j, 0)),
                  pl.BlockSpec((1, 6, d), lambda i, j: (i, 0, 0)),
                  pl.BlockSpec((1, d), lambda i, j: (0, 0)),
                  pl.BlockSpec((d, C_END), lambda i, j: (0, 0)),
                  pl.BlockSpec((LANES, 256), lambda i, j: (0, 0)),
                  pl.BlockSpec((1, 256), lambda i, j: (0, 0)),
                  pl.BlockSpec((tm, LANES), lambda i, j: (j, 0)),
                  pl.BlockSpec((tm, LANES), lambda i, j: (j, 0))],
        out_specs=[pl.BlockSpec((1, tm, w), lambda i, j: (i, j, 0)) for w, _ in widths],
        out_shape=[jax.ShapeDtypeStruct((b, t, w), dt) for w, dt in widths],
        name="premix",
        compiler_params=_params("arbitrary", "arbitrary"),
    )(x, mod, g_mix, w_r, wg_pad, bg, cos_t, sin_t)


def _attn_kernel(*refs, tq, n_ctx, top_k, causal, n_keys, q_off, n_groups, n_past):
    if n_past:
        q_ref, qi_ref, misc_ref, k_ref, v_ref, ki_ref, pk_ref, pv_ref, pki_ref, o_ref = refs
        assert n_ctx - n_past == LANES and k_ref.shape[1] <= LANES

        def new_columns(rows_ref):
            new = rows_ref[0]
            n_new, width = new.shape
            if width < LANES:
                new = jnp.concatenate([new, jnp.zeros((n_new, LANES - width), F32)], axis=1)
            square = jnp.concatenate([new, jnp.zeros((LANES - n_new, LANES), F32)], axis=0)
            return square.T[:width].astype(BF16)

        k = jnp.concatenate([pk_ref[0].astype(BF16), new_columns(k_ref)], axis=1)
        v = jnp.concatenate([pv_ref[0].astype(BF16), new_columns(v_ref)], axis=1)
        vx = jnp.concatenate([v, jnp.ones_like(v)], axis=0)
        kib = jnp.concatenate([pki_ref[0].astype(BF16), new_columns(ki_ref)], axis=1)
    else:
        q_ref, qi_ref, misc_ref, k_ref, vx_ref, ki_ref, o_ref = refs
        k = k_ref[0]
        vx = vx_ref[0]
        kib = ki_ref[0]
    feature_major = bool(n_past)
    rows = tq // n_groups
    row0 = q_off + pl.program_id(1) * tq
    keys, kks = [], []
    for g in range(n_groups):
        rs = slice(g * rows, (g + 1) * rows)
        qi = qi_ref[0, rs, :]
        misc = misc_ref[0, rs, :]
        isc = jnp.zeros((rows, n_ctx), F32)
        for h in range(IDX_HEADS):
            qh = qi[:, h * IDX_DIM:(h + 1) * IDX_DIM]
            s = jnp.dot(qh, kib, preferred_element_type=F32) if feature_major else _nt(qh, kib)
            isc = isc + misc[:, MISC_WI + h:MISC_WI + h + 1] * jnp.maximum(s, 0.0)
        kpos = lax.broadcasted_iota(I32, (rows, n_ctx), 1)
        if causal:
            row = lax.broadcasted_iota(I32, (rows, 1), 0) + (row0 + g * rows)
            key_lim = (lax.shift_right_logical(row, CHUNK_SHIFT) + 1) * CHUNK
        else:
            key_lim = jnp.full((rows, 1), n_keys, I32)
        bits = pltpu.bitcast(isc, I32)
        key = jnp.where(bits < 0, INT_MIN - bits, bits)
        keys.append(jnp.where(kpos < key_lim, key, INT_MIN))
        kks.append(jnp.minimum(key_lim, top_k).astype(F32))

    def count_at_least(g, cand):
        return jnp.sum(jnp.where(keys[g] >= (cand ^ INT_MIN), 1.0, 0.0), axis=1, keepdims=True)

    bits_per_pass = 2 if n_ctx <= SEARCH_TWO_BIT_MAX_KEYS else 1

    def search(i, ans):
        out = []
        for g in range(n_groups):
            if bits_per_pass == 1:
                cand = ans[g] | lax.shift_left(jnp.int32(1), 31 - i)
                out.append(jnp.where(count_at_least(g, cand) >= kks[g], cand, ans[g]))
            else:
                a1 = ans[g] | lax.shift_left(jnp.int32(1), 31 - 2 * i)
                a2 = ans[g] | lax.shift_left(jnp.int32(1), 30 - 2 * i)
                a3 = a1 | a2
                c1, c2, c3 = count_at_least(g, a1), count_at_least(g, a2), count_at_least(g, a3)
                kk = kks[g]
                out.append(jnp.where(c3 >= kk, a3, jnp.where(c1 >= kk, a1, jnp.where(c2 >= kk, a2, ans[g]))))
        return tuple(out)

    ans = lax.fori_loop(0, 32 // bits_per_pass, search,
                        tuple(jnp.zeros((rows, 1), I32) for _ in range(n_groups)))
    r = lax.broadcasted_iota(I32, (LANES, LANES), 0)
    c = lax.broadcasted_iota(I32, (LANES, LANES), 1)
    upper = jnp.where(r < c, 1.0, 0.0).astype(BF16)
    bias_groups = []
    for g in range(n_groups):
        key = keys[g]
        thr = ans[g] ^ INT_MIN
        gt = key > thr
        need = kks[g] - jnp.sum(jnp.where(gt, 1.0, 0.0), axis=1, keepdims=True)
        carry = jnp.zeros((rows, 1), F32)
        blocks = []
        for j in range(n_ctx // LANES):
            sl = slice(j * LANES, (j + 1) * LANES)
            tie = jnp.where(key[:, sl] == thr, 1.0, 0.0)
            before = jnp.dot(tie.astype(BF16), upper, preferred_element_type=F32) + carry
            take_tie = jnp.where(before < need, tie, 0.0)
            blocks.append(jnp.where(gt[:, sl], 0.0, jnp.where(take_tie > 0.0, 0.0, NEG_BIG)))
            carry = carry + jnp.sum(tie, axis=1, keepdims=True)
        bias_groups.append(jnp.concatenate(blocks, axis=1))
    bias = jnp.concatenate(bias_groups, axis=0)

    q = q_ref[0]
    outs = []
    rep = A_HEADS // A_KV
    for g in range(A_KV):
        kg = k[g * A_HD:(g + 1) * A_HD, :] if feature_major else k[:, g * A_HD:(g + 1) * A_HD]
        for rr in range(rep):
            hh = g * rep + rr
            qh = q[:, hh * A_HD:(hh + 1) * A_HD]
            s = (jnp.dot(qh, kg, preferred_element_type=F32) if feature_major else _nt(qh, kg)) + bias
            p = jnp.exp((s - jnp.max(s, axis=1, keepdims=True)).astype(BF16))
            ox = _nt(p, vx) if feature_major else jnp.dot(p, vx, preferred_element_type=F32)
            outs.append(ox[:, g * A_HD:(g + 1) * A_HD] / ox[:, LANES:LANES + 1])
    o_ref[0] = jnp.concatenate(outs, axis=1).astype(BF16)


def _attn_call(q, qi, misc, kb, vx, kib, past, *, tq, q_off, n_tiles, n_ctx, causal, n_keys, n_groups):
    b = q.shape[0]
    t0 = q_off // tq
    top_k = min(INDEX_TOPK, n_keys // 4)
    n_past = 0 if past is None else past[0].shape[2]
    n_own = n_ctx if past is None else kb.shape[1]
    kern = functools.partial(_attn_kernel, tq=tq, n_ctx=n_ctx, top_k=top_k, causal=causal, n_keys=n_keys,
                             q_off=q_off, n_groups=n_groups, n_past=n_past)

    def tok(w):
        return pl.BlockSpec((1, tq, w), lambda i, j: (i, t0 + j, 0))

    def ctx(rows, w):
        return pl.BlockSpec((1, rows, w), lambda i, j: (i, 0, 0))

    in_specs = [tok(512), tok(256), tok(LANES)] + [ctx(n_own, a.shape[2]) for a in (kb, vx, kib)]
    args = [q, qi, misc, kb, vx, kib]
    if past is not None:
        in_specs += [ctx(a.shape[1], n_past) for a in past]
        args += list(past)
    return pl.pallas_call(
        kern,
        grid=(b, n_tiles),
        in_specs=in_specs,
        out_specs=pl.BlockSpec((1, tq, 512), lambda i, j: (i, j, 0)),
        out_shape=jax.ShapeDtypeStruct((b, n_tiles * tq, 512), BF16),
        name="attn",
        compiler_params=_params("arbitrary", "arbitrary"),
    )(*args)


def _attn_causal(q, qi, misc, kb, vx, kib, tq):
    t = q.shape[1]
    return jnp.stack([
        _attn_call(q, qi, misc, kb, vx, kib, None, tq=tq, q_off=c * tq, n_tiles=1, n_ctx=(c + 1) * tq,
                   causal=True, n_keys=t, n_groups=2)
        for c in range(t // tq)])


def _gla_kernel(q_ref, k_ref, v_ref, g_ref, rg_ref, s0_ref, go_ref, ob_ref, st_ref, st_scr, *, cs, nc, bb):
    j = pl.program_id(1)

    @pl.when(j == 0)
    def _():
        st_scr[...] = s0_ref[...]

    r = lax.broadcasted_iota(I32, (cs, cs), 0)
    c = lax.broadcasted_iota(I32, (cs, cs), 1)
    causal = r >= c
    tri = jnp.where(causal, 1.0, 0.0)
    go = go_ref[...]
    for ci in range(nc):
        for bi in range(bb):
            _gla_chunk(q_ref, k_ref, v_ref, g_ref, rg_ref, ob_ref, st_scr, bi, slice(ci * cs, (ci + 1) * cs),
                       causal, tri, go, cs)

    @pl.when(j == pl.num_programs(1) - 1)
    def _():
        st_ref[...] = st_scr[...]


def _gla_chunk(q_ref, k_ref, v_ref, g_ref, rg_ref, ob_ref, st_scr, bi, sl, causal, tri, go, cs):
    q = q_ref[bi, sl, :]
    k = k_ref[bi, sl, :]
    v = v_ref[bi, sl, :]
    rg = rg_ref[bi, sl, :]
    bcum = jnp.dot(tri, g_ref[bi, sl, :], preferred_element_type=F32, precision=HI)
    blast = bcum[cs - 1:cs, :]
    qdec = (q * jnp.exp(bcum)).astype(BF16)
    kdec = (k * jnp.exp(blast - bcum)).astype(BF16)
    eb = jnp.exp(blast)
    a_rows = [[] for _ in range(B_HEADS)]
    for i in range(cs // GLA_SUB):
        rs = slice(i * GLA_SUB, (i + 1) * GLA_SUB)
        ref = bcum[i * GLA_SUB:i * GLA_SUB + 1, :]
        qt = (q[rs, :] * jnp.exp(bcum[rs, :] - ref)).astype(BF16)
        kt = (k * jnp.exp(jnp.minimum(ref - bcum, GLA_EXP_CLAMP))).astype(BF16)
        for h in range(B_HEADS):
            hs = slice(h * B_DK, (h + 1) * B_DK)
            a_rows[h].append(_nt(qt[:, hs], kt[:, hs]))
    for h in range(B_HEADS):
        hs = slice(h * B_DK, (h + 1) * B_DK)
        vs = slice(h * B_DV, (h + 1) * B_DV)
        a = jnp.where(causal, jnp.concatenate(a_rows[h], axis=0), 0.0).astype(BF16)
        vh = v[:, vs].astype(BF16)
        st = st_scr[bi, h]
        o = jnp.dot(a, vh, preferred_element_type=F32) + _nt(qdec[:, hs], st.astype(BF16))
        st_scr[bi, h] = st * eb[:, hs] + _tn(vh, kdec[:, hs])
        y = _rms(o) * go
        ob_ref[bi, sl, vs] = (y * _silu(rg[:, vs])).astype(BF16)


def _gla(qg, kg, vg, gg, rg, state_t, g_out, *, cs, nc, bb):
    b, t, _ = qg.shape
    assert b % bb == 0
    tt = cs * nc
    kern = functools.partial(_gla_kernel, cs=cs, nc=nc, bb=bb)

    def tok(w):
        return pl.BlockSpec((bb, tt, w), lambda i, j: (i, j, 0))

    st_spec = pl.BlockSpec((bb, B_HEADS, B_DV, B_DK), lambda i, j: (i, 0, 0, 0))
    return pl.pallas_call(
        kern,
        grid=(b // bb, t // tt),
        in_specs=[tok(256), tok(256), tok(512), tok(256), tok(512), st_spec,
                  pl.BlockSpec((1, B_DV), lambda i, j: (0, 0))],
        out_specs=[tok(512), st_spec],
        out_shape=[jax.ShapeDtypeStruct((b, t, 512), BF16),
                   jax.ShapeDtypeStruct((b, B_HEADS, B_DV, B_DK), F32)],
        scratch_shapes=[pltpu.VMEM((bb, B_HEADS, B_DV, B_DK), F32)],
        name="gla",
        compiler_params=_params("arbitrary", "arbitrary"),
    )(qg, kg, vg, gg, rg, state_t, g_out)


def _merge_kernel(oa_ref, ob_ref, x_ref, mod_ref, wo_ref, gf_ref, wr_ref, br_ref, cnt0_ref,
                  x1_ref, h2_ref, ri_ref, rgate_ref, cnt_ref, carry_scr):
    @pl.when((pl.program_id(0) == 0) & (pl.program_id(1) == 0))
    def _():
        carry_scr[...] = cnt0_ref[...]

    mod = mod_ref[0]
    cat = jnp.concatenate([oa_ref[0, 0], ob_ref[0]], axis=1)
    x1 = x_ref[0] + mod[2:3] * jnp.dot(cat, wo_ref[...], preferred_element_type=F32)
    x1_ref[0] = x1
    h2 = _rms(x1) * gf_ref[...] * (1.0 + mod[4:5]) + mod[3:4]
    _store_row_tiles(h2_ref.at[0], h2)
    tm = x1.shape[0]
    lane = lax.broadcasted_iota(I32, (tm, LANES), 1).astype(F32)
    hi = h2.astype(BF16)
    lo = (h2 - hi.astype(F32)).astype(BF16)
    wr = wr_ref[...]
    a = jnp.dot(hi, wr, preferred_element_type=F32)
    left = (a[:, :LANES] + a[:, LANES:]) + jnp.dot(lo, wr[:, :LANES], preferred_element_type=F32) + br_ref[...]
    idx, val = [], []
    for _ in range(TOP_K):
        m = jnp.max(left, axis=1, keepdims=True)
        e = jnp.min(jnp.where(left == m, lane, float(LANES)), axis=1, keepdims=True)
        idx.append(e)
        val.append(m)
        left = jnp.where(lane == e, -jnp.inf, left)
    ex = [jnp.exp(vv - val[0]) for vv in val]
    den = ex[0] + ex[1] + ex[2] + ex[3]
    onehot = jnp.zeros((tm, LANES), F32)
    for e in idx:
        onehot = onehot + jnp.where(lane == e, 1.0, 0.0)
    r = lax.broadcasted_iota(I32, (tm, tm), 0)
    c = lax.broadcasted_iota(I32, (tm, tm), 1)
    earlier = jnp.where(r > c, 1.0, 0.0).astype(BF16)
    before = jnp.dot(earlier, onehot.astype(BF16), preferred_element_type=F32) + carry_scr[...]
    ri = jnp.zeros((tm, LANES), F32)
    rgate = jnp.zeros((tm, LANES), F32)
    for kq in range(TOP_K):
        rank = jnp.sum(jnp.where(lane == idx[kq], before, 0.0), axis=1, keepdims=True)
        ri = jnp.where(lane == kq, idx[kq], ri)
        ri = jnp.where(lane == TOP_K + kq, rank, ri)
        rgate = jnp.where(lane == kq, ex[kq] / den, rgate)
    ri_ref[0] = ri.astype(I32)
    rgate_ref[0] = rgate
    carry_scr[...] = carry_scr[...] + jnp.sum(onehot, axis=0, keepdims=True)
    cnt_ref[...] = carry_scr[...]


def _merge(oa, ob, x, mod, w_out, g_ffn, wr_pad, br_pad, cnt0, tm):
    b, t, d = x.shape

    def tok(w):
        return pl.BlockSpec((1, tm, w), lambda i, j: (i, j, 0))

    def const(s):
        return pl.BlockSpec(s, lambda i, j: (0, 0))

    return pl.pallas_call(
        _merge_kernel,
        grid=(b, t // tm),
        in_specs=[pl.BlockSpec((1, 1, tm, 512), lambda i, j: (j, i, 0, 0)),
                  tok(512), tok(d), pl.BlockSpec((1, 6, d), lambda i, j: (i, 0, 0)),
                  const((d, d)), const((1, d)), const((d, 2 * LANES)), const((1, LANES)), const((1, LANES))],
        out_specs=[tok(d), pl.BlockSpec((1, tm * SUBLANES, LANES), lambda i, j: (i, j, 0)),
                   tok(LANES), tok(LANES), const((1, LANES))],
        out_shape=[jax.ShapeDtypeStruct((b, t, d), F32), jax.ShapeDtypeStruct((b, t * SUBLANES, LANES), F32),
                   jax.ShapeDtypeStruct((b, t, LANES), I32), jax.ShapeDtypeStruct((b, t, LANES), F32),
                   jax.ShapeDtypeStruct((1, LANES), F32)],
        scratch_shapes=[pltpu.VMEM((1, LANES), F32)],
        name="merge",
        compiler_params=_params("arbitrary", "arbitrary"),
    )(oa, ob, x, mod, w_out, g_ffn, wr_pad, br_pad, cnt0)


def _dispatch_kernel(pe_ref, hp_ref, dest_ref, h_ref, *rest, tm, zero_init):
    if zero_init:
        xs_ref, zbuf, sem, zsem = rest

        @pl.when(pl.program_id(0) == 0)
        def _():
            zbuf[...] = jnp.zeros_like(zbuf)

            def zero_copy(e):
                start = pl.multiple_of(pe_ref[e] - SLOT_BLOCK, SLOT_BLOCK)
                return pltpu.make_async_copy(zbuf, xs_ref.at[pl.ds(start, SLOT_BLOCK)], zsem)

            for e in range(N_EXPERTS):
                @pl.when(hp_ref[e] > 0)
                def _():
                    zero_copy(e).start()
            for e in range(N_EXPERTS):
                @pl.when(hp_ref[e] > 0)
                def _():
                    zero_copy(e).wait()
    else:
        _, xs_ref, sem = rest

    def row_copy(r, slot):
        return pltpu.make_async_copy(h_ref.at[r], xs_ref.at[slot], sem)

    def issue(i, carry):
        for j in range(ROW_UNROLL):
            for kq in range(TOP_K):
                row_copy(i * ROW_UNROLL + j, dest_ref[(i * ROW_UNROLL + j) * TOP_K + kq]).start(priority=kq % 2)
        return carry

    lax.fori_loop(0, tm // ROW_UNROLL, issue, 0)

    def drain(i, carry):
        for _ in range(ROW_UNROLL * TOP_K):
            row_copy(0, 0).wait()
        return carry

    lax.fori_loop(0, tm // ROW_UNROLL, drain, 0)


def _dispatch(pad_end, has_pad, dest_flat, h2_tiles, xs, n_slots, tm):
    n = h2_tiles.shape[0]
    tile = h2_tiles.shape[1:]
    zero_init = xs is None
    kern = functools.partial(_dispatch_kernel, tm=tm, zero_init=zero_init)
    in_specs = [pl.BlockSpec((tm * TOP_K,), lambda i, pe, hp: (i,), memory_space=pltpu.SMEM),
                pl.BlockSpec((tm,) + tile, lambda i, pe, hp: (i, 0, 0))]
    args = [pad_end, has_pad, dest_flat, h2_tiles]
    scratch = [pltpu.SemaphoreType.DMA(())]
    aliases = {}
    if zero_init:
        scratch = [pltpu.VMEM((SLOT_BLOCK,) + tile, F32), pltpu.SemaphoreType.DMA(()), pltpu.SemaphoreType.DMA(())]
    else:
        in_specs.append(pl.BlockSpec(memory_space=pl.ANY))
        args.append(xs)
        aliases = {4: 0}
    grid_spec = pltpu.PrefetchScalarGridSpec(
        num_scalar_prefetch=2,
        grid=(n // tm,),
        in_specs=in_specs,
        out_specs=pl.BlockSpec(memory_space=pl.ANY),
        scratch_shapes=scratch,
    )
    return pl.pallas_call(
        kern,
        grid_spec=grid_spec,
        out_shape=jax.ShapeDtypeStruct((n_slots,) + tile, F32),
        input_output_aliases=aliases,
        name="dispatch",
        compiler_params=_params("arbitrary"),
    )(*args)


def _expert_kernel(be_ref, nu_ref, xs_ref, wgu_ref, bgu_ref, wdn_ref, bdn_ref, ys_ref):
    @pl.when(pl.program_id(0) < nu_ref[0])
    def _():
        f = wdn_ref.shape[1]
        x = _load_row_tiles(xs_ref, SLOT_BLOCK).astype(BF16)
        gu = jnp.dot(x, wgu_ref[0], preferred_element_type=F32) + bgu_ref[0]
        gate = jnp.minimum(gu[:, :f], SWIGLU_LIMIT)
        up = jnp.clip(gu[:, f:], -SWIGLU_LIMIT, SWIGLU_LIMIT)
        glu = gate / (1.0 + jnp.exp(-SWIGLU_ALPHA * gate))
        act = ((up + 1.0) * glu).astype(BF16)
        _store_row_tiles(ys_ref, jnp.dot(act, wdn_ref[0], preferred_element_type=F32) + bdn_ref[0])


def _experts(block_expert, n_used, xs, wgu, bgu, wdn, bdn):
    n_slots = xs.shape[0] // SUBLANES
    nb = n_slots // SLOT_BLOCK
    f, d = wdn.shape[1:]
    rows = SLOT_BLOCK * SUBLANES

    def blk(i, be, nu):
        return (jnp.minimum(i, nu[0] - 1), 0)

    def wsel(i, be, nu):
        return (be[i], 0, 0)

    grid_spec = pltpu.PrefetchScalarGridSpec(
        num_scalar_prefetch=2,
        grid=(nb,),
        in_specs=[pl.BlockSpec((rows, LANES), blk),
                  pl.BlockSpec((1, d, 2 * f), wsel),
                  pl.BlockSpec((1, 1, 2 * f), wsel),
                  pl.BlockSpec((1, f, d), wsel),
                  pl.BlockSpec((1, 1, d), wsel)],
        out_specs=pl.BlockSpec((rows, LANES), blk),
    )
    return pl.pallas_call(
        _expert_kernel,
        grid_spec=grid_spec,
        out_shape=jax.ShapeDtypeStruct(xs.shape, F32),
        name="experts",
        compiler_params=_params("arbitrary"),
    )(block_expert, n_used, xs, wgu, bgu, wdn, bdn)


def _combine_kernel(dest_ref, dnext_ref, x1_ref, gate_ref, mod_ref, modf_ref, gfin_ref, ys_ref, y_ref,
                    buf, sem, *, tm):
    nt = pl.num_programs(1)
    step = pl.program_id(0) * nt + pl.program_id(1)
    n_steps = pl.num_programs(0) * nt
    half = TOP_K * tm * SUBLANES
    cur = lax.rem(step, 2)

    def row_copy(which, kq, r, slot):
        off = pl.multiple_of(which * half + (kq * tm + r) * SUBLANES, SUBLANES)
        return pltpu.make_async_copy(ys_ref.at[slot], buf.at[pl.ds(off, SUBLANES), :], sem.at[which])

    def gather(slots_ref, which):
        def issue(i, carry):
            for j in range(ROW_UNROLL):
                for kq in range(TOP_K):
                    r = i * ROW_UNROLL + j
                    row_copy(which, kq, r, slots_ref[r * TOP_K + kq]).start(priority=kq % 2)
            return carry

        lax.fori_loop(0, tm // ROW_UNROLL, issue, 0)

    @pl.when(step == 0)
    def _():
        gather(dest_ref, cur)

    @pl.when(step + 1 < n_steps)
    def _():
        gather(dnext_ref, 1 - cur)

    def drain(i, carry):
        for _ in range(ROW_UNROLL * TOP_K):
            row_copy(cur, 0, 0, 0).wait()
        return carry

    lax.fori_loop(0, tm // ROW_UNROLL, drain, 0)
    gates = gate_ref[0]
    moe = None
    for kq in range(TOP_K):
        start = pl.multiple_of(cur * half + kq * tm * SUBLANES, SUBLANES)
        rows = _load_row_tiles(buf.at[pl.ds(start, tm * SUBLANES), :], tm)
        term = gates[:, kq:kq + 1] * rows
        moe = term if moe is None else moe + term
    xo = x1_ref[0] + mod_ref[0][5:6] * moe
    modf = modf_ref[0]
    y_ref[0] = _rms(xo) * gfin_ref[...] * (1.0 + modf[1:2]) + modf[0:1]


def _combine(dest_flat, x1, gates, mod, modf, g_final, ys, tm):
    b, t, d = x1.shape
    nt = t // tm
    last = b * nt - 1
    kern = functools.partial(_combine_kernel, tm=tm)
    return pl.pallas_call(
        kern,
        grid=(b, nt),
        in_specs=[pl.BlockSpec((tm * TOP_K,), lambda i, j: (i * nt + j,), memory_space=pltpu.SMEM),
                  pl.BlockSpec((tm * TOP_K,), lambda i, j: (jnp.minimum(i * nt + j + 1, last),),
                               memory_space=pltpu.SMEM),
                  pl.BlockSpec((1, tm, d), lambda i, j: (i, j, 0)),
                  pl.BlockSpec((1, tm, LANES), lambda i, j: (i, j, 0)),
                  pl.BlockSpec((1, 6, d), lambda i, j: (i, 0, 0)),
                  pl.BlockSpec((1, 2, d), lambda i, j: (i, 0, 0)),
                  pl.BlockSpec((1, d), lambda i, j: (0, 0)),
                  pl.BlockSpec(memory_space=pl.ANY)],
        out_specs=pl.BlockSpec((1, tm, d), lambda i, j: (i, j, 0)),
        out_shape=jax.ShapeDtypeStruct((b, t, d), F32),
        scratch_shapes=[pltpu.VMEM((2 * TOP_K * tm * SUBLANES, LANES), F32), pltpu.SemaphoreType.DMA((2,))],
        name="combine",
        compiler_params=_params("arbitrary", "arbitrary"),
    )(dest_flat, dest_flat, x1, gates, mod, modf, g_final, ys)


def _rope_tables(pos):
    half = A_HD // 2
    inv_freq = ROPE_THETA ** (-jnp.arange(half, dtype=F32) / half)
    ang = pos.astype(F32)[:, None] * inv_freq[None, :]
    cos = jnp.cos(ang)
    sin = jnp.sin(ang)
    reps = LANES // A_HD
    return jnp.tile(jnp.concatenate([cos, cos], axis=1), (1, reps)), jnp.tile(jnp.concatenate([-sin, sin], axis=1), (1, reps))


def _reorder_w_in(w_in):
    offs = [0]
    for w in (512, 128, 128, 256, 64, 4, 256, 256, 512, 16, 512):
        offs.append(offs[-1] + w)
    seg = [w_in[:, offs[i]:offs[i + 1]] for i in range(11)]
    qa, ka, va, qi, ki, wi, qg, kg, vg, lr, rg = seg
    pad = jnp.zeros((w_in.shape[0], LANES - IDX_DIM - IDX_HEADS - GATE_RANK), w_in.dtype)
    return jnp.concatenate([qa, ka, va, qi, ki, wi, lr, pad, qg, kg, vg, rg], axis=1).astype(BF16)


def kernel(x_prompt, x_sample, cache_k, cache_v, cache_kidx, state_gla, c_prompt, c_sample,
           w_mod, b_mod, g_mix, g_ffn, w_in, gla_w_gate, gla_b_gate, gla_g_out, w_out,
           w_router, b_router, w_gate_up, b_gate_up, w_down, b_down,
           w_mod_final, b_mod_final, g_final):
    depth = w_in.shape[0]
    assert depth == 1
    bp, sp, d = x_prompt.shape
    bs, ts, _ = x_sample.shape
    past = cache_k.shape[2]
    n_exp = w_router.shape[2]
    f = w_down.shape[2]

    c_all = jnp.concatenate([c_prompt, c_sample], axis=0)
    w_r = _reorder_w_in(w_in[0])
    wg_pad = jnp.zeros((LANES, B_HEADS * B_DK), F32).at[MISC_LR:MISC_LR + GATE_RANK].set(gla_w_gate[0])
    bg = gla_b_gate[0].reshape(1, -1)
    wr_pad = jnp.zeros((d, LANES), F32).at[:, :n_exp].set(w_router[0])
    wr_hi = wr_pad.astype(BF16)
    wr_cat = jnp.concatenate([wr_hi, (wr_pad - wr_hi.astype(F32)).astype(BF16)], axis=1)
    br_pad = jnp.full((1, LANES), NEG_BIG, F32).at[0, :n_exp].set(b_router[0])
    w_out_b = w_out[0].astype(BF16)
    wgu_b = w_gate_up[0].astype(BF16)
    wdn_b = w_down[0].astype(BF16)
    bgu = b_gate_up[0].reshape(n_exp, 1, 2 * f)
    bdn = b_down[0].reshape(n_exp, 1, d)
    g_out = gla_g_out[0].reshape(1, B_DV)

    mod_all = _adaln(c_all, w_mod[0], b_mod[0]).reshape(bp + bs, 6, d)
    modf_all = _adaln(c_all, w_mod_final, b_mod_final).reshape(bp + bs, 2, d)
    mod_p, mod_s = mod_all[:bp], mod_all[bp:]
    modf_p, modf_s = modf_all[:bp], modf_all[bp:]

    def mixer(x, mod, pos, tm, past_kv, state_t, cs, nc, tq):
        cos_t, sin_t = _rope_tables(pos)
        qa, ka, va, qi, ki, misc, qg, kg, vg, gg, rg, kb, vx, kib = _premix(
            x, mod, g_mix[0].reshape(1, d), w_r, wg_pad, bg, cos_t, sin_t, tm)
        if past_kv is None:
            oa = _attn_causal(qa, qi, misc, kb, vx, kib, tq)
        else:
            n_past = past_kv[0].shape[2]
            assert tq == x.shape[1] and n_past % LANES == 0
            oa = _attn_call(qa, qi, misc, ka, va, ki, past_kv, tq=tq, q_off=0, n_tiles=1,
                            n_ctx=n_past + LANES, causal=False, n_keys=n_past + ka.shape[1], n_groups=1)[None]
        ob, st = _gla(qg, kg, vg, gg, rg, state_t, g_out, cs=cs, nc=nc, bb=GLA_BATCH)
        return oa, ob, (ka, va, ki, st)

    state0_p = jnp.zeros((bp, B_HEADS, B_DV, B_DK), F32)
    oa_p, ob_p, (ka_p, va_p, ki_p, st_p) = mixer(
        x_prompt, mod_p, jnp.arange(sp), min(256, sp), None, state0_p, CHUNK, 4, min(256, sp))
    state0_s = jnp.swapaxes(state_gla[0], -1, -2)
    past_kv = (jnp.transpose(cache_k[0], (0, 2, 3, 1)).reshape(bs, A_KV * A_HD, past),
               jnp.transpose(cache_v[0], (0, 2, 3, 1)).reshape(bs, A_KV * A_HD, past),
               jnp.swapaxes(cache_kidx[0], 1, 2))
    oa_s, ob_s, (ka_s, va_s, ki_s, st_s) = mixer(
        x_sample, mod_s, past + jnp.arange(ts), ts, past_kv, state0_s, ts, 1, ts)

    cnt0 = jnp.zeros((1, LANES), F32)
    x1_p, h2_p, ri_p, gate_p, cnt1 = _merge(oa_p, ob_p, x_prompt, mod_p, w_out_b, g_ffn[0].reshape(1, d),
                                            wr_cat, br_pad, cnt0, min(256, sp))
    x1_s, h2_s, ri_s, gate_s, cnt2 = _merge(oa_s, ob_s, x_sample, mod_s, w_out_b, g_ffn[0].reshape(1, d),
                                            wr_cat, br_pad, cnt1, ts)
    counts = cnt2[0, :n_exp].astype(I32)
    padded = (counts + SLOT_BLOCK - 1) // SLOT_BLOCK * SLOT_BLOCK
    pad_end = jnp.cumsum(padded)
    pad_start = pad_end - padded
    n_asg = (bp * sp + bs * ts) * TOP_K
    nb = -(-n_asg // SLOT_BLOCK) + n_exp
    n_slots = nb * SLOT_BLOCK
    block_start = jnp.arange(nb, dtype=I32) * SLOT_BLOCK
    block_expert = jnp.minimum(jnp.sum((pad_end[None, :] <= block_start[:, None]).astype(I32), axis=1), n_exp - 1)
    n_used = (pad_end[-1:] // SLOT_BLOCK).astype(I32)
    has_pad = (padded > counts).astype(I32)
    expert_ids = jnp.arange(n_exp, dtype=I32)

    def dests(ri):
        e, rank = ri[..., :TOP_K], ri[..., TOP_K:2 * TOP_K]
        start = jnp.sum(jnp.where(e[..., None] == expert_ids, pad_start.astype(I32), 0), axis=-1)
        return (start + rank).reshape(-1)

    dest_p, dest_s = dests(ri_p), dests(ri_s)
    pad_end_i = pad_end.astype(I32)
    tile = (SUBLANES, LANES)
    xs = _dispatch(pad_end_i, has_pad, dest_p, h2_p.reshape((bp * sp,) + tile), None, n_slots, min(256, sp))
    xs = _dispatch(pad_end_i, has_pad, dest_s, h2_s.reshape((bs * ts,) + tile), xs, n_slots, min(256, bs * ts))
    ys = _experts(block_expert, n_used, xs.reshape(n_slots * SUBLANES, LANES), wgu_b, bgu, wdn_b, bdn)
    ys = ys.reshape((n_slots,) + tile)
    y_p = _combine(dest_p, x1_p, gate_p, mod_p, modf_p, g_final.reshape(1, d), ys, min(256, sp))
    y_s = _combine(dest_s, x1_s, gate_s, mod_s, modf_s, g_final.reshape(1, d), ys, ts)

    def kv(a, b, t):
        return a.reshape(1, b, t, A_KV, A_HD)

    return (y_p, y_s,
            kv(ka_p, bp, sp), kv(va_p, bp, sp), ki_p[None], jnp.swapaxes(st_p, -1, -2)[None],
            kv(ka_s, bs, ts), kv(va_s, bs, ts), ki_s[None], jnp.swapaxes(st_s, -1, -2)[None])
```

```python
import functools

import jax
import jax.numpy as jnp
from jax import lax
from jax.experimental import pallas as pl
from jax.experimental.pallas import tpu as pltpu

F32 = jnp.float32
BF16 = jnp.bfloat16
I32 = jnp.int32
HI = lax.Precision.HIGHEST

CHUNK = 64
CHUNK_SHIFT = 6
EPS = 1e-6
ROPE_THETA = 10000.0
A_HD = 64
A_HEADS = 8
A_KV = 2
IDX_HEADS = 4
IDX_DIM = 64
INDEX_TOPK = 256
B_HEADS = 4
B_DK = 64
B_DV = 128
GATE_RANK = 16
GATE_TAU = 16.0
N_EXPERTS = 32
TOP_K = 4
SWIGLU_LIMIT = 7.0
SWIGLU_ALPHA = 1.702

LANES = 128
SUBLANES = 8
ROW_UNROLL = 8
GLA_SUB = 16
GLA_BATCH = 2
GLA_CHUNKS_PER_STEP = 8
GLA_EXP_CLAMP = 80.0
SLOT_BLOCK = 512
VMEM_LIMIT = 56 * 1024 * 1024
INT_MIN = -2147483648
NEG_BIG = -1e30
SEARCH_TWO_BIT_MAX_KEYS = 768

C_QA, C_KA, C_VA, C_QI, C_MISC, C_QG, C_KG, C_VG, C_RG, C_END = 0, 512, 640, 768, 1024, 1152, 1408, 1664, 2176, 2688
MISC_WI = 64
MISC_LR = 68


def _params(*sem):
    return pltpu.CompilerParams(dimension_semantics=sem, vmem_limit_bytes=VMEM_LIMIT)


def _nt(a, b):
    return lax.dot_general(a, b, (((1,), (1,)), ((), ())), preferred_element_type=F32)


def _tn(a, b):
    return lax.dot_general(a, b, (((0,), (0,)), ((), ())), preferred_element_type=F32)


def _rms(x):
    return x * lax.rsqrt(jnp.mean(x * x, axis=-1, keepdims=True) + EPS)


def _silu(x):
    return x / (1.0 + jnp.exp(-x))


def _store_row_tiles(ref, val):
    rows, width = val.shape
    assert width == SUBLANES * LANES
    for s in range(SUBLANES):
        ref[pl.ds(s, rows, stride=SUBLANES), :] = val[:, s * LANES:(s + 1) * LANES]


def _load_row_tiles(ref, rows):
    return jnp.concatenate([ref[pl.ds(s, rows, stride=SUBLANES), :] for s in range(SUBLANES)], axis=1)


def _value_with_ones(vb):
    return jnp.concatenate([vb, jnp.ones_like(vb)], axis=-1)


def _adaln_kernel(c_ref, w_ref, b_ref, o_ref):
    a = _silu(c_ref[...])
    o_ref[...] = jnp.dot(a, w_ref[...], preferred_element_type=F32, precision=HI) + b_ref[...]


def _adaln(c, w, b):
    r, d = c.shape
    n = w.shape[1]
    tn = 512
    return pl.pallas_call(
        _adaln_kernel,
        grid=(n // tn,),
        in_specs=[pl.BlockSpec((r, d), lambda j: (0, 0)),
                  pl.BlockSpec((d, tn), lambda j: (0, j)),
                  pl.BlockSpec((1, tn), lambda j: (0, j))],
        out_specs=pl.BlockSpec((r, tn), lambda j: (0, j)),
        out_shape=jax.ShapeDtypeStruct((r, n), F32),
        name="adaln",
        compiler_params=_params("arbitrary"),
    )(c, w, b.reshape(1, n))


def _premix_kernel(x_ref, mod_ref, g_ref, w_ref, wg_ref, bg_ref, cos_ref, sin_ref,
                   qa_ref, ka_ref, va_ref, qi_ref, ki_ref, misc_ref, qg_ref, kg_ref, vg_ref, gg_ref, rg_ref,
                   kb_ref, vx_ref, kib_ref):
    x = x_ref[0]
    mod = mod_ref[0]
    hb = (_rms(x) * g_ref[...] * (1.0 + mod[1:2]) + mod[0:1]).astype(BF16)
    tm = x.shape[0]

    def project(c0, c1):
        return jnp.dot(hb, w_ref[:, c0:c1], preferred_element_type=F32)

    cos = cos_ref[...]
    sin = sin_ref[...]
    lane = lax.broadcasted_iota(I32, (tm, LANES), 1)
    lower_half = (lane & (A_HD - 1)) < (A_HD // 2)

    def rope(xc):
        rot = jnp.where(lower_half, pltpu.roll(xc, LANES - A_HD // 2, 1), pltpu.roll(xc, A_HD // 2, 1))
        return xc * cos + rot * sin

    seg = project(C_QA, C_KA)
    for j in range((C_KA - C_QA) // LANES):
        qa_ref[0, :, j * LANES:(j + 1) * LANES] = (rope(seg[:, j * LANES:(j + 1) * LANES]) * (A_HD ** -0.5)).astype(BF16)
    seg = project(C_KA, C_QI)
    ka = rope(seg[:, :LANES])
    va = seg[:, LANES:]
    ka_ref[0] = ka
    va_ref[0] = va
    kb_ref[0] = ka.astype(BF16)
    vx_ref[0] = _value_with_ones(va.astype(BF16))
    seg = project(C_QI, C_QG)
    for j in range((C_MISC - C_QI) // LANES):
        qi_ref[0, :, j * LANES:(j + 1) * LANES] = rope(seg[:, j * LANES:(j + 1) * LANES]).astype(BF16)
    m = seg[:, C_MISC - C_QI:]
    mr = rope(m)
    ki_ref[0] = mr[:, :IDX_DIM]
    kib_ref[0] = mr[:, :IDX_DIM].astype(BF16)
    misc_ref[0] = jnp.where(lane < IDX_DIM, mr, m * (IDX_HEADS ** -0.5))
    xg = jnp.dot(m, wg_ref[...], preferred_element_type=F32, precision=HI) + bg_ref[...]
    gg_ref[0] = (jnp.minimum(xg, 0.0) - jnp.log(1.0 + jnp.exp(-jnp.abs(xg)))) * (1.0 / GATE_TAU)
    qg_ref[0] = project(C_QG, C_KG) * (B_DK ** -0.5)
    kg_ref[0] = project(C_KG, C_VG)
    vg_ref[0] = project(C_VG, C_RG)
    rg_ref[0] = project(C_RG, C_END)


def _premix(x, mod, g_mix, w_r, wg_pad, bg, cos_t, sin_t, tm):
    b, t, d = x.shape
    widths = [(512, BF16), (128, F32), (128, F32), (256, BF16), (64, F32), (128, F32),
              (256, F32), (256, F32), (512, F32), (256, F32), (512, F32),
              (128, BF16), (2 * LANES, BF16), (IDX_DIM, BF16)]
    return pl.pallas_call(
        _premix_kernel,
        grid=(b, t // tm),
        in_specs=[pl.BlockSpec((1, tm, d), lambda i, j: (i, j, 0)),
                  pl.BlockSpec((1, 6, d), lambda i, j: (i, 0, 0)),
                  pl.BlockSpec((1, d), lambda i, j: (0, 0)),
                  pl.BlockSpec((d, C_END), lambda i, j: (0, 0)),
                  pl.BlockSpec((LANES, 256), lambda i, j: (0, 0)),
                  pl.BlockSpec((1, 256), lambda i, j: (0, 0)),
                  pl.BlockSpec((tm, LANES), lambda i, j: (j, 0)),
                  pl.BlockSpec((tm, LANES), lambda i, j: (j, 0))],
        out_specs=[pl.BlockSpec((1, tm, w), lambda i, j: (i, j, 0)) for w, _ in widths],
        out_shape=[jax.ShapeDtypeStruct((b, t, w), dt) for w, dt in widths],
        name="premix",
        compiler_params=_params("arbitrary", "arbitrary"),
    )(x, mod, g_mix, w_r, wg_pad, bg, cos_t, sin_t)


def _attn_kernel(*refs, tq, n_ctx, top_k, causal, n_keys, q_off, n_groups, n_past):
    if n_past:
        q_ref, qi_ref, misc_ref, k_ref, v_ref, ki_ref, pk_ref, pv_ref, pki_ref, o_ref = refs
        assert n_ctx - n_past == LANES and k_ref.shape[1] <= LANES

        def new_columns(rows_ref):
            new = rows_ref[0]
            n_new, width = new.shape
            if width < LANES:
                new = jnp.concatenate([new, jnp.zeros((n_new, LANES - width), F32)], axis=1)
            square = jnp.concatenate([new, jnp.zeros((LANES - n_new, LANES), F32)], axis=0)
            return square.T[:width].astype(BF16)

        k = jnp.concatenate([pk_ref[0].astype(BF16), new_columns(k_ref)], axis=1)
        v = jnp.concatenate([pv_ref[0].astype(BF16), new_columns(v_ref)], axis=1)
        vx = jnp.concatenate([v, jnp.ones_like(v)], axis=0)
        kib = jnp.concatenate([pki_ref[0].astype(BF16), new_columns(ki_ref)], axis=1)
    else:
        q_ref, qi_ref, misc_ref, k_ref, vx_ref, ki_ref, o_ref = refs
        k = k_ref[0]
        vx = vx_ref[0]
        kib = ki_ref[0]
    feature_major = bool(n_past)
    rows = tq // n_groups
    row0 = q_off + pl.program_id(1) * tq
    keys, kks = [], []
    idx_dots = {}
    for g in range(n_groups):
        qi = qi_ref[0, g * rows:(g + 1) * rows, :]
        for h in range(IDX_HEADS):
            qh = qi[:, h * IDX_DIM:(h + 1) * IDX_DIM]
            idx_dots[g, h] = jnp.dot(qh, kib, preferred_element_type=F32) if feature_major else _nt(qh, kib)
    for g in range(n_groups):
        misc = misc_ref[0, g * rows:(g + 1) * rows, :]
        isc = jnp.zeros((rows, n_ctx), F32)
        for h in range(IDX_HEADS):
            isc = isc + misc[:, MISC_WI + h:MISC_WI + h + 1] * jnp.maximum(idx_dots[g, h], 0.0)
        kpos = lax.broadcasted_iota(I32, (rows, n_ctx), 1)
        if causal:
            row = lax.broadcasted_iota(I32, (rows, 1), 0) + (row0 + g * rows)
            key_lim = (lax.shift_right_logical(row, CHUNK_SHIFT) + 1) * CHUNK
        else:
            key_lim = jnp.full((rows, 1), n_keys, I32)
        bits = pltpu.bitcast(isc, I32)
        key = jnp.where(bits < 0, INT_MIN - bits, bits)
        keys.append(jnp.where(kpos < key_lim, key, INT_MIN))
        kks.append(jnp.minimum(key_lim, top_k).astype(F32))

    def count_at_least(g, cand):
        return jnp.sum(jnp.where(keys[g] >= (cand ^ INT_MIN), 1.0, 0.0), axis=1, keepdims=True)

    bits_per_pass = 2 if n_ctx <= SEARCH_TWO_BIT_MAX_KEYS else 1

    def search(i, ans):
        out = []
        for g in range(n_groups):
            if bits_per_pass == 1:
                cand = ans[g] | lax.shift_left(jnp.int32(1), 31 - i)
                out.append(jnp.where(count_at_least(g, cand) >= kks[g], cand, ans[g]))
            else:
                a1 = ans[g] | lax.shift_left(jnp.int32(1), 31 - 2 * i)
                a2 = ans[g] | lax.shift_left(jnp.int32(1), 30 - 2 * i)
                a3 = a1 | a2
                c1, c2, c3 = count_at_least(g, a1), count_at_least(g, a2), count_at_least(g, a3)
                kk = kks[g]
                out.append(jnp.where(c3 >= kk, a3, jnp.where(c1 >= kk, a1, jnp.where(c2 >= kk, a2, ans[g]))))
        return tuple(out)

    ans = lax.fori_loop(0, 32 // bits_per_pass, search,
                        tuple(jnp.zeros((rows, 1), I32) for _ in range(n_groups)))
    r = lax.broadcasted_iota(I32, (LANES, LANES), 0)
    c = lax.broadcasted_iota(I32, (LANES, LANES), 1)
    upper = jnp.where(r < c, 1.0, 0.0).astype(BF16)
    n_blk = n_ctx // LANES
    thrs = [ans[g] ^ INT_MIN for g in range(n_groups)]
    ties = {(g, j): jnp.where(keys[g][:, j * LANES:(j + 1) * LANES] == thrs[g], 1.0, 0.0)
            for g in range(n_groups) for j in range(n_blk)}
    within = {gj: jnp.dot(t.astype(BF16), upper, preferred_element_type=F32) for gj, t in ties.items()}
    bias_groups = []
    for g in range(n_groups):
        gt = keys[g] > thrs[g]
        need = kks[g] - jnp.sum(jnp.where(gt, 1.0, 0.0), axis=1, keepdims=True)
        carry = jnp.zeros((rows, 1), F32)
        blocks = []
        for j in range(n_blk):
            sl = slice(j * LANES, (j + 1) * LANES)
            take_tie = jnp.where(within[g, j] + carry < need, ties[g, j], 0.0)
            blocks.append(jnp.where(gt[:, sl], 0.0, jnp.where(take_tie > 0.0, 0.0, NEG_BIG)))
            carry = carry + jnp.sum(ties[g, j], axis=1, keepdims=True)
        bias_groups.append(jnp.concatenate(blocks, axis=1))
    bias = jnp.concatenate(bias_groups, axis=0)

    q = q_ref[0]
    outs = []
    rep = A_HEADS // A_KV
    kgs = [k[g * A_HD:(g + 1) * A_HD, :] if feature_major else k[:, g * A_HD:(g + 1) * A_HD] for g in range(A_KV)]
    qs = [q[:, hh * A_HD:(hh + 1) * A_HD] for hh in range(A_HEADS)]
    ss = [(jnp.dot(qh, kgs[hh // rep], preferred_element_type=F32) if feature_major else _nt(qh, kgs[hh // rep]))
          + bias for hh, qh in enumerate(qs)]
    ps = [jnp.exp((s - jnp.max(s, axis=1, keepdims=True)).astype(BF16)) for s in ss]
    oxs = [_nt(p, vx) if feature_major else jnp.dot(p, vx, preferred_element_type=F32) for p in ps]
    outs = [ox[:, (hh // rep) * A_HD:(hh // rep + 1) * A_HD] / ox[:, LANES:LANES + 1] for hh, ox in enumerate(oxs)]
    o_ref[0] = jnp.concatenate(outs, axis=1).astype(BF16)


def _attn_call(q, qi, misc, kb, vx, kib, past, *, tq, q_off, n_tiles, n_ctx, causal, n_keys, n_groups):
    b = q.shape[0]
    t0 = q_off // tq
    top_k = min(INDEX_TOPK, n_keys // 4)
    n_past = 0 if past is None else past[0].shape[2]
    n_own = n_ctx if past is None else kb.shape[1]
    kern = functools.partial(_attn_kernel, tq=tq, n_ctx=n_ctx, top_k=top_k, causal=causal, n_keys=n_keys,
                             q_off=q_off, n_groups=n_groups, n_past=n_past)

    def tok(w):
        return pl.BlockSpec((1, tq, w), lambda i, j: (i, t0 + j, 0))

    def ctx(rows, w):
        return pl.BlockSpec((1, rows, w), lambda i, j: (i, 0, 0))

    in_specs = [tok(512), tok(256), tok(LANES)] + [ctx(n_own, a.shape[2]) for a in (kb, vx, kib)]
    args = [q, qi, misc, kb, vx, kib]
    if past is not None:
        in_specs += [ctx(a.shape[1], n_past) for a in past]
        args += list(past)
    return pl.pallas_call(
        kern,
        grid=(b, n_tiles),
        in_specs=in_specs,
        out_specs=pl.BlockSpec((1, tq, 512), lambda i, j: (i, j, 0)),
        out_shape=jax.ShapeDtypeStruct((b, n_tiles * tq, 512), BF16),
        name="attn",
        compiler_params=_params("arbitrary", "arbitrary"),
    )(*args)


def _attn_causal(q, qi, misc, kb, vx, kib, tq):
    t = q.shape[1]
    return jnp.stack([
        _attn_call(q, qi, misc, kb, vx, kib, None, tq=tq, q_off=c * tq, n_tiles=1, n_ctx=(c + 1) * tq,
                   causal=True, n_keys=t, n_groups=2)
        for c in range(t // tq)])


def _gla_kernel(q_ref, k_ref, v_ref, g_ref, rg_ref, s0_ref, go_ref, ob_ref, st_ref, st_scr, *, cs, nc, bb):
    j = pl.program_id(1)
    hk, hv, hc = B_HEADS * B_DK, B_HEADS * B_DV, B_HEADS * cs

    def head_of(idx, width):
        return lax.shift_right_logical(idx, width.bit_length() - 1)

    def same_head(rows, row_w, cols, col_w):
        r = head_of(lax.broadcasted_iota(I32, (rows, cols), 0), row_w)
        c = head_of(lax.broadcasted_iota(I32, (rows, cols), 1), col_w)
        return r == c

    keep_kb = same_head(hc, cs, hk, B_DK)
    keep_vb = same_head(hc, cs, hv, B_DV)
    keep_st = same_head(hv, B_DV, hk, B_DK)
    t_idx = lax.broadcasted_iota(I32, (cs, hc), 0)
    s_idx = lax.broadcasted_iota(I32, (cs, hc), 1) & (cs - 1)
    keep_a = t_idx >= s_idx
    r = lax.broadcasted_iota(I32, (cs, cs), 0)
    c = lax.broadcasted_iota(I32, (cs, cs), 1)
    tri = jnp.where(r >= c, 1.0, 0.0)
    go = go_ref[...]

    @pl.when(j == 0)
    def _():
        for bi in range(bb):
            blocks = []
            for h in range(B_HEADS):
                parts = [jnp.zeros((B_DV, B_DK), F32)] * B_HEADS
                parts[h] = s0_ref[bi, h]
                blocks.append(jnp.concatenate(parts, axis=1))
            st_scr[bi] = jnp.concatenate(blocks, axis=0)

    inst = [(ci, bi) for ci in range(nc) for bi in range(bb)]
    sls = {ci: slice(ci * cs, (ci + 1) * cs) for ci in range(nc)}
    bcum = {t: jnp.dot(tri, g_ref[t[1], sls[t[0]], :], preferred_element_type=F32, precision=HI) for t in inst}
    n_sub = cs // GLA_SUB
    v_bf, qdec, kdec, qts, kts = {}, {}, {}, {}, {}
    for t in inst:
        ci, bi = t
        q = q_ref[bi, sls[ci], :]
        k = k_ref[bi, sls[ci], :]
        bc = bcum[t]
        blast = bc[cs - 1:cs, :]
        qdec[t] = (q * jnp.exp(bc)).astype(BF16)
        kdec[t] = (k * jnp.exp(blast - bc)).astype(BF16)
        v_bf[t] = v_ref[bi, sls[ci], :].astype(BF16)
        for i in range(n_sub):
            rs = slice(i * GLA_SUB, (i + 1) * GLA_SUB)
            ref = bc[i * GLA_SUB:i * GLA_SUB + 1, :]
            qts[t, i] = (q[rs, :] * jnp.exp(bc[rs, :] - ref)).astype(BF16)
            kt = (k * jnp.exp(jnp.minimum(ref - bc, GLA_EXP_CLAMP))).astype(BF16)
            kts[t, i] = jnp.where(keep_kb, jnp.concatenate([kt] * B_HEADS, axis=0), 0.0)
    a_rows = {(t, i): _nt(qts[t, i], kts[t, i]) for t in inst for i in range(n_sub)}
    a_mat = {t: jnp.where(keep_a, jnp.concatenate([a_rows[t, i] for i in range(n_sub)], axis=0), 0.0).astype(BF16)
             for t in inst}
    o_intra = {t: jnp.dot(a_mat[t], jnp.where(keep_vb, jnp.concatenate([v_bf[t]] * B_HEADS, axis=0), 0.0),
                          preferred_element_type=F32) for t in inst}
    kv = {t: jnp.where(keep_st, _tn(v_bf[t], kdec[t]), 0.0) for t in inst}
    for t in inst:
        ci, bi = t
        st = st_scr[bi]
        o = o_intra[t] + _nt(qdec[t], st.astype(BF16))
        st_scr[bi] = st * jnp.exp(bcum[t][cs - 1:cs, :]) + kv[t]
        rg = rg_ref[bi, sls[ci], :]
        for h in range(B_HEADS):
            vs = slice(h * B_DV, (h + 1) * B_DV)
            ob_ref[bi, sls[ci], vs] = (_rms(o[:, vs]) * go * _silu(rg[:, vs])).astype(BF16)

    @pl.when(j == pl.num_programs(1) - 1)
    def _():
        for bi in range(bb):
            st = st_scr[bi]
            for h in range(B_HEADS):
                st_ref[bi, h] = st[h * B_DV:(h + 1) * B_DV, h * B_DK:(h + 1) * B_DK]


def _gla(qg, kg, vg, gg, rg, state_t, g_out, *, cs, nc, bb):
    b, t, _ = qg.shape
    assert b % bb == 0
    tt = cs * nc
    kern = functools.partial(_gla_kernel, cs=cs, nc=nc, bb=bb)

    def tok(w):
        return pl.BlockSpec((bb, tt, w), lambda i, j: (i, j, 0))

    st_spec = pl.BlockSpec((bb, B_HEADS, B_DV, B_DK), lambda i, j: (i, 0, 0, 0))
    return pl.pallas_call(
        kern,
        grid=(b // bb, t // tt),
        in_specs=[tok(256), tok(256), tok(512), tok(256), tok(512), st_spec,
                  pl.BlockSpec((1, B_DV), lambda i, j: (0, 0))],
        out_specs=[tok(512), st_spec],
        out_shape=[jax.ShapeDtypeStruct((b, t, 512), BF16),
                   jax.ShapeDtypeStruct((b, B_HEADS, B_DV, B_DK), F32)],
        scratch_shapes=[pltpu.VMEM((bb, B_HEADS * B_DV, B_HEADS * B_DK), F32)],
        name="gla",
        compiler_params=_params("arbitrary", "arbitrary"),
    )(qg, kg, vg, gg, rg, state_t, g_out)


def _merge_kernel(oa_ref, ob_ref, x_ref, mod_ref, wo_ref, gf_ref, wr_ref, br_ref, cnt0_ref,
                  x1_ref, h2_ref, ri_ref, rgate_ref, cnt_ref, carry_scr):
    @pl.when((pl.program_id(0) == 0) & (pl.program_id(1) == 0))
    def _():
        carry_scr[...] = cnt0_ref[...]

    mod = mod_ref[0]
    cat = jnp.concatenate([oa_ref[0, 0], ob_ref[0]], axis=1)
    x1 = x_ref[0] + mod[2:3] * jnp.dot(cat, wo_ref[...], preferred_element_type=F32)
    x1_ref[0] = x1
    h2 = _rms(x1) * gf_ref[...] * (1.0 + mod[4:5]) + mod[3:4]
    _store_row_tiles(h2_ref.at[0], h2)
    tm = x1.shape[0]
    lane = lax.broadcasted_iota(I32, (tm, LANES), 1).astype(F32)
    hi = h2.astype(BF16)
    lo = (h2 - hi.astype(F32)).astype(BF16)
    wr = wr_ref[...]
    a = jnp.dot(hi, wr, preferred_element_type=F32)
    left = (a[:, :LANES] + a[:, LANES:]) + jnp.dot(lo, wr[:, :LANES], preferred_element_type=F32) + br_ref[...]
    idx, val = [], []
    for _ in range(TOP_K):
        m = jnp.max(left, axis=1, keepdims=True)
        e = jnp.min(jnp.where(left == m, lane, float(LANES)), axis=1, keepdims=True)
        idx.append(e)
        val.append(m)
        left = jnp.where(lane == e, -jnp.inf, left)
    ex = [jnp.exp(vv - val[0]) for vv in val]
    den = ex[0] + ex[1] + ex[2] + ex[3]
    onehot = jnp.zeros((tm, LANES), F32)
    for e in idx:
        onehot = onehot + jnp.where(lane == e, 1.0, 0.0)
    r = lax.broadcasted_iota(I32, (tm, tm), 0)
    c = lax.broadcasted_iota(I32, (tm, tm), 1)
    earlier = jnp.where(r > c, 1.0, 0.0).astype(BF16)
    before = jnp.dot(earlier, onehot.astype(BF16), preferred_element_type=F32) + carry_scr[...]
    ri = jnp.zeros((tm, LANES), F32)
    rgate = jnp.zeros((tm, LANES), F32)
    for kq in range(TOP_K):
        rank = jnp.sum(jnp.where(lane == idx[kq], before, 0.0), axis=1, keepdims=True)
        ri = jnp.where(lane == kq, idx[kq], ri)
        ri = jnp.where(lane == TOP_K + kq, rank, ri)
        rgate = jnp.where(lane == kq, ex[kq] / den, rgate)
    ri_ref[0] = ri.astype(I32)
    rgate_ref[0] = rgate
    carry_scr[...] = carry_scr[...] + jnp.sum(onehot, axis=0, keepdims=True)
    cnt_ref[...] = carry_scr[...]


def _merge(oa, ob, x, mod, w_out, g_ffn, wr_pad, br_pad, cnt0, tm):
    b, t, d = x.shape

    def tok(w):
        return pl.BlockSpec((1, tm, w), lambda i, j: (i, j, 0))

    def const(s):
        return pl.BlockSpec(s, lambda i, j: (0, 0))

    return pl.pallas_call(
        _merge_kernel,
        grid=(b, t // tm),
        in_specs=[pl.BlockSpec((1, 1, tm, 512), lambda i, j: (j, i, 0, 0)),
                  tok(512), tok(d), pl.BlockSpec((1, 6, d), lambda i, j: (i, 0, 0)),
                  const((d, d)), const((1, d)), const((d, 2 * LANES)), const((1, LANES)), const((1, LANES))],
        out_specs=[tok(d), pl.BlockSpec((1, tm * SUBLANES, LANES), lambda i, j: (i, j, 0)),
                   tok(LANES), tok(LANES), const((1, LANES))],
        out_shape=[jax.ShapeDtypeStruct((b, t, d), F32), jax.ShapeDtypeStruct((b, t * SUBLANES, LANES), F32),
                   jax.ShapeDtypeStruct((b, t, LANES), I32), jax.ShapeDtypeStruct((b, t, LANES), F32),
                   jax.ShapeDtypeStruct((1, LANES), F32)],
        scratch_shapes=[pltpu.VMEM((1, LANES), F32)],
        name="merge",
        compiler_params=_params("arbitrary", "arbitrary"),
    )(oa, ob, x, mod, w_out, g_ffn, wr_pad, br_pad, cnt0)


def _dispatch_kernel(pe_ref, hp_ref, dest_ref, h_ref, *rest, tm, zero_init):
    if zero_init:
        xs_ref, zbuf, sem, zsem = rest

        @pl.when(pl.program_id(0) == 0)
        def _():
            zbuf[...] = jnp.zeros_like(zbuf)

            def zero_copy(e):
                start = pl.multiple_of(pe_ref[e] - SLOT_BLOCK, SLOT_BLOCK)
                return pltpu.make_async_copy(zbuf, xs_ref.at[pl.ds(start, SLOT_BLOCK)], zsem)

            for e in range(N_EXPERTS):
                @pl.when(hp_ref[e] > 0)
                def _():
                    zero_copy(e).start()
            for e in range(N_EXPERTS):
                @pl.when(hp_ref[e] > 0)
                def _():
                    zero_copy(e).wait()
    else:
        _, xs_ref, sem = rest

    def row_copy(r, slot):
        return pltpu.make_async_copy(h_ref.at[r], xs_ref.at[slot], sem)

    def issue(i, carry):
        for j in range(ROW_UNROLL):
            for kq in range(TOP_K):
                row_copy(i * ROW_UNROLL + j, dest_ref[(i * ROW_UNROLL + j) * TOP_K + kq]).start(priority=kq % 2)
        return carry

    lax.fori_loop(0, tm // ROW_UNROLL, issue, 0)

    def drain(i, carry):
        for _ in range(ROW_UNROLL * TOP_K):
            row_copy(0, 0).wait()
        return carry

    lax.fori_loop(0, tm // ROW_UNROLL, drain, 0)


def _dispatch(pad_end, has_pad, dest_flat, h2_tiles, xs, n_slots, tm):
    n = h2_tiles.shape[0]
    tile = h2_tiles.shape[1:]
    zero_init = xs is None
    kern = functools.partial(_dispatch_kernel, tm=tm, zero_init=zero_init)
    in_specs = [pl.BlockSpec((tm * TOP_K,), lambda i, pe, hp: (i,), memory_space=pltpu.SMEM),
                pl.BlockSpec((tm,) + tile, lambda i, pe, hp: (i, 0, 0))]
    args = [pad_end, has_pad, dest_flat, h2_tiles]
    scratch = [pltpu.SemaphoreType.DMA(())]
    aliases = {}
    if zero_init:
        scratch = [pltpu.VMEM((SLOT_BLOCK,) + tile, F32), pltpu.SemaphoreType.DMA(()), pltpu.SemaphoreType.DMA(())]
    else:
        in_specs.append(pl.BlockSpec(memory_space=pl.ANY))
        args.append(xs)
        aliases = {4: 0}
    grid_spec = pltpu.PrefetchScalarGridSpec(
        num_scalar_prefetch=2,
        grid=(n // tm,),
        in_specs=in_specs,
        out_specs=pl.BlockSpec(memory_space=pl.ANY),
        scratch_shapes=scratch,
    )
    return pl.pallas_call(
        kern,
        grid_spec=grid_spec,
        out_shape=jax.ShapeDtypeStruct((n_slots,) + tile, F32),
        input_output_aliases=aliases,
        name="dispatch",
        compiler_params=_params("arbitrary"),
    )(*args)


def _expert_kernel(be_ref, nu_ref, xs_ref, wgu_ref, bgu_ref, wdn_ref, bdn_ref, ys_ref):
    @pl.when(pl.program_id(0) < nu_ref[0])
    def _():
        f = wdn_ref.shape[1]
        x = _load_row_tiles(xs_ref, SLOT_BLOCK).astype(BF16)
        gu = jnp.dot(x, wgu_ref[0], preferred_element_type=F32) + bgu_ref[0]
        gate = jnp.minimum(gu[:, :f], SWIGLU_LIMIT)
        up = jnp.clip(gu[:, f:], -SWIGLU_LIMIT, SWIGLU_LIMIT)
        glu = gate / (1.0 + jnp.exp(-SWIGLU_ALPHA * gate))
        act = ((up + 1.0) * glu).astype(BF16)
        _store_row_tiles(ys_ref, jnp.dot(act, wdn_ref[0], preferred_element_type=F32) + bdn_ref[0])


def _experts(block_expert, n_used, xs, wgu, bgu, wdn, bdn):
    n_slots = xs.shape[0] // SUBLANES
    nb = n_slots // SLOT_BLOCK
    f, d = wdn.shape[1:]
    rows = SLOT_BLOCK * SUBLANES

    def blk(i, be, nu):
        return (jnp.minimum(i, nu[0] - 1), 0)

    def wsel(i, be, nu):
        return (be[i], 0, 0)

    grid_spec = pltpu.PrefetchScalarGridSpec(
        num_scalar_prefetch=2,
        grid=(nb,),
        in_specs=[pl.BlockSpec((rows, LANES), blk),
                  pl.BlockSpec((1, d, 2 * f), wsel),
                  pl.BlockSpec((1, 1, 2 * f), wsel),
                  pl.BlockSpec((1, f, d), wsel),
                  pl.BlockSpec((1, 1, d), wsel)],
        out_specs=pl.BlockSpec((rows, LANES), blk),
    )
    return pl.pallas_call(
        _expert_kernel,
        grid_spec=grid_spec,
        out_shape=jax.ShapeDtypeStruct(xs.shape, F32),
        name="experts",
        compiler_params=_params("arbitrary"),
    )(block_expert, n_used, xs, wgu, bgu, wdn, bdn)


def _combine_kernel(dest_ref, dnext_ref, x1_ref, gate_ref, mod_ref, modf_ref, gfin_ref, ys_ref, y_ref,
                    buf, sem, *, tm):
    nt = pl.num_programs(1)
    step = pl.program_id(0) * nt + pl.program_id(1)
    n_steps = pl.num_programs(0) * nt
    half = TOP_K * tm * SUBLANES
    cur = lax.rem(step, 2)

    def row_copy(which, kq, r, slot):
        off = pl.multiple_of(which * half + (kq * tm + r) * SUBLANES, SUBLANES)
        return pltpu.make_async_copy(ys_ref.at[slot], buf.at[pl.ds(off, SUBLANES), :], sem.at[which])

    def gather(slots_ref, which):
        def issue(i, carry):
            for j in range(ROW_UNROLL):
                for kq in range(TOP_K):
                    r = i * ROW_UNROLL + j
                    row_copy(which, kq, r, slots_ref[r * TOP_K + kq]).start(priority=kq % 2)
            return carry

        lax.fori_loop(0, tm // ROW_UNROLL, issue, 0)

    @pl.when(step == 0)
    def _():
        gather(dest_ref, cur)

    @pl.when(step + 1 < n_steps)
    def _():
        gather(dnext_ref, 1 - cur)

    def drain(i, carry):
        for _ in range(ROW_UNROLL * TOP_K):
            row_copy(cur, 0, 0, 0).wait()
        return carry

    lax.fori_loop(0, tm // ROW_UNROLL, drain, 0)
    gates = gate_ref[0]
    moe = None
    for kq in range(TOP_K):
        start = pl.multiple_of(cur * half + kq * tm * SUBLANES, SUBLANES)
        rows = _load_row_tiles(buf.at[pl.ds(start, tm * SUBLANES), :], tm)
        term = gates[:, kq:kq + 1] * rows
        moe = term if moe is None else moe + term
    xo = x1_ref[0] + mod_ref[0][5:6] * moe
    modf = modf_ref[0]
    y_ref[0] = _rms(xo) * gfin_ref[...] * (1.0 + modf[1:2]) + modf[0:1]


def _combine(dest_flat, x1, gates, mod, modf, g_final, ys, tm):
    b, t, d = x1.shape
    nt = t // tm
    last = b * nt - 1
    kern = functools.partial(_combine_kernel, tm=tm)
    return pl.pallas_call(
        kern,
        grid=(b, nt),
        in_specs=[pl.BlockSpec((tm * TOP_K,), lambda i, j: (i * nt + j,), memory_space=pltpu.SMEM),
                  pl.BlockSpec((tm * TOP_K,), lambda i, j: (jnp.minimum(i * nt + j + 1, last),),
                               memory_space=pltpu.SMEM),
                  pl.BlockSpec((1, tm, d), lambda i, j: (i, j, 0)),
                  pl.BlockSpec((1, tm, LANES), lambda i, j: (i, j, 0)),
                  pl.BlockSpec((1, 6, d), lambda i, j: (i, 0, 0)),
                  pl.BlockSpec((1, 2, d), lambda i, j: (i, 0, 0)),
                  pl.BlockSpec((1, d), lambda i, j: (0, 0)),
                  pl.BlockSpec(memory_space=pl.ANY)],
        out_specs=pl.BlockSpec((1, tm, d), lambda i, j: (i, j, 0)),
        out_shape=jax.ShapeDtypeStruct((b, t, d), F32),
        scratch_shapes=[pltpu.VMEM((2 * TOP_K * tm * SUBLANES, LANES), F32), pltpu.SemaphoreType.DMA((2,))],
        name="combine",
        compiler_params=_params("arbitrary", "arbitrary"),
    )(dest_flat, dest_flat, x1, gates, mod, modf, g_final, ys)


def _rope_tables(pos):
    half = A_HD // 2
    inv_freq = ROPE_THETA ** (-jnp.arange(half, dtype=F32) / half)
    ang = pos.astype(F32)[:, None] * inv_freq[None, :]
    cos = jnp.cos(ang)
    sin = jnp.sin(ang)
    reps = LANES // A_HD
    return jnp.tile(jnp.concatenate([cos, cos], axis=1), (1, reps)), jnp.tile(jnp.concatenate([-sin, sin], axis=1), (1, reps))


def _reorder_w_in(w_in):
    offs = [0]
    for w in (512, 128, 128, 256, 64, 4, 256, 256, 512, 16, 512):
        offs.append(offs[-1] + w)
    seg = [w_in[:, offs[i]:offs[i + 1]] for i in range(11)]
    qa, ka, va, qi, ki, wi, qg, kg, vg, lr, rg = seg
    pad = jnp.zeros((w_in.shape[0], LANES - IDX_DIM - IDX_HEADS - GATE_RANK), w_in.dtype)
    return jnp.concatenate([qa, ka, va, qi, ki, wi, lr, pad, qg, kg, vg, rg], axis=1).astype(BF16)


def kernel(x_prompt, x_sample, cache_k, cache_v, cache_kidx, state_gla, c_prompt, c_sample,
           w_mod, b_mod, g_mix, g_ffn, w_in, gla_w_gate, gla_b_gate, gla_g_out, w_out,
           w_router, b_router, w_gate_up, b_gate_up, w_down, b_down,
           w_mod_final, b_mod_final, g_final):
    depth = w_in.shape[0]
    assert depth == 1
    bp, sp, d = x_prompt.shape
    bs, ts, _ = x_sample.shape
    past = cache_k.shape[2]
    n_exp = w_router.shape[2]
    f = w_down.shape[2]

    c_all = jnp.concatenate([c_prompt, c_sample], axis=0)
    w_r = _reorder_w_in(w_in[0])
    wg_pad = jnp.zeros((LANES, B_HEADS * B_DK), F32).at[MISC_LR:MISC_LR + GATE_RANK].set(gla_w_gate[0])
    bg = gla_b_gate[0].reshape(1, -1)
    wr_pad = jnp.zeros((d, LANES), F32).at[:, :n_exp].set(w_router[0])
    wr_hi = wr_pad.astype(BF16)
    wr_cat = jnp.concatenate([wr_hi, (wr_pad - wr_hi.astype(F32)).astype(BF16)], axis=1)
    br_pad = jnp.full((1, LANES), NEG_BIG, F32).at[0, :n_exp].set(b_router[0])
    w_out_b = w_out[0].astype(BF16)
    wgu_b = w_gate_up[0].astype(BF16)
    wdn_b = w_down[0].astype(BF16)
    bgu = b_gate_up[0].reshape(n_exp, 1, 2 * f)
    bdn = b_down[0].reshape(n_exp, 1, d)
    g_out = gla_g_out[0].reshape(1, B_DV)

    mod_all = _adaln(c_all, w_mod[0], b_mod[0]).reshape(bp + bs, 6, d)
    modf_all = _adaln(c_all, w_mod_final, b_mod_final).reshape(bp + bs, 2, d)
    mod_p, mod_s = mod_all[:bp], mod_all[bp:]
    modf_p, modf_s = modf_all[:bp], modf_all[bp:]

    def mixer(x, mod, pos, tm, past_kv, state_t, cs, nc, tq):
        cos_t, sin_t = _rope_tables(pos)
        qa, ka, va, qi, ki, misc, qg, kg, vg, gg, rg, kb, vx, kib = _premix(
            x, mod, g_mix[0].reshape(1, d), w_r, wg_pad, bg, cos_t, sin_t, tm)
        if past_kv is None:
            oa = _attn_causal(qa, qi, misc, kb, vx, kib, tq)
        else:
            n_past = past_kv[0].shape[2]
            assert tq == x.shape[1] and n_past % LANES == 0
            oa = _attn_call(qa, qi, misc, ka, va, ki, past_kv, tq=tq, q_off=0, n_tiles=1,
                            n_ctx=n_past + LANES, causal=False, n_keys=n_past + ka.shape[1], n_groups=1)[None]
        ob, st = _gla(qg, kg, vg, gg, rg, state_t, g_out, cs=cs, nc=nc, bb=GLA_BATCH)
        return oa, ob, (ka, va, ki, st)

    state0_p = jnp.zeros((bp, B_HEADS, B_DV, B_DK), F32)
    oa_p, ob_p, (ka_p, va_p, ki_p, st_p) = mixer(
        x_prompt, mod_p, jnp.arange(sp), min(256, sp), None, state0_p, CHUNK, min(GLA_CHUNKS_PER_STEP, sp // CHUNK),
        min(256, sp))
    state0_s = jnp.swapaxes(state_gla[0], -1, -2)
    past_kv = (jnp.transpose(cache_k[0], (0, 2, 3, 1)).reshape(bs, A_KV * A_HD, past),
               jnp.transpose(cache_v[0], (0, 2, 3, 1)).reshape(bs, A_KV * A_HD, past),
               jnp.swapaxes(cache_kidx[0], 1, 2))
    oa_s, ob_s, (ka_s, va_s, ki_s, st_s) = mixer(
        x_sample, mod_s, past + jnp.arange(ts), ts, past_kv, state0_s, ts, 1, ts)

    cnt0 = jnp.zeros((1, LANES), F32)
    x1_p, h2_p, ri_p, gate_p, cnt1 = _merge(oa_p, ob_p, x_prompt, mod_p, w_out_b, g_ffn[0].reshape(1, d),
                                            wr_cat, br_pad, cnt0, min(256, sp))
    x1_s, h2_s, ri_s, gate_s, cnt2 = _merge(oa_s, ob_s, x_sample, mod_s, w_out_b, g_ffn[0].reshape(1, d),
                                            wr_cat, br_pad, cnt1, ts)
    counts = cnt2[0, :n_exp].astype(I32)
    padded = (counts + SLOT_BLOCK - 1) // SLOT_BLOCK * SLOT_BLOCK
    pad_end = jnp.cumsum(padded)
    pad_start = pad_end - padded
    n_asg = (bp * sp + bs * ts) * TOP_K
    nb = -(-n_asg // SLOT_BLOCK) + n_exp
    n_slots = nb * SLOT_BLOCK
    block_start = jnp.arange(nb, dtype=I32) * SLOT_BLOCK
    block_expert = jnp.minimum(jnp.sum((pad_end[None, :] <= block_start[:, None]).astype(I32), axis=1), n_exp - 1)
    n_used = (pad_end[-1:] // SLOT_BLOCK).astype(I32)
    has_pad = (padded > counts).astype(I32)
    expert_ids = jnp.arange(n_exp, dtype=I32)

    def dests(ri):
        e, rank = ri[..., :TOP_K], ri[..., TOP_K:2 * TOP_K]
        start = jnp.sum(jnp.where(e[..., None] == expert_ids, pad_start.astype(I32), 0), axis=-1)
        return (start + rank).reshape(-1)

    dest_p, dest_s = dests(ri_p), dests(ri_s)
    pad_end_i = pad_end.astype(I32)
    tile = (SUBLANES, LANES)
    xs = _dispatch(pad_end_i, has_pad, dest_p, h2_p.reshape((bp * sp,) + tile), None, n_slots, min(256, sp))
    xs = _dispatch(pad_end_i, has_pad, dest_s, h2_s.reshape((bs * ts,) + tile), xs, n_slots, min(256, bs * ts))
    ys = _experts(block_expert, n_used, xs.reshape(n_slots * SUBLANES, LANES), wgu_b, bgu, wdn_b, bdn)
    ys = ys.reshape((n_slots,) + tile)
    y_p = _combine(dest_p, x1_p, gate_p, mod_p, modf_p, g_final.reshape(1, d), ys, min(256, sp))
    y_s = _combine(dest_s, x1_s, gate_s, mod_s, modf_s, g_final.reshape(1, d), ys, ts)

    def kv(a, b, t):
        return a.reshape(1, b, t, A_KV, A_HD)

    return (y_p, y_s,
            kv(ka_p, bp, sp), kv(va_p, bp, sp), ki_p[None], jnp.swapaxes(st_p, -1, -2)[None],
            kv(ka_s, bs, ts), kv(va_s, bs, ts), ki_s[None], jnp.swapaxes(st_s, -1, -2)[None])
```

```python
import functools

import jax
import jax.numpy as jnp
from jax import lax
from jax.experimental import pallas as pl
from jax.experimental.pallas import tpu as pltpu

F32 = jnp.float32
BF16 = jnp.bfloat16
I32 = jnp.int32
HI = lax.Precision.HIGHEST

CHUNK = 64
CHUNK_SHIFT = 6
EPS = 1e-6
ROPE_THETA = 10000.0
A_HD = 64
A_HEADS = 8
A_KV = 2
IDX_HEADS = 4
IDX_DIM = 64
INDEX_TOPK = 256
B_HEADS = 4
B_DK = 64
B_DV = 128
GATE_RANK = 16
GATE_TAU = 16.0
N_EXPERTS = 32
TOP_K = 4
SWIGLU_LIMIT = 7.0
SWIGLU_ALPHA = 1.702

LANES = 128
SUBLANES = 8
ROW_UNROLL = 8
GLA_SUB = 16
GLA_BATCH = 2
GLA_CHUNKS_PER_STEP = 8
GLA_EXP_CLAMP = 80.0
SLOT_BLOCK = 512
VMEM_LIMIT = 56 * 1024 * 1024
INT_MIN = -2147483648
NEG_BIG = -1e30
SEARCH_TWO_BIT_MAX_KEYS = 768

C_QA, C_KA, C_VA, C_QI, C_MISC, C_QG, C_KG, C_VG, C_RG, C_END = 0, 512, 640, 768, 1024, 1152, 1408, 1664, 2176, 2688
MISC_WI = 64
MISC_LR = 68


def _params(*sem):
    return pltpu.CompilerParams(dimension_semantics=sem, vmem_limit_bytes=VMEM_LIMIT)


def _nt(a, b):
    return lax.dot_general(a, b, (((1,), (1,)), ((), ())), preferred_element_type=F32)


def _tn(a, b):
    return lax.dot_general(a, b, (((0,), (0,)), ((), ())), preferred_element_type=F32)


def _rms(x):
    return x * lax.rsqrt(jnp.mean(x * x, axis=-1, keepdims=True) + EPS)


def _silu(x):
    return x / (1.0 + jnp.exp(-x))


def _store_row_tiles(ref, val):
    rows, width = val.shape
    assert width == SUBLANES * LANES
    for s in range(SUBLANES):
        ref[pl.ds(s, rows, stride=SUBLANES), :] = val[:, s * LANES:(s + 1) * LANES]


def _load_row_tiles(ref, rows):
    return jnp.concatenate([ref[pl.ds(s, rows, stride=SUBLANES), :] for s in range(SUBLANES)], axis=1)


def _value_with_ones(vb):
    return jnp.concatenate([vb, jnp.ones_like(vb)], axis=-1)


def _adaln_kernel(c_ref, w_ref, b_ref, o_ref):
    a = _silu(c_ref[...])
    o_ref[...] = jnp.dot(a, w_ref[...], preferred_element_type=F32, precision=HI) + b_ref[...]


def _adaln(c, w, b):
    r, d = c.shape
    n = w.shape[1]
    tn = 512
    return pl.pallas_call(
        _adaln_kernel,
        grid=(n // tn,),
        in_specs=[pl.BlockSpec((r, d), lambda j: (0, 0)),
                  pl.BlockSpec((d, tn), lambda j: (0, j)),
                  pl.BlockSpec((1, tn), lambda j: (0, j))],
        out_specs=pl.BlockSpec((r, tn), lambda j: (0, j)),
        out_shape=jax.ShapeDtypeStruct((r, n), F32),
        name="adaln",
        compiler_params=_params("arbitrary"),
    )(c, w, b.reshape(1, n))


def _premix_kernel(x_ref, mod_ref, g_ref, w_ref, wg_ref, bg_ref, cos_ref, sin_ref,
                   qa_ref, ka_ref, va_ref, qi_ref, ki_ref, misc_ref, qg_ref, kg_ref, vg_ref, gg_ref, rg_ref,
                   kb_ref, vx_ref, kib_ref):
    x = x_ref[0]
    mod = mod_ref[0]
    hb = (_rms(x) * g_ref[...] * (1.0 + mod[1:2]) + mod[0:1]).astype(BF16)
    tm = x.shape[0]

    def project(c0, c1):
        return jnp.dot(hb, w_ref[:, c0:c1], preferred_element_type=F32)

    cos = cos_ref[...]
    sin = sin_ref[...]
    lane = lax.broadcasted_iota(I32, (tm, LANES), 1)
    lower_half = (lane & (A_HD - 1)) < (A_HD // 2)

    def rope(xc):
        rot = jnp.where(lower_half, pltpu.roll(xc, LANES - A_HD // 2, 1), pltpu.roll(xc, A_HD // 2, 1))
        return xc * cos + rot * sin

    seg = project(C_QA, C_KA)
    for j in range((C_KA - C_QA) // LANES):
        qa_ref[0, :, j * LANES:(j + 1) * LANES] = (rope(seg[:, j * LANES:(j + 1) * LANES]) * (A_HD ** -0.5)).astype(BF16)
    seg = project(C_KA, C_QI)
    ka = rope(seg[:, :LANES])
    va = seg[:, LANES:]
    ka_ref[0] = ka
    va_ref[0] = va
    kb_ref[0] = ka.astype(BF16)
    vx_ref[0] = _value_with_ones(va.astype(BF16))
    seg = project(C_QI, C_QG)
    for j in range((C_MISC - C_QI) // LANES):
        qi_ref[0, :, j * LANES:(j + 1) * LANES] = rope(seg[:, j * LANES:(j + 1) * LANES]).astype(BF16)
    m = seg[:, C_MISC - C_QI:]
    mr = rope(m)
    ki_ref[0] = mr[:, :IDX_DIM]
    kib_ref[0] = mr[:, :IDX_DIM].astype(BF16)
    misc_ref[0] = jnp.where(lane < IDX_DIM, mr, m * (IDX_HEADS ** -0.5))
    xg = jnp.dot(m, wg_ref[...], preferred_element_type=F32, precision=HI) + bg_ref[...]
    gg_ref[0] = (jnp.minimum(xg, 0.0) - jnp.log(1.0 + jnp.exp(-jnp.abs(xg)))) * (1.0 / GATE_TAU)
    qg_ref[0] = project(C_QG, C_KG) * (B_DK ** -0.5)
    kg_ref[0] = project(C_KG, C_VG)
    vg_ref[0] = project(C_VG, C_RG)
    rg_ref[0] = project(C_RG, C_END)


def _premix(x, mod, g_mix, w_r, wg_pad, bg, cos_t, sin_t, tm):
    b, t, d = x.shape
    widths = [(512, BF16), (128, F32), (128, F32), (256, BF16), (64, F32), (128, F32),
              (256, F32), (256, F32), (512, F32), (256, F32), (512, F32),
              (128, BF16), (2 * LANES, BF16), (IDX_DIM, BF16)]
    return pl.pallas_call(
        _premix_kernel,
        grid=(b, t // tm),
        in_specs=[pl.BlockSpec((1, tm, d), lambda i, j: (i, j, 0)),
                  pl.BlockSpec((1, 6, d), lambda i, j: (i, 0, 0)),
                  pl.BlockSpec((1, d), lambda i, j: (0, 0)),
                  pl.BlockSpec((d, C_END), lambda i, j: (0, 0)),
                  pl.BlockSpec((LANES, 256), lambda i, j: (0, 0)),
                  pl.BlockSpec((1, 256), lambda i, j: (0, 0)),
                  pl.BlockSpec((tm, LANES), lambda i, j: (j, 0)),
                  pl.BlockSpec((tm, LANES), lambda i, j: (j, 0))],
        out_specs=[pl.BlockSpec((1, tm, w), lambda i, j: (i, j, 0)) for w, _ in widths],
        out_shape=[jax.ShapeDtypeStruct((b, t, w), dt) for w, dt in widths],
        name="premix",
        compiler_params=_params("arbitrary", "arbitrary"),
    )(x, mod, g_mix, w_r, wg_pad, bg, cos_t, sin_t)


def _attn_kernel(*refs, tq, n_ctx, top_k, causal, n_keys, q_off, n_groups, n_past, keep_all):
    if n_past:
        q_ref, qi_ref, misc_ref, k_ref, v_ref, ki_ref, pk_ref, pv_ref, pki_ref, o_ref = refs
        assert n_ctx - n_past == LANES and k_ref.shape[1] <= LANES

        def new_columns(rows_ref):
            new = rows_ref[0]
            n_new, width = new.shape
            if width < LANES:
                new = jnp.concatenate([new, jnp.zeros((n_new, LANES - width), F32)], axis=1)
            square = jnp.concatenate([new, jnp.zeros((LANES - n_new, LANES), F32)], axis=0)
            return square.T[:width].astype(BF16)

        k = jnp.concatenate([pk_ref[0].astype(BF16), new_columns(k_ref)], axis=1)
        v = jnp.concatenate([pv_ref[0].astype(BF16), new_columns(v_ref)], axis=1)
        vx = jnp.concatenate([v, jnp.ones_like(v)], axis=0)
        kib = jnp.concatenate([pki_ref[0].astype(BF16), new_columns(ki_ref)], axis=1)
    else:
        q_ref, qi_ref, misc_ref, k_ref, vx_ref, ki_ref, o_ref = refs
        k = k_ref[0]
        vx = vx_ref[0]
        kib = ki_ref[0]
    feature_major = bool(n_past)
    rows = tq // n_groups
    row0 = q_off + pl.program_id(1) * tq
    keys, kks = [], []
    idx_dots = {}
    for g in range(n_groups):
        qi = qi_ref[0, g * rows:(g + 1) * rows, :]
        for h in range(IDX_HEADS):
            qh = qi[:, h * IDX_DIM:(h + 1) * IDX_DIM]
            idx_dots[g, h] = jnp.dot(qh, kib, preferred_element_type=F32) if feature_major else _nt(qh, kib)
    for g in range(n_groups):
        misc = misc_ref[0, g * rows:(g + 1) * rows, :]
        isc = jnp.zeros((rows, n_ctx), F32)
        for h in range(IDX_HEADS):
            isc = isc + misc[:, MISC_WI + h:MISC_WI + h + 1] * jnp.maximum(idx_dots[g, h], 0.0)
        kpos = lax.broadcasted_iota(I32, (rows, n_ctx), 1)
        if causal:
            row = lax.broadcasted_iota(I32, (rows, 1), 0) + (row0 + g * rows)
            key_lim = (lax.shift_right_logical(row, CHUNK_SHIFT) + 1) * CHUNK
        else:
            key_lim = jnp.full((rows, 1), n_keys, I32)
        bits = pltpu.bitcast(isc, I32)
        key = jnp.where(bits < 0, INT_MIN - bits, bits)
        keys.append(jnp.where(kpos < key_lim, key, INT_MIN))
        kks.append(jnp.minimum(key_lim, top_k).astype(F32))

    def count_at_least(g, cand):
        return jnp.sum(jnp.where(keys[g] >= (cand ^ INT_MIN), 1.0, 0.0), axis=1, keepdims=True)

    bits_per_pass = 2 if n_ctx <= SEARCH_TWO_BIT_MAX_KEYS else 1

    def search(i, ans):
        out = []
        for g in range(n_groups):
            if bits_per_pass == 1:
                cand = ans[g] | lax.shift_left(jnp.int32(1), 31 - i)
                out.append(jnp.where(count_at_least(g, cand) >= kks[g], cand, ans[g]))
            else:
                a1 = ans[g] | lax.shift_left(jnp.int32(1), 31 - 2 * i)
                a2 = ans[g] | lax.shift_left(jnp.int32(1), 30 - 2 * i)
                a3 = a1 | a2
                c1, c2, c3 = count_at_least(g, a1), count_at_least(g, a2), count_at_least(g, a3)
                kk = kks[g]
                out.append(jnp.where(c3 >= kk, a3, jnp.where(c1 >= kk, a1, jnp.where(c2 >= kk, a2, ans[g]))))
        return tuple(out)

    ans = tuple(jnp.zeros((rows, 1), I32) for _ in range(n_groups))
    if not keep_all:
        ans = lax.fori_loop(0, 32 // bits_per_pass, search, ans)
    r = lax.broadcasted_iota(I32, (LANES, LANES), 0)
    c = lax.broadcasted_iota(I32, (LANES, LANES), 1)
    upper = jnp.where(r < c, 1.0, 0.0).astype(BF16)
    n_blk = n_ctx // LANES
    thrs = [ans[g] ^ INT_MIN for g in range(n_groups)]
    ties = {(g, j): jnp.where(keys[g][:, j * LANES:(j + 1) * LANES] == thrs[g], 1.0, 0.0)
            for g in range(n_groups) for j in range(n_blk)}
    within = {gj: jnp.dot(t.astype(BF16), upper, preferred_element_type=F32) for gj, t in ties.items()}
    bias_groups = []
    for g in range(n_groups):
        gt = keys[g] > thrs[g]
        need = kks[g] - jnp.sum(jnp.where(gt, 1.0, 0.0), axis=1, keepdims=True)
        carry = jnp.zeros((rows, 1), F32)
        blocks = []
        for j in range(n_blk):
            sl = slice(j * LANES, (j + 1) * LANES)
            take_tie = jnp.where(within[g, j] + carry < need, ties[g, j], 0.0)
            blocks.append(jnp.where(gt[:, sl], 0.0, jnp.where(take_tie > 0.0, 0.0, NEG_BIG)))
            carry = carry + jnp.sum(ties[g, j], axis=1, keepdims=True)
        bias_groups.append(jnp.concatenate(blocks, axis=1))
    bias = jnp.concatenate(bias_groups, axis=0)

    q = q_ref[0]
    rep = A_HEADS // A_KV
    if feature_major and tq * A_HEADS <= 2 * LANES:
        q_groups = [jnp.concatenate([q[:, hh * A_HD:(hh + 1) * A_HD] for hh in range(g * rep, (g + 1) * rep)], axis=0)
                    for g in range(A_KV)]
        s_all = jnp.concatenate([jnp.dot(q_groups[g], k[g * A_HD:(g + 1) * A_HD, :], preferred_element_type=F32)
                                 for g in range(A_KV)], axis=0) + jnp.concatenate([bias] * A_HEADS, axis=0)
        p_all = jnp.exp((s_all - jnp.max(s_all, axis=1, keepdims=True)).astype(BF16))
        ox = _nt(p_all, vx)
        outs = [ox[hh * tq:(hh + 1) * tq, (hh // rep) * A_HD:(hh // rep + 1) * A_HD]
                / ox[hh * tq:(hh + 1) * tq, LANES:LANES + 1] for hh in range(A_HEADS)]
        o_ref[0] = jnp.concatenate(outs, axis=1).astype(BF16)
        return
    kgs = [k[g * A_HD:(g + 1) * A_HD, :] if feature_major else k[:, g * A_HD:(g + 1) * A_HD] for g in range(A_KV)]
    qs = [q[:, hh * A_HD:(hh + 1) * A_HD] for hh in range(A_HEADS)]
    ss = [(jnp.dot(qh, kgs[hh // rep], preferred_element_type=F32) if feature_major else _nt(qh, kgs[hh // rep]))
          + bias for hh, qh in enumerate(qs)]
    ps = [jnp.exp((s - jnp.max(s, axis=1, keepdims=True)).astype(BF16)) for s in ss]
    oxs = [_nt(p, vx) if feature_major else jnp.dot(p, vx, preferred_element_type=F32) for p in ps]
    outs = [ox[:, (hh // rep) * A_HD:(hh // rep + 1) * A_HD] / ox[:, LANES:LANES + 1] for hh, ox in enumerate(oxs)]
    o_ref[0] = jnp.concatenate(outs, axis=1).astype(BF16)


def _attn_call(q, qi, misc, kb, vx, kib, past, *, tq, q_off, n_tiles, n_ctx, causal, n_keys, n_groups):
    b = q.shape[0]
    t0 = q_off // tq
    top_k = min(INDEX_TOPK, n_keys // 4)
    n_past = 0 if past is None else past[0].shape[2]
    n_own = n_ctx if past is None else kb.shape[1]
    kern = functools.partial(_attn_kernel, tq=tq, n_ctx=n_ctx, top_k=top_k, causal=causal, n_keys=n_keys,
                             q_off=q_off, n_groups=n_groups, n_past=n_past,
                             keep_all=causal and q_off + n_tiles * tq <= top_k)

    def tok(w):
        return pl.BlockSpec((1, tq, w), lambda i, j: (i, t0 + j, 0))

    def ctx(rows, w):
        return pl.BlockSpec((1, rows, w), lambda i, j: (i, 0, 0))

    in_specs = [tok(512), tok(256), tok(LANES)] + [ctx(n_own, a.shape[2]) for a in (kb, vx, kib)]
    args = [q, qi, misc, kb, vx, kib]
    if past is not None:
        in_specs += [ctx(a.shape[1], n_past) for a in past]
        args += list(past)
    return pl.pallas_call(
        kern,
        grid=(b, n_tiles),
        in_specs=in_specs,
        out_specs=pl.BlockSpec((1, tq, 512), lambda i, j: (i, j, 0)),
        out_shape=jax.ShapeDtypeStruct((b, n_tiles * tq, 512), BF16),
        name="attn",
        compiler_params=_params("arbitrary", "arbitrary"),
    )(*args)


def _attn_causal(q, qi, misc, kb, vx, kib, tq):
    t = q.shape[1]
    return jnp.stack([
        _attn_call(q, qi, misc, kb, vx, kib, None, tq=tq, q_off=c * tq, n_tiles=1, n_ctx=(c + 1) * tq,
                   causal=True, n_keys=t, n_groups=2)
        for c in range(t // tq)])


def _gla_kernel(q_ref, k_ref, v_ref, g_ref, rg_ref, s0_ref, go_ref, ob_ref, st_ref, st_scr, *, cs, nc, bb):
    j = pl.program_id(1)
    hk, hv, hc = B_HEADS * B_DK, B_HEADS * B_DV, B_HEADS * cs

    def head_of(idx, width):
        return lax.shift_right_logical(idx, width.bit_length() - 1)

    def same_head(rows, row_w, cols, col_w):
        r = head_of(lax.broadcasted_iota(I32, (rows, cols), 0), row_w)
        c = head_of(lax.broadcasted_iota(I32, (rows, cols), 1), col_w)
        return r == c

    keep_kb = same_head(hc, cs, hk, B_DK)
    keep_vb = same_head(hc, cs, hv, B_DV)
    keep_st = same_head(hv, B_DV, hk, B_DK)
    t_idx = lax.broadcasted_iota(I32, (cs, hc), 0)
    s_idx = lax.broadcasted_iota(I32, (cs, hc), 1) & (cs - 1)
    keep_a = t_idx >= s_idx
    r = lax.broadcasted_iota(I32, (cs, cs), 0)
    c = lax.broadcasted_iota(I32, (cs, cs), 1)
    tri = jnp.where(r >= c, 1.0, 0.0)
    go = go_ref[...]

    @pl.when(j == 0)
    def _():
        for bi in range(bb):
            blocks = []
            for h in range(B_HEADS):
                parts = [jnp.zeros((B_DV, B_DK), F32)] * B_HEADS
                parts[h] = s0_ref[bi, h]
                blocks.append(jnp.concatenate(parts, axis=1))
            st_scr[bi] = jnp.concatenate(blocks, axis=0)

    inst = [(ci, bi) for ci in range(nc) for bi in range(bb)]
    sls = {ci: slice(ci * cs, (ci + 1) * cs) for ci in range(nc)}
    bcum = {t: jnp.dot(tri, g_ref[t[1], sls[t[0]], :], preferred_element_type=F32, precision=HI) for t in inst}
    n_sub = cs // GLA_SUB
    v_bf, qdec, kdec, qts, kts = {}, {}, {}, {}, {}
    for t in inst:
        ci, bi = t
        q = q_ref[bi, sls[ci], :]
        k = k_ref[bi, sls[ci], :]
        bc = bcum[t]
        blast = bc[cs - 1:cs, :]
        qdec[t] = (q * jnp.exp(bc)).astype(BF16)
        kdec[t] = (k * jnp.exp(blast - bc)).astype(BF16)
        v_bf[t] = v_ref[bi, sls[ci], :].astype(BF16)
        for i in range(n_sub):
            rs = slice(i * GLA_SUB, (i + 1) * GLA_SUB)
            ref = bc[i * GLA_SUB:i * GLA_SUB + 1, :]
            qts[t, i] = (q[rs, :] * jnp.exp(bc[rs, :] - ref)).astype(BF16)
            kt = (k * jnp.exp(jnp.minimum(ref - bc, GLA_EXP_CLAMP))).astype(BF16)
            kts[t, i] = jnp.where(keep_kb, jnp.concatenate([kt] * B_HEADS, axis=0), 0.0)
    a_rows = {(t, i): _nt(qts[t, i], kts[t, i]) for t in inst for i in range(n_sub)}
    a_mat = {t: jnp.where(keep_a, jnp.concatenate([a_rows[t, i] for i in range(n_sub)], axis=0), 0.0).astype(BF16)
             for t in inst}
    o_intra = {t: jnp.dot(a_mat[t], jnp.where(keep_vb, jnp.concatenate([v_bf[t]] * B_HEADS, axis=0), 0.0),
                          preferred_element_type=F32) for t in inst}
    kv = {t: jnp.where(keep_st, _tn(v_bf[t], kdec[t]), 0.0) for t in inst}
    for t in inst:
        ci, bi = t
        st = st_scr[bi]
        o = o_intra[t] + _nt(qdec[t], st.astype(BF16))
        st_scr[bi] = st * jnp.exp(bcum[t][cs - 1:cs, :]) + kv[t]
        rg = rg_ref[bi, sls[ci], :]
        for h in range(B_HEADS):
            vs = slice(h * B_DV, (h + 1) * B_DV)
            ob_ref[bi, sls[ci], vs] = (_rms(o[:, vs]) * go * _silu(rg[:, vs])).astype(BF16)

    @pl.when(j == pl.num_programs(1) - 1)
    def _():
        for bi in range(bb):
            st = st_scr[bi]
            for h in range(B_HEADS):
                st_ref[bi, h] = st[h * B_DV:(h + 1) * B_DV, h * B_DK:(h + 1) * B_DK]


def _gla(qg, kg, vg, gg, rg, state_t, g_out, *, cs, nc, bb):
    b, t, _ = qg.shape
    assert b % bb == 0
    tt = cs * nc
    kern = functools.partial(_gla_kernel, cs=cs, nc=nc, bb=bb)

    def tok(w):
        return pl.BlockSpec((bb, tt, w), lambda i, j: (i, j, 0))

    st_spec = pl.BlockSpec((bb, B_HEADS, B_DV, B_DK), lambda i, j: (i, 0, 0, 0))
    return pl.pallas_call(
        kern,
        grid=(b // bb, t // tt),
        in_specs=[tok(256), tok(256), tok(512), tok(256), tok(512), st_spec,
                  pl.BlockSpec((1, B_DV), lambda i, j: (0, 0))],
        out_specs=[tok(512), st_spec],
        out_shape=[jax.ShapeDtypeStruct((b, t, 512), BF16),
                   jax.ShapeDtypeStruct((b, B_HEADS, B_DV, B_DK), F32)],
        scratch_shapes=[pltpu.VMEM((bb, B_HEADS * B_DV, B_HEADS * B_DK), F32)],
        name="gla",
        compiler_params=_params("arbitrary", "arbitrary"),
    )(qg, kg, vg, gg, rg, state_t, g_out)


def _merge_kernel(oa_ref, ob_ref, x_ref, mod_ref, wo_ref, gf_ref, wr_ref, br_ref, cnt0_ref,
                  x1_ref, h2_ref, ri_ref, rgate_ref, cnt_ref, carry_scr):
    @pl.when((pl.program_id(0) == 0) & (pl.program_id(1) == 0))
    def _():
        carry_scr[...] = cnt0_ref[...]

    mod = mod_ref[0]
    cat = jnp.concatenate([oa_ref[0, 0], ob_ref[0]], axis=1)
    x1 = x_ref[0] + mod[2:3] * jnp.dot(cat, wo_ref[...], preferred_element_type=F32)
    x1_ref[0] = x1
    h2 = _rms(x1) * gf_ref[...] * (1.0 + mod[4:5]) + mod[3:4]
    _store_row_tiles(h2_ref.at[0], h2)
    tm = x1.shape[0]
    lane = lax.broadcasted_iota(I32, (tm, LANES), 1).astype(F32)
    hi = h2.astype(BF16)
    lo = (h2 - hi.astype(F32)).astype(BF16)
    wr = wr_ref[...]
    a = jnp.dot(hi, wr, preferred_element_type=F32)
    left = (a[:, :LANES] + a[:, LANES:]) + jnp.dot(lo, wr[:, :LANES], preferred_element_type=F32) + br_ref[...]
    idx, val = [], []
    for _ in range(TOP_K):
        m = jnp.max(left, axis=1, keepdims=True)
        e = jnp.argmax(left, axis=1, keepdims=True).astype(F32)
        idx.append(e)
        val.append(m)
        left = jnp.where(lane == e, -jnp.inf, left)
    ex = [jnp.exp(vv - val[0]) for vv in val]
    den = ex[0] + ex[1] + ex[2] + ex[3]
    onehot = jnp.zeros((tm, LANES), F32)
    for e in idx:
        onehot = onehot + jnp.where(lane == e, 1.0, 0.0)
    r = lax.broadcasted_iota(I32, (tm, tm), 0)
    c = lax.broadcasted_iota(I32, (tm, tm), 1)
    earlier = jnp.where(r > c, 1.0, 0.0).astype(BF16)
    before = jnp.dot(earlier, onehot.astype(BF16), preferred_element_type=F32) + carry_scr[...]
    ri = jnp.zeros((tm, LANES), F32)
    rgate = jnp.zeros((tm, LANES), F32)
    for kq in range(TOP_K):
        rank = jnp.sum(jnp.where(lane == idx[kq], before, 0.0), axis=1, keepdims=True)
        ri = jnp.where(lane == kq, idx[kq], ri)
        ri = jnp.where(lane == TOP_K + kq, rank, ri)
        rgate = jnp.where(lane == kq, ex[kq] / den, rgate)
    ri_ref[0] = ri.astype(I32)
    rgate_ref[0] = rgate
    carry_scr[...] = carry_scr[...] + jnp.sum(onehot, axis=0, keepdims=True)
    cnt_ref[...] = carry_scr[...]


def _merge(oa, ob, x, mod, w_out, g_ffn, wr_pad, br_pad, cnt0, tm):
    b, t, d = x.shape

    def tok(w):
        return pl.BlockSpec((1, tm, w), lambda i, j: (i, j, 0))

    def const(s):
        return pl.BlockSpec(s, lambda i, j: (0, 0))

    return pl.pallas_call(
        _merge_kernel,
        grid=(b, t // tm),
        in_specs=[pl.BlockSpec((1, 1, tm, 512), lambda i, j: (j, i, 0, 0)),
                  tok(512), tok(d), pl.BlockSpec((1, 6, d), lambda i, j: (i, 0, 0)),
                  const((d, d)), const((1, d)), const((d, 2 * LANES)), const((1, LANES)), const((1, LANES))],
        out_specs=[tok(d), pl.BlockSpec((1, tm * SUBLANES, LANES), lambda i, j: (i, j, 0)),
                   tok(LANES), tok(LANES), const((1, LANES))],
        out_shape=[jax.ShapeDtypeStruct((b, t, d), F32), jax.ShapeDtypeStruct((b, t * SUBLANES, LANES), F32),
                   jax.ShapeDtypeStruct((b, t, LANES), I32), jax.ShapeDtypeStruct((b, t, LANES), F32),
                   jax.ShapeDtypeStruct((1, LANES), F32)],
        scratch_shapes=[pltpu.VMEM((1, LANES), F32)],
        name="merge",
        compiler_params=_params("arbitrary", "arbitrary"),
    )(oa, ob, x, mod, w_out, g_ffn, wr_pad, br_pad, cnt0)


def _dispatch_kernel(pe_ref, hp_ref, dest_ref, h_ref, *rest, tm, zero_init):
    if zero_init:
        xs_ref, zbuf, sem, zsem = rest

        @pl.when(pl.program_id(0) == 0)
        def _():
            zbuf[...] = jnp.zeros_like(zbuf)

            def zero_copy(e):
                start = pl.multiple_of(pe_ref[e] - SLOT_BLOCK, SLOT_BLOCK)
                return pltpu.make_async_copy(zbuf, xs_ref.at[pl.ds(start, SLOT_BLOCK)], zsem)

            for e in range(N_EXPERTS):
                @pl.when(hp_ref[e] > 0)
                def _():
                    zero_copy(e).start()
            for e in range(N_EXPERTS):
                @pl.when(hp_ref[e] > 0)
                def _():
                    zero_copy(e).wait()
    else:
        _, xs_ref, sem = rest

    def row_copy(r, slot):
        return pltpu.make_async_copy(h_ref.at[r], xs_ref.at[slot], sem)

    def issue(i, carry):
        for j in range(ROW_UNROLL):
            for kq in range(TOP_K):
                row_copy(i * ROW_UNROLL + j, dest_ref[(i * ROW_UNROLL + j) * TOP_K + kq]).start(priority=kq % 2)
        return carry

    lax.fori_loop(0, tm // ROW_UNROLL, issue, 0)

    def drain(i, carry):
        for _ in range(ROW_UNROLL * TOP_K):
            row_copy(0, 0).wait()
        return carry

    lax.fori_loop(0, tm // ROW_UNROLL, drain, 0)


def _dispatch(pad_end, has_pad, dest_flat, h2_tiles, xs, n_slots, tm):
    n = h2_tiles.shape[0]
    tile = h2_tiles.shape[1:]
    zero_init = xs is None
    kern = functools.partial(_dispatch_kernel, tm=tm, zero_init=zero_init)
    in_specs = [pl.BlockSpec((tm * TOP_K,), lambda i, pe, hp: (i,), memory_space=pltpu.SMEM),
                pl.BlockSpec((tm,) + tile, lambda i, pe, hp: (i, 0, 0))]
    args = [pad_end, has_pad, dest_flat, h2_tiles]
    scratch = [pltpu.SemaphoreType.DMA(())]
    aliases = {}
    if zero_init:
        scratch = [pltpu.VMEM((SLOT_BLOCK,) + tile, F32), pltpu.SemaphoreType.DMA(()), pltpu.SemaphoreType.DMA(())]
    else:
        in_specs.append(pl.BlockSpec(memory_space=pl.ANY))
        args.append(xs)
        aliases = {4: 0}
    grid_spec = pltpu.PrefetchScalarGridSpec(
        num_scalar_prefetch=2,
        grid=(n // tm,),
        in_specs=in_specs,
        out_specs=pl.BlockSpec(memory_space=pl.ANY),
        scratch_shapes=scratch,
    )
    return pl.pallas_call(
        kern,
        grid_spec=grid_spec,
        out_shape=jax.ShapeDtypeStruct((n_slots,) + tile, F32),
        input_output_aliases=aliases,
        name="dispatch",
        compiler_params=_params("arbitrary"),
    )(*args)


def _expert_kernel(be_ref, nu_ref, xs_ref, wgu_ref, bgu_ref, wdn_ref, bdn_ref, ys_ref):
    @pl.when(pl.program_id(0) < nu_ref[0])
    def _():
        f = wdn_ref.shape[1]
        x = _load_row_tiles(xs_ref, SLOT_BLOCK).astype(BF16)
        gu = jnp.dot(x, wgu_ref[0], preferred_element_type=F32) + bgu_ref[0]
        gate = jnp.minimum(gu[:, :f], SWIGLU_LIMIT)
        up = jnp.clip(gu[:, f:], -SWIGLU_LIMIT, SWIGLU_LIMIT)
        glu = gate / (1.0 + jnp.exp(-SWIGLU_ALPHA * gate))
        act = ((up + 1.0) * glu).astype(BF16)
        _store_row_tiles(ys_ref, jnp.dot(act, wdn_ref[0], preferred_element_type=F32) + bdn_ref[0])


def _experts(block_expert, n_used, xs, wgu, bgu, wdn, bdn):
    n_slots = xs.shape[0] // SUBLANES
    nb = n_slots // SLOT_BLOCK
    f, d = wdn.shape[1:]
    rows = SLOT_BLOCK * SUBLANES

    def blk(i, be, nu):
        return (jnp.minimum(i, nu[0] - 1), 0)

    def wsel(i, be, nu):
        return (be[i], 0, 0)

    grid_spec = pltpu.PrefetchScalarGridSpec(
        num_scalar_prefetch=2,
        grid=(nb,),
        in_specs=[pl.BlockSpec((rows, LANES), blk),
                  pl.BlockSpec((1, d, 2 * f), wsel),
                  pl.BlockSpec((1, 1, 2 * f), wsel),
                  pl.BlockSpec((1, f, d), wsel),
                  pl.BlockSpec((1, 1, d), wsel)],
        out_specs=pl.BlockSpec((rows, LANES), blk),
    )
    return pl.pallas_call(
        _expert_kernel,
        grid_spec=grid_spec,
        out_shape=jax.ShapeDtypeStruct(xs.shape, F32),
        name="experts",
        compiler_params=_params("arbitrary"),
    )(block_expert, n_used, xs, wgu, bgu, wdn, bdn)


def _combine_kernel(dest_ref, dnext_ref, x1_ref, gate_ref, mod_ref, modf_ref, gfin_ref, ys_ref, y_ref,
                    buf, sem, *, tm):
    nt = pl.num_programs(1)
    step = pl.program_id(0) * nt + pl.program_id(1)
    n_steps = pl.num_programs(0) * nt
    half = TOP_K * tm * SUBLANES
    cur = lax.rem(step, 2)

    def row_copy(which, kq, r, slot):
        off = pl.multiple_of(which * half + (kq * tm + r) * SUBLANES, SUBLANES)
        return pltpu.make_async_copy(ys_ref.at[slot], buf.at[pl.ds(off, SUBLANES), :], sem.at[which])

    def gather(slots_ref, which):
        def issue(i, carry):
            for j in range(ROW_UNROLL):
                for kq in range(TOP_K):
                    r = i * ROW_UNROLL + j
                    row_copy(which, kq, r, slots_ref[r * TOP_K + kq]).start(priority=kq % 2)
            return carry

        lax.fori_loop(0, tm // ROW_UNROLL, issue, 0)

    @pl.when(step == 0)
    def _():
        gather(dest_ref, cur)

    @pl.when(step + 1 < n_steps)
    def _():
        gather(dnext_ref, 1 - cur)

    def drain(i, carry):
        for _ in range(ROW_UNROLL * TOP_K):
            row_copy(cur, 0, 0, 0).wait()
        return carry

    lax.fori_loop(0, tm // ROW_UNROLL, drain, 0)
    gates = gate_ref[0]
    moe = None
    for kq in range(TOP_K):
        start = pl.multiple_of(cur * half + kq * tm * SUBLANES, SUBLANES)
        rows = _load_row_tiles(buf.at[pl.ds(start, tm * SUBLANES), :], tm)
        term = gates[:, kq:kq + 1] * rows
        moe = term if moe is None else moe + term
    xo = x1_ref[0] + mod_ref[0][5:6] * moe
    modf = modf_ref[0]
    y_ref[0] = _rms(xo) * gfin_ref[...] * (1.0 + modf[1:2]) + modf[0:1]


def _combine(dest_flat, x1, gates, mod, modf, g_final, ys, tm):
    b, t, d = x1.shape
    nt = t // tm
    last = b * nt - 1
    kern = functools.partial(_combine_kernel, tm=tm)
    return pl.pallas_call(
        kern,
        grid=(b, nt),
        in_specs=[pl.BlockSpec((tm * TOP_K,), lambda i, j: (i * nt + j,), memory_space=pltpu.SMEM),
                  pl.BlockSpec((tm * TOP_K,), lambda i, j: (jnp.minimum(i * nt + j + 1, last),),
                               memory_space=pltpu.SMEM),
                  pl.BlockSpec((1, tm, d), lambda i, j: (i, j, 0)),
                  pl.BlockSpec((1, tm, LANES), lambda i, j: (i, j, 0)),
                  pl.BlockSpec((1, 6, d), lambda i, j: (i, 0, 0)),
                  pl.BlockSpec((1, 2, d), lambda i, j: (i, 0, 0)),
                  pl.BlockSpec((1, d), lambda i, j: (0, 0)),
                  pl.BlockSpec(memory_space=pl.ANY)],
        out_specs=pl.BlockSpec((1, tm, d), lambda i, j: (i, j, 0)),
        out_shape=jax.ShapeDtypeStruct((b, t, d), F32),
        scratch_shapes=[pltpu.VMEM((2 * TOP_K * tm * SUBLANES, LANES), F32), pltpu.SemaphoreType.DMA((2,))],
        name="combine",
        compiler_params=_params("arbitrary", "arbitrary"),
    )(dest_flat, dest_flat, x1, gates, mod, modf, g_final, ys)


def _rope_tables(pos):
    half = A_HD // 2
    inv_freq = ROPE_THETA ** (-jnp.arange(half, dtype=F32) / half)
    ang = pos.astype(F32)[:, None] * inv_freq[None, :]
    cos = jnp.cos(ang)
    sin = jnp.sin(ang)
    reps = LANES // A_HD
    return jnp.tile(jnp.concatenate([cos, cos], axis=1), (1, reps)), jnp.tile(jnp.concatenate([-sin, sin], axis=1), (1, reps))


def _reorder_w_in(w_in):
    offs = [0]
    for w in (512, 128, 128, 256, 64, 4, 256, 256, 512, 16, 512):
        offs.append(offs[-1] + w)
    seg = [w_in[:, offs[i]:offs[i + 1]] for i in range(11)]
    qa, ka, va, qi, ki, wi, qg, kg, vg, lr, rg = seg
    pad = jnp.zeros((w_in.shape[0], LANES - IDX_DIM - IDX_HEADS - GATE_RANK), w_in.dtype)
    return jnp.concatenate([qa, ka, va, qi, ki, wi, lr, pad, qg, kg, vg, rg], axis=1).astype(BF16)


def kernel(x_prompt, x_sample, cache_k, cache_v, cache_kidx, state_gla, c_prompt, c_sample,
           w_mod, b_mod, g_mix, g_ffn, w_in, gla_w_gate, gla_b_gate, gla_g_out, w_out,
           w_router, b_router, w_gate_up, b_gate_up, w_down, b_down,
           w_mod_final, b_mod_final, g_final):
    depth = w_in.shape[0]
    assert depth == 1
    bp, sp, d = x_prompt.shape
    bs, ts, _ = x_sample.shape
    past = cache_k.shape[2]
    n_exp = w_router.shape[2]
    f = w_down.shape[2]

    c_all = jnp.concatenate([c_prompt, c_sample], axis=0)
    w_r = _reorder_w_in(w_in[0])
    wg_pad = jnp.zeros((LANES, B_HEADS * B_DK), F32).at[MISC_LR:MISC_LR + GATE_RANK].set(gla_w_gate[0])
    bg = gla_b_gate[0].reshape(1, -1)
    wr_pad = jnp.zeros((d, LANES), F32).at[:, :n_exp].set(w_router[0])
    wr_hi = wr_pad.astype(BF16)
    wr_cat = jnp.concatenate([wr_hi, (wr_pad - wr_hi.astype(F32)).astype(BF16)], axis=1)
    br_pad = jnp.full((1, LANES), NEG_BIG, F32).at[0, :n_exp].set(b_router[0])
    w_out_b = w_out[0].astype(BF16)
    wgu_b = w_gate_up[0].astype(BF16)
    wdn_b = w_down[0].astype(BF16)
    bgu = b_gate_up[0].reshape(n_exp, 1, 2 * f)
    bdn = b_down[0].reshape(n_exp, 1, d)
    g_out = gla_g_out[0].reshape(1, B_DV)

    mod_all = _adaln(c_all, w_mod[0], b_mod[0]).reshape(bp + bs, 6, d)
    modf_all = _adaln(c_all, w_mod_final, b_mod_final).reshape(bp + bs, 2, d)
    mod_p, mod_s = mod_all[:bp], mod_all[bp:]
    modf_p, modf_s = modf_all[:bp], modf_all[bp:]

    def mixer(x, mod, pos, tm, past_kv, state_t, cs, nc, tq):
        cos_t, sin_t = _rope_tables(pos)
        qa, ka, va, qi, ki, misc, qg, kg, vg, gg, rg, kb, vx, kib = _premix(
            x, mod, g_mix[0].reshape(1, d), w_r, wg_pad, bg, cos_t, sin_t, tm)
        if past_kv is None:
            oa = _attn_causal(qa, qi, misc, kb, vx, kib, tq)
        else:
            n_past = past_kv[0].shape[2]
            assert tq == x.shape[1] and n_past % LANES == 0
            oa = _attn_call(qa, qi, misc, ka, va, ki, past_kv, tq=tq, q_off=0, n_tiles=1,
                            n_ctx=n_past + LANES, causal=False, n_keys=n_past + ka.shape[1], n_groups=1)[None]
        ob, st = _gla(qg, kg, vg, gg, rg, state_t, g_out, cs=cs, nc=nc, bb=GLA_BATCH)
        return oa, ob, (ka, va, ki, st)

    state0_p = jnp.zeros((bp, B_HEADS, B_DV, B_DK), F32)
    oa_p, ob_p, (ka_p, va_p, ki_p, st_p) = mixer(
        x_prompt, mod_p, jnp.arange(sp), min(256, sp), None, state0_p, CHUNK, min(GLA_CHUNKS_PER_STEP, sp // CHUNK),
        min(256, sp))
    state0_s = jnp.swapaxes(state_gla[0], -1, -2)
    past_kv = (jnp.transpose(cache_k[0], (0, 2, 3, 1)).reshape(bs, A_KV * A_HD, past),
               jnp.transpose(cache_v[0], (0, 2, 3, 1)).reshape(bs, A_KV * A_HD, past),
               jnp.swapaxes(cache_kidx[0], 1, 2))
    oa_s, ob_s, (ka_s, va_s, ki_s, st_s) = mixer(
        x_sample, mod_s, past + jnp.arange(ts), ts, past_kv, state0_s, ts, 1, ts)

    cnt0 = jnp.zeros((1, LANES), F32)
    x1_p, h2_p, ri_p, gate_p, cnt1 = _merge(oa_p, ob_p, x_prompt, mod_p, w_out_b, g_ffn[0].reshape(1, d),
                                            wr_cat, br_pad, cnt0, min(256, sp))
    x1_s, h2_s, ri_s, gate_s, cnt2 = _merge(oa_s, ob_s, x_sample, mod_s, w_out_b, g_ffn[0].reshape(1, d),
                                            wr_cat, br_pad, cnt1, ts)
    counts = cnt2[0, :n_exp].astype(I32)
    padded = (counts + SLOT_BLOCK - 1) // SLOT_BLOCK * SLOT_BLOCK
    pad_end = jnp.cumsum(padded)
    pad_start = pad_end - padded
    n_asg = (bp * sp + bs * ts) * TOP_K
    nb = -(-n_asg // SLOT_BLOCK) + n_exp
    n_slots = nb * SLOT_BLOCK
    block_start = jnp.arange(nb, dtype=I32) * SLOT_BLOCK
    block_expert = jnp.minimum(jnp.sum((pad_end[None, :] <= block_start[:, None]).astype(I32), axis=1), n_exp - 1)
    n_used = (pad_end[-1:] // SLOT_BLOCK).astype(I32)
    has_pad = (padded > counts).astype(I32)
    expert_ids = jnp.arange(n_exp, dtype=I32)

    def dests(ri):
        e, rank = ri[..., :TOP_K], ri[..., TOP_K:2 * TOP_K]
        start = jnp.sum(jnp.where(e[..., None] == expert_ids, pad_start.astype(I32), 0), axis=-1)
        return (start + rank).reshape(-1)

    dest_p, dest_s = dests(ri_p), dests(ri_s)
    pad_end_i = pad_end.astype(I32)
    tile = (SUBLANES, LANES)
    xs = _dispatch(pad_end_i, has_pad, dest_p, h2_p.reshape((bp * sp,) + tile), None, n_slots, min(256, sp))
    xs = _dispatch(pad_end_i, has_pad, dest_s, h2_s.reshape((bs * ts,) + tile), xs, n_slots, min(256, bs * ts))
    ys = _experts(block_expert, n_used, xs.reshape(n_slots * SUBLANES, LANES), wgu_b, bgu, wdn_b, bdn)
    ys = ys.reshape((n_slots,) + tile)
    y_p = _combine(dest_p, x1_p, gate_p, mod_p, modf_p, g_final.reshape(1, d), ys, min(256, sp))
    y_s = _combine(dest_s, x1_s, gate_s, mod_s, modf_s, g_final.reshape(1, d), ys, ts)

    def kv(a, b, t):
        return a.reshape(1, b, t, A_KV, A_HD)

    return (y_p, y_s,
            kv(ka_p, bp, sp), kv(va_p, bp, sp), ki_p[None], jnp.swapaxes(st_p, -1, -2)[None],
            kv(ka_s, bs, ts), kv(va_s, bs, ts), ki_s[None], jnp.swapaxes(st_s, -1, -2)[None])
```

```python
import functools

import jax
import jax.numpy as jnp
from jax import lax
from jax.experimental import pallas as pl
from jax.experimental.pallas import tpu as pltpu

F32 = jnp.float32
BF16 = jnp.bfloat16
I32 = jnp.int32
HI = lax.Precision.HIGHEST

CHUNK = 64
CHUNK_SHIFT = 6
EPS = 1e-6
ROPE_THETA = 10000.0
A_HD = 64
A_HEADS = 8
A_KV = 2
IDX_HEADS = 4
IDX_DIM = 64
INDEX_TOPK = 256
B_HEADS = 4
B_DK = 64
B_DV = 128
GATE_RANK = 16
GATE_TAU = 16.0
N_EXPERTS = 32
TOP_K = 4
SWIGLU_LIMIT = 7.0
SWIGLU_ALPHA = 1.702

LANES = 128
SUBLANES = 8
ROW_UNROLL = 8
ROW_DMA_TILE = 512
GLA_SUB = 16
GLA_BATCH = 2
GLA_CHUNKS_PER_STEP = 8
GLA_EXP_CLAMP = 80.0
SLOT_BLOCK = 512
VMEM_LIMIT = 56 * 1024 * 1024
INT_MIN = -2147483648
NEG_BIG = -1e30
SEARCH_TWO_BIT_MAX_KEYS = 768

C_QA, C_KA, C_VA, C_QI, C_MISC, C_QG, C_KG, C_VG, C_RG, C_END = 0, 512, 640, 768, 1024, 1152, 1408, 1664, 2176, 2688
MISC_WI = 64
MISC_LR = 68


def _params(*sem):
    return pltpu.CompilerParams(dimension_semantics=sem, vmem_limit_bytes=VMEM_LIMIT)


def _nt(a, b):
    return lax.dot_general(a, b, (((1,), (1,)), ((), ())), preferred_element_type=F32)


def _tn(a, b):
    return lax.dot_general(a, b, (((0,), (0,)), ((), ())), preferred_element_type=F32)


def _rms(x):
    return x * lax.rsqrt(jnp.mean(x * x, axis=-1, keepdims=True) + EPS)


def _silu(x):
    return x / (1.0 + jnp.exp(-x))


def _store_row_tiles(ref, val):
    rows, width = val.shape
    assert width == SUBLANES * LANES
    for s in range(SUBLANES):
        ref[pl.ds(s, rows, stride=SUBLANES), :] = val[:, s * LANES:(s + 1) * LANES]


def _load_row_tiles(ref, rows):
    return jnp.concatenate([ref[pl.ds(s, rows, stride=SUBLANES), :] for s in range(SUBLANES)], axis=1)


def _value_with_ones(vb):
    return jnp.concatenate([vb, jnp.ones_like(vb)], axis=-1)


def _adaln_kernel(c_ref, w_ref, b_ref, o_ref):
    a = _silu(c_ref[...])
    o_ref[...] = jnp.dot(a, w_ref[...], preferred_element_type=F32, precision=HI) + b_ref[...]


def _adaln(c, w, b):
    r, d = c.shape
    n = w.shape[1]
    tn = 512
    return pl.pallas_call(
        _adaln_kernel,
        grid=(n // tn,),
        in_specs=[pl.BlockSpec((r, d), lambda j: (0, 0)),
                  pl.BlockSpec((d, tn), lambda j: (0, j)),
                  pl.BlockSpec((1, tn), lambda j: (0, j))],
        out_specs=pl.BlockSpec((r, tn), lambda j: (0, j)),
        out_shape=jax.ShapeDtypeStruct((r, n), F32),
        name="adaln",
        compiler_params=_params("arbitrary"),
    )(c, w, b.reshape(1, n))


def _premix_kernel(x_ref, mod_ref, g_ref, w_ref, wg_ref, bg_ref, cos_ref, sin_ref,
                   qa_ref, ka_ref, va_ref, qi_ref, ki_ref, misc_ref, qg_ref, kg_ref, vg_ref, gg_ref, rg_ref,
                   kb_ref, vx_ref, kib_ref):
    x = x_ref[0]
    mod = mod_ref[0]
    hb = (_rms(x) * g_ref[...] * (1.0 + mod[1:2]) + mod[0:1]).astype(BF16)
    tm = x.shape[0]

    def project(c0, c1):
        return jnp.dot(hb, w_ref[:, c0:c1], preferred_element_type=F32)

    cos = cos_ref[...]
    sin = sin_ref[...]
    lane = lax.broadcasted_iota(I32, (tm, LANES), 1)
    lower_half = (lane & (A_HD - 1)) < (A_HD // 2)

    def rope(xc):
        rot = jnp.where(lower_half, pltpu.roll(xc, LANES - A_HD // 2, 1), pltpu.roll(xc, A_HD // 2, 1))
        return xc * cos + rot * sin

    seg = project(C_QA, C_KA)
    for j in range((C_KA - C_QA) // LANES):
        qa_ref[0, :, j * LANES:(j + 1) * LANES] = (rope(seg[:, j * LANES:(j + 1) * LANES]) * (A_HD ** -0.5)).astype(BF16)
    seg = project(C_KA, C_QI)
    ka = rope(seg[:, :LANES])
    va = seg[:, LANES:]
    ka_ref[0] = ka
    va_ref[0] = va
    kb_ref[0] = ka.astype(BF16)
    vx_ref[0] = _value_with_ones(va.astype(BF16))
    seg = project(C_QI, C_QG)
    for j in range((C_MISC - C_QI) // LANES):
        qi_ref[0, :, j * LANES:(j + 1) * LANES] = rope(seg[:, j * LANES:(j + 1) * LANES]).astype(BF16)
    m = seg[:, C_MISC - C_QI:]
    mr = rope(m)
    ki_ref[0] = mr[:, :IDX_DIM]
    kib_ref[0] = mr[:, :IDX_DIM].astype(BF16)
    misc_ref[0] = jnp.where(lane < IDX_DIM, mr, m * (IDX_HEADS ** -0.5))
    xg = jnp.dot(m, wg_ref[...], preferred_element_type=F32, precision=HI) + bg_ref[...]
    gg_ref[0] = (jnp.minimum(xg, 0.0) - jnp.log(1.0 + jnp.exp(-jnp.abs(xg)))) * (1.0 / GATE_TAU)
    qg_ref[0] = project(C_QG, C_KG) * (B_DK ** -0.5)
    kg_ref[0] = project(C_KG, C_VG)
    vg_ref[0] = project(C_VG, C_RG)
    rg_ref[0] = project(C_RG, C_END)


def _premix(x, mod, g_mix, w_r, wg_pad, bg, cos_t, sin_t, tm):
    b, t, d = x.shape
    widths = [(512, BF16), (128, F32), (128, F32), (256, BF16), (64, F32), (128, F32),
              (256, F32), (256, F32), (512, F32), (256, F32), (512, F32),
              (128, BF16), (2 * LANES, BF16), (IDX_DIM, BF16)]
    return pl.pallas_call(
        _premix_kernel,
        grid=(b, t // tm),
        in_specs=[pl.BlockSpec((1, tm, d), lambda i, j: (i, j, 0)),
                  pl.BlockSpec((1, 6, d), lambda i, j: (i, 0, 0)),
                  pl.BlockSpec((1, d), lambda i, j: (0, 0)),
                  pl.BlockSpec((d, C_END), lambda i, j: (0, 0)),
                  pl.BlockSpec((LANES, 256), lambda i, j: (0, 0)),
                  pl.BlockSpec((1, 256), lambda i, j: (0, 0)),
                  pl.BlockSpec((tm, LANES), lambda i, j: (j, 0)),
                  pl.BlockSpec((tm, LANES), lambda i, j: (j, 0))],
        out_specs=[pl.BlockSpec((1, tm, w), lambda i, j: (i, j, 0)) for w, _ in widths],
        out_shape=[jax.ShapeDtypeStruct((b, t, w), dt) for w, dt in widths],
        name="premix",
        compiler_params=_params("arbitrary", "arbitrary"),
    )(x, mod, g_mix, w_r, wg_pad, bg, cos_t, sin_t)


def _attn_kernel(*refs, tq, n_ctx, top_k, causal, n_keys, q_off, n_groups, n_past, keep_all):
    if n_past:
        q_ref, qi_ref, misc_ref, k_ref, v_ref, ki_ref, pk_ref, pv_ref, pki_ref, o_ref = refs
        assert n_ctx - n_past == LANES and k_ref.shape[1] <= LANES

        def new_columns(rows_ref):
            new = rows_ref[0]
            n_new, width = new.shape
            if width < LANES:
                new = jnp.concatenate([new, jnp.zeros((n_new, LANES - width), F32)], axis=1)
            square = jnp.concatenate([new, jnp.zeros((LANES - n_new, LANES), F32)], axis=0)
            return square.T[:width].astype(BF16)

        k = jnp.concatenate([pk_ref[0].astype(BF16), new_columns(k_ref)], axis=1)
        v = jnp.concatenate([pv_ref[0].astype(BF16), new_columns(v_ref)], axis=1)
        vx = jnp.concatenate([v, jnp.ones_like(v)], axis=0)
        kib = jnp.concatenate([pki_ref[0].astype(BF16), new_columns(ki_ref)], axis=1)
    else:
        q_ref, qi_ref, misc_ref, k_ref, vx_ref, ki_ref, o_ref = refs
        k = k_ref[0]
        vx = vx_ref[0]
        kib = ki_ref[0]
    feature_major = bool(n_past)
    rows = tq // n_groups
    row0 = q_off + pl.program_id(1) * tq
    keys, kks = [], []
    idx_dots = {}
    for g in range(n_groups):
        qi = qi_ref[0, g * rows:(g + 1) * rows, :]
        for h in range(IDX_HEADS):
            qh = qi[:, h * IDX_DIM:(h + 1) * IDX_DIM]
            idx_dots[g, h] = jnp.dot(qh, kib, preferred_element_type=F32) if feature_major else _nt(qh, kib)
    for g in range(n_groups):
        misc = misc_ref[0, g * rows:(g + 1) * rows, :]
        isc = jnp.zeros((rows, n_ctx), F32)
        for h in range(IDX_HEADS):
            isc = isc + misc[:, MISC_WI + h:MISC_WI + h + 1] * jnp.maximum(idx_dots[g, h], 0.0)
        kpos = lax.broadcasted_iota(I32, (rows, n_ctx), 1)
        if causal:
            row = lax.broadcasted_iota(I32, (rows, 1), 0) + (row0 + g * rows)
            key_lim = (lax.shift_right_logical(row, CHUNK_SHIFT) + 1) * CHUNK
        else:
            key_lim = jnp.full((rows, 1), n_keys, I32)
        bits = pltpu.bitcast(isc, I32)
        key = jnp.where(bits < 0, INT_MIN - bits, bits)
        keys.append(jnp.where(kpos < key_lim, key, INT_MIN))
        kks.append(jnp.minimum(key_lim, top_k).astype(F32))

    def count_at_least(g, cand):
        return jnp.sum(jnp.where(keys[g] >= (cand ^ INT_MIN), 1.0, 0.0), axis=1, keepdims=True)

    bits_per_pass = 2 if n_ctx <= SEARCH_TWO_BIT_MAX_KEYS else 1

    def search(i, ans):
        out = []
        for g in range(n_groups):
            if bits_per_pass == 1:
                cand = ans[g] | lax.shift_left(jnp.int32(1), 31 - i)
                out.append(jnp.where(count_at_least(g, cand) >= kks[g], cand, ans[g]))
            else:
                a1 = ans[g] | lax.shift_left(jnp.int32(1), 31 - 2 * i)
                a2 = ans[g] | lax.shift_left(jnp.int32(1), 30 - 2 * i)
                a3 = a1 | a2
                c1, c2, c3 = count_at_least(g, a1), count_at_least(g, a2), count_at_least(g, a3)
                kk = kks[g]
                out.append(jnp.where(c3 >= kk, a3, jnp.where(c1 >= kk, a1, jnp.where(c2 >= kk, a2, ans[g]))))
        return tuple(out)

    ans = tuple(jnp.zeros((rows, 1), I32) for _ in range(n_groups))
    if not keep_all:
        ans = lax.fori_loop(0, 32 // bits_per_pass, search, ans)
    r = lax.broadcasted_iota(I32, (LANES, LANES), 0)
    c = lax.broadcasted_iota(I32, (LANES, LANES), 1)
    upper = jnp.where(r < c, 1.0, 0.0).astype(BF16)
    n_blk = n_ctx // LANES
    thrs = [ans[g] ^ INT_MIN for g in range(n_groups)]
    ties = {(g, j): jnp.where(keys[g][:, j * LANES:(j + 1) * LANES] == thrs[g], 1.0, 0.0)
            for g in range(n_groups) for j in range(n_blk)}
    within = {gj: jnp.dot(t.astype(BF16), upper, preferred_element_type=F32) for gj, t in ties.items()}
    bias_groups = []
    for g in range(n_groups):
        gt = keys[g] > thrs[g]
        need = kks[g] - jnp.sum(jnp.where(gt, 1.0, 0.0), axis=1, keepdims=True)
        carry = jnp.zeros((rows, 1), F32)
        blocks = []
        for j in range(n_blk):
            sl = slice(j * LANES, (j + 1) * LANES)
            take_tie = jnp.where(within[g, j] + carry < need, ties[g, j], 0.0)
            blocks.append(jnp.where(gt[:, sl], 0.0, jnp.where(take_tie > 0.0, 0.0, NEG_BIG)))
            carry = carry + jnp.sum(ties[g, j], axis=1, keepdims=True)
        bias_groups.append(jnp.concatenate(blocks, axis=1))
    bias = jnp.concatenate(bias_groups, axis=0)

    q = q_ref[0]
    rep = A_HEADS // A_KV
    if feature_major and tq * A_HEADS <= 2 * LANES:
        q_groups = [jnp.concatenate([q[:, hh * A_HD:(hh + 1) * A_HD] for hh in range(g * rep, (g + 1) * rep)], axis=0)
                    for g in range(A_KV)]
        s_all = jnp.concatenate([jnp.dot(q_groups[g], k[g * A_HD:(g + 1) * A_HD, :], preferred_element_type=F32)
                                 for g in range(A_KV)], axis=0) + jnp.concatenate([bias] * A_HEADS, axis=0)
        p_all = jnp.exp((s_all - jnp.max(s_all, axis=1, keepdims=True)).astype(BF16))
        ox = _nt(p_all, vx)
        outs = [ox[hh * tq:(hh + 1) * tq, (hh // rep) * A_HD:(hh // rep + 1) * A_HD]
                / ox[hh * tq:(hh + 1) * tq, LANES:LANES + 1] for hh in range(A_HEADS)]
        o_ref[0] = jnp.concatenate(outs, axis=1).astype(BF16)
        return
    kgs = [k[g * A_HD:(g + 1) * A_HD, :] if feature_major else k[:, g * A_HD:(g + 1) * A_HD] for g in range(A_KV)]
    qs = [q[:, hh * A_HD:(hh + 1) * A_HD] for hh in range(A_HEADS)]
    ss = [(jnp.dot(qh, kgs[hh // rep], preferred_element_type=F32) if feature_major else _nt(qh, kgs[hh // rep]))
          + bias for hh, qh in enumerate(qs)]
    ps = [jnp.exp((s - jnp.max(s, axis=1, keepdims=True)).astype(BF16)) for s in ss]
    oxs = [_nt(p, vx) if feature_major else jnp.dot(p, vx, preferred_element_type=F32) for p in ps]
    outs = [ox[:, (hh // rep) * A_HD:(hh // rep + 1) * A_HD] / ox[:, LANES:LANES + 1] for hh, ox in enumerate(oxs)]
    o_ref[0] = jnp.concatenate(outs, axis=1).astype(BF16)


def _attn_call(q, qi, misc, kb, vx, kib, past, *, tq, q_off, n_tiles, n_ctx, causal, n_keys, n_groups):
    b = q.shape[0]
    t0 = q_off // tq
    top_k = min(INDEX_TOPK, n_keys // 4)
    n_past = 0 if past is None else past[0].shape[2]
    n_own = n_ctx if past is None else kb.shape[1]
    kern = functools.partial(_attn_kernel, tq=tq, n_ctx=n_ctx, top_k=top_k, causal=causal, n_keys=n_keys,
                             q_off=q_off, n_groups=n_groups, n_past=n_past,
                             keep_all=causal and q_off + n_tiles * tq <= top_k)

    def tok(w):
        return pl.BlockSpec((1, tq, w), lambda i, j: (i, t0 + j, 0))

    def ctx(rows, w):
        return pl.BlockSpec((1, rows, w), lambda i, j: (i, 0, 0))

    in_specs = [tok(512), tok(256), tok(LANES)] + [ctx(n_own, a.shape[2]) for a in (kb, vx, kib)]
    args = [q, qi, misc, kb, vx, kib]
    if past is not None:
        in_specs += [ctx(a.shape[1], n_past) for a in past]
        args += list(past)
    return pl.pallas_call(
        kern,
        grid=(b, n_tiles),
        in_specs=in_specs,
        out_specs=pl.BlockSpec((1, tq, 512), lambda i, j: (i, j, 0)),
        out_shape=jax.ShapeDtypeStruct((b, n_tiles * tq, 512), BF16),
        name="attn",
        compiler_params=_params("arbitrary", "arbitrary"),
    )(*args)


def _attn_causal(q, qi, misc, kb, vx, kib, tq):
    t = q.shape[1]
    return jnp.stack([
        _attn_call(q, qi, misc, kb, vx, kib, None, tq=tq, q_off=c * tq, n_tiles=1, n_ctx=(c + 1) * tq,
                   causal=True, n_keys=t, n_groups=2)
        for c in range(t // tq)])


def _gla_kernel(q_ref, k_ref, v_ref, g_ref, rg_ref, s0_ref, go_ref, ob_ref, st_ref, st_scr, *, cs, nc, bb):
    j = pl.program_id(1)
    hk, hv, hc = B_HEADS * B_DK, B_HEADS * B_DV, B_HEADS * cs

    def head_of(idx, width):
        return lax.shift_right_logical(idx, width.bit_length() - 1)

    def same_head(rows, row_w, cols, col_w):
        r = head_of(lax.broadcasted_iota(I32, (rows, cols), 0), row_w)
        c = head_of(lax.broadcasted_iota(I32, (rows, cols), 1), col_w)
        return r == c

    keep_kb = same_head(hc, cs, hk, B_DK)
    keep_vb = same_head(hc, cs, hv, B_DV)
    keep_st = same_head(hv, B_DV, hk, B_DK)
    t_idx = lax.broadcasted_iota(I32, (cs, hc), 0)
    s_idx = lax.broadcasted_iota(I32, (cs, hc), 1) & (cs - 1)
    keep_a = t_idx >= s_idx
    r = lax.broadcasted_iota(I32, (cs, cs), 0)
    c = lax.broadcasted_iota(I32, (cs, cs), 1)
    tri = jnp.where(r >= c, 1.0, 0.0)
    go = go_ref[...]

    @pl.when(j == 0)
    def _():
        for bi in range(bb):
            blocks = []
            for h in range(B_HEADS):
                parts = [jnp.zeros((B_DV, B_DK), F32)] * B_HEADS
                parts[h] = s0_ref[bi, h]
                blocks.append(jnp.concatenate(parts, axis=1))
            st_scr[bi] = jnp.concatenate(blocks, axis=0)

    inst = [(ci, bi) for ci in range(nc) for bi in range(bb)]
    sls = {ci: slice(ci * cs, (ci + 1) * cs) for ci in range(nc)}
    bcum = {t: jnp.dot(tri, g_ref[t[1], sls[t[0]], :], preferred_element_type=F32, precision=HI) for t in inst}
    n_sub = cs // GLA_SUB
    v_bf, qdec, kdec, qts, kts = {}, {}, {}, {}, {}
    for t in inst:
        ci, bi = t
        q = q_ref[bi, sls[ci], :]
        k = k_ref[bi, sls[ci], :]
        bc = bcum[t]
        blast = bc[cs - 1:cs, :]
        qdec[t] = (q * jnp.exp(bc)).astype(BF16)
        kdec[t] = (k * jnp.exp(blast - bc)).astype(BF16)
        v_bf[t] = v_ref[bi, sls[ci], :].astype(BF16)
        for i in range(n_sub):
            rs = slice(i * GLA_SUB, (i + 1) * GLA_SUB)
            ref = bc[i * GLA_SUB:i * GLA_SUB + 1, :]
            qts[t, i] = (q[rs, :] * jnp.exp(bc[rs, :] - ref)).astype(BF16)
            kt = (k * jnp.exp(jnp.minimum(ref - bc, GLA_EXP_CLAMP))).astype(BF16)
            kts[t, i] = jnp.where(keep_kb, jnp.concatenate([kt] * B_HEADS, axis=0), 0.0)
    a_rows = {(t, i): _nt(qts[t, i], kts[t, i]) for t in inst for i in range(n_sub)}
    a_mat = {t: jnp.where(keep_a, jnp.concatenate([a_rows[t, i] for i in range(n_sub)], axis=0), 0.0).astype(BF16)
             for t in inst}
    o_intra = {t: jnp.dot(a_mat[t], jnp.where(keep_vb, jnp.concatenate([v_bf[t]] * B_HEADS, axis=0), 0.0),
                          preferred_element_type=F32) for t in inst}
    kv = {t: jnp.where(keep_st, _tn(v_bf[t], kdec[t]), 0.0) for t in inst}
    for t in inst:
        ci, bi = t
        st = st_scr[bi]
        o = o_intra[t] + _nt(qdec[t], st.astype(BF16))
        st_scr[bi] = st * jnp.exp(bcum[t][cs - 1:cs, :]) + kv[t]
        rg = rg_ref[bi, sls[ci], :]
        for h in range(B_HEADS):
            vs = slice(h * B_DV, (h + 1) * B_DV)
            ob_ref[bi, sls[ci], vs] = (_rms(o[:, vs]) * go * _silu(rg[:, vs])).astype(BF16)

    @pl.when(j == pl.num_programs(1) - 1)
    def _():
        for bi in range(bb):
            st = st_scr[bi]
            for h in range(B_HEADS):
                st_ref[bi, h] = st[h * B_DV:(h + 1) * B_DV, h * B_DK:(h + 1) * B_DK]


def _gla(qg, kg, vg, gg, rg, state_t, g_out, *, cs, nc, bb):
    b, t, _ = qg.shape
    assert b % bb == 0
    tt = cs * nc
    kern = functools.partial(_gla_kernel, cs=cs, nc=nc, bb=bb)

    def tok(w):
        return pl.BlockSpec((bb, tt, w), lambda i, j: (i, j, 0))

    st_spec = pl.BlockSpec((bb, B_HEADS, B_DV, B_DK), lambda i, j: (i, 0, 0, 0))
    return pl.pallas_call(
        kern,
        grid=(b // bb, t // tt),
        in_specs=[tok(256), tok(256), tok(512), tok(256), tok(512), st_spec,
                  pl.BlockSpec((1, B_DV), lambda i, j: (0, 0))],
        out_specs=[tok(512), st_spec],
        out_shape=[jax.ShapeDtypeStruct((b, t, 512), BF16),
                   jax.ShapeDtypeStruct((b, B_HEADS, B_DV, B_DK), F32)],
        scratch_shapes=[pltpu.VMEM((bb, B_HEADS * B_DV, B_HEADS * B_DK), F32)],
        name="gla",
        compiler_params=_params("arbitrary", "arbitrary"),
    )(qg, kg, vg, gg, rg, state_t, g_out)


def _merge_kernel(oa_ref, ob_ref, x_ref, mod_ref, wo_ref, gf_ref, wr_ref, br_ref, cnt0_ref,
                  x1_ref, h2_ref, ri_ref, rgate_ref, cnt_ref, carry_scr):
    @pl.when((pl.program_id(0) == 0) & (pl.program_id(1) == 0))
    def _():
        carry_scr[...] = cnt0_ref[...]

    mod = mod_ref[0]
    cat = jnp.concatenate([oa_ref[0, 0], ob_ref[0]], axis=1)
    x1 = x_ref[0] + mod[2:3] * jnp.dot(cat, wo_ref[...], preferred_element_type=F32)
    x1_ref[0] = x1
    h2 = _rms(x1) * gf_ref[...] * (1.0 + mod[4:5]) + mod[3:4]
    _store_row_tiles(h2_ref.at[0], h2)
    tm = x1.shape[0]
    lane = lax.broadcasted_iota(I32, (tm, LANES), 1).astype(F32)
    hi = h2.astype(BF16)
    lo = (h2 - hi.astype(F32)).astype(BF16)
    wr = wr_ref[...]
    a = jnp.dot(hi, wr, preferred_element_type=F32)
    left = (a[:, :LANES] + a[:, LANES:]) + jnp.dot(lo, wr[:, :LANES], preferred_element_type=F32) + br_ref[...]
    idx, val = [], []
    for _ in range(TOP_K):
        m = jnp.max(left, axis=1, keepdims=True)
        e = jnp.argmax(left, axis=1, keepdims=True).astype(F32)
        idx.append(e)
        val.append(m)
        left = jnp.where(lane == e, -jnp.inf, left)
    ex = [jnp.exp(vv - val[0]) for vv in val]
    den = ex[0] + ex[1] + ex[2] + ex[3]
    onehot = jnp.zeros((tm, LANES), F32)
    for e in idx:
        onehot = onehot + jnp.where(lane == e, 1.0, 0.0)
    r = lax.broadcasted_iota(I32, (tm, tm), 0)
    c = lax.broadcasted_iota(I32, (tm, tm), 1)
    earlier = jnp.where(r > c, 1.0, 0.0).astype(BF16)
    before = jnp.dot(earlier, onehot.astype(BF16), preferred_element_type=F32) + carry_scr[...]
    ri = jnp.zeros((tm, LANES), F32)
    rgate = jnp.zeros((tm, LANES), F32)
    for kq in range(TOP_K):
        rank = jnp.sum(jnp.where(lane == idx[kq], before, 0.0), axis=1, keepdims=True)
        ri = jnp.where(lane == kq, idx[kq], ri)
        ri = jnp.where(lane == TOP_K + kq, rank, ri)
        rgate = jnp.where(lane == kq, ex[kq] / den, rgate)
    ri_ref[0] = ri.astype(I32)
    rgate_ref[0] = rgate
    carry_scr[...] = carry_scr[...] + jnp.sum(onehot, axis=0, keepdims=True)
    cnt_ref[...] = carry_scr[...]


def _merge(oa, ob, x, mod, w_out, g_ffn, wr_pad, br_pad, cnt0, tm):
    b, t, d = x.shape

    def tok(w):
        return pl.BlockSpec((1, tm, w), lambda i, j: (i, j, 0))

    def const(s):
        return pl.BlockSpec(s, lambda i, j: (0, 0))

    return pl.pallas_call(
        _merge_kernel,
        grid=(b, t // tm),
        in_specs=[pl.BlockSpec((1, 1, tm, 512), lambda i, j: (j, i, 0, 0)),
                  tok(512), tok(d), pl.BlockSpec((1, 6, d), lambda i, j: (i, 0, 0)),
                  const((d, d)), const((1, d)), const((d, 2 * LANES)), const((1, LANES)), const((1, LANES))],
        out_specs=[tok(d), pl.BlockSpec((1, tm * SUBLANES, LANES), lambda i, j: (i, j, 0)),
                   tok(LANES), tok(LANES), const((1, LANES))],
        out_shape=[jax.ShapeDtypeStruct((b, t, d), F32), jax.ShapeDtypeStruct((b, t * SUBLANES, LANES), F32),
                   jax.ShapeDtypeStruct((b, t, LANES), I32), jax.ShapeDtypeStruct((b, t, LANES), F32),
                   jax.ShapeDtypeStruct((1, LANES), F32)],
        scratch_shapes=[pltpu.VMEM((1, LANES), F32)],
        name="merge",
        compiler_params=_params("arbitrary", "arbitrary"),
    )(oa, ob, x, mod, w_out, g_ffn, wr_pad, br_pad, cnt0)


def _dispatch_kernel(pe_ref, hp_ref, dest_ref, h_ref, *rest, tm, zero_init):
    if zero_init:
        xs_ref, zbuf, sem, zsem = rest

        @pl.when(pl.program_id(0) == 0)
        def _():
            zbuf[...] = jnp.zeros_like(zbuf)

            def zero_copy(e):
                start = pl.multiple_of(pe_ref[e] - SLOT_BLOCK, SLOT_BLOCK)
                return pltpu.make_async_copy(zbuf, xs_ref.at[pl.ds(start, SLOT_BLOCK)], zsem)

            for e in range(N_EXPERTS):
                @pl.when(hp_ref[e] > 0)
                def _():
                    zero_copy(e).start()
            for e in range(N_EXPERTS):
                @pl.when(hp_ref[e] > 0)
                def _():
                    zero_copy(e).wait()
    else:
        _, xs_ref, sem = rest

    def row_copy(r, slot):
        return pltpu.make_async_copy(h_ref.at[r], xs_ref.at[slot], sem)

    def issue(i, carry):
        for j in range(ROW_UNROLL):
            for kq in range(TOP_K):
                row_copy(i * ROW_UNROLL + j, dest_ref[(i * ROW_UNROLL + j) * TOP_K + kq]).start(priority=kq % 2)
        return carry

    lax.fori_loop(0, tm // ROW_UNROLL, issue, 0)

    def drain(i, carry):
        for _ in range(ROW_UNROLL * TOP_K):
            row_copy(0, 0).wait()
        return carry

    lax.fori_loop(0, tm // ROW_UNROLL, drain, 0)


def _dispatch(pad_end, has_pad, dest_flat, h2_tiles, xs, n_slots, tm):
    n = h2_tiles.shape[0]
    tile = h2_tiles.shape[1:]
    zero_init = xs is None
    kern = functools.partial(_dispatch_kernel, tm=tm, zero_init=zero_init)
    in_specs = [pl.BlockSpec((tm * TOP_K,), lambda i, pe, hp: (i,), memory_space=pltpu.SMEM),
                pl.BlockSpec((tm,) + tile, lambda i, pe, hp: (i, 0, 0))]
    args = [pad_end, has_pad, dest_flat, h2_tiles]
    scratch = [pltpu.SemaphoreType.DMA(())]
    aliases = {}
    if zero_init:
        scratch = [pltpu.VMEM((SLOT_BLOCK,) + tile, F32), pltpu.SemaphoreType.DMA(()), pltpu.SemaphoreType.DMA(())]
    else:
        in_specs.append(pl.BlockSpec(memory_space=pl.ANY))
        args.append(xs)
        aliases = {4: 0}
    grid_spec = pltpu.PrefetchScalarGridSpec(
        num_scalar_prefetch=2,
        grid=(n // tm,),
        in_specs=in_specs,
        out_specs=pl.BlockSpec(memory_space=pl.ANY),
        scratch_shapes=scratch,
    )
    return pl.pallas_call(
        kern,
        grid_spec=grid_spec,
        out_shape=jax.ShapeDtypeStruct((n_slots,) + tile, F32),
        input_output_aliases=aliases,
        name="dispatch",
        compiler_params=_params("arbitrary"),
    )(*args)


def _expert_kernel(be_ref, nu_ref, xs_ref, wgu_ref, bgu_ref, wdn_ref, bdn_ref, ys_ref, wgu_b, wdn_b):
    j = pl.program_id(0)

    @pl.when(j < nu_ref[0])
    def _():
        @pl.when((j == 0) | (be_ref[j] != be_ref[jnp.maximum(j - 1, 0)]))
        def _():
            wgu_b[...] = wgu_ref[0].astype(BF16)
            wdn_b[...] = wdn_ref[0].astype(BF16)

        f = wdn_ref.shape[1]
        x = _load_row_tiles(xs_ref, SLOT_BLOCK).astype(BF16)
        gu = jnp.dot(x, wgu_b[...], preferred_element_type=F32) + bgu_ref[0]
        gate = jnp.minimum(gu[:, :f], SWIGLU_LIMIT)
        up = jnp.clip(gu[:, f:], -SWIGLU_LIMIT, SWIGLU_LIMIT)
        glu = gate / (1.0 + jnp.exp(-SWIGLU_ALPHA * gate))
        act = ((up + 1.0) * glu).astype(BF16)
        _store_row_tiles(ys_ref, jnp.dot(act, wdn_b[...], preferred_element_type=F32) + bdn_ref[0])


def _experts(block_expert, n_used, xs, wgu, bgu, wdn, bdn):
    n_slots = xs.shape[0] // SUBLANES
    nb = n_slots // SLOT_BLOCK
    f, d = wdn.shape[1:]
    rows = SLOT_BLOCK * SUBLANES

    def blk(i, be, nu):
        return (jnp.minimum(i, nu[0] - 1), 0)

    def wsel(i, be, nu):
        return (be[i], 0, 0)

    grid_spec = pltpu.PrefetchScalarGridSpec(
        num_scalar_prefetch=2,
        grid=(nb,),
        in_specs=[pl.BlockSpec((rows, LANES), blk),
                  pl.BlockSpec((1, d, 2 * f), wsel),
                  pl.BlockSpec((1, 1, 2 * f), wsel),
                  pl.BlockSpec((1, f, d), wsel),
                  pl.BlockSpec((1, 1, d), wsel)],
        out_specs=pl.BlockSpec((rows, LANES), blk),
        scratch_shapes=[pltpu.VMEM((d, 2 * f), BF16), pltpu.VMEM((f, d), BF16)],
    )
    return pl.pallas_call(
        _expert_kernel,
        grid_spec=grid_spec,
        out_shape=jax.ShapeDtypeStruct(xs.shape, F32),
        name="experts",
        compiler_params=_params("arbitrary"),
    )(block_expert, n_used, xs, wgu, bgu, wdn, bdn)


def _combine_kernel(dest_ref, dnext_ref, x1_ref, gate_ref, mod_ref, modf_ref, gfin_ref, ys_ref, y_ref,
                    buf, sem, *, tm):
    nt = pl.num_programs(1)
    step = pl.program_id(0) * nt + pl.program_id(1)
    n_steps = pl.num_programs(0) * nt
    half = TOP_K * tm * SUBLANES
    cur = lax.rem(step, 2)

    def row_copy(which, kq, r, slot):
        off = pl.multiple_of(which * half + (kq * tm + r) * SUBLANES, SUBLANES)
        return pltpu.make_async_copy(ys_ref.at[slot], buf.at[pl.ds(off, SUBLANES), :], sem.at[which])

    def gather(slots_ref, which):
        def issue(i, carry):
            for j in range(ROW_UNROLL):
                for kq in range(TOP_K):
                    r = i * ROW_UNROLL + j
                    row_copy(which, kq, r, slots_ref[r * TOP_K + kq]).start(priority=kq % 2)
            return carry

        lax.fori_loop(0, tm // ROW_UNROLL, issue, 0)

    @pl.when(step == 0)
    def _():
        gather(dest_ref, cur)

    @pl.when(step + 1 < n_steps)
    def _():
        gather(dnext_ref, 1 - cur)

    def drain(i, carry):
        for _ in range(ROW_UNROLL * TOP_K):
            row_copy(cur, 0, 0, 0).wait()
        return carry

    lax.fori_loop(0, tm // ROW_UNROLL, drain, 0)
    gates = gate_ref[0]
    moe = None
    for kq in range(TOP_K):
        start = pl.multiple_of(cur * half + kq * tm * SUBLANES, SUBLANES)
        rows = _load_row_tiles(buf.at[pl.ds(start, tm * SUBLANES), :], tm)
        term = gates[:, kq:kq + 1] * rows
        moe = term if moe is None else moe + term
    xo = x1_ref[0] + mod_ref[0][5:6] * moe
    modf = modf_ref[0]
    y_ref[0] = _rms(xo) * gfin_ref[...] * (1.0 + modf[1:2]) + modf[0:1]


def _combine(dest_flat, x1, gates, mod, modf, g_final, ys, tm):
    b, t, d = x1.shape
    nt = t // tm
    last = b * nt - 1
    kern = functools.partial(_combine_kernel, tm=tm)
    return pl.pallas_call(
        kern,
        grid=(b, nt),
        in_specs=[pl.BlockSpec((tm * TOP_K,), lambda i, j: (i * nt + j,), memory_space=pltpu.SMEM),
                  pl.BlockSpec((tm * TOP_K,), lambda i, j: (jnp.minimum(i * nt + j + 1, last),),
                               memory_space=pltpu.SMEM),
                  pl.BlockSpec((1, tm, d), lambda i, j: (i, j, 0)),
                  pl.BlockSpec((1, tm, LANES), lambda i, j: (i, j, 0)),
                  pl.BlockSpec((1, 6, d), lambda i, j: (i, 0, 0)),
                  pl.BlockSpec((1, 2, d), lambda i, j: (i, 0, 0)),
                  pl.BlockSpec((1, d), lambda i, j: (0, 0)),
                  pl.BlockSpec(memory_space=pl.ANY)],
        out_specs=pl.BlockSpec((1, tm, d), lambda i, j: (i, j, 0)),
        out_shape=jax.ShapeDtypeStruct((b, t, d), F32),
        scratch_shapes=[pltpu.VMEM((2 * TOP_K * tm * SUBLANES, LANES), F32), pltpu.SemaphoreType.DMA((2,))],
        name="combine",
        compiler_params=_params("arbitrary", "arbitrary"),
    )(dest_flat, dest_flat, x1, gates, mod, modf, g_final, ys)


def _rope_tables(pos):
    half = A_HD // 2
    inv_freq = ROPE_THETA ** (-jnp.arange(half, dtype=F32) / half)
    ang = pos.astype(F32)[:, None] * inv_freq[None, :]
    cos = jnp.cos(ang)
    sin = jnp.sin(ang)
    reps = LANES // A_HD
    return jnp.tile(jnp.concatenate([cos, cos], axis=1), (1, reps)), jnp.tile(jnp.concatenate([-sin, sin], axis=1), (1, reps))


def _reorder_w_in(w_in):
    offs = [0]
    for w in (512, 128, 128, 256, 64, 4, 256, 256, 512, 16, 512):
        offs.append(offs[-1] + w)
    seg = [w_in[:, offs[i]:offs[i + 1]] for i in range(11)]
    qa, ka, va, qi, ki, wi, qg, kg, vg, lr, rg = seg
    pad = jnp.zeros((w_in.shape[0], LANES - IDX_DIM - IDX_HEADS - GATE_RANK), w_in.dtype)
    return jnp.concatenate([qa, ka, va, qi, ki, wi, lr, pad, qg, kg, vg, rg], axis=1).astype(BF16)


def kernel(x_prompt, x_sample, cache_k, cache_v, cache_kidx, state_gla, c_prompt, c_sample,
           w_mod, b_mod, g_mix, g_ffn, w_in, gla_w_gate, gla_b_gate, gla_g_out, w_out,
           w_router, b_router, w_gate_up, b_gate_up, w_down, b_down,
           w_mod_final, b_mod_final, g_final):
    depth = w_in.shape[0]
    assert depth == 1
    bp, sp, d = x_prompt.shape
    bs, ts, _ = x_sample.shape
    past = cache_k.shape[2]
    n_exp = w_router.shape[2]
    f = w_down.shape[2]

    c_all = jnp.concatenate([c_prompt, c_sample], axis=0)
    w_r = _reorder_w_in(w_in[0])
    wg_pad = jnp.zeros((LANES, B_HEADS * B_DK), F32).at[MISC_LR:MISC_LR + GATE_RANK].set(gla_w_gate[0])
    bg = gla_b_gate[0].reshape(1, -1)
    wr_pad = jnp.zeros((d, LANES), F32).at[:, :n_exp].set(w_router[0])
    wr_hi = wr_pad.astype(BF16)
    wr_cat = jnp.concatenate([wr_hi, (wr_pad - wr_hi.astype(F32)).astype(BF16)], axis=1)
    br_pad = jnp.full((1, LANES), NEG_BIG, F32).at[0, :n_exp].set(b_router[0])
    w_out_b = w_out[0].astype(BF16)
    bgu = b_gate_up[0].reshape(n_exp, 1, 2 * f)
    bdn = b_down[0].reshape(n_exp, 1, d)
    g_out = gla_g_out[0].reshape(1, B_DV)

    mod_all = _adaln(c_all, w_mod[0], b_mod[0]).reshape(bp + bs, 6, d)
    modf_all = _adaln(c_all, w_mod_final, b_mod_final).reshape(bp + bs, 2, d)
    mod_p, mod_s = mod_all[:bp], mod_all[bp:]
    modf_p, modf_s = modf_all[:bp], modf_all[bp:]

    def mixer(x, mod, pos, tm, past_kv, state_t, cs, nc, tq):
        cos_t, sin_t = _rope_tables(pos)
        qa, ka, va, qi, ki, misc, qg, kg, vg, gg, rg, kb, vx, kib = _premix(
            x, mod, g_mix[0].reshape(1, d), w_r, wg_pad, bg, cos_t, sin_t, tm)
        if past_kv is None:
            oa = _attn_causal(qa, qi, misc, kb, vx, kib, tq)
        else:
            n_past = past_kv[0].shape[2]
            assert tq == x.shape[1] and n_past % LANES == 0
            oa = _attn_call(qa, qi, misc, ka, va, ki, past_kv, tq=tq, q_off=0, n_tiles=1,
                            n_ctx=n_past + LANES, causal=False, n_keys=n_past + ka.shape[1], n_groups=1)[None]
        ob, st = _gla(qg, kg, vg, gg, rg, state_t, g_out, cs=cs, nc=nc, bb=GLA_BATCH)
        return oa, ob, (ka, va, ki, st)

    state0_p = jnp.zeros((bp, B_HEADS, B_DV, B_DK), F32)
    oa_p, ob_p, (ka_p, va_p, ki_p, st_p) = mixer(
        x_prompt, mod_p, jnp.arange(sp), min(256, sp), None, state0_p, CHUNK, min(GLA_CHUNKS_PER_STEP, sp // CHUNK),
        min(256, sp))
    state0_s = jnp.swapaxes(state_gla[0], -1, -2)
    past_kv = (jnp.transpose(cache_k[0], (0, 2, 3, 1)).reshape(bs, A_KV * A_HD, past),
               jnp.transpose(cache_v[0], (0, 2, 3, 1)).reshape(bs, A_KV * A_HD, past),
               jnp.swapaxes(cache_kidx[0], 1, 2))
    oa_s, ob_s, (ka_s, va_s, ki_s, st_s) = mixer(
        x_sample, mod_s, past + jnp.arange(ts), ts, past_kv, state0_s, ts, 1, ts)

    cnt0 = jnp.zeros((1, LANES), F32)
    x1_p, h2_p, ri_p, gate_p, cnt1 = _merge(oa_p, ob_p, x_prompt, mod_p, w_out_b, g_ffn[0].reshape(1, d),
                                            wr_cat, br_pad, cnt0, min(256, sp))
    x1_s, h2_s, ri_s, gate_s, cnt2 = _merge(oa_s, ob_s, x_sample, mod_s, w_out_b, g_ffn[0].reshape(1, d),
                                            wr_cat, br_pad, cnt1, ts)
    counts = cnt2[0, :n_exp].astype(I32)
    padded = (counts + SLOT_BLOCK - 1) // SLOT_BLOCK * SLOT_BLOCK
    pad_end = jnp.cumsum(padded)
    pad_start = pad_end - padded
    n_asg = (bp * sp + bs * ts) * TOP_K
    nb = -(-n_asg // SLOT_BLOCK) + n_exp
    n_slots = nb * SLOT_BLOCK
    block_start = jnp.arange(nb, dtype=I32) * SLOT_BLOCK
    block_expert = jnp.minimum(jnp.sum((pad_end[None, :] <= block_start[:, None]).astype(I32), axis=1), n_exp - 1)
    n_used = (pad_end[-1:] // SLOT_BLOCK).astype(I32)
    has_pad = (padded > counts).astype(I32)
    expert_ids = jnp.arange(n_exp, dtype=I32)

    def dests(ri):
        e, rank = ri[..., :TOP_K], ri[..., TOP_K:2 * TOP_K]
        start = jnp.sum(jnp.where(e[..., None] == expert_ids, pad_start.astype(I32), 0), axis=-1)
        return (start + rank).reshape(-1)

    dest_p, dest_s = dests(ri_p), dests(ri_s)
    pad_end_i = pad_end.astype(I32)
    tile = (SUBLANES, LANES)
    xs = _dispatch(pad_end_i, has_pad, dest_p, h2_p.reshape((bp * sp,) + tile), None, n_slots, min(ROW_DMA_TILE, sp))
    xs = _dispatch(pad_end_i, has_pad, dest_s, h2_s.reshape((bs * ts,) + tile), xs, n_slots, min(256, bs * ts))
    ys = _experts(block_expert, n_used, xs.reshape(n_slots * SUBLANES, LANES), w_gate_up[0], bgu, w_down[0], bdn)
    ys = ys.reshape((n_slots,) + tile)
    y_p = _combine(dest_p, x1_p, gate_p, mod_p, modf_p, g_final.reshape(1, d), ys, min(ROW_DMA_TILE, sp))
    y_s = _combine(dest_s, x1_s, gate_s, mod_s, modf_s, g_final.reshape(1, d), ys, ts)

    def kv(a, b, t):
        return a.reshape(1, b, t, A_KV, A_HD)

    return (y_p, y_s,
            kv(ka_p, bp, sp), kv(va_p, bp, sp), ki_p[None], jnp.swapaxes(st_p, -1, -2)[None],
            kv(ka_s, bs, ts), kv(va_s, bs, ts), ki_s[None], jnp.swapaxes(st_s, -1, -2)[None])
```

```python
import functools

import jax
import jax.numpy as jnp
from jax import lax
from jax.experimental import pallas as pl
from jax.experimental.pallas import tpu as pltpu

F32 = jnp.float32
BF16 = jnp.bfloat16
I32 = jnp.int32
HI = lax.Precision.HIGHEST

CHUNK = 64
CHUNK_SHIFT = 6
EPS = 1e-6
ROPE_THETA = 10000.0
A_HD = 64
A_HEADS = 8
A_KV = 2
IDX_HEADS = 4
IDX_DIM = 64
INDEX_TOPK = 256
B_HEADS = 4
B_DK = 64
B_DV = 128
GATE_RANK = 16
GATE_TAU = 16.0
N_EXPERTS = 32
TOP_K = 4
SWIGLU_LIMIT = 7.0
SWIGLU_ALPHA = 1.702

LANES = 128
SUBLANES = 8
ROW_UNROLL = 8
ROW_DMA_TILE = 512
GLA_SUB = 16
GLA_BATCH = 2
GLA_CHUNKS_PER_STEP = 8
GLA_EXP_CLAMP = 80.0
SLOT_BLOCK = 512
VMEM_LIMIT = 56 * 1024 * 1024
INT_MIN = -2147483648
NEG_BIG = -1e30
SEARCH_TWO_BIT_MAX_KEYS = 768

C_QA, C_KA, C_VA, C_QI, C_MISC, C_QG, C_KG, C_VG, C_RG, C_END = 0, 512, 640, 768, 1024, 1152, 1408, 1664, 2176, 2688
MISC_WI = 64
MISC_LR = 68


def _params(*sem):
    return pltpu.CompilerParams(dimension_semantics=sem, vmem_limit_bytes=VMEM_LIMIT)


def _nt(a, b):
    return lax.dot_general(a, b, (((1,), (1,)), ((), ())), preferred_element_type=F32)


def _tn(a, b):
    return lax.dot_general(a, b, (((0,), (0,)), ((), ())), preferred_element_type=F32)


def _rms(x):
    return x * lax.rsqrt(jnp.mean(x * x, axis=-1, keepdims=True) + EPS)


def _silu(x):
    return x / (1.0 + jnp.exp(-x))


def _store_row_tiles(ref, val):
    rows, width = val.shape
    assert width == SUBLANES * LANES
    for s in range(SUBLANES):
        ref[pl.ds(s, rows, stride=SUBLANES), :] = val[:, s * LANES:(s + 1) * LANES]


def _load_row_tiles(ref, rows):
    return jnp.concatenate([ref[pl.ds(s, rows, stride=SUBLANES), :] for s in range(SUBLANES)], axis=1)


def _value_with_ones(vb):
    return jnp.concatenate([vb, jnp.ones_like(vb)], axis=-1)


def _adaln_kernel(c_ref, w_ref, b_ref, o_ref):
    a = _silu(c_ref[...])
    o_ref[...] = jnp.dot(a, w_ref[...], preferred_element_type=F32, precision=HI) + b_ref[...]


def _adaln(c, w, b):
    r, d = c.shape
    n = w.shape[1]
    tn = 512
    return pl.pallas_call(
        _adaln_kernel,
        grid=(n // tn,),
        in_specs=[pl.BlockSpec((r, d), lambda j: (0, 0)),
                  pl.BlockSpec((d, tn), lambda j: (0, j)),
                  pl.BlockSpec((1, tn), lambda j: (0, j))],
        out_specs=pl.BlockSpec((r, tn), lambda j: (0, j)),
        out_shape=jax.ShapeDtypeStruct((r, n), F32),
        name="adaln",
        compiler_params=_params("arbitrary"),
    )(c, w, b.reshape(1, n))


def _premix_kernel(x_ref, mod_ref, g_ref, w_ref, wg_ref, bg_ref, cos_ref, sin_ref,
                   qa_ref, ka_ref, va_ref, qi_ref, ki_ref, misc_ref, qg_ref, kg_ref, vg_ref, gg_ref, rg_ref,
                   kb_ref, vx_ref, kib_ref):
    x = x_ref[0]
    mod = mod_ref[0]
    hb = (_rms(x) * g_ref[...] * (1.0 + mod[1:2]) + mod[0:1]).astype(BF16)
    tm = x.shape[0]

    def project(c0, c1):
        return jnp.dot(hb, w_ref[:, c0:c1], preferred_element_type=F32)

    cos = cos_ref[...]
    sin = sin_ref[...]
    lane = lax.broadcasted_iota(I32, (tm, LANES), 1)
    lower_half = (lane & (A_HD - 1)) < (A_HD // 2)

    def rope(xc):
        rot = jnp.where(lower_half, pltpu.roll(xc, LANES - A_HD // 2, 1), pltpu.roll(xc, A_HD // 2, 1))
        return xc * cos + rot * sin

    seg = project(C_QA, C_KA)
    for j in range((C_KA - C_QA) // LANES):
        qa_ref[0, :, j * LANES:(j + 1) * LANES] = (rope(seg[:, j * LANES:(j + 1) * LANES]) * (A_HD ** -0.5)).astype(BF16)
    seg = project(C_KA, C_QI)
    ka = rope(seg[:, :LANES])
    va = seg[:, LANES:]
    ka_ref[0] = ka
    va_ref[0] = va
    kb_ref[0] = ka.astype(BF16)
    vx_ref[0] = _value_with_ones(va.astype(BF16))
    seg = project(C_QI, C_QG)
    for j in range((C_MISC - C_QI) // LANES):
        qi_ref[0, :, j * LANES:(j + 1) * LANES] = rope(seg[:, j * LANES:(j + 1) * LANES]).astype(BF16)
    m = seg[:, C_MISC - C_QI:]
    mr = rope(m)
    ki_ref[0] = mr[:, :IDX_DIM]
    kib_ref[0] = mr[:, :IDX_DIM].astype(BF16)
    misc_ref[0] = jnp.where(lane < IDX_DIM, mr, m * (IDX_HEADS ** -0.5))
    xg = jnp.dot(m, wg_ref[...], preferred_element_type=F32, precision=HI) + bg_ref[...]
    gg_ref[0] = (jnp.minimum(xg, 0.0) - jnp.log(1.0 + jnp.exp(-jnp.abs(xg)))) * (1.0 / GATE_TAU)
    qg_ref[0] = project(C_QG, C_KG) * (B_DK ** -0.5)
    kg_ref[0] = project(C_KG, C_VG)
    vg_ref[0] = project(C_VG, C_RG)
    rg_ref[0] = project(C_RG, C_END)


def _premix(x, mod, g_mix, w_r, wg_pad, bg, cos_t, sin_t, tm):
    b, t, d = x.shape
    widths = [(512, BF16), (128, F32), (128, F32), (256, BF16), (64, F32), (128, F32),
              (256, F32), (256, F32), (512, F32), (256, F32), (512, F32),
              (128, BF16), (2 * LANES, BF16), (IDX_DIM, BF16)]
    return pl.pallas_call(
        _premix_kernel,
        grid=(b, t // tm),
        in_specs=[pl.BlockSpec((1, tm, d), lambda i, j: (i, j, 0)),
                  pl.BlockSpec((1, 6, d), lambda i, j: (i, 0, 0)),
                  pl.BlockSpec((1, d), lambda i, j: (0, 0)),
                  pl.BlockSpec((d, C_END), lambda i, j: (0, 0)),
                  pl.BlockSpec((LANES, 256), lambda i, j: (0, 0)),
                  pl.BlockSpec((1, 256), lambda i, j: (0, 0)),
                  pl.BlockSpec((tm, LANES), lambda i, j: (j, 0)),
                  pl.BlockSpec((tm, LANES), lambda i, j: (j, 0))],
        out_specs=[pl.BlockSpec((1, tm, w), lambda i, j: (i, j, 0)) for w, _ in widths],
        out_shape=[jax.ShapeDtypeStruct((b, t, w), dt) for w, dt in widths],
        name="premix",
        compiler_params=_params("arbitrary", "arbitrary"),
    )(x, mod, g_mix, w_r, wg_pad, bg, cos_t, sin_t)


def _attn_kernel(*refs, tq, n_ctx, top_k, causal, n_keys, q_off, n_groups, n_past, keep_all):
    if n_past:
        q_ref, qi_ref, misc_ref, k_ref, v_ref, ki_ref, pk_ref, pv_ref, pki_ref, o_ref = refs
        assert n_ctx - n_past == LANES and k_ref.shape[1] <= LANES

        def new_columns(rows_ref):
            new = rows_ref[0]
            n_new, width = new.shape
            if width < LANES:
                new = jnp.concatenate([new, jnp.zeros((n_new, LANES - width), F32)], axis=1)
            square = jnp.concatenate([new, jnp.zeros((LANES - n_new, LANES), F32)], axis=0)
            return square.T[:width].astype(BF16)

        k = jnp.concatenate([pk_ref[0].astype(BF16), new_columns(k_ref)], axis=1)
        v = jnp.concatenate([pv_ref[0].astype(BF16), new_columns(v_ref)], axis=1)
        vx = jnp.concatenate([v, jnp.ones_like(v)], axis=0)
        kib = jnp.concatenate([pki_ref[0].astype(BF16), new_columns(ki_ref)], axis=1)
    else:
        q_ref, qi_ref, misc_ref, k_ref, vx_ref, ki_ref, o_ref = refs
        k = k_ref[0]
        vx = vx_ref[0]
        kib = ki_ref[0]
    feature_major = bool(n_past)
    rows = tq // n_groups
    row0 = q_off + pl.program_id(1) * tq
    keys, kks = [], []
    idx_dots = {}
    for g in range(n_groups):
        qi = qi_ref[0, g * rows:(g + 1) * rows, :]
        for h in range(IDX_HEADS):
            qh = qi[:, h * IDX_DIM:(h + 1) * IDX_DIM]
            idx_dots[g, h] = jnp.dot(qh, kib, preferred_element_type=F32) if feature_major else _nt(qh, kib)
    for g in range(n_groups):
        misc = misc_ref[0, g * rows:(g + 1) * rows, :]
        isc = jnp.zeros((rows, n_ctx), F32)
        for h in range(IDX_HEADS):
            isc = isc + misc[:, MISC_WI + h:MISC_WI + h + 1] * jnp.maximum(idx_dots[g, h], 0.0)
        kpos = lax.broadcasted_iota(I32, (rows, n_ctx), 1)
        if causal:
            row = lax.broadcasted_iota(I32, (rows, 1), 0) + (row0 + g * rows)
            key_lim = (lax.shift_right_logical(row, CHUNK_SHIFT) + 1) * CHUNK
        else:
            key_lim = jnp.full((rows, 1), n_keys, I32)
        bits = pltpu.bitcast(isc, I32)
        key = jnp.where(bits < 0, INT_MIN - bits, bits)
        keys.append(jnp.where(kpos < key_lim, key, INT_MIN))
        kks.append(jnp.minimum(key_lim, top_k).astype(F32))

    def count_at_least(g, cand):
        return jnp.sum(jnp.where(keys[g] >= (cand ^ INT_MIN), 1.0, 0.0), axis=1, keepdims=True)

    bits_per_pass = 2 if n_ctx <= SEARCH_TWO_BIT_MAX_KEYS else 1

    def search(i, ans):
        out = []
        for g in range(n_groups):
            if bits_per_pass == 1:
                cand = ans[g] | lax.shift_left(jnp.int32(1), 31 - i)
                out.append(jnp.where(count_at_least(g, cand) >= kks[g], cand, ans[g]))
            else:
                a1 = ans[g] | lax.shift_left(jnp.int32(1), 31 - 2 * i)
                a2 = ans[g] | lax.shift_left(jnp.int32(1), 30 - 2 * i)
                a3 = a1 | a2
                c1, c2, c3 = count_at_least(g, a1), count_at_least(g, a2), count_at_least(g, a3)
                kk = kks[g]
                out.append(jnp.where(c3 >= kk, a3, jnp.where(c1 >= kk, a1, jnp.where(c2 >= kk, a2, ans[g]))))
        return tuple(out)

    ans = tuple(jnp.zeros((rows, 1), I32) for _ in range(n_groups))
    if not keep_all:
        ans = lax.fori_loop(0, 32 // bits_per_pass, search, ans)
    r = lax.broadcasted_iota(I32, (LANES, LANES), 0)
    c = lax.broadcasted_iota(I32, (LANES, LANES), 1)
    upper = jnp.where(r < c, 1.0, 0.0).astype(BF16)
    n_blk = n_ctx // LANES
    thrs = [ans[g] ^ INT_MIN for g in range(n_groups)]
    ties = {(g, j): jnp.where(keys[g][:, j * LANES:(j + 1) * LANES] == thrs[g], 1.0, 0.0)
            for g in range(n_groups) for j in range(n_blk)}
    within = {gj: jnp.dot(t.astype(BF16), upper, preferred_element_type=F32) for gj, t in ties.items()}
    bias_groups = []
    for g in range(n_groups):
        gt = keys[g] > thrs[g]
        need = kks[g] - jnp.sum(jnp.where(gt, 1.0, 0.0), axis=1, keepdims=True)
        carry = jnp.zeros((rows, 1), F32)
        blocks = []
        for j in range(n_blk):
            sl = slice(j * LANES, (j + 1) * LANES)
            take_tie = jnp.where(within[g, j] + carry < need, ties[g, j], 0.0)
            blocks.append(jnp.where(gt[:, sl], 0.0, jnp.where(take_tie > 0.0, 0.0, NEG_BIG)))
            carry = carry + jnp.sum(ties[g, j], axis=1, keepdims=True)
        bias_groups.append(jnp.concatenate(blocks, axis=1))
    bias = jnp.concatenate(bias_groups, axis=0)

    q = q_ref[0]
    rep = A_HEADS // A_KV
    if feature_major and tq * A_HEADS <= 2 * LANES:
        q_groups = [jnp.concatenate([q[:, hh * A_HD:(hh + 1) * A_HD] for hh in range(g * rep, (g + 1) * rep)], axis=0)
                    for g in range(A_KV)]
        s_all = jnp.concatenate([jnp.dot(q_groups[g], k[g * A_HD:(g + 1) * A_HD, :], preferred_element_type=F32)
                                 for g in range(A_KV)], axis=0) + jnp.concatenate([bias] * A_HEADS, axis=0)
        p_all = jnp.exp((s_all - jnp.max(s_all, axis=1, keepdims=True)).astype(BF16))
        ox = _nt(p_all, vx)
        outs = [ox[hh * tq:(hh + 1) * tq, (hh // rep) * A_HD:(hh // rep + 1) * A_HD]
                / ox[hh * tq:(hh + 1) * tq, LANES:LANES + 1] for hh in range(A_HEADS)]
        o_ref[0] = jnp.concatenate(outs, axis=1).astype(BF16)
        return
    kgs = [k[g * A_HD:(g + 1) * A_HD, :] if feature_major else k[:, g * A_HD:(g + 1) * A_HD] for g in range(A_KV)]
    qs = [q[:, hh * A_HD:(hh + 1) * A_HD] for hh in range(A_HEADS)]
    ss = [(jnp.dot(qh, kgs[hh // rep], preferred_element_type=F32) if feature_major else _nt(qh, kgs[hh // rep]))
          + bias for hh, qh in enumerate(qs)]
    ps = [jnp.exp((s - jnp.max(s, axis=1, keepdims=True)).astype(BF16)) for s in ss]
    oxs = [_nt(p, vx) if feature_major else jnp.dot(p, vx, preferred_element_type=F32) for p in ps]
    outs = [ox[:, (hh // rep) * A_HD:(hh // rep + 1) * A_HD] / ox[:, LANES:LANES + 1] for hh, ox in enumerate(oxs)]
    o_ref[0] = jnp.concatenate(outs, axis=1).astype(BF16)


def _attn_call(q, qi, misc, kb, vx, kib, past, *, tq, q_off, n_tiles, n_ctx, causal, n_keys, n_groups):
    b = q.shape[0]
    t0 = q_off // tq
    top_k = min(INDEX_TOPK, n_keys // 4)
    n_past = 0 if past is None else past[0].shape[2]
    n_own = n_ctx if past is None else kb.shape[1]
    kern = functools.partial(_attn_kernel, tq=tq, n_ctx=n_ctx, top_k=top_k, causal=causal, n_keys=n_keys,
                             q_off=q_off, n_groups=n_groups, n_past=n_past,
                             keep_all=causal and q_off + n_tiles * tq <= top_k)

    def tok(w):
        return pl.BlockSpec((1, tq, w), lambda i, j: (i, t0 + j, 0))

    def ctx(rows, w):
        return pl.BlockSpec((1, rows, w), lambda i, j: (i, 0, 0))

    in_specs = [tok(512), tok(256), tok(LANES)] + [ctx(n_own, a.shape[2]) for a in (kb, vx, kib)]
    args = [q, qi, misc, kb, vx, kib]
    if past is not None:
        in_specs += [ctx(a.shape[1], n_past) for a in past]
        args += list(past)
    return pl.pallas_call(
        kern,
        grid=(b, n_tiles),
        in_specs=in_specs,
        out_specs=pl.BlockSpec((1, tq, 512), lambda i, j: (i, j, 0)),
        out_shape=jax.ShapeDtypeStruct((b, n_tiles * tq, 512), BF16),
        name="attn",
        compiler_params=_params("arbitrary", "arbitrary"),
    )(*args)


def _attn_causal(q, qi, misc, kb, vx, kib, tq):
    t = q.shape[1]
    return jnp.stack([
        _attn_call(q, qi, misc, kb, vx, kib, None, tq=tq, q_off=c * tq, n_tiles=1, n_ctx=(c + 1) * tq,
                   causal=True, n_keys=t, n_groups=2)
        for c in range(t // tq)])


def _gla_kernel(q_ref, k_ref, v_ref, g_ref, rg_ref, s0_ref, go_ref, ob_ref, st_ref, st_scr, *, cs, nc, bb):
    j = pl.program_id(1)
    hk, hv, hc = B_HEADS * B_DK, B_HEADS * B_DV, B_HEADS * cs

    def head_of(idx, width):
        return lax.shift_right_logical(idx, width.bit_length() - 1)

    def same_head(rows, row_w, cols, col_w):
        r = head_of(lax.broadcasted_iota(I32, (rows, cols), 0), row_w)
        c = head_of(lax.broadcasted_iota(I32, (rows, cols), 1), col_w)
        return r == c

    keep_kb = same_head(hc, cs, hk, B_DK)
    keep_vb = same_head(hc, cs, hv, B_DV)
    keep_st = same_head(hv, B_DV, hk, B_DK)
    t_idx = lax.broadcasted_iota(I32, (cs, hc), 0)
    s_idx = lax.broadcasted_iota(I32, (cs, hc), 1) & (cs - 1)
    keep_a = t_idx >= s_idx
    r = lax.broadcasted_iota(I32, (cs, cs), 0)
    c = lax.broadcasted_iota(I32, (cs, cs), 1)
    tri = jnp.where(r >= c, 1.0, 0.0)
    go = go_ref[...]

    @pl.when(j == 0)
    def _():
        for bi in range(bb):
            blocks = []
            for h in range(B_HEADS):
                parts = [jnp.zeros((B_DV, B_DK), F32)] * B_HEADS
                parts[h] = s0_ref[bi, h]
                blocks.append(jnp.concatenate(parts, axis=1))
            st_scr[bi] = jnp.concatenate(blocks, axis=0)

    inst = [(ci, bi) for ci in range(nc) for bi in range(bb)]
    sls = {ci: slice(ci * cs, (ci + 1) * cs) for ci in range(nc)}
    bcum = {t: jnp.dot(tri, g_ref[t[1], sls[t[0]], :], preferred_element_type=F32, precision=HI) for t in inst}
    n_sub = cs // GLA_SUB
    v_bf, qdec, kdec, qts, kts = {}, {}, {}, {}, {}
    for t in inst:
        ci, bi = t
        q = q_ref[bi, sls[ci], :]
        k = k_ref[bi, sls[ci], :]
        bc = bcum[t]
        blast = bc[cs - 1:cs, :]
        qdec[t] = (q * jnp.exp(bc)).astype(BF16)
        kdec[t] = (k * jnp.exp(blast - bc)).astype(BF16)
        v_bf[t] = v_ref[bi, sls[ci], :].astype(BF16)
        for i in range(n_sub):
            rs = slice(i * GLA_SUB, (i + 1) * GLA_SUB)
            ref = bc[i * GLA_SUB:i * GLA_SUB + 1, :]
            qts[t, i] = (q[rs, :] * jnp.exp(bc[rs, :] - ref)).astype(BF16)
            kt = (k * jnp.exp(jnp.minimum(ref - bc, GLA_EXP_CLAMP))).astype(BF16)
            kts[t, i] = jnp.where(keep_kb, jnp.concatenate([kt] * B_HEADS, axis=0), 0.0)
    a_rows = {(t, i): _nt(qts[t, i], kts[t, i]) for t in inst for i in range(n_sub)}
    a_mat = {t: jnp.where(keep_a, jnp.concatenate([a_rows[t, i] for i in range(n_sub)], axis=0), 0.0).astype(BF16)
             for t in inst}
    o_intra = {t: jnp.dot(a_mat[t], jnp.where(keep_vb, jnp.concatenate([v_bf[t]] * B_HEADS, axis=0), 0.0),
                          preferred_element_type=F32) for t in inst}
    kv = {t: jnp.where(keep_st, _tn(v_bf[t], kdec[t]), 0.0) for t in inst}
    for t in inst:
        ci, bi = t
        st = st_scr[bi]
        o = o_intra[t] + _nt(qdec[t], st.astype(BF16))
        st_scr[bi] = st * jnp.exp(bcum[t][cs - 1:cs, :]) + kv[t]
        rg = rg_ref[bi, sls[ci], :]
        for h in range(B_HEADS):
            vs = slice(h * B_DV, (h + 1) * B_DV)
            ob_ref[bi, sls[ci], vs] = (_rms(o[:, vs]) * go * _silu(rg[:, vs])).astype(BF16)

    @pl.when(j == pl.num_programs(1) - 1)
    def _():
        for bi in range(bb):
            st = st_scr[bi]
            for h in range(B_HEADS):
                st_ref[bi, h] = st[h * B_DV:(h + 1) * B_DV, h * B_DK:(h + 1) * B_DK]


def _gla(qg, kg, vg, gg, rg, state_t, g_out, *, cs, nc, bb):
    b, t, _ = qg.shape
    assert b % bb == 0
    tt = cs * nc
    kern = functools.partial(_gla_kernel, cs=cs, nc=nc, bb=bb)

    def tok(w):
        return pl.BlockSpec((bb, tt, w), lambda i, j: (i, j, 0))

    st_spec = pl.BlockSpec((bb, B_HEADS, B_DV, B_DK), lambda i, j: (i, 0, 0, 0))
    return pl.pallas_call(
        kern,
        grid=(b // bb, t // tt),
        in_specs=[tok(256), tok(256), tok(512), tok(256), tok(512), st_spec,
                  pl.BlockSpec((1, B_DV), lambda i, j: (0, 0))],
        out_specs=[tok(512), st_spec],
        out_shape=[jax.ShapeDtypeStruct((b, t, 512), BF16),
                   jax.ShapeDtypeStruct((b, B_HEADS, B_DV, B_DK), F32)],
        scratch_shapes=[pltpu.VMEM((bb, B_HEADS * B_DV, B_HEADS * B_DK), F32)],
        name="gla",
        compiler_params=_params("arbitrary", "arbitrary"),
    )(qg, kg, vg, gg, rg, state_t, g_out)


def _merge_kernel(oa_ref, ob_ref, x_ref, mod_ref, wo_ref, gf_ref, wr_ref, br_ref, cnt0_ref,
                  x1_ref, h2_ref, ri_ref, rgate_ref, cnt_ref, carry_scr):
    @pl.when((pl.program_id(0) == 0) & (pl.program_id(1) == 0))
    def _():
        carry_scr[...] = cnt0_ref[...]

    mod = mod_ref[0]
    cat = jnp.concatenate([oa_ref[0, 0], ob_ref[0]], axis=1)
    x1 = x_ref[0] + mod[2:3] * jnp.dot(cat, wo_ref[...], preferred_element_type=F32)
    x1_ref[0] = x1
    h2 = _rms(x1) * gf_ref[...] * (1.0 + mod[4:5]) + mod[3:4]
    _store_row_tiles(h2_ref.at[0], h2)
    tm = x1.shape[0]
    lane = lax.broadcasted_iota(I32, (tm, LANES), 1).astype(F32)
    hi = h2.astype(BF16)
    lo = (h2 - hi.astype(F32)).astype(BF16)
    wr = wr_ref[...]
    a = jnp.dot(hi, wr, preferred_element_type=F32)
    left = (a[:, :LANES] + a[:, LANES:]) + jnp.dot(lo, wr[:, :LANES], preferred_element_type=F32) + br_ref[...]
    idx, val = [], []
    for _ in range(TOP_K):
        m = jnp.max(left, axis=1, keepdims=True)
        e = jnp.argmax(left, axis=1, keepdims=True).astype(F32)
        idx.append(e)
        val.append(m)
        left = jnp.where(lane == e, -jnp.inf, left)
    ex = [jnp.exp(vv - val[0]) for vv in val]
    den = ex[0] + ex[1] + ex[2] + ex[3]
    onehot = jnp.zeros((tm, LANES), F32)
    for e in idx:
        onehot = onehot + jnp.where(lane == e, 1.0, 0.0)
    r = lax.broadcasted_iota(I32, (tm, tm), 0)
    c = lax.broadcasted_iota(I32, (tm, tm), 1)
    earlier = jnp.where(r > c, 1.0, 0.0).astype(BF16)
    before = jnp.dot(earlier, onehot.astype(BF16), preferred_element_type=F32) + carry_scr[...]
    ri = jnp.zeros((tm, LANES), F32)
    rgate = jnp.zeros((tm, LANES), F32)
    for kq in range(TOP_K):
        rank = jnp.sum(jnp.where(lane == idx[kq], before, 0.0), axis=1, keepdims=True)
        ri = jnp.where(lane == kq, idx[kq], ri)
        ri = jnp.where(lane == TOP_K + kq, rank, ri)
        rgate = jnp.where(lane == kq, ex[kq] / den, rgate)
    ri_ref[0] = ri.astype(I32)
    rgate_ref[0] = rgate
    carry_scr[...] = carry_scr[...] + jnp.sum(onehot, axis=0, keepdims=True)
    cnt_ref[...] = carry_scr[...]


def _merge(oa, ob, x, mod, w_out, g_ffn, wr_pad, br_pad, cnt0, tm):
    b, t, d = x.shape

    def tok(w):
        return pl.BlockSpec((1, tm, w), lambda i, j: (i, j, 0))

    def const(s):
        return pl.BlockSpec(s, lambda i, j: (0, 0))

    return pl.pallas_call(
        _merge_kernel,
        grid=(b, t // tm),
        in_specs=[pl.BlockSpec((1, 1, tm, 512), lambda i, j: (j, i, 0, 0)),
                  tok(512), tok(d), pl.BlockSpec((1, 6, d), lambda i, j: (i, 0, 0)),
                  const((d, d)), const((1, d)), const((d, 2 * LANES)), const((1, LANES)), const((1, LANES))],
        out_specs=[tok(d), pl.BlockSpec((1, tm * SUBLANES, LANES), lambda i, j: (i, j, 0)),
                   tok(LANES), tok(LANES), const((1, LANES))],
        out_shape=[jax.ShapeDtypeStruct((b, t, d), F32), jax.ShapeDtypeStruct((b, t * SUBLANES, LANES), F32),
                   jax.ShapeDtypeStruct((b, t, LANES), I32), jax.ShapeDtypeStruct((b, t, LANES), F32),
                   jax.ShapeDtypeStruct((1, LANES), F32)],
        scratch_shapes=[pltpu.VMEM((1, LANES), F32)],
        name="merge",
        compiler_params=_params("arbitrary", "arbitrary"),
    )(oa, ob, x, mod, w_out, g_ffn, wr_pad, br_pad, cnt0)


def _dispatch_kernel(pe_ref, hp_ref, dest_ref, h_ref, *rest, tm, zero_init):
    if zero_init:
        xs_ref, zbuf, sem, zsem = rest

        @pl.when(pl.program_id(0) == 0)
        def _():
            zbuf[...] = jnp.zeros_like(zbuf)

            def zero_copy(e):
                start = pl.multiple_of(pe_ref[e] - SLOT_BLOCK, SLOT_BLOCK)
                return pltpu.make_async_copy(zbuf, xs_ref.at[pl.ds(start, SLOT_BLOCK)], zsem)

            for e in range(N_EXPERTS):
                @pl.when(hp_ref[e] > 0)
                def _():
                    zero_copy(e).start()
            for e in range(N_EXPERTS):
                @pl.when(hp_ref[e] > 0)
                def _():
                    zero_copy(e).wait()
    else:
        _, xs_ref, sem = rest

    def row_copy(r, slot):
        return pltpu.make_async_copy(h_ref.at[r], xs_ref.at[slot], sem)

    def issue(i, carry):
        for j in range(ROW_UNROLL):
            for kq in range(TOP_K):
                row_copy(i * ROW_UNROLL + j, dest_ref[(i * ROW_UNROLL + j) * TOP_K + kq]).start(priority=kq % 2)
        return carry

    lax.fori_loop(0, tm // ROW_UNROLL, issue, 0)

    def drain(i, carry):
        for _ in range(ROW_UNROLL * TOP_K):
            row_copy(0, 0).wait()
        return carry

    lax.fori_loop(0, tm // ROW_UNROLL, drain, 0)


def _dispatch(pad_end, has_pad, dest_flat, h2_tiles, xs, n_slots, tm):
    n = h2_tiles.shape[0]
    tile = h2_tiles.shape[1:]
    zero_init = xs is None
    kern = functools.partial(_dispatch_kernel, tm=tm, zero_init=zero_init)
    in_specs = [pl.BlockSpec((tm * TOP_K,), lambda i, pe, hp: (i,), memory_space=pltpu.SMEM),
                pl.BlockSpec((tm,) + tile, lambda i, pe, hp: (i, 0, 0))]
    args = [pad_end, has_pad, dest_flat, h2_tiles]
    scratch = [pltpu.SemaphoreType.DMA(())]
    aliases = {}
    if zero_init:
        scratch = [pltpu.VMEM((SLOT_BLOCK,) + tile, F32), pltpu.SemaphoreType.DMA(()), pltpu.SemaphoreType.DMA(())]
    else:
        in_specs.append(pl.BlockSpec(memory_space=pl.ANY))
        args.append(xs)
        aliases = {4: 0}
    grid_spec = pltpu.PrefetchScalarGridSpec(
        num_scalar_prefetch=2,
        grid=(n // tm,),
        in_specs=in_specs,
        out_specs=pl.BlockSpec(memory_space=pl.ANY),
        scratch_shapes=scratch,
    )
    return pl.pallas_call(
        kern,
        grid_spec=grid_spec,
        out_shape=jax.ShapeDtypeStruct((n_slots,) + tile, F32),
        input_output_aliases=aliases,
        name="dispatch",
        compiler_params=_params("arbitrary"),
    )(*args)


def _expert_kernel(be_ref, nx_ref, nu_ref, xs_ref, wgu_hbm, bgu_ref, wdn_hbm, bdn_ref, ys_ref,
                   gu_f32, dn_f32, wgu_b, wdn_b, sem):
    j = pl.program_id(0)

    def fetch(e):
        return (pltpu.make_async_copy(wgu_hbm.at[e], gu_f32, sem.at[0]),
                pltpu.make_async_copy(wdn_hbm.at[e], dn_f32, sem.at[1]))

    @pl.when(j < nu_ref[0])
    def _():
        @pl.when(j == 0)
        def _():
            for cp in fetch(be_ref[0]):
                cp.start()

        @pl.when((j == 0) | (be_ref[j] != be_ref[jnp.maximum(j - 1, 0)]))
        def _():
            for cp in fetch(be_ref[j]):
                cp.wait()
            wgu_b[...] = gu_f32[...].astype(BF16)
            wdn_b[...] = dn_f32[...].astype(BF16)

            @pl.when(nx_ref[j] >= 0)
            def _():
                for cp in fetch(nx_ref[j]):
                    cp.start()

        f = wdn_b.shape[0]
        x = _load_row_tiles(xs_ref, SLOT_BLOCK).astype(BF16)
        gu = jnp.dot(x, wgu_b[...], preferred_element_type=F32) + bgu_ref[0]
        gate = jnp.minimum(gu[:, :f], SWIGLU_LIMIT)
        up = jnp.clip(gu[:, f:], -SWIGLU_LIMIT, SWIGLU_LIMIT)
        glu = gate / (1.0 + jnp.exp(-SWIGLU_ALPHA * gate))
        act = ((up + 1.0) * glu).astype(BF16)
        _store_row_tiles(ys_ref, jnp.dot(act, wdn_b[...], preferred_element_type=F32) + bdn_ref[0])


def _experts(block_expert, next_expert, n_used, xs, wgu, bgu, wdn, bdn):
    n_slots = xs.shape[0] // SUBLANES
    nb = n_slots // SLOT_BLOCK
    f, d = wdn.shape[1:]
    rows = SLOT_BLOCK * SUBLANES

    def blk(i, be, nx, nu):
        return (jnp.minimum(i, nu[0] - 1), 0)

    def bsel(i, be, nx, nu):
        return (be[i], 0, 0)

    grid_spec = pltpu.PrefetchScalarGridSpec(
        num_scalar_prefetch=3,
        grid=(nb,),
        in_specs=[pl.BlockSpec((rows, LANES), blk),
                  pl.BlockSpec(memory_space=pl.ANY),
                  pl.BlockSpec((1, 1, 2 * f), bsel),
                  pl.BlockSpec(memory_space=pl.ANY),
                  pl.BlockSpec((1, 1, d), bsel)],
        out_specs=pl.BlockSpec((rows, LANES), blk),
        scratch_shapes=[pltpu.VMEM((d, 2 * f), F32), pltpu.VMEM((f, d), F32),
                        pltpu.VMEM((d, 2 * f), BF16), pltpu.VMEM((f, d), BF16), pltpu.SemaphoreType.DMA((2,))],
    )
    return pl.pallas_call(
        _expert_kernel,
        grid_spec=grid_spec,
        out_shape=jax.ShapeDtypeStruct(xs.shape, F32),
        name="experts",
        compiler_params=_params("arbitrary"),
    )(block_expert, next_expert, n_used, xs, wgu, bgu, wdn, bdn)


def _combine_kernel(dest_ref, dnext_ref, x1_ref, gate_ref, mod_ref, modf_ref, gfin_ref, ys_ref, y_ref,
                    buf, sem, *, tm):
    nt = pl.num_programs(1)
    step = pl.program_id(0) * nt + pl.program_id(1)
    n_steps = pl.num_programs(0) * nt
    half = TOP_K * tm * SUBLANES
    cur = lax.rem(step, 2)

    def row_copy(which, kq, r, slot):
        off = pl.multiple_of(which * half + (kq * tm + r) * SUBLANES, SUBLANES)
        return pltpu.make_async_copy(ys_ref.at[slot], buf.at[pl.ds(off, SUBLANES), :], sem.at[which])

    def gather(slots_ref, which):
        def issue(i, carry):
            for j in range(ROW_UNROLL):
                for kq in range(TOP_K):
                    r = i * ROW_UNROLL + j
                    row_copy(which, kq, r, slots_ref[r * TOP_K + kq]).start(priority=kq % 2)
            return carry

        lax.fori_loop(0, tm // ROW_UNROLL, issue, 0)

    @pl.when(step == 0)
    def _():
        gather(dest_ref, cur)

    @pl.when(step + 1 < n_steps)
    def _():
        gather(dnext_ref, 1 - cur)

    def drain(i, carry):
        for _ in range(ROW_UNROLL * TOP_K):
            row_copy(cur, 0, 0, 0).wait()
        return carry

    lax.fori_loop(0, tm // ROW_UNROLL, drain, 0)
    gates = gate_ref[0]
    moe = None
    for kq in range(TOP_K):
        start = pl.multiple_of(cur * half + kq * tm * SUBLANES, SUBLANES)
        rows = _load_row_tiles(buf.at[pl.ds(start, tm * SUBLANES), :], tm)
        term = gates[:, kq:kq + 1] * rows
        moe = term if moe is None else moe + term
    xo = x1_ref[0] + mod_ref[0][5:6] * moe
    modf = modf_ref[0]
    y_ref[0] = _rms(xo) * gfin_ref[...] * (1.0 + modf[1:2]) + modf[0:1]


def _combine(dest_flat, x1, gates, mod, modf, g_final, ys, tm):
    b, t, d = x1.shape
    nt = t // tm
    last = b * nt - 1
    kern = functools.partial(_combine_kernel, tm=tm)
    return pl.pallas_call(
        kern,
        grid=(b, nt),
        in_specs=[pl.BlockSpec((tm * TOP_K,), lambda i, j: (i * nt + j,), memory_space=pltpu.SMEM),
                  pl.BlockSpec((tm * TOP_K,), lambda i, j: (jnp.minimum(i * nt + j + 1, last),),
                               memory_space=pltpu.SMEM),
                  pl.BlockSpec((1, tm, d), lambda i, j: (i, j, 0)),
                  pl.BlockSpec((1, tm, LANES), lambda i, j: (i, j, 0)),
                  pl.BlockSpec((1, 6, d), lambda i, j: (i, 0, 0)),
                  pl.BlockSpec((1, 2, d), lambda i, j: (i, 0, 0)),
                  pl.BlockSpec((1, d), lambda i, j: (0, 0)),
                  pl.BlockSpec(memory_space=pl.ANY)],
        out_specs=pl.BlockSpec((1, tm, d), lambda i, j: (i, j, 0)),
        out_shape=jax.ShapeDtypeStruct((b, t, d), F32),
        scratch_shapes=[pltpu.VMEM((2 * TOP_K * tm * SUBLANES, LANES), F32), pltpu.SemaphoreType.DMA((2,))],
        name="combine",
        compiler_params=_params("arbitrary", "arbitrary"),
    )(dest_flat, dest_flat, x1, gates, mod, modf, g_final, ys)


def _rope_tables(pos):
    half = A_HD // 2
    inv_freq = ROPE_THETA ** (-jnp.arange(half, dtype=F32) / half)
    ang = pos.astype(F32)[:, None] * inv_freq[None, :]
    cos = jnp.cos(ang)
    sin = jnp.sin(ang)
    reps = LANES // A_HD
    return jnp.tile(jnp.concatenate([cos, cos], axis=1), (1, reps)), jnp.tile(jnp.concatenate([-sin, sin], axis=1), (1, reps))


def _reorder_w_in(w_in):
    offs = [0]
    for w in (512, 128, 128, 256, 64, 4, 256, 256, 512, 16, 512):
        offs.append(offs[-1] + w)
    seg = [w_in[:, offs[i]:offs[i + 1]] for i in range(11)]
    qa, ka, va, qi, ki, wi, qg, kg, vg, lr, rg = seg
    pad = jnp.zeros((w_in.shape[0], LANES - IDX_DIM - IDX_HEADS - GATE_RANK), w_in.dtype)
    return jnp.concatenate([qa, ka, va, qi, ki, wi, lr, pad, qg, kg, vg, rg], axis=1).astype(BF16)


def kernel(x_prompt, x_sample, cache_k, cache_v, cache_kidx, state_gla, c_prompt, c_sample,
           w_mod, b_mod, g_mix, g_ffn, w_in, gla_w_gate, gla_b_gate, gla_g_out, w_out,
           w_router, b_router, w_gate_up, b_gate_up, w_down, b_down,
           w_mod_final, b_mod_final, g_final):
    depth = w_in.shape[0]
    assert depth == 1
    bp, sp, d = x_prompt.shape
    bs, ts, _ = x_sample.shape
    past = cache_k.shape[2]
    n_exp = w_router.shape[2]
    f = w_down.shape[2]

    c_all = jnp.concatenate([c_prompt, c_sample], axis=0)
    w_r = _reorder_w_in(w_in[0])
    wg_pad = jnp.zeros((LANES, B_HEADS * B_DK), F32).at[MISC_LR:MISC_LR + GATE_RANK].set(gla_w_gate[0])
    bg = gla_b_gate[0].reshape(1, -1)
    wr_pad = jnp.zeros((d, LANES), F32).at[:, :n_exp].set(w_router[0])
    wr_hi = wr_pad.astype(BF16)
    wr_cat = jnp.concatenate([wr_hi, (wr_pad - wr_hi.astype(F32)).astype(BF16)], axis=1)
    br_pad = jnp.full((1, LANES), NEG_BIG, F32).at[0, :n_exp].set(b_router[0])
    w_out_b = w_out[0].astype(BF16)
    bgu = b_gate_up[0].reshape(n_exp, 1, 2 * f)
    bdn = b_down[0].reshape(n_exp, 1, d)
    g_out = gla_g_out[0].reshape(1, B_DV)

    mod_all = _adaln(c_all, w_mod[0], b_mod[0]).reshape(bp + bs, 6, d)
    modf_all = _adaln(c_all, w_mod_final, b_mod_final).reshape(bp + bs, 2, d)
    mod_p, mod_s = mod_all[:bp], mod_all[bp:]
    modf_p, modf_s = modf_all[:bp], modf_all[bp:]

    def mixer(x, mod, pos, tm, past_kv, state_t, cs, nc, tq):
        cos_t, sin_t = _rope_tables(pos)
        qa, ka, va, qi, ki, misc, qg, kg, vg, gg, rg, kb, vx, kib = _premix(
            x, mod, g_mix[0].reshape(1, d), w_r, wg_pad, bg, cos_t, sin_t, tm)
        if past_kv is None:
            oa = _attn_causal(qa, qi, misc, kb, vx, kib, tq)
        else:
            n_past = past_kv[0].shape[2]
            assert tq == x.shape[1] and n_past % LANES == 0
            oa = _attn_call(qa, qi, misc, ka, va, ki, past_kv, tq=tq, q_off=0, n_tiles=1,
                            n_ctx=n_past + LANES, causal=False, n_keys=n_past + ka.shape[1], n_groups=1)[None]
        ob, st = _gla(qg, kg, vg, gg, rg, state_t, g_out, cs=cs, nc=nc, bb=GLA_BATCH)
        return oa, ob, (ka, va, ki, st)

    state0_p = jnp.zeros((bp, B_HEADS, B_DV, B_DK), F32)
    oa_p, ob_p, (ka_p, va_p, ki_p, st_p) = mixer(
        x_prompt, mod_p, jnp.arange(sp), min(256, sp), None, state0_p, CHUNK, min(GLA_CHUNKS_PER_STEP, sp // CHUNK),
        min(256, sp))
    state0_s = jnp.swapaxes(state_gla[0], -1, -2)
    past_kv = (jnp.transpose(cache_k[0], (0, 2, 3, 1)).reshape(bs, A_KV * A_HD, past),
               jnp.transpose(cache_v[0], (0, 2, 3, 1)).reshape(bs, A_KV * A_HD, past),
               jnp.swapaxes(cache_kidx[0], 1, 2))
    oa_s, ob_s, (ka_s, va_s, ki_s, st_s) = mixer(
        x_sample, mod_s, past + jnp.arange(ts), ts, past_kv, state0_s, ts, 1, ts)

    cnt0 = jnp.zeros((1, LANES), F32)
    x1_p, h2_p, ri_p, gate_p, cnt1 = _merge(oa_p, ob_p, x_prompt, mod_p, w_out_b, g_ffn[0].reshape(1, d),
                                            wr_cat, br_pad, cnt0, min(256, sp))
    x1_s, h2_s, ri_s, gate_s, cnt2 = _merge(oa_s, ob_s, x_sample, mod_s, w_out_b, g_ffn[0].reshape(1, d),
                                            wr_cat, br_pad, cnt1, ts)
    counts = cnt2[0, :n_exp].astype(I32)
    padded = (counts + SLOT_BLOCK - 1) // SLOT_BLOCK * SLOT_BLOCK
    pad_end = jnp.cumsum(padded)
    pad_start = pad_end - padded
    n_asg = (bp * sp + bs * ts) * TOP_K
    nb = -(-n_asg // SLOT_BLOCK) + n_exp
    n_slots = nb * SLOT_BLOCK
    block_start = jnp.arange(nb, dtype=I32) * SLOT_BLOCK
    block_expert = jnp.minimum(jnp.sum((pad_end[None, :] <= block_start[:, None]).astype(I32), axis=1), n_exp - 1)
    n_used = (pad_end[-1:] // SLOT_BLOCK).astype(I32)
    following = pad_end.astype(I32)[block_expert] // SLOT_BLOCK
    next_expert = jnp.where(following < n_used[0], block_expert[jnp.minimum(following, nb - 1)], -1).astype(I32)
    has_pad = (padded > counts).astype(I32)
    expert_ids = jnp.arange(n_exp, dtype=I32)

    def dests(ri):
        e, rank = ri[..., :TOP_K], ri[..., TOP_K:2 * TOP_K]
        start = jnp.sum(jnp.where(e[..., None] == expert_ids, pad_start.astype(I32), 0), axis=-1)
        return (start + rank).reshape(-1)

    dest_p, dest_s = dests(ri_p), dests(ri_s)
    pad_end_i = pad_end.astype(I32)
    tile = (SUBLANES, LANES)
    xs = _dispatch(pad_end_i, has_pad, dest_p, h2_p.reshape((bp * sp,) + tile), None, n_slots, min(ROW_DMA_TILE, sp))
    xs = _dispatch(pad_end_i, has_pad, dest_s, h2_s.reshape((bs * ts,) + tile), xs, n_slots, min(256, bs * ts))
    ys = _experts(block_expert, next_expert, n_used, xs.reshape(n_slots * SUBLANES, LANES), w_gate_up[0], bgu,
                  w_down[0], bdn)
    ys = ys.reshape((n_slots,) + tile)
    y_p = _combine(dest_p, x1_p, gate_p, mod_p, modf_p, g_final.reshape(1, d), ys, min(256, sp))
    y_s = _combine(dest_s, x1_s, gate_s, mod_s, modf_s, g_final.reshape(1, d), ys, ts)

    def kv(a, b, t):
        return a.reshape(1, b, t, A_KV, A_HD)

    return (y_p, y_s,
            kv(ka_p, bp, sp), kv(va_p, bp, sp), ki_p[None], jnp.swapaxes(st_p, -1, -2)[None],
            kv(ka_s, bs, ts), kv(va_s, bs, ts), ki_s[None], jnp.swapaxes(st_s, -1, -2)[None])
```

```python
import functools

import jax
import jax.numpy as jnp
from jax import lax
from jax.experimental import pallas as pl
from jax.experimental.pallas import tpu as pltpu

F32 = jnp.float32
BF16 = jnp.bfloat16
I32 = jnp.int32
HI = lax.Precision.HIGHEST

CHUNK = 64
CHUNK_SHIFT = 6
EPS = 1e-6
ROPE_THETA = 10000.0
A_HD = 64
A_HEADS = 8
A_KV = 2
IDX_HEADS = 4
IDX_DIM = 64
INDEX_TOPK = 256
B_HEADS = 4
B_DK = 64
B_DV = 128
GATE_RANK = 16
GATE_TAU = 16.0
N_EXPERTS = 32
TOP_K = 4
SWIGLU_LIMIT = 7.0
SWIGLU_ALPHA = 1.702

LANES = 128
SUBLANES = 8
TOKEN_TILE = 256
MERGE_TILE = 512
ROW_UNROLL = 8
ROW_DMA_TILE = 512
GLA_SUB = 16
GLA_BATCH = 2
GLA_CHUNKS_PER_STEP = 8
GLA_EXP_CLAMP = 80.0
SLOT_BLOCK = 512
VMEM_LIMIT = 56 * 1024 * 1024
INT_MIN = -2147483648
NEG_BIG = -1e30
SEARCH_TWO_BIT_MAX_KEYS = 768

C_QA, C_KA, C_VA, C_QI, C_MISC, C_QG, C_KG, C_VG, C_RG, C_END = 0, 512, 640, 768, 1024, 1152, 1408, 1664, 2176, 2688
MISC_WI = 64
MISC_LR = 68


def _params(*sem):
    return pltpu.CompilerParams(dimension_semantics=sem, vmem_limit_bytes=VMEM_LIMIT)


def _nt(a, b):
    return lax.dot_general(a, b, (((1,), (1,)), ((), ())), preferred_element_type=F32)


def _tn(a, b):
    return lax.dot_general(a, b, (((0,), (0,)), ((), ())), preferred_element_type=F32)


def _rms(x):
    return x * lax.rsqrt(jnp.mean(x * x, axis=-1, keepdims=True) + EPS)


def _silu(x):
    return x / (1.0 + jnp.exp(-x))


def _store_row_tiles(ref, val):
    rows, width = val.shape
    assert width == SUBLANES * LANES
    for s in range(SUBLANES):
        ref[pl.ds(s, rows, stride=SUBLANES), :] = val[:, s * LANES:(s + 1) * LANES]


def _load_row_tiles(ref, rows):
    return jnp.concatenate([ref[pl.ds(s, rows, stride=SUBLANES), :] for s in range(SUBLANES)], axis=1)


def _value_with_ones(vb):
    return jnp.concatenate([vb, jnp.ones_like(vb)], axis=-1)


def _adaln_kernel(c_ref, w_ref, b_ref, o_ref):
    a = _silu(c_ref[...])
    o_ref[...] = jnp.dot(a, w_ref[...], preferred_element_type=F32, precision=HI) + b_ref[...]


def _adaln(c, w, b):
    r, d = c.shape
    n = w.shape[1]
    tn = 512
    return pl.pallas_call(
        _adaln_kernel,
        grid=(n // tn,),
        in_specs=[pl.BlockSpec((r, d), lambda j: (0, 0)),
                  pl.BlockSpec((d, tn), lambda j: (0, j)),
                  pl.BlockSpec((1, tn), lambda j: (0, j))],
        out_specs=pl.BlockSpec((r, tn), lambda j: (0, j)),
        out_shape=jax.ShapeDtypeStruct((r, n), F32),
        name="adaln",
        compiler_params=_params("arbitrary"),
    )(c, w, b.reshape(1, n))


def _premix_kernel(x_ref, mod_ref, g_ref, w_ref, wg_ref, bg_ref, cos_ref, sin_ref,
                   qa_ref, ka_ref, va_ref, qi_ref, ki_ref, misc_ref, qg_ref, kg_ref, vg_ref, gg_ref, rg_ref,
                   kb_ref, vx_ref, kib_ref):
    x = x_ref[0]
    mod = mod_ref[0]
    hb = (_rms(x) * g_ref[...] * (1.0 + mod[1:2]) + mod[0:1]).astype(BF16)
    tm = x.shape[0]

    def project(c0, c1):
        return jnp.dot(hb, w_ref[:, c0:c1], preferred_element_type=F32)

    cos = cos_ref[...]
    sin = sin_ref[...]
    lane = lax.broadcasted_iota(I32, (tm, LANES), 1)
    lower_half = (lane & (A_HD - 1)) < (A_HD // 2)

    def rope(xc):
        rot = jnp.where(lower_half, pltpu.roll(xc, LANES - A_HD // 2, 1), pltpu.roll(xc, A_HD // 2, 1))
        return xc * cos + rot * sin

    seg = project(C_QA, C_KA)
    for j in range((C_KA - C_QA) // LANES):
        qa_ref[0, :, j * LANES:(j + 1) * LANES] = (rope(seg[:, j * LANES:(j + 1) * LANES]) * (A_HD ** -0.5)).astype(BF16)
    seg = project(C_KA, C_QI)
    ka = rope(seg[:, :C_VA - C_KA])
    va = seg[:, C_VA - C_KA:]
    ka_ref[0] = ka
    va_ref[0] = va
    kb_ref[0] = ka.astype(BF16)
    vx_ref[0] = _value_with_ones(va.astype(BF16))
    seg = project(C_QI, C_QG)
    for j in range((C_MISC - C_QI) // LANES):
        qi_ref[0, :, j * LANES:(j + 1) * LANES] = rope(seg[:, j * LANES:(j + 1) * LANES]).astype(BF16)
    m = seg[:, C_MISC - C_QI:]
    mr = rope(m)
    ki_ref[0] = mr[:, :IDX_DIM]
    kib_ref[0] = mr[:, :IDX_DIM].astype(BF16)
    misc_ref[0] = jnp.where(lane < IDX_DIM, mr, m * (IDX_HEADS ** -0.5))
    xg = jnp.dot(m, wg_ref[...], preferred_element_type=F32, precision=HI) + bg_ref[...]
    gg_ref[0] = (jnp.minimum(xg, 0.0) - jnp.log(1.0 + jnp.exp(-jnp.abs(xg)))) * (1.0 / GATE_TAU)
    qg_ref[0] = project(C_QG, C_KG) * (B_DK ** -0.5)
    kg_ref[0] = project(C_KG, C_VG)
    vg_ref[0] = project(C_VG, C_RG)
    rg_ref[0] = project(C_RG, C_END)


def _premix(x, mod, g_mix, w_r, wg_pad, bg, cos_t, sin_t, tm):
    b, t, d = x.shape
    widths = [(512, BF16), (128, F32), (128, F32), (256, BF16), (64, F32), (128, F32),
              (256, F32), (256, F32), (512, F32), (256, F32), (512, F32),
              (128, BF16), (2 * LANES, BF16), (IDX_DIM, BF16)]
    return pl.pallas_call(
        _premix_kernel,
        grid=(b, t // tm),
        in_specs=[pl.BlockSpec((1, tm, d), lambda i, j: (i, j, 0)),
                  pl.BlockSpec((1, 6, d), lambda i, j: (i, 0, 0)),
                  pl.BlockSpec((1, d), lambda i, j: (0, 0)),
                  pl.BlockSpec((d, C_END), lambda i, j: (0, 0)),
                  pl.BlockSpec((LANES, 256), lambda i, j: (0, 0)),
                  pl.BlockSpec((1, 256), lambda i, j: (0, 0)),
                  pl.BlockSpec((tm, LANES), lambda i, j: (j, 0)),
                  pl.BlockSpec((tm, LANES), lambda i, j: (j, 0))],
        out_specs=[pl.BlockSpec((1, tm, w), lambda i, j: (i, j, 0)) for w, _ in widths],
        out_shape=[jax.ShapeDtypeStruct((b, t, w), dt) for w, dt in widths],
        name="premix",
        compiler_params=_params("arbitrary", "arbitrary"),
    )(x, mod, g_mix, w_r, wg_pad, bg, cos_t, sin_t)


def _attn_kernel(*refs, tq, n_ctx, top_k, causal, n_keys, q_off, n_groups, n_past, keep_all):
    if n_past:
        q_ref, qi_ref, misc_ref, k_ref, v_ref, ki_ref, pk_ref, pv_ref, pki_ref, o_ref = refs
        assert n_ctx - n_past == LANES and k_ref.shape[1] <= LANES

        def new_columns(rows_ref):
            new = rows_ref[0]
            n_new, width = new.shape
            if width < LANES:
                new = jnp.concatenate([new, jnp.zeros((n_new, LANES - width), F32)], axis=1)
            square = jnp.concatenate([new, jnp.zeros((LANES - n_new, LANES), F32)], axis=0)
            return square.T[:width].astype(BF16)

        k = jnp.concatenate([pk_ref[0].astype(BF16), new_columns(k_ref)], axis=1)
        v = jnp.concatenate([pv_ref[0].astype(BF16), new_columns(v_ref)], axis=1)
        vx = jnp.concatenate([v, jnp.ones_like(v)], axis=0)
        kib = jnp.concatenate([pki_ref[0].astype(BF16), new_columns(ki_ref)], axis=1)
    else:
        q_ref, qi_ref, misc_ref, k_ref, vx_ref, ki_ref, o_ref = refs
        k = k_ref[0]
        vx = vx_ref[0]
        kib = ki_ref[0]
    feature_major = bool(n_past)
    rows = tq // n_groups
    row0 = q_off + pl.program_id(1) * tq
    keys, kks = [], []
    idx_dots = {}
    for g in range(n_groups):
        qi = qi_ref[0, g * rows:(g + 1) * rows, :]
        for h in range(IDX_HEADS):
            qh = qi[:, h * IDX_DIM:(h + 1) * IDX_DIM]
            idx_dots[g, h] = jnp.dot(qh, kib, preferred_element_type=F32) if feature_major else _nt(qh, kib)
    for g in range(n_groups):
        misc = misc_ref[0, g * rows:(g + 1) * rows, :]
        isc = jnp.zeros((rows, n_ctx), F32)
        for h in range(IDX_HEADS):
            isc = isc + misc[:, MISC_WI + h:MISC_WI + h + 1] * jnp.maximum(idx_dots[g, h], 0.0)
        kpos = lax.broadcasted_iota(I32, (rows, n_ctx), 1)
        if causal:
            row = lax.broadcasted_iota(I32, (rows, 1), 0) + (row0 + g * rows)
            key_lim = (lax.shift_right_logical(row, CHUNK_SHIFT) + 1) * CHUNK
        else:
            key_lim = jnp.full((rows, 1), n_keys, I32)
        bits = pltpu.bitcast(isc, I32)
        key = jnp.where(bits < 0, INT_MIN - bits, bits)
        keys.append(jnp.where(kpos < key_lim, key, INT_MIN))
        kks.append(jnp.minimum(key_lim, top_k).astype(F32))

    def count_at_least(g, cand):
        return jnp.sum(jnp.where(keys[g] >= (cand ^ INT_MIN), 1.0, 0.0), axis=1, keepdims=True)

    bits_per_pass = 2 if n_ctx <= SEARCH_TWO_BIT_MAX_KEYS else 1

    def search(i, ans):
        out = []
        for g in range(n_groups):
            if bits_per_pass == 1:
                cand = ans[g] | lax.shift_left(jnp.int32(1), 31 - i)
                out.append(jnp.where(count_at_least(g, cand) >= kks[g], cand, ans[g]))
            else:
                a1 = ans[g] | lax.shift_left(jnp.int32(1), 31 - 2 * i)
                a2 = ans[g] | lax.shift_left(jnp.int32(1), 30 - 2 * i)
                a3 = a1 | a2
                c1, c2, c3 = count_at_least(g, a1), count_at_least(g, a2), count_at_least(g, a3)
                kk = kks[g]
                out.append(jnp.where(c3 >= kk, a3, jnp.where(c1 >= kk, a1, jnp.where(c2 >= kk, a2, ans[g]))))
        return tuple(out)

    ans = tuple(jnp.zeros((rows, 1), I32) for _ in range(n_groups))
    if not keep_all:
        ans = lax.fori_loop(0, 32 // bits_per_pass, search, ans)
    r = lax.broadcasted_iota(I32, (LANES, LANES), 0)
    c = lax.broadcasted_iota(I32, (LANES, LANES), 1)
    upper = jnp.where(r < c, 1.0, 0.0).astype(BF16)
    n_blk = n_ctx // LANES
    thrs = [ans[g] ^ INT_MIN for g in range(n_groups)]
    ties = {(g, j): jnp.where(keys[g][:, j * LANES:(j + 1) * LANES] == thrs[g], 1.0, 0.0)
            for g in range(n_groups) for j in range(n_blk)}
    within = {gj: jnp.dot(t.astype(BF16), upper, preferred_element_type=F32) for gj, t in ties.items()}
    bias_groups = []
    for g in range(n_groups):
        gt = keys[g] > thrs[g]
        need = kks[g] - jnp.sum(jnp.where(gt, 1.0, 0.0), axis=1, keepdims=True)
        carry = jnp.zeros((rows, 1), F32)
        blocks = []
        for j in range(n_blk):
            sl = slice(j * LANES, (j + 1) * LANES)
            take_tie = jnp.where(within[g, j] + carry < need, ties[g, j], 0.0)
            blocks.append(jnp.where(gt[:, sl], 0.0, jnp.where(take_tie > 0.0, 0.0, NEG_BIG)))
            carry = carry + jnp.sum(ties[g, j], axis=1, keepdims=True)
        bias_groups.append(jnp.concatenate(blocks, axis=1))
    bias = jnp.concatenate(bias_groups, axis=0)

    q = q_ref[0]
    rep = A_HEADS // A_KV
    if feature_major and tq * A_HEADS <= 2 * LANES:
        q_groups = [jnp.concatenate([q[:, hh * A_HD:(hh + 1) * A_HD] for hh in range(g * rep, (g + 1) * rep)], axis=0)
                    for g in range(A_KV)]
        s_all = jnp.concatenate([jnp.dot(q_groups[g], k[g * A_HD:(g + 1) * A_HD, :], preferred_element_type=F32)
                                 for g in range(A_KV)], axis=0) + jnp.concatenate([bias] * A_HEADS, axis=0)
        p_all = jnp.exp((s_all - jnp.max(s_all, axis=1, keepdims=True)).astype(BF16))
        ox = _nt(p_all, vx)
        outs = [ox[hh * tq:(hh + 1) * tq, (hh // rep) * A_HD:(hh // rep + 1) * A_HD]
                / ox[hh * tq:(hh + 1) * tq, LANES:LANES + 1] for hh in range(A_HEADS)]
        o_ref[0] = jnp.concatenate(outs, axis=1).astype(BF16)
        return
    kgs = [k[g * A_HD:(g + 1) * A_HD, :] if feature_major else k[:, g * A_HD:(g + 1) * A_HD] for g in range(A_KV)]
    qs = [q[:, hh * A_HD:(hh + 1) * A_HD] for hh in range(A_HEADS)]
    ss = [(jnp.dot(qh, kgs[hh // rep], preferred_element_type=F32) if feature_major else _nt(qh, kgs[hh // rep]))
          + bias for hh, qh in enumerate(qs)]
    ps = [jnp.exp((s - jnp.max(s, axis=1, keepdims=True)).astype(BF16)) for s in ss]
    oxs = [_nt(p, vx) if feature_major else jnp.dot(p, vx, preferred_element_type=F32) for p in ps]
    outs = [ox[:, (hh // rep) * A_HD:(hh // rep + 1) * A_HD] / ox[:, LANES:LANES + 1] for hh, ox in enumerate(oxs)]
    o_ref[0] = jnp.concatenate(outs, axis=1).astype(BF16)


def _attn_call(q, qi, misc, kb, vx, kib, past, *, tq, q_off, n_tiles, n_ctx, causal, n_keys, n_groups):
    b = q.shape[0]
    t0 = q_off // tq
    top_k = min(INDEX_TOPK, n_keys // 4)
    n_past = 0 if past is None else past[0].shape[2]
    n_own = n_ctx if past is None else kb.shape[1]
    kern = functools.partial(_attn_kernel, tq=tq, n_ctx=n_ctx, top_k=top_k, causal=causal, n_keys=n_keys,
                             q_off=q_off, n_groups=n_groups, n_past=n_past,
                             keep_all=causal and q_off + n_tiles * tq <= top_k)

    def tok(w):
        return pl.BlockSpec((1, tq, w), lambda i, j: (i, t0 + j, 0))

    def ctx(rows, w):
        return pl.BlockSpec((1, rows, w), lambda i, j: (i, 0, 0))

    in_specs = [tok(512), tok(256), tok(LANES)] + [ctx(n_own, a.shape[2]) for a in (kb, vx, kib)]
    args = [q, qi, misc, kb, vx, kib]
    if past is not None:
        in_specs += [ctx(a.shape[1], n_past) for a in past]
        args += list(past)
    return pl.pallas_call(
        kern,
        grid=(b, n_tiles),
        in_specs=in_specs,
        out_specs=pl.BlockSpec((1, tq, 512), lambda i, j: (i, j, 0)),
        out_shape=jax.ShapeDtypeStruct((b, n_tiles * tq, 512), BF16),
        name="attn",
        compiler_params=_params("arbitrary", "arbitrary"),
    )(*args)


def _attn_causal(q, qi, misc, kb, vx, kib, tq):
    t = q.shape[1]
    return jnp.stack([
        _attn_call(q, qi, misc, kb, vx, kib, None, tq=tq, q_off=c * tq, n_tiles=1, n_ctx=(c + 1) * tq,
                   causal=True, n_keys=t, n_groups=2)
        for c in range(t // tq)])


def _gla_kernel(q_ref, k_ref, v_ref, g_ref, rg_ref, s0_ref, go_ref, ob_ref, st_ref, st_scr, *, cs, nc, bb):
    j = pl.program_id(1)
    hk, hv, hc = B_HEADS * B_DK, B_HEADS * B_DV, B_HEADS * cs

    def head_of(idx, width):
        return lax.shift_right_logical(idx, width.bit_length() - 1)

    def same_head(rows, row_w, cols, col_w):
        r = head_of(lax.broadcasted_iota(I32, (rows, cols), 0), row_w)
        c = head_of(lax.broadcasted_iota(I32, (rows, cols), 1), col_w)
        return r == c

    keep_kb = same_head(hc, cs, hk, B_DK)
    keep_vb = same_head(hc, cs, hv, B_DV)
    keep_st = same_head(hv, B_DV, hk, B_DK)
    t_idx = lax.broadcasted_iota(I32, (cs, hc), 0)
    s_idx = lax.broadcasted_iota(I32, (cs, hc), 1) & (cs - 1)
    keep_a = t_idx >= s_idx
    r = lax.broadcasted_iota(I32, (cs, cs), 0)
    c = lax.broadcasted_iota(I32, (cs, cs), 1)
    tri = jnp.where(r >= c, 1.0, 0.0)
    go = go_ref[...]

    @pl.when(j == 0)
    def _():
        for bi in range(bb):
            blocks = []
            for h in range(B_HEADS):
                parts = [jnp.zeros((B_DV, B_DK), F32)] * B_HEADS
                parts[h] = s0_ref[bi, h]
                blocks.append(jnp.concatenate(parts, axis=1))
            st_scr[bi] = jnp.concatenate(blocks, axis=0)

    inst = [(ci, bi) for ci in range(nc) for bi in range(bb)]
    sls = {ci: slice(ci * cs, (ci + 1) * cs) for ci in range(nc)}
    bcum = {t: jnp.dot(tri, g_ref[t[1], sls[t[0]], :], preferred_element_type=F32, precision=HI) for t in inst}
    n_sub = cs // GLA_SUB
    v_bf, qdec, kdec, qts, kts = {}, {}, {}, {}, {}
    for t in inst:
        ci, bi = t
        q = q_ref[bi, sls[ci], :]
        k = k_ref[bi, sls[ci], :]
        bc = bcum[t]
        blast = bc[cs - 1:cs, :]
        qdec[t] = (q * jnp.exp(bc)).astype(BF16)
        kdec[t] = (k * jnp.exp(blast - bc)).astype(BF16)
        v_bf[t] = v_ref[bi, sls[ci], :].astype(BF16)
        for i in range(n_sub):
            rs = slice(i * GLA_SUB, (i + 1) * GLA_SUB)
            ref = bc[i * GLA_SUB:i * GLA_SUB + 1, :]
            qts[t, i] = (q[rs, :] * jnp.exp(bc[rs, :] - ref)).astype(BF16)
            kt = (k * jnp.exp(jnp.minimum(ref - bc, GLA_EXP_CLAMP))).astype(BF16)
            kts[t, i] = jnp.where(keep_kb, jnp.concatenate([kt] * B_HEADS, axis=0), 0.0)
    a_rows = {(t, i): _nt(qts[t, i], kts[t, i]) for t in inst for i in range(n_sub)}
    a_mat = {t: jnp.where(keep_a, jnp.concatenate([a_rows[t, i] for i in range(n_sub)], axis=0), 0.0).astype(BF16)
             for t in inst}
    o_intra = {t: jnp.dot(a_mat[t], jnp.where(keep_vb, jnp.concatenate([v_bf[t]] * B_HEADS, axis=0), 0.0),
                          preferred_element_type=F32) for t in inst}
    kv = {t: jnp.where(keep_st, _tn(v_bf[t], kdec[t]), 0.0) for t in inst}
    for t in inst:
        ci, bi = t
        st = st_scr[bi]
        o = o_intra[t] + _nt(qdec[t], st.astype(BF16))
        st_scr[bi] = st * jnp.exp(bcum[t][cs - 1:cs, :]) + kv[t]
        rg = rg_ref[bi, sls[ci], :]
        for h in range(B_HEADS):
            vs = slice(h * B_DV, (h + 1) * B_DV)
            ob_ref[bi, sls[ci], vs] = (_rms(o[:, vs]) * go * _silu(rg[:, vs])).astype(BF16)

    @pl.when(j == pl.num_programs(1) - 1)
    def _():
        for bi in range(bb):
            st = st_scr[bi]
            for h in range(B_HEADS):
                st_ref[bi, h] = st[h * B_DV:(h + 1) * B_DV, h * B_DK:(h + 1) * B_DK]


def _gla(qg, kg, vg, gg, rg, state_t, g_out, *, cs, nc, bb):
    b, t, _ = qg.shape
    assert b % bb == 0
    tt = cs * nc
    kern = functools.partial(_gla_kernel, cs=cs, nc=nc, bb=bb)

    def tok(w):
        return pl.BlockSpec((bb, tt, w), lambda i, j: (i, j, 0))

    st_spec = pl.BlockSpec((bb, B_HEADS, B_DV, B_DK), lambda i, j: (i, 0, 0, 0))
    return pl.pallas_call(
        kern,
        grid=(b // bb, t // tt),
        in_specs=[tok(256), tok(256), tok(512), tok(256), tok(512), st_spec,
                  pl.BlockSpec((1, B_DV), lambda i, j: (0, 0))],
        out_specs=[tok(512), st_spec],
        out_shape=[jax.ShapeDtypeStruct((b, t, 512), BF16),
                   jax.ShapeDtypeStruct((b, B_HEADS, B_DV, B_DK), F32)],
        scratch_shapes=[pltpu.VMEM((bb, B_HEADS * B_DV, B_HEADS * B_DK), F32)],
        name="gla",
        compiler_params=_params("arbitrary", "arbitrary"),
    )(qg, kg, vg, gg, rg, state_t, g_out)


def _merge_kernel(oa_ref, ob_ref, x_ref, mod_ref, wo_ref, gf_ref, wr_ref, br_ref, cnt0_ref,
                  x1_ref, h2_ref, ri_ref, rgate_ref, cnt_ref, carry_scr):
    @pl.when((pl.program_id(0) == 0) & (pl.program_id(1) == 0))
    def _():
        carry_scr[...] = cnt0_ref[...]

    mod = mod_ref[0]
    oa = oa_ref[:, 0].reshape(-1, oa_ref.shape[-1])
    cat = jnp.concatenate([oa, ob_ref[0]], axis=1)
    x1 = x_ref[0] + mod[2:3] * jnp.dot(cat, wo_ref[...], preferred_element_type=F32)
    x1_ref[0] = x1
    h2 = _rms(x1) * gf_ref[...] * (1.0 + mod[4:5]) + mod[3:4]
    _store_row_tiles(h2_ref.at[0], h2)
    tm = x1.shape[0]
    lane = lax.broadcasted_iota(I32, (tm, LANES), 1).astype(F32)
    hi = h2.astype(BF16)
    lo = (h2 - hi.astype(F32)).astype(BF16)
    wr = wr_ref[...]
    a = jnp.dot(hi, wr, preferred_element_type=F32)
    left = (a[:, :LANES] + a[:, LANES:]) + jnp.dot(lo, wr[:, :LANES], preferred_element_type=F32) + br_ref[...]
    idx, val = [], []
    for _ in range(TOP_K):
        m = jnp.max(left, axis=1, keepdims=True)
        e = jnp.argmax(left, axis=1, keepdims=True).astype(F32)
        idx.append(e)
        val.append(m)
        left = jnp.where(lane == e, -jnp.inf, left)
    ex = [jnp.exp(vv - val[0]) for vv in val]
    den = ex[0] + ex[1] + ex[2] + ex[3]
    onehot = jnp.zeros((tm, LANES), F32)
    for e in idx:
        onehot = onehot + jnp.where(lane == e, 1.0, 0.0)
    r = lax.broadcasted_iota(I32, (tm, tm), 0)
    c = lax.broadcasted_iota(I32, (tm, tm), 1)
    earlier = jnp.where(r > c, 1.0, 0.0).astype(BF16)
    before = jnp.dot(earlier, onehot.astype(BF16), preferred_element_type=F32) + carry_scr[...]
    ri = jnp.zeros((tm, LANES), F32)
    rgate = jnp.zeros((tm, LANES), F32)
    for kq in range(TOP_K):
        rank = jnp.sum(jnp.where(lane == idx[kq], before, 0.0), axis=1, keepdims=True)
        ri = jnp.where(lane == kq, idx[kq], ri)
        ri = jnp.where(lane == TOP_K + kq, rank, ri)
        rgate = jnp.where(lane == kq, ex[kq] / den, rgate)
    ri_ref[0] = ri.astype(I32)
    rgate_ref[0] = rgate
    carry_scr[...] = carry_scr[...] + jnp.sum(onehot, axis=0, keepdims=True)
    cnt_ref[...] = carry_scr[...]


def _merge(oa, ob, x, mod, w_out, g_ffn, wr_pad, br_pad, cnt0, tm):
    b, t, d = x.shape

    def tok(w):
        return pl.BlockSpec((1, tm, w), lambda i, j: (i, j, 0))

    def const(s):
        return pl.BlockSpec(s, lambda i, j: (0, 0))

    return pl.pallas_call(
        _merge_kernel,
        grid=(b, t // tm),
        in_specs=[pl.BlockSpec((tm // oa.shape[2], 1, oa.shape[2], 512), lambda i, j: (j, i, 0, 0)),
                  tok(512), tok(d), pl.BlockSpec((1, 6, d), lambda i, j: (i, 0, 0)),
                  const((d, d)), const((1, d)), const((d, 2 * LANES)), const((1, LANES)), const((1, LANES))],
        out_specs=[tok(d), pl.BlockSpec((1, tm * SUBLANES, LANES), lambda i, j: (i, j, 0)),
                   tok(LANES), tok(LANES), const((1, LANES))],
        out_shape=[jax.ShapeDtypeStruct((b, t, d), F32), jax.ShapeDtypeStruct((b, t * SUBLANES, LANES), F32),
                   jax.ShapeDtypeStruct((b, t, LANES), I32), jax.ShapeDtypeStruct((b, t, LANES), F32),
                   jax.ShapeDtypeStruct((1, LANES), F32)],
        scratch_shapes=[pltpu.VMEM((1, LANES), F32)],
        name="merge",
        compiler_params=_params("arbitrary", "arbitrary"),
    )(oa, ob, x, mod, w_out, g_ffn, wr_pad, br_pad, cnt0)


def _dispatch_kernel(pe_ref, hp_ref, dest_ref, h_hbm, *rest, tm, zero_init):
    if zero_init:
        xs_ref, ring, zbuf, in_sem, sem, zsem = rest

        @pl.when(pl.program_id(0) == 0)
        def _():
            zbuf[...] = jnp.zeros_like(zbuf)

            def zero_copy(e):
                start = pl.multiple_of(pe_ref[e] - SLOT_BLOCK, SLOT_BLOCK)
                return pltpu.make_async_copy(zbuf, xs_ref.at[pl.ds(start, SLOT_BLOCK)], zsem)

            for e in range(N_EXPERTS):
                @pl.when(hp_ref[e] > 0)
                def _():
                    zero_copy(e).start()
            for e in range(N_EXPERTS):
                @pl.when(hp_ref[e] > 0)
                def _():
                    zero_copy(e).wait()
    else:
        _, xs_ref, ring, in_sem, sem = rest

    step = pl.program_id(0)
    n_steps = pl.num_programs(0)
    slot_in = lax.rem(step, 3)
    cur = lax.rem(step, 2)

    def load(tile_idx, ring_slot):
        rows = pl.ds(pl.multiple_of(tile_idx * tm, tm), tm)
        return pltpu.make_async_copy(h_hbm.at[rows], ring.at[ring_slot], in_sem.at[ring_slot])

    @pl.when(step == 0)
    def _():
        load(0, 0).start()

    @pl.when(step + 1 < n_steps)
    def _():
        load(step + 1, lax.rem(step + 1, 3)).start()

    load(step, slot_in).wait()

    def row_copy(r, slot, which):
        return pltpu.make_async_copy(ring.at[slot_in, r], xs_ref.at[slot], sem.at[which])

    def issue(i, carry):
        for j in range(ROW_UNROLL):
            for kq in range(TOP_K):
                r = i * ROW_UNROLL + j
                row_copy(r, dest_ref[r * TOP_K + kq], cur).start(priority=kq % 2)
        return carry

    lax.fori_loop(0, tm // ROW_UNROLL, issue, 0)

    def drain(which):
        def body(i, carry):
            for _ in range(ROW_UNROLL * TOP_K):
                row_copy(0, 0, which).wait()
            return carry

        lax.fori_loop(0, tm // ROW_UNROLL, body, 0)

    @pl.when(step > 0)
    def _():
        drain(1 - cur)

    @pl.when(step == n_steps - 1)
    def _():
        drain(cur)


def _dispatch(pad_end, has_pad, dest_flat, h2_tiles, xs, n_slots, tm):
    n = h2_tiles.shape[0]
    tile = h2_tiles.shape[1:]
    zero_init = xs is None
    kern = functools.partial(_dispatch_kernel, tm=tm, zero_init=zero_init)
    in_specs = [pl.BlockSpec((tm * TOP_K,), lambda i, pe, hp: (i,), memory_space=pltpu.SMEM),
                pl.BlockSpec(memory_space=pl.ANY)]
    args = [pad_end, has_pad, dest_flat, h2_tiles]
    ring = pltpu.VMEM((3, tm) + tile, F32)
    scratch = [ring, pltpu.SemaphoreType.DMA((3,)), pltpu.SemaphoreType.DMA((2,))]
    aliases = {}
    if zero_init:
        scratch = [ring, pltpu.VMEM((SLOT_BLOCK,) + tile, F32), pltpu.SemaphoreType.DMA((3,)),
                   pltpu.SemaphoreType.DMA((2,)), pltpu.SemaphoreType.DMA(())]
    else:
        in_specs.append(pl.BlockSpec(memory_space=pl.ANY))
        args.append(xs)
        aliases = {4: 0}
    grid_spec = pltpu.PrefetchScalarGridSpec(
        num_scalar_prefetch=2,
        grid=(n // tm,),
        in_specs=in_specs,
        out_specs=pl.BlockSpec(memory_space=pl.ANY),
        scratch_shapes=scratch,
    )
    return pl.pallas_call(
        kern,
        grid_spec=grid_spec,
        out_shape=jax.ShapeDtypeStruct((n_slots,) + tile, F32),
        input_output_aliases=aliases,
        name="dispatch",
        compiler_params=_params("arbitrary"),
    )(*args)


def _expert_kernel(be_ref, nx_ref, nu_ref, xs_ref, wgu_hbm, bgu_ref, wdn_hbm, bdn_ref, ys_ref,
                   gu_f32, dn_f32, wgu_b, wdn_b, sem):
    j = pl.program_id(0)

    def fetch(e):
        return (pltpu.make_async_copy(wgu_hbm.at[e], gu_f32, sem.at[0]),
                pltpu.make_async_copy(wdn_hbm.at[e], dn_f32, sem.at[1]))

    @pl.when(j < nu_ref[0])
    def _():
        @pl.when(j == 0)
        def _():
            for cp in fetch(be_ref[0]):
                cp.start()

        @pl.when((j == 0) | (be_ref[j] != be_ref[jnp.maximum(j - 1, 0)]))
        def _():
            for cp in fetch(be_ref[j]):
                cp.wait()
            wgu_b[...] = gu_f32[...].astype(BF16)
            wdn_b[...] = dn_f32[...].astype(BF16)

            @pl.when(nx_ref[j] >= 0)
            def _():
                for cp in fetch(nx_ref[j]):
                    cp.start()

        f = wdn_b.shape[0]
        x = _load_row_tiles(xs_ref, SLOT_BLOCK).astype(BF16)
        gu = jnp.dot(x, wgu_b[...], preferred_element_type=F32) + bgu_ref[0]
        gate = jnp.minimum(gu[:, :f], SWIGLU_LIMIT)
        up = jnp.clip(gu[:, f:], -SWIGLU_LIMIT, SWIGLU_LIMIT)
        glu = gate / (1.0 + jnp.exp(-SWIGLU_ALPHA * gate))
        act = ((up + 1.0) * glu).astype(BF16)
        _store_row_tiles(ys_ref, jnp.dot(act, wdn_b[...], preferred_element_type=F32) + bdn_ref[0])


def _experts(block_expert, next_expert, n_used, xs, wgu, bgu, wdn, bdn):
    n_slots = xs.shape[0] // SUBLANES
    nb = n_slots // SLOT_BLOCK
    f, d = wdn.shape[1:]
    rows = SLOT_BLOCK * SUBLANES

    def blk(i, be, nx, nu):
        return (jnp.minimum(i, nu[0] - 1), 0)

    def bsel(i, be, nx, nu):
        return (be[i], 0, 0)

    grid_spec = pltpu.PrefetchScalarGridSpec(
        num_scalar_prefetch=3,
        grid=(nb,),
        in_specs=[pl.BlockSpec((rows, LANES), blk),
                  pl.BlockSpec(memory_space=pl.ANY),
                  pl.BlockSpec((1, 1, 2 * f), bsel),
                  pl.BlockSpec(memory_space=pl.ANY),
                  pl.BlockSpec((1, 1, d), bsel)],
        out_specs=pl.BlockSpec((rows, LANES), blk),
        scratch_shapes=[pltpu.VMEM((d, 2 * f), F32), pltpu.VMEM((f, d), F32),
                        pltpu.VMEM((d, 2 * f), BF16), pltpu.VMEM((f, d), BF16), pltpu.SemaphoreType.DMA((2,))],
    )
    return pl.pallas_call(
        _expert_kernel,
        grid_spec=grid_spec,
        out_shape=jax.ShapeDtypeStruct(xs.shape, F32),
        name="experts",
        compiler_params=_params("arbitrary"),
    )(block_expert, next_expert, n_used, xs, wgu, bgu, wdn, bdn)


def _combine_kernel(dest_ref, dnext_ref, x1_ref, gate_ref, mod_ref, modf_ref, gfin_ref, ys_ref, y_ref,
                    buf, sem, *, tm):
    nt = pl.num_programs(1)
    step = pl.program_id(0) * nt + pl.program_id(1)
    n_steps = pl.num_programs(0) * nt
    half = TOP_K * tm * SUBLANES
    cur = lax.rem(step, 2)

    def row_copy(which, kq, r, slot):
        off = pl.multiple_of(which * half + (kq * tm + r) * SUBLANES, SUBLANES)
        return pltpu.make_async_copy(ys_ref.at[slot], buf.at[pl.ds(off, SUBLANES), :], sem.at[which])

    def gather(slots_ref, which):
        def issue(i, carry):
            for j in range(ROW_UNROLL):
                for kq in range(TOP_K):
                    r = i * ROW_UNROLL + j
                    row_copy(which, kq, r, slots_ref[r * TOP_K + kq]).start(priority=kq % 2)
            return carry

        lax.fori_loop(0, tm // ROW_UNROLL, issue, 0)

    @pl.when(step == 0)
    def _():
        gather(dest_ref, cur)

    @pl.when(step + 1 < n_steps)
    def _():
        gather(dnext_ref, 1 - cur)

    def drain(i, carry):
        for _ in range(ROW_UNROLL * TOP_K):
            row_copy(cur, 0, 0, 0).wait()
        return carry

    lax.fori_loop(0, tm // ROW_UNROLL, drain, 0)
    gates = gate_ref[0]
    moe = None
    for kq in range(TOP_K):
        start = pl.multiple_of(cur * half + kq * tm * SUBLANES, SUBLANES)
        rows = _load_row_tiles(buf.at[pl.ds(start, tm * SUBLANES), :], tm)
        term = gates[:, kq:kq + 1] * rows
        moe = term if moe is None else moe + term
    xo = x1_ref[0] + mod_ref[0][5:6] * moe
    modf = modf_ref[0]
    y_ref[0] = _rms(xo) * gfin_ref[...] * (1.0 + modf[1:2]) + modf[0:1]


def _combine(dest_flat, x1, gates, mod, modf, g_final, ys, tm):
    b, t, d = x1.shape
    nt = t // tm
    last = b * nt - 1
    kern = functools.partial(_combine_kernel, tm=tm)
    return pl.pallas_call(
        kern,
        grid=(b, nt),
        in_specs=[pl.BlockSpec((tm * TOP_K,), lambda i, j: (i * nt + j,), memory_space=pltpu.SMEM),
                  pl.BlockSpec((tm * TOP_K,), lambda i, j: (jnp.minimum(i * nt + j + 1, last),),
                               memory_space=pltpu.SMEM),
                  pl.BlockSpec((1, tm, d), lambda i, j: (i, j, 0)),
                  pl.BlockSpec((1, tm, LANES), lambda i, j: (i, j, 0)),
                  pl.BlockSpec((1, 6, d), lambda i, j: (i, 0, 0)),
                  pl.BlockSpec((1, 2, d), lambda i, j: (i, 0, 0)),
                  pl.BlockSpec((1, d), lambda i, j: (0, 0)),
                  pl.BlockSpec(memory_space=pl.ANY)],
        out_specs=pl.BlockSpec((1, tm, d), lambda i, j: (i, j, 0)),
        out_shape=jax.ShapeDtypeStruct((b, t, d), F32),
        scratch_shapes=[pltpu.VMEM((2 * TOP_K * tm * SUBLANES, LANES), F32), pltpu.SemaphoreType.DMA((2,))],
        name="combine",
        compiler_params=_params("arbitrary", "arbitrary"),
    )(dest_flat, dest_flat, x1, gates, mod, modf, g_final, ys)


def _rope_tables(pos):
    half = A_HD // 2
    inv_freq = ROPE_THETA ** (-jnp.arange(half, dtype=F32) / half)
    ang = pos.astype(F32)[:, None] * inv_freq[None, :]
    cos = jnp.cos(ang)
    sin = jnp.sin(ang)
    reps = LANES // A_HD
    return jnp.tile(jnp.concatenate([cos, cos], axis=1), (1, reps)), jnp.tile(jnp.concatenate([-sin, sin], axis=1), (1, reps))


def _reorder_w_in(w_in):
    offs = [0]
    for w in (512, 128, 128, 256, 64, 4, 256, 256, 512, 16, 512):
        offs.append(offs[-1] + w)
    seg = [w_in[:, offs[i]:offs[i + 1]] for i in range(11)]
    qa, ka, va, qi, ki, wi, qg, kg, vg, lr, rg = seg
    pad = jnp.zeros((w_in.shape[0], LANES - IDX_DIM - IDX_HEADS - GATE_RANK), w_in.dtype)
    return jnp.concatenate([qa, ka, va, qi, ki, wi, lr, pad, qg, kg, vg, rg], axis=1).astype(BF16)


def kernel(x_prompt, x_sample, cache_k, cache_v, cache_kidx, state_gla, c_prompt, c_sample,
           w_mod, b_mod, g_mix, g_ffn, w_in, gla_w_gate, gla_b_gate, gla_g_out, w_out,
           w_router, b_router, w_gate_up, b_gate_up, w_down, b_down,
           w_mod_final, b_mod_final, g_final):
    depth = w_in.shape[0]
    assert depth == 1
    bp, sp, d = x_prompt.shape
    bs, ts, _ = x_sample.shape
    past = cache_k.shape[2]
    n_exp = w_router.shape[2]
    f = w_down.shape[2]

    c_all = jnp.concatenate([c_prompt, c_sample], axis=0)
    w_r = _reorder_w_in(w_in[0])
    wg_pad = jnp.zeros((LANES, B_HEADS * B_DK), F32).at[MISC_LR:MISC_LR + GATE_RANK].set(gla_w_gate[0])
    bg = gla_b_gate[0].reshape(1, -1)
    wr_pad = jnp.zeros((d, LANES), F32).at[:, :n_exp].set(w_router[0])
    wr_hi = wr_pad.astype(BF16)
    wr_cat = jnp.concatenate([wr_hi, (wr_pad - wr_hi.astype(F32)).astype(BF16)], axis=1)
    br_pad = jnp.full((1, LANES), NEG_BIG, F32).at[0, :n_exp].set(b_router[0])
    w_out_b = w_out[0].astype(BF16)
    bgu = b_gate_up[0].reshape(n_exp, 1, 2 * f)
    bdn = b_down[0].reshape(n_exp, 1, d)
    g_out = gla_g_out[0].reshape(1, B_DV)

    mod_all = _adaln(c_all, w_mod[0], b_mod[0]).reshape(bp + bs, 6, d)
    modf_all = _adaln(c_all, w_mod_final, b_mod_final).reshape(bp + bs, 2, d)
    mod_p, mod_s = mod_all[:bp], mod_all[bp:]
    modf_p, modf_s = modf_all[:bp], modf_all[bp:]

    def mixer(x, mod, pos, tm, past_kv, state_t, cs, nc, tq):
        cos_t, sin_t = _rope_tables(pos)
        qa, ka, va, qi, ki, misc, qg, kg, vg, gg, rg, kb, vx, kib = _premix(
            x, mod, g_mix[0].reshape(1, d), w_r, wg_pad, bg, cos_t, sin_t, tm)
        if past_kv is None:
            oa = _attn_causal(qa, qi, misc, kb, vx, kib, tq)
        else:
            n_past = past_kv[0].shape[2]
            assert tq == x.shape[1] and n_past % LANES == 0
            oa = _attn_call(qa, qi, misc, ka, va, ki, past_kv, tq=tq, q_off=0, n_tiles=1,
                            n_ctx=n_past + LANES, causal=False, n_keys=n_past + ka.shape[1], n_groups=1)[None]
        ob, st = _gla(qg, kg, vg, gg, rg, state_t, g_out, cs=cs, nc=nc, bb=GLA_BATCH)
        return oa, ob, (ka, va, ki, st)

    state0_p = jnp.zeros((bp, B_HEADS, B_DV, B_DK), F32)
    oa_p, ob_p, (ka_p, va_p, ki_p, st_p) = mixer(
        x_prompt, mod_p, jnp.arange(sp), min(TOKEN_TILE, sp), None, state0_p, CHUNK,
        min(GLA_CHUNKS_PER_STEP, sp // CHUNK), min(TOKEN_TILE, sp))
    state0_s = jnp.swapaxes(state_gla[0], -1, -2)
    past_kv = (jnp.transpose(cache_k[0], (0, 2, 3, 1)).reshape(bs, A_KV * A_HD, past),
               jnp.transpose(cache_v[0], (0, 2, 3, 1)).reshape(bs, A_KV * A_HD, past),
               jnp.swapaxes(cache_kidx[0], 1, 2))
    oa_s, ob_s, (ka_s, va_s, ki_s, st_s) = mixer(
        x_sample, mod_s, past + jnp.arange(ts), ts, past_kv, state0_s, ts, 1, ts)

    cnt0 = jnp.zeros((1, LANES), F32)
    x1_p, h2_p, ri_p, gate_p, cnt1 = _merge(oa_p, ob_p, x_prompt, mod_p, w_out_b, g_ffn[0].reshape(1, d),
                                            wr_cat, br_pad, cnt0, min(MERGE_TILE, sp))
    x1_s, h2_s, ri_s, gate_s, cnt2 = _merge(oa_s, ob_s, x_sample, mod_s, w_out_b, g_ffn[0].reshape(1, d),
                                            wr_cat, br_pad, cnt1, ts)
    counts = cnt2[0, :n_exp].astype(I32)
    padded = (counts + SLOT_BLOCK - 1) // SLOT_BLOCK * SLOT_BLOCK
    pad_end = jnp.cumsum(padded)
    pad_start = pad_end - padded
    n_asg = (bp * sp + bs * ts) * TOP_K
    nb = -(-n_asg // SLOT_BLOCK) + n_exp
    n_slots = nb * SLOT_BLOCK
    block_start = jnp.arange(nb, dtype=I32) * SLOT_BLOCK
    block_expert = jnp.minimum(jnp.sum((pad_end[None, :] <= block_start[:, None]).astype(I32), axis=1), n_exp - 1)
    n_used = (pad_end[-1:] // SLOT_BLOCK).astype(I32)
    following = pad_end.astype(I32)[block_expert] // SLOT_BLOCK
    next_expert = jnp.where(following < n_used[0], block_expert[jnp.minimum(following, nb - 1)], -1).astype(I32)
    has_pad = (padded > counts).astype(I32)
    expert_ids = jnp.arange(n_exp, dtype=I32)

    def dests(ri):
        e, rank = ri[..., :TOP_K], ri[..., TOP_K:2 * TOP_K]
        start = jnp.sum(jnp.where(e[..., None] == expert_ids, pad_start.astype(I32), 0), axis=-1)
        return (start + rank).reshape(-1)

    dest_p, dest_s = dests(ri_p), dests(ri_s)
    pad_end_i = pad_end.astype(I32)
    tile = (SUBLANES, LANES)
    xs = _dispatch(pad_end_i, has_pad, dest_p, h2_p.reshape((bp * sp,) + tile), None, n_slots, min(ROW_DMA_TILE, sp))
    xs = _dispatch(pad_end_i, has_pad, dest_s, h2_s.reshape((bs * ts,) + tile), xs, n_slots, min(TOKEN_TILE, bs * ts))
    ys = _experts(block_expert, next_expert, n_used, xs.reshape(n_slots * SUBLANES, LANES), w_gate_up[0], bgu,
                  w_down[0], bdn)
    ys = ys.reshape((n_slots,) + tile)
    y_p = _combine(dest_p, x1_p, gate_p, mod_p, modf_p, g_final.reshape(1, d), ys, min(TOKEN_TILE, sp))
    y_s = _combine(dest_s, x1_s, gate_s, mod_s, modf_s, g_final.reshape(1, d), ys, ts)

    def kv(a, b, t):
        return a.reshape(1, b, t, A_KV, A_HD)

    return (y_p, y_s,
            kv(ka_p, bp, sp), kv(va_p, bp, sp), ki_p[None], jnp.swapaxes(st_p, -1, -2)[None],
            kv(ka_s, bs, ts), kv(va_s, bs, ts), ki_s[None], jnp.swapaxes(st_s, -1, -2)[None])
```

```python
import functools

import jax
import jax.numpy as jnp
from jax import lax
from jax.experimental import pallas as pl
from jax.experimental.pallas import tpu as pltpu

F32 = jnp.float32
BF16 = jnp.bfloat16
I32 = jnp.int32
HI = lax.Precision.HIGHEST

CHUNK = 64
CHUNK_SHIFT = 6
EPS = 1e-6
ROPE_THETA = 10000.0
A_HD = 64
A_HEADS = 8
A_KV = 2
IDX_HEADS = 4
IDX_DIM = 64
INDEX_TOPK = 256
B_HEADS = 4
B_DK = 64
B_DV = 128
GATE_RANK = 16
GATE_TAU = 16.0
N_EXPERTS = 32
TOP_K = 4
SWIGLU_LIMIT = 7.0
SWIGLU_ALPHA = 1.702

LANES = 128
SUBLANES = 8
TOKEN_TILE = 256
MERGE_TILE = 512
ROW_UNROLL = 8
ROW_DMA_TILE = 512
GLA_SUB = 16
GLA_BATCH = 2
GLA_CHUNKS_PER_STEP = 8
GLA_EXP_CLAMP = 80.0
SLOT_BLOCK = 512
VMEM_LIMIT = 56 * 1024 * 1024
INT_MIN = -2147483648
NEG_BIG = -1e30
SEARCH_TWO_BIT_MAX_KEYS = 768

C_QA, C_KA, C_VA, C_QI, C_MISC, C_QG, C_KG, C_VG, C_RG, C_END = 0, 512, 640, 768, 1024, 1152, 1408, 1664, 2176, 2688
MISC_WI = 64
MISC_LR = 68


def _params(*sem):
    return pltpu.CompilerParams(dimension_semantics=sem, vmem_limit_bytes=VMEM_LIMIT)


def _nt(a, b):
    return lax.dot_general(a, b, (((1,), (1,)), ((), ())), preferred_element_type=F32)


def _tn(a, b):
    return lax.dot_general(a, b, (((0,), (0,)), ((), ())), preferred_element_type=F32)


def _rms(x):
    return x * lax.rsqrt(jnp.mean(x * x, axis=-1, keepdims=True) + EPS)


def _silu(x):
    return x / (1.0 + jnp.exp(-x))


def _store_row_tiles(ref, val):
    rows, width = val.shape
    assert width == SUBLANES * LANES
    for s in range(SUBLANES):
        ref[pl.ds(s, rows, stride=SUBLANES), :] = val[:, s * LANES:(s + 1) * LANES]


def _load_row_tiles(ref, rows):
    return jnp.concatenate([ref[pl.ds(s, rows, stride=SUBLANES), :] for s in range(SUBLANES)], axis=1)


def _value_with_ones(vb):
    return jnp.concatenate([vb, jnp.ones_like(vb)], axis=-1)


def _adaln_kernel(c_ref, w_ref, b_ref, o_ref):
    a = _silu(c_ref[...])
    o_ref[...] = jnp.dot(a, w_ref[...], preferred_element_type=F32, precision=HI) + b_ref[...]


def _adaln(c, w, b):
    r, d = c.shape
    n = w.shape[1]
    tn = 512
    return pl.pallas_call(
        _adaln_kernel,
        grid=(n // tn,),
        in_specs=[pl.BlockSpec((r, d), lambda j: (0, 0)),
                  pl.BlockSpec((d, tn), lambda j: (0, j)),
                  pl.BlockSpec((1, tn), lambda j: (0, j))],
        out_specs=pl.BlockSpec((r, tn), lambda j: (0, j)),
        out_shape=jax.ShapeDtypeStruct((r, n), F32),
        name="adaln",
        compiler_params=_params("arbitrary"),
    )(c, w, b.reshape(1, n))


def _premix_kernel(x_ref, mod_ref, g_ref, w_ref, wg_ref, bg_ref, cos_ref, sin_ref,
                   qa_ref, ka_ref, va_ref, qi_ref, ki_ref, misc_ref, qg_ref, kg_ref, vg_ref, gg_ref, rg_ref,
                   kb_ref, vx_ref, kib_ref):
    x = x_ref[0]
    mod = mod_ref[0]
    hb = (_rms(x) * g_ref[...] * (1.0 + mod[1:2]) + mod[0:1]).astype(BF16)
    tm = x.shape[0]

    def project(c0, c1):
        return jnp.dot(hb, w_ref[:, c0:c1], preferred_element_type=F32)

    cos = cos_ref[...]
    sin = sin_ref[...]
    lane = lax.broadcasted_iota(I32, (tm, LANES), 1)
    lower_half = (lane & (A_HD - 1)) < (A_HD // 2)

    def rope(xc):
        rot = jnp.where(lower_half, pltpu.roll(xc, LANES - A_HD // 2, 1), pltpu.roll(xc, A_HD // 2, 1))
        return xc * cos + rot * sin

    seg = project(C_QA, C_KA)
    for j in range((C_KA - C_QA) // LANES):
        qa_ref[0, :, j * LANES:(j + 1) * LANES] = (rope(seg[:, j * LANES:(j + 1) * LANES]) * (A_HD ** -0.5)).astype(BF16)
    seg = project(C_KA, C_QI)
    ka = rope(seg[:, :C_VA - C_KA])
    va = seg[:, C_VA - C_KA:]
    ka_ref[0] = ka
    va_ref[0] = va
    kb_ref[0] = ka.T.astype(BF16)
    vx_ref[0] = _value_with_ones(va.astype(BF16))
    seg = project(C_QI, C_QG)
    for j in range((C_MISC - C_QI) // LANES):
        qi_ref[0, :, j * LANES:(j + 1) * LANES] = rope(seg[:, j * LANES:(j + 1) * LANES]).astype(BF16)
    m = seg[:, C_MISC - C_QI:]
    mr = rope(m)
    ki_ref[0] = mr[:, :IDX_DIM]
    kib_ref[0] = mr.T[:IDX_DIM].astype(BF16)
    misc_ref[0] = jnp.where(lane < IDX_DIM, mr, m * (IDX_HEADS ** -0.5))
    xg = jnp.dot(m, wg_ref[...], preferred_element_type=F32, precision=HI) + bg_ref[...]
    gg_ref[0] = (jnp.minimum(xg, 0.0) - jnp.log(1.0 + jnp.exp(-jnp.abs(xg)))) * (1.0 / GATE_TAU)
    qg_ref[0] = project(C_QG, C_KG) * (B_DK ** -0.5)
    kg_ref[0] = project(C_KG, C_VG)
    vg_ref[0] = project(C_VG, C_RG)
    rg_ref[0] = project(C_RG, C_END)


def _premix(x, mod, g_mix, w_r, wg_pad, bg, cos_t, sin_t, tm):
    b, t, d = x.shape
    rm, fm = False, True
    widths = [(512, BF16, rm), (128, F32, rm), (128, F32, rm), (256, BF16, rm), (64, F32, rm), (128, F32, rm),
              (256, F32, rm), (256, F32, rm), (512, F32, rm), (256, F32, rm), (512, F32, rm),
              (128, BF16, fm), (2 * LANES, BF16, rm), (IDX_DIM, BF16, fm)]
    return pl.pallas_call(
        _premix_kernel,
        grid=(b, t // tm),
        in_specs=[pl.BlockSpec((1, tm, d), lambda i, j: (i, j, 0)),
                  pl.BlockSpec((1, 6, d), lambda i, j: (i, 0, 0)),
                  pl.BlockSpec((1, d), lambda i, j: (0, 0)),
                  pl.BlockSpec((d, C_END), lambda i, j: (0, 0)),
                  pl.BlockSpec((LANES, 256), lambda i, j: (0, 0)),
                  pl.BlockSpec((1, 256), lambda i, j: (0, 0)),
                  pl.BlockSpec((tm, LANES), lambda i, j: (j, 0)),
                  pl.BlockSpec((tm, LANES), lambda i, j: (j, 0))],
        out_specs=[pl.BlockSpec((1, tm, w), lambda i, j: (i, j, 0)) if not fm else
                   pl.BlockSpec((1, w, tm), lambda i, j: (i, 0, j)) for w, _, fm in widths],
        out_shape=[jax.ShapeDtypeStruct((b, w, t) if fm else (b, t, w), dt) for w, dt, fm in widths],
        name="premix",
        compiler_params=_params("arbitrary", "arbitrary"),
    )(x, mod, g_mix, w_r, wg_pad, bg, cos_t, sin_t)


def _attn_kernel(*refs, tq, n_ctx, top_k, causal, n_keys, q_off, n_groups, n_past, keep_all):
    if n_past:
        q_ref, qi_ref, misc_ref, k_ref, v_ref, ki_ref, pk_ref, pv_ref, pki_ref, o_ref = refs
        assert n_ctx - n_past == LANES and k_ref.shape[1] <= LANES

        def new_columns(rows_ref):
            new = rows_ref[0]
            n_new, width = new.shape
            if width < LANES:
                new = jnp.concatenate([new, jnp.zeros((n_new, LANES - width), F32)], axis=1)
            square = jnp.concatenate([new, jnp.zeros((LANES - n_new, LANES), F32)], axis=0)
            return square.T[:width].astype(BF16)

        k = jnp.concatenate([pk_ref[0].astype(BF16), new_columns(k_ref)], axis=1)
        v = jnp.concatenate([pv_ref[0].astype(BF16), new_columns(v_ref)], axis=1)
        vx = jnp.concatenate([v, jnp.ones_like(v)], axis=0)
        kib = jnp.concatenate([pki_ref[0].astype(BF16), new_columns(ki_ref)], axis=1)
    else:
        q_ref, qi_ref, misc_ref, k_ref, vx_ref, ki_ref, o_ref = refs
        k = k_ref[0]
        vx = vx_ref[0]
        kib = ki_ref[0]
    values_feature_major = bool(n_past)
    rows = tq // n_groups
    row0 = q_off + pl.program_id(1) * tq
    keys, kks = [], []
    idx_dots = {}
    for g in range(n_groups):
        qi = qi_ref[0, g * rows:(g + 1) * rows, :]
        for h in range(IDX_HEADS):
            qh = qi[:, h * IDX_DIM:(h + 1) * IDX_DIM]
            idx_dots[g, h] = jnp.dot(qh, kib, preferred_element_type=F32)
    for g in range(n_groups):
        misc = misc_ref[0, g * rows:(g + 1) * rows, :]
        isc = jnp.zeros((rows, n_ctx), F32)
        for h in range(IDX_HEADS):
            isc = isc + misc[:, MISC_WI + h:MISC_WI + h + 1] * jnp.maximum(idx_dots[g, h], 0.0)
        kpos = lax.broadcasted_iota(I32, (rows, n_ctx), 1)
        if causal:
            row = lax.broadcasted_iota(I32, (rows, 1), 0) + (row0 + g * rows)
            key_lim = (lax.shift_right_logical(row, CHUNK_SHIFT) + 1) * CHUNK
        else:
            key_lim = jnp.full((rows, 1), n_keys, I32)
        bits = pltpu.bitcast(isc, I32)
        key = jnp.where(bits < 0, INT_MIN - bits, bits)
        keys.append(jnp.where(kpos < key_lim, key, INT_MIN))
        kks.append(jnp.minimum(key_lim, top_k).astype(F32))

    def count_at_least(g, cand):
        return jnp.sum(jnp.where(keys[g] >= (cand ^ INT_MIN), 1.0, 0.0), axis=1, keepdims=True)

    bits_per_pass = 2 if n_ctx <= SEARCH_TWO_BIT_MAX_KEYS else 1

    def search(i, ans):
        out = []
        for g in range(n_groups):
            if bits_per_pass == 1:
                cand = ans[g] | lax.shift_left(jnp.int32(1), 31 - i)
                out.append(jnp.where(count_at_least(g, cand) >= kks[g], cand, ans[g]))
            else:
                a1 = ans[g] | lax.shift_left(jnp.int32(1), 31 - 2 * i)
                a2 = ans[g] | lax.shift_left(jnp.int32(1), 30 - 2 * i)
                a3 = a1 | a2
                c1, c2, c3 = count_at_least(g, a1), count_at_least(g, a2), count_at_least(g, a3)
                kk = kks[g]
                out.append(jnp.where(c3 >= kk, a3, jnp.where(c1 >= kk, a1, jnp.where(c2 >= kk, a2, ans[g]))))
        return tuple(out)

    ans = tuple(jnp.zeros((rows, 1), I32) for _ in range(n_groups))
    if not keep_all:
        ans = lax.fori_loop(0, 32 // bits_per_pass, search, ans)
    r = lax.broadcasted_iota(I32, (LANES, LANES), 0)
    c = lax.broadcasted_iota(I32, (LANES, LANES), 1)
    upper = jnp.where(r < c, 1.0, 0.0).astype(BF16)
    n_blk = n_ctx // LANES
    thrs = [ans[g] ^ INT_MIN for g in range(n_groups)]
    ties = {(g, j): jnp.where(keys[g][:, j * LANES:(j + 1) * LANES] == thrs[g], 1.0, 0.0)
            for g in range(n_groups) for j in range(n_blk)}
    within = {gj: jnp.dot(t.astype(BF16), upper, preferred_element_type=F32) for gj, t in ties.items()}
    bias_groups = []
    for g in range(n_groups):
        gt = keys[g] > thrs[g]
        need = kks[g] - jnp.sum(jnp.where(gt, 1.0, 0.0), axis=1, keepdims=True)
        carry = jnp.zeros((rows, 1), F32)
        blocks = []
        for j in range(n_blk):
            sl = slice(j * LANES, (j + 1) * LANES)
            take_tie = jnp.where(within[g, j] + carry < need, ties[g, j], 0.0)
            blocks.append(jnp.where(gt[:, sl], 0.0, jnp.where(take_tie > 0.0, 0.0, NEG_BIG)))
            carry = carry + jnp.sum(ties[g, j], axis=1, keepdims=True)
        bias_groups.append(jnp.concatenate(blocks, axis=1))
    bias = jnp.concatenate(bias_groups, axis=0)

    q = q_ref[0]
    rep = A_HEADS // A_KV
    if values_feature_major and tq * A_HEADS <= 2 * LANES:
        q_groups = [jnp.concatenate([q[:, hh * A_HD:(hh + 1) * A_HD] for hh in range(g * rep, (g + 1) * rep)], axis=0)
                    for g in range(A_KV)]
        s_all = jnp.concatenate([jnp.dot(q_groups[g], k[g * A_HD:(g + 1) * A_HD, :], preferred_element_type=F32)
                                 for g in range(A_KV)], axis=0) + jnp.concatenate([bias] * A_HEADS, axis=0)
        p_all = jnp.exp((s_all - jnp.max(s_all, axis=1, keepdims=True)).astype(BF16))
        ox = _nt(p_all, vx)
        outs = [ox[hh * tq:(hh + 1) * tq, (hh // rep) * A_HD:(hh // rep + 1) * A_HD]
                / ox[hh * tq:(hh + 1) * tq, LANES:LANES + 1] for hh in range(A_HEADS)]
        o_ref[0] = jnp.concatenate(outs, axis=1).astype(BF16)
        return
    kgs = [k[g * A_HD:(g + 1) * A_HD, :] for g in range(A_KV)]
    qs = [q[:, hh * A_HD:(hh + 1) * A_HD] for hh in range(A_HEADS)]
    ss = [jnp.dot(qh, kgs[hh // rep], preferred_element_type=F32) + bias for hh, qh in enumerate(qs)]
    ps = [jnp.exp((s - jnp.max(s, axis=1, keepdims=True)).astype(BF16)) for s in ss]
    oxs = [_nt(p, vx) if values_feature_major else jnp.dot(p, vx, preferred_element_type=F32) for p in ps]
    outs = [ox[:, (hh // rep) * A_HD:(hh // rep + 1) * A_HD] / ox[:, LANES:LANES + 1] for hh, ox in enumerate(oxs)]
    o_ref[0] = jnp.concatenate(outs, axis=1).astype(BF16)


def _attn_call(q, qi, misc, kb, vx, kib, past, *, tq, q_off, n_tiles, n_ctx, causal, n_keys, n_groups):
    b = q.shape[0]
    t0 = q_off // tq
    top_k = min(INDEX_TOPK, n_keys // 4)
    n_past = 0 if past is None else past[0].shape[2]
    n_own = n_ctx if past is None else kb.shape[1]
    kern = functools.partial(_attn_kernel, tq=tq, n_ctx=n_ctx, top_k=top_k, causal=causal, n_keys=n_keys,
                             q_off=q_off, n_groups=n_groups, n_past=n_past,
                             keep_all=causal and q_off + n_tiles * tq <= top_k)

    def tok(w):
        return pl.BlockSpec((1, tq, w), lambda i, j: (i, t0 + j, 0))

    def ctx(rows, w):
        return pl.BlockSpec((1, rows, w), lambda i, j: (i, 0, 0))

    if past is None:
        own_specs = [ctx(kb.shape[1], n_ctx), ctx(n_ctx, vx.shape[2]), ctx(kib.shape[1], n_ctx)]
    else:
        own_specs = [ctx(n_own, a.shape[2]) for a in (kb, vx, kib)]
    in_specs = [tok(512), tok(256), tok(LANES)] + own_specs
    args = [q, qi, misc, kb, vx, kib]
    if past is not None:
        in_specs += [ctx(a.shape[1], n_past) for a in past]
        args += list(past)
    return pl.pallas_call(
        kern,
        grid=(b, n_tiles),
        in_specs=in_specs,
        out_specs=pl.BlockSpec((1, tq, 512), lambda i, j: (i, j, 0)),
        out_shape=jax.ShapeDtypeStruct((b, n_tiles * tq, 512), BF16),
        name="attn",
        compiler_params=_params("arbitrary", "arbitrary"),
    )(*args)


def _attn_causal(q, qi, misc, kb, vx, kib, tq):
    t = q.shape[1]
    return jnp.stack([
        _attn_call(q, qi, misc, kb, vx, kib, None, tq=tq, q_off=c * tq, n_tiles=1, n_ctx=(c + 1) * tq,
                   causal=True, n_keys=t, n_groups=2)
        for c in range(t // tq)])


def _gla_kernel(q_ref, k_ref, v_ref, g_ref, rg_ref, s0_ref, go_ref, ob_ref, st_ref, st_scr, *, cs, nc, bb):
    j = pl.program_id(1)
    hk, hv, hc = B_HEADS * B_DK, B_HEADS * B_DV, B_HEADS * cs

    def head_of(idx, width):
        return lax.shift_right_logical(idx, width.bit_length() - 1)

    def same_head(rows, row_w, cols, col_w):
        r = head_of(lax.broadcasted_iota(I32, (rows, cols), 0), row_w)
        c = head_of(lax.broadcasted_iota(I32, (rows, cols), 1), col_w)
        return r == c

    keep_kb = same_head(hc, cs, hk, B_DK)
    keep_vb = same_head(hc, cs, hv, B_DV)
    keep_st = same_head(hv, B_DV, hk, B_DK)
    t_idx = lax.broadcasted_iota(I32, (cs, hc), 0)
    s_idx = lax.broadcasted_iota(I32, (cs, hc), 1) & (cs - 1)
    keep_a = t_idx >= s_idx
    r = lax.broadcasted_iota(I32, (cs, cs), 0)
    c = lax.broadcasted_iota(I32, (cs, cs), 1)
    tri = jnp.where(r >= c, 1.0, 0.0)
    go = go_ref[...]

    @pl.when(j == 0)
    def _():
        for bi in range(bb):
            blocks = []
            for h in range(B_HEADS):
                parts = [jnp.zeros((B_DV, B_DK), F32)] * B_HEADS
                parts[h] = s0_ref[bi, h]
                blocks.append(jnp.concatenate(parts, axis=1))
            st_scr[bi] = jnp.concatenate(blocks, axis=0)

    inst = [(ci, bi) for ci in range(nc) for bi in range(bb)]
    sls = {ci: slice(ci * cs, (ci + 1) * cs) for ci in range(nc)}
    bcum = {t: jnp.dot(tri, g_ref[t[1], sls[t[0]], :], preferred_element_type=F32, precision=HI) for t in inst}
    n_sub = cs // GLA_SUB
    v_bf, qdec, kdec, qts, kts = {}, {}, {}, {}, {}
    for t in inst:
        ci, bi = t
        q = q_ref[bi, sls[ci], :]
        k = k_ref[bi, sls[ci], :]
        bc = bcum[t]
        blast = bc[cs - 1:cs, :]
        qdec[t] = (q * jnp.exp(bc)).astype(BF16)
        kdec[t] = (k * jnp.exp(blast - bc)).astype(BF16)
        v_bf[t] = v_ref[bi, sls[ci], :].astype(BF16)
        for i in range(n_sub):
            rs = slice(i * GLA_SUB, (i + 1) * GLA_SUB)
            ref = bc[i * GLA_SUB:i * GLA_SUB + 1, :]
            qts[t, i] = (q[rs, :] * jnp.exp(bc[rs, :] - ref)).astype(BF16)
            kt = (k * jnp.exp(jnp.minimum(ref - bc, GLA_EXP_CLAMP))).astype(BF16)
            kts[t, i] = jnp.where(keep_kb, jnp.concatenate([kt] * B_HEADS, axis=0), 0.0)
    a_rows = {(t, i): _nt(qts[t, i], kts[t, i]) for t in inst for i in range(n_sub)}
    a_mat = {t: jnp.where(keep_a, jnp.concatenate([a_rows[t, i] for i in range(n_sub)], axis=0), 0.0).astype(BF16)
             for t in inst}
    o_intra = {t: jnp.dot(a_mat[t], jnp.where(keep_vb, jnp.concatenate([v_bf[t]] * B_HEADS, axis=0), 0.0),
                          preferred_element_type=F32) for t in inst}
    kv = {t: jnp.where(keep_st, _tn(v_bf[t], kdec[t]), 0.0) for t in inst}
    for t in inst:
        ci, bi = t
        st = st_scr[bi]
        o = o_intra[t] + _nt(qdec[t], st.astype(BF16))
        st_scr[bi] = st * jnp.exp(bcum[t][cs - 1:cs, :]) + kv[t]
        rg = rg_ref[bi, sls[ci], :]
        for h in range(B_HEADS):
            vs = slice(h * B_DV, (h + 1) * B_DV)
            ob_ref[bi, sls[ci], vs] = (_rms(o[:, vs]) * go * _silu(rg[:, vs])).astype(BF16)

    @pl.when(j == pl.num_programs(1) - 1)
    def _():
        for bi in range(bb):
            st = st_scr[bi]
            for h in range(B_HEADS):
                st_ref[bi, h] = st[h * B_DV:(h + 1) * B_DV, h * B_DK:(h + 1) * B_DK]


def _gla(qg, kg, vg, gg, rg, state_t, g_out, *, cs, nc, bb):
    b, t, _ = qg.shape
    assert b % bb == 0
    tt = cs * nc
    kern = functools.partial(_gla_kernel, cs=cs, nc=nc, bb=bb)

    def tok(w):
        return pl.BlockSpec((bb, tt, w), lambda i, j: (i, j, 0))

    st_spec = pl.BlockSpec((bb, B_HEADS, B_DV, B_DK), lambda i, j: (i, 0, 0, 0))
    return pl.pallas_call(
        kern,
        grid=(b // bb, t // tt),
        in_specs=[tok(256), tok(256), tok(512), tok(256), tok(512), st_spec,
                  pl.BlockSpec((1, B_DV), lambda i, j: (0, 0))],
        out_specs=[tok(512), st_spec],
        out_shape=[jax.ShapeDtypeStruct((b, t, 512), BF16),
                   jax.ShapeDtypeStruct((b, B_HEADS, B_DV, B_DK), F32)],
        scratch_shapes=[pltpu.VMEM((bb, B_HEADS * B_DV, B_HEADS * B_DK), F32)],
        name="gla",
        compiler_params=_params("arbitrary", "arbitrary"),
    )(qg, kg, vg, gg, rg, state_t, g_out)


def _merge_kernel(oa_ref, ob_ref, x_ref, mod_ref, wo_ref, gf_ref, wr_ref, br_ref, cnt0_ref,
                  x1_ref, h2_ref, ri_ref, rgate_ref, cnt_ref, carry_scr):
    @pl.when((pl.program_id(0) == 0) & (pl.program_id(1) == 0))
    def _():
        carry_scr[...] = cnt0_ref[...]

    mod = mod_ref[0]
    oa = oa_ref[:, 0].reshape(-1, oa_ref.shape[-1])
    cat = jnp.concatenate([oa, ob_ref[0]], axis=1)
    x1 = x_ref[0] + mod[2:3] * jnp.dot(cat, wo_ref[...], preferred_element_type=F32)
    x1_ref[0] = x1
    h2 = _rms(x1) * gf_ref[...] * (1.0 + mod[4:5]) + mod[3:4]
    _store_row_tiles(h2_ref.at[0], h2)
    tm = x1.shape[0]
    lane = lax.broadcasted_iota(I32, (tm, LANES), 1).astype(F32)
    hi = h2.astype(BF16)
    lo = (h2 - hi.astype(F32)).astype(BF16)
    wr = wr_ref[...]
    a = jnp.dot(hi, wr, preferred_element_type=F32)
    left = (a[:, :LANES] + a[:, LANES:]) + jnp.dot(lo, wr[:, :LANES], preferred_element_type=F32) + br_ref[...]
    idx, val = [], []
    for _ in range(TOP_K):
        m = jnp.max(left, axis=1, keepdims=True)
        e = jnp.argmax(left, axis=1, keepdims=True).astype(F32)
        idx.append(e)
        val.append(m)
        left = jnp.where(lane == e, -jnp.inf, left)
    ex = [jnp.exp(vv - val[0]) for vv in val]
    den = ex[0] + ex[1] + ex[2] + ex[3]
    onehot = jnp.zeros((tm, LANES), F32)
    for e in idx:
        onehot = onehot + jnp.where(lane == e, 1.0, 0.0)
    r = lax.broadcasted_iota(I32, (tm, tm), 0)
    c = lax.broadcasted_iota(I32, (tm, tm), 1)
    earlier = jnp.where(r > c, 1.0, 0.0).astype(BF16)
    before = jnp.dot(earlier, onehot.astype(BF16), preferred_element_type=F32) + carry_scr[...]
    ri = jnp.zeros((tm, LANES), F32)
    rgate = jnp.zeros((tm, LANES), F32)
    for kq in range(TOP_K):
        rank = jnp.sum(jnp.where(lane == idx[kq], before, 0.0), axis=1, keepdims=True)
        ri = jnp.where(lane == kq, idx[kq], ri)
        ri = jnp.where(lane == TOP_K + kq, rank, ri)
        rgate = jnp.where(lane == kq, ex[kq] / den, rgate)
    ri_ref[0] = ri.astype(I32)
    rgate_ref[0] = rgate
    carry_scr[...] = carry_scr[...] + jnp.sum(onehot, axis=0, keepdims=True)
    cnt_ref[...] = carry_scr[...]


def _merge(oa, ob, x, mod, w_out, g_ffn, wr_pad, br_pad, cnt0, tm):
    b, t, d = x.shape

    def tok(w):
        return pl.BlockSpec((1, tm, w), lambda i, j: (i, j, 0))

    def const(s):
        return pl.BlockSpec(s, lambda i, j: (0, 0))

    return pl.pallas_call(
        _merge_kernel,
        grid=(b, t // tm),
        in_specs=[pl.BlockSpec((tm // oa.shape[2], 1, oa.shape[2], 512), lambda i, j: (j, i, 0, 0)),
                  tok(512), tok(d), pl.BlockSpec((1, 6, d), lambda i, j: (i, 0, 0)),
                  const((d, d)), const((1, d)), const((d, 2 * LANES)), const((1, LANES)), const((1, LANES))],
        out_specs=[tok(d), pl.BlockSpec((1, tm * SUBLANES, LANES), lambda i, j: (i, j, 0)),
                   tok(LANES), tok(LANES), const((1, LANES))],
        out_shape=[jax.ShapeDtypeStruct((b, t, d), F32), jax.ShapeDtypeStruct((b, t * SUBLANES, LANES), F32),
                   jax.ShapeDtypeStruct((b, t, LANES), I32), jax.ShapeDtypeStruct((b, t, LANES), F32),
                   jax.ShapeDtypeStruct((1, LANES), F32)],
        scratch_shapes=[pltpu.VMEM((1, LANES), F32)],
        name="merge",
        compiler_params=_params("arbitrary", "arbitrary"),
    )(oa, ob, x, mod, w_out, g_ffn, wr_pad, br_pad, cnt0)


def _dispatch_kernel(pe_ref, hp_ref, dest_ref, h_hbm, *rest, tm, zero_init):
    if zero_init:
        xs_ref, ring, zbuf, in_sem, sem, zsem = rest

        @pl.when(pl.program_id(0) == 0)
        def _():
            zbuf[...] = jnp.zeros_like(zbuf)

            def zero_copy(e):
                start = pl.multiple_of(pe_ref[e] - SLOT_BLOCK, SLOT_BLOCK)
                return pltpu.make_async_copy(zbuf, xs_ref.at[pl.ds(start, SLOT_BLOCK)], zsem)

            for e in range(N_EXPERTS):
                @pl.when(hp_ref[e] > 0)
                def _():
                    zero_copy(e).start()
            for e in range(N_EXPERTS):
                @pl.when(hp_ref[e] > 0)
                def _():
                    zero_copy(e).wait()
    else:
        _, xs_ref, ring, in_sem, sem = rest

    step = pl.program_id(0)
    n_steps = pl.num_programs(0)
    slot_in = lax.rem(step, 3)
    cur = lax.rem(step, 2)

    def load(tile_idx, ring_slot):
        rows = pl.ds(pl.multiple_of(tile_idx * tm, tm), tm)
        return pltpu.make_async_copy(h_hbm.at[rows], ring.at[ring_slot], in_sem.at[ring_slot])

    @pl.when(step == 0)
    def _():
        load(0, 0).start()

    @pl.when(step + 1 < n_steps)
    def _():
        load(step + 1, lax.rem(step + 1, 3)).start()

    load(step, slot_in).wait()

    def row_copy(r, slot, which):
        return pltpu.make_async_copy(ring.at[slot_in, r], xs_ref.at[slot], sem.at[which])

    def issue(i, carry):
        for j in range(ROW_UNROLL):
            for kq in range(TOP_K):
                r = i * ROW_UNROLL + j
                row_copy(r, dest_ref[r * TOP_K + kq], cur).start(priority=kq % 2)
        return carry

    lax.fori_loop(0, tm // ROW_UNROLL, issue, 0)

    def drain(which):
        def body(i, carry):
            for _ in range(ROW_UNROLL * TOP_K):
                row_copy(0, 0, which).wait()
            return carry

        lax.fori_loop(0, tm // ROW_UNROLL, body, 0)

    @pl.when(step > 0)
    def _():
        drain(1 - cur)

    @pl.when(step == n_steps - 1)
    def _():
        drain(cur)


def _dispatch(pad_end, has_pad, dest_flat, h2_tiles, xs, n_slots, tm):
    n = h2_tiles.shape[0]
    tile = h2_tiles.shape[1:]
    zero_init = xs is None
    kern = functools.partial(_dispatch_kernel, tm=tm, zero_init=zero_init)
    in_specs = [pl.BlockSpec((tm * TOP_K,), lambda i, pe, hp: (i,), memory_space=pltpu.SMEM),
                pl.BlockSpec(memory_space=pl.ANY)]
    args = [pad_end, has_pad, dest_flat, h2_tiles]
    ring = pltpu.VMEM((3, tm) + tile, F32)
    scratch = [ring, pltpu.SemaphoreType.DMA((3,)), pltpu.SemaphoreType.DMA((2,))]
    aliases = {}
    if zero_init:
        scratch = [ring, pltpu.VMEM((SLOT_BLOCK,) + tile, F32), pltpu.SemaphoreType.DMA((3,)),
                   pltpu.SemaphoreType.DMA((2,)), pltpu.SemaphoreType.DMA(())]
    else:
        in_specs.append(pl.BlockSpec(memory_space=pl.ANY))
        args.append(xs)
        aliases = {4: 0}
    grid_spec = pltpu.PrefetchScalarGridSpec(
        num_scalar_prefetch=2,
        grid=(n // tm,),
        in_specs=in_specs,
        out_specs=pl.BlockSpec(memory_space=pl.ANY),
        scratch_shapes=scratch,
    )
    return pl.pallas_call(
        kern,
        grid_spec=grid_spec,
        out_shape=jax.ShapeDtypeStruct((n_slots,) + tile, F32),
        input_output_aliases=aliases,
        name="dispatch",
        compiler_params=_params("arbitrary"),
    )(*args)


def _expert_kernel(be_ref, nx_ref, nu_ref, xs_ref, wgu_hbm, bgu_ref, wdn_hbm, bdn_ref, ys_ref,
                   gu_f32, dn_f32, wgu_b, wdn_b, sem):
    j = pl.program_id(0)

    def fetch(e):
        return (pltpu.make_async_copy(wgu_hbm.at[e], gu_f32, sem.at[0]),
                pltpu.make_async_copy(wdn_hbm.at[e], dn_f32, sem.at[1]))

    @pl.when(j < nu_ref[0])
    def _():
        @pl.when(j == 0)
        def _():
            for cp in fetch(be_ref[0]):
                cp.start()

        @pl.when((j == 0) | (be_ref[j] != be_ref[jnp.maximum(j - 1, 0)]))
        def _():
            for cp in fetch(be_ref[j]):
                cp.wait()
            wgu_b[...] = gu_f32[...].astype(BF16)
            wdn_b[...] = dn_f32[...].astype(BF16)

            @pl.when(nx_ref[j] >= 0)
            def _():
                for cp in fetch(nx_ref[j]):
                    cp.start()

        f = wdn_b.shape[0]
        x = _load_row_tiles(xs_ref, SLOT_BLOCK).astype(BF16)
        gu = jnp.dot(x, wgu_b[...], preferred_element_type=F32) + bgu_ref[0]
        gate = jnp.minimum(gu[:, :f], SWIGLU_LIMIT)
        up = jnp.clip(gu[:, f:], -SWIGLU_LIMIT, SWIGLU_LIMIT)
        glu = gate / (1.0 + jnp.exp(-SWIGLU_ALPHA * gate))
        act = ((up + 1.0) * glu).astype(BF16)
        _store_row_tiles(ys_ref, jnp.dot(act, wdn_b[...], preferred_element_type=F32) + bdn_ref[0])


def _experts(block_expert, next_expert, n_used, xs, wgu, bgu, wdn, bdn):
    n_slots = xs.shape[0] // SUBLANES
    nb = n_slots // SLOT_BLOCK
    f, d = wdn.shape[1:]
    rows = SLOT_BLOCK * SUBLANES

    def blk(i, be, nx, nu):
        return (jnp.minimum(i, nu[0] - 1), 0)

    def bsel(i, be, nx, nu):
        return (be[i], 0, 0)

    grid_spec = pltpu.PrefetchScalarGridSpec(
        num_scalar_prefetch=3,
        grid=(nb,),
        in_specs=[pl.BlockSpec((rows, LANES), blk),
                  pl.BlockSpec(memory_space=pl.ANY),
                  pl.BlockSpec((1, 1, 2 * f), bsel),
                  pl.BlockSpec(memory_space=pl.ANY),
                  pl.BlockSpec((1, 1, d), bsel)],
        out_specs=pl.BlockSpec((rows, LANES), blk),
        scratch_shapes=[pltpu.VMEM((d, 2 * f), F32), pltpu.VMEM((f, d), F32),
                        pltpu.VMEM((d, 2 * f), BF16), pltpu.VMEM((f, d), BF16), pltpu.SemaphoreType.DMA((2,))],
    )
    return pl.pallas_call(
        _expert_kernel,
        grid_spec=grid_spec,
        out_shape=jax.ShapeDtypeStruct(xs.shape, F32),
        name="experts",
        compiler_params=_params("arbitrary"),
    )(block_expert, next_expert, n_used, xs, wgu, bgu, wdn, bdn)


def _combine_kernel(dest_ref, dnext_ref, x1_ref, gate_ref, mod_ref, modf_ref, gfin_ref, ys_ref, y_ref,
                    buf, sem, *, tm):
    nt = pl.num_programs(1)
    step = pl.program_id(0) * nt + pl.program_id(1)
    n_steps = pl.num_programs(0) * nt
    half = TOP_K * tm * SUBLANES
    cur = lax.rem(step, 2)

    def row_copy(which, kq, r, slot):
        off = pl.multiple_of(which * half + (kq * tm + r) * SUBLANES, SUBLANES)
        return pltpu.make_async_copy(ys_ref.at[slot], buf.at[pl.ds(off, SUBLANES), :], sem.at[which])

    def gather(slots_ref, which):
        def issue(i, carry):
            for j in range(ROW_UNROLL):
                for kq in range(TOP_K):
                    r = i * ROW_UNROLL + j
                    row_copy(which, kq, r, slots_ref[r * TOP_K + kq]).start(priority=kq % 2)
            return carry

        lax.fori_loop(0, tm // ROW_UNROLL, issue, 0)

    @pl.when(step == 0)
    def _():
        gather(dest_ref, cur)

    @pl.when(step + 1 < n_steps)
    def _():
        gather(dnext_ref, 1 - cur)

    def drain(i, carry):
        for _ in range(ROW_UNROLL * TOP_K):
            row_copy(cur, 0, 0, 0).wait()
        return carry

    lax.fori_loop(0, tm // ROW_UNROLL, drain, 0)
    gates = gate_ref[0]
    moe = None
    for kq in range(TOP_K):
        start = pl.multiple_of(cur * half + kq * tm * SUBLANES, SUBLANES)
        rows = _load_row_tiles(buf.at[pl.ds(start, tm * SUBLANES), :], tm)
        term = gates[:, kq:kq + 1] * rows
        moe = term if moe is None else moe + term
    xo = x1_ref[0] + mod_ref[0][5:6] * moe
    modf = modf_ref[0]
    y_ref[0] = _rms(xo) * gfin_ref[...] * (1.0 + modf[1:2]) + modf[0:1]


def _combine(dest_flat, x1, gates, mod, modf, g_final, ys, tm):
    b, t, d = x1.shape
    nt = t // tm
    last = b * nt - 1
    kern = functools.partial(_combine_kernel, tm=tm)
    return pl.pallas_call(
        kern,
        grid=(b, nt),
        in_specs=[pl.BlockSpec((tm * TOP_K,), lambda i, j: (i * nt + j,), memory_space=pltpu.SMEM),
                  pl.BlockSpec((tm * TOP_K,), lambda i, j: (jnp.minimum(i * nt + j + 1, last),),
                               memory_space=pltpu.SMEM),
                  pl.BlockSpec((1, tm, d), lambda i, j: (i, j, 0)),
                  pl.BlockSpec((1, tm, LANES), lambda i, j: (i, j, 0)),
                  pl.BlockSpec((1, 6, d), lambda i, j: (i, 0, 0)),
                  pl.BlockSpec((1, 2, d), lambda i, j: (i, 0, 0)),
                  pl.BlockSpec((1, d), lambda i, j: (0, 0)),
                  pl.BlockSpec(memory_space=pl.ANY)],
        out_specs=pl.BlockSpec((1, tm, d), lambda i, j: (i, j, 0)),
        out_shape=jax.ShapeDtypeStruct((b, t, d), F32),
        scratch_shapes=[pltpu.VMEM((2 * TOP_K * tm * SUBLANES, LANES), F32), pltpu.SemaphoreType.DMA((2,))],
        name="combine",
        compiler_params=_params("arbitrary", "arbitrary"),
    )(dest_flat, dest_flat, x1, gates, mod, modf, g_final, ys)


def _rope_tables(pos):
    half = A_HD // 2
    inv_freq = ROPE_THETA ** (-jnp.arange(half, dtype=F32) / half)
    ang = pos.astype(F32)[:, None] * inv_freq[None, :]
    cos = jnp.cos(ang)
    sin = jnp.sin(ang)
    reps = LANES // A_HD
    return jnp.tile(jnp.concatenate([cos, cos], axis=1), (1, reps)), jnp.tile(jnp.concatenate([-sin, sin], axis=1), (1, reps))


def _reorder_w_in(w_in):
    offs = [0]
    for w in (512, 128, 128, 256, 64, 4, 256, 256, 512, 16, 512):
        offs.append(offs[-1] + w)
    seg = [w_in[:, offs[i]:offs[i + 1]] for i in range(11)]
    qa, ka, va, qi, ki, wi, qg, kg, vg, lr, rg = seg
    pad = jnp.zeros((w_in.shape[0], LANES - IDX_DIM - IDX_HEADS - GATE_RANK), w_in.dtype)
    return jnp.concatenate([qa, ka, va, qi, ki, wi, lr, pad, qg, kg, vg, rg], axis=1).astype(BF16)


def kernel(x_prompt, x_sample, cache_k, cache_v, cache_kidx, state_gla, c_prompt, c_sample,
           w_mod, b_mod, g_mix, g_ffn, w_in, gla_w_gate, gla_b_gate, gla_g_out, w_out,
           w_router, b_router, w_gate_up, b_gate_up, w_down, b_down,
           w_mod_final, b_mod_final, g_final):
    depth = w_in.shape[0]
    assert depth == 1
    bp, sp, d = x_prompt.shape
    bs, ts, _ = x_sample.shape
    past = cache_k.shape[2]
    n_exp = w_router.shape[2]
    f = w_down.shape[2]

    c_all = jnp.concatenate([c_prompt, c_sample], axis=0)
    w_r = _reorder_w_in(w_in[0])
    wg_pad = jnp.zeros((LANES, B_HEADS * B_DK), F32).at[MISC_LR:MISC_LR + GATE_RANK].set(gla_w_gate[0])
    bg = gla_b_gate[0].reshape(1, -1)
    wr_pad = jnp.zeros((d, LANES), F32).at[:, :n_exp].set(w_router[0])
    wr_hi = wr_pad.astype(BF16)
    wr_cat = jnp.concatenate([wr_hi, (wr_pad - wr_hi.astype(F32)).astype(BF16)], axis=1)
    br_pad = jnp.full((1, LANES), NEG_BIG, F32).at[0, :n_exp].set(b_router[0])
    w_out_b = w_out[0].astype(BF16)
    bgu = b_gate_up[0].reshape(n_exp, 1, 2 * f)
    bdn = b_down[0].reshape(n_exp, 1, d)
    g_out = gla_g_out[0].reshape(1, B_DV)

    mod_all = _adaln(c_all, w_mod[0], b_mod[0]).reshape(bp + bs, 6, d)
    modf_all = _adaln(c_all, w_mod_final, b_mod_final).reshape(bp + bs, 2, d)
    mod_p, mod_s = mod_all[:bp], mod_all[bp:]
    modf_p, modf_s = modf_all[:bp], modf_all[bp:]

    def mixer(x, mod, pos, tm, past_kv, state_t, cs, nc, tq):
        cos_t, sin_t = _rope_tables(pos)
        qa, ka, va, qi, ki, misc, qg, kg, vg, gg, rg, kb, vx, kib = _premix(
            x, mod, g_mix[0].reshape(1, d), w_r, wg_pad, bg, cos_t, sin_t, tm)
        if past_kv is None:
            oa = _attn_causal(qa, qi, misc, kb, vx, kib, tq)
        else:
            n_past = past_kv[0].shape[2]
            assert tq == x.shape[1] and n_past % LANES == 0
            oa = _attn_call(qa, qi, misc, ka, va, ki, past_kv, tq=tq, q_off=0, n_tiles=1,
                            n_ctx=n_past + LANES, causal=False, n_keys=n_past + ka.shape[1], n_groups=1)[None]
        ob, st = _gla(qg, kg, vg, gg, rg, state_t, g_out, cs=cs, nc=nc, bb=GLA_BATCH)
        return oa, ob, (ka, va, ki, st)

    state0_p = jnp.zeros((bp, B_HEADS, B_DV, B_DK), F32)
    oa_p, ob_p, (ka_p, va_p, ki_p, st_p) = mixer(
        x_prompt, mod_p, jnp.arange(sp), min(TOKEN_TILE, sp), None, state0_p, CHUNK,
        min(GLA_CHUNKS_PER_STEP, sp // CHUNK), min(TOKEN_TILE, sp))
    state0_s = jnp.swapaxes(state_gla[0], -1, -2)
    past_kv = (jnp.transpose(cache_k[0], (0, 2, 3, 1)).reshape(bs, A_KV * A_HD, past),
               jnp.transpose(cache_v[0], (0, 2, 3, 1)).reshape(bs, A_KV * A_HD, past),
               jnp.swapaxes(cache_kidx[0], 1, 2))
    oa_s, ob_s, (ka_s, va_s, ki_s, st_s) = mixer(
        x_sample, mod_s, past + jnp.arange(ts), ts, past_kv, state0_s, ts, 1, ts)

    cnt0 = jnp.zeros((1, LANES), F32)
    x1_p, h2_p, ri_p, gate_p, cnt1 = _merge(oa_p, ob_p, x_prompt, mod_p, w_out_b, g_ffn[0].reshape(1, d),
                                            wr_cat, br_pad, cnt0, min(MERGE_TILE, sp))
    x1_s, h2_s, ri_s, gate_s, cnt2 = _merge(oa_s, ob_s, x_sample, mod_s, w_out_b, g_ffn[0].reshape(1, d),
                                            wr_cat, br_pad, cnt1, ts)
    counts = cnt2[0, :n_exp].astype(I32)
    padded = (counts + SLOT_BLOCK - 1) // SLOT_BLOCK * SLOT_BLOCK
    pad_end = jnp.cumsum(padded)
    pad_start = pad_end - padded
    n_asg = (bp * sp + bs * ts) * TOP_K
    nb = -(-n_asg // SLOT_BLOCK) + n_exp
    n_slots = nb * SLOT_BLOCK
    block_start = jnp.arange(nb, dtype=I32) * SLOT_BLOCK
    block_expert = jnp.minimum(jnp.sum((pad_end[None, :] <= block_start[:, None]).astype(I32), axis=1), n_exp - 1)
    n_used = (pad_end[-1:] // SLOT_BLOCK).astype(I32)
    following = pad_end.astype(I32)[block_expert] // SLOT_BLOCK
    next_expert = jnp.where(following < n_used[0], block_expert[jnp.minimum(following, nb - 1)], -1).astype(I32)
    has_pad = (padded > counts).astype(I32)
    expert_ids = jnp.arange(n_exp, dtype=I32)

    def dests(ri):
        e, rank = ri[..., :TOP_K], ri[..., TOP_K:2 * TOP_K]
        start = jnp.sum(jnp.where(e[..., None] == expert_ids, pad_start.astype(I32), 0), axis=-1)
        return (start + rank).reshape(-1)

    dest_p, dest_s = dests(ri_p), dests(ri_s)
    pad_end_i = pad_end.astype(I32)
    tile = (SUBLANES, LANES)
    xs = _dispatch(pad_end_i, has_pad, dest_p, h2_p.reshape((bp * sp,) + tile), None, n_slots, min(ROW_DMA_TILE, sp))
    xs = _dispatch(pad_end_i, has_pad, dest_s, h2_s.reshape((bs * ts,) + tile), xs, n_slots, min(TOKEN_TILE, bs * ts))
    ys = _experts(block_expert, next_expert, n_used, xs.reshape(n_slots * SUBLANES, LANES), w_gate_up[0], bgu,
                  w_down[0], bdn)
    ys = ys.reshape((n_slots,) + tile)
    y_p = _combine(dest_p, x1_p, gate_p, mod_p, modf_p, g_final.reshape(1, d), ys, min(TOKEN_TILE, sp))
    y_s = _combine(dest_s, x1_s, gate_s, mod_s, modf_s, g_final.reshape(1, d), ys, ts)

    def kv(a, b, t):
        return a.reshape(1, b, t, A_KV, A_HD)

    return (y_p, y_s,
            kv(ka_p, bp, sp), kv(va_p, bp, sp), ki_p[None], jnp.swapaxes(st_p, -1, -2)[None],
            kv(ka_s, bs, ts), kv(va_s, bs, ts), ki_s[None], jnp.swapaxes(st_s, -1, -2)[None])
```

```python
import functools

import jax
import jax.numpy as jnp
from jax import lax
from jax.experimental import pallas as pl
from jax.experimental.pallas import tpu as pltpu

F32 = jnp.float32
BF16 = jnp.bfloat16
I32 = jnp.int32
HI = lax.Precision.HIGHEST

CHUNK = 64
CHUNK_SHIFT = 6
EPS = 1e-6
ROPE_THETA = 10000.0
A_HD = 64
A_HEADS = 8
A_KV = 2
IDX_HEADS = 4
IDX_DIM = 64
INDEX_TOPK = 256
B_HEADS = 4
B_DK = 64
B_DV = 128
GATE_RANK = 16
GATE_TAU = 16.0
N_EXPERTS = 32
TOP_K = 4
SWIGLU_LIMIT = 7.0
SWIGLU_ALPHA = 1.702

LANES = 128
SUBLANES = 8
TOKEN_TILE = 256
MERGE_TILE = 512
ROW_UNROLL = 8
ROW_DMA_TILE = 512
GLA_SUB = 16
GLA_BATCH = 2
GLA_CHUNKS_PER_STEP = 8
GLA_EXP_CLAMP = 80.0
SLOT_BLOCK = 512
VMEM_LIMIT = 56 * 1024 * 1024
INT_MIN = -2147483648
NEG_BIG = -1e30
SEARCH_TWO_BIT_MAX_KEYS = 768

C_QA, C_KA, C_VA, C_QI, C_MISC, C_QG, C_KG, C_VG, C_RG, C_END = 0, 512, 640, 768, 1024, 1152, 1408, 1664, 2176, 2688
MISC_WI = 64
MISC_LR = 68


def _params(*sem):
    return pltpu.CompilerParams(dimension_semantics=sem, vmem_limit_bytes=VMEM_LIMIT)


def _nt(a, b):
    return lax.dot_general(a, b, (((1,), (1,)), ((), ())), preferred_element_type=F32)


def _tn(a, b):
    return lax.dot_general(a, b, (((0,), (0,)), ((), ())), preferred_element_type=F32)


def _rms(x):
    return x * lax.rsqrt(jnp.mean(x * x, axis=-1, keepdims=True) + EPS)


def _silu(x):
    return x / (1.0 + jnp.exp(-x))


def _slot_index_base(r0, mt):
    shift = mt.bit_length() - 1
    assert mt == 1 << shift and mt % ROW_UNROLL == 0
    return lax.shift_left(lax.shift_right_logical(r0, shift), shift + 2) + (r0 & (mt - 1))


def _store_row_tiles(ref, val):
    rows, width = val.shape
    assert width == SUBLANES * LANES
    for s in range(SUBLANES):
        ref[pl.ds(s, rows, stride=SUBLANES), :] = val[:, s * LANES:(s + 1) * LANES]


def _load_row_tiles(ref, rows):
    return jnp.concatenate([ref[pl.ds(s, rows, stride=SUBLANES), :] for s in range(SUBLANES)], axis=1)


def _value_with_ones(vb):
    return jnp.concatenate([vb, jnp.ones_like(vb)], axis=-1)


def _adaln_kernel(c_ref, w_ref, b_ref, o_ref):
    a = _silu(c_ref[...])
    o_ref[...] = jnp.dot(a, w_ref[...], preferred_element_type=F32, precision=HI) + b_ref[...]


def _adaln(c, w, b):
    r, d = c.shape
    n = w.shape[1]
    tn = 512
    return pl.pallas_call(
        _adaln_kernel,
        grid=(n // tn,),
        in_specs=[pl.BlockSpec((r, d), lambda j: (0, 0)),
                  pl.BlockSpec((d, tn), lambda j: (0, j)),
                  pl.BlockSpec((1, tn), lambda j: (0, j))],
        out_specs=pl.BlockSpec((r, tn), lambda j: (0, j)),
        out_shape=jax.ShapeDtypeStruct((r, n), F32),
        name="adaln",
        compiler_params=_params("arbitrary"),
    )(c, w, b.reshape(1, n))


def _premix_kernel(x_ref, mod_ref, g_ref, w_ref, wg_ref, bg_ref, cos_ref, sin_ref,
                   qa_ref, ka_ref, va_ref, qi_ref, ki_ref, misc_ref, qg_ref, kg_ref, vg_ref, gg_ref, rg_ref,
                   kb_ref, vx_ref, kib_ref):
    x = x_ref[0]
    mod = mod_ref[0]
    hb = (_rms(x) * g_ref[...] * (1.0 + mod[1:2]) + mod[0:1]).astype(BF16)
    tm = x.shape[0]

    def project(c0, c1):
        return jnp.dot(hb, w_ref[:, c0:c1], preferred_element_type=F32)

    cos = cos_ref[...]
    sin = sin_ref[...]
    lane = lax.broadcasted_iota(I32, (tm, LANES), 1)
    lower_half = (lane & (A_HD - 1)) < (A_HD // 2)

    def rope(xc):
        rot = jnp.where(lower_half, pltpu.roll(xc, LANES - A_HD // 2, 1), pltpu.roll(xc, A_HD // 2, 1))
        return xc * cos + rot * sin

    seg = project(C_QA, C_KA)
    for j in range((C_KA - C_QA) // LANES):
        qa_ref[0, :, j * LANES:(j + 1) * LANES] = (rope(seg[:, j * LANES:(j + 1) * LANES]) * (A_HD ** -0.5)).astype(BF16)
    seg = project(C_KA, C_QI)
    ka = rope(seg[:, :C_VA - C_KA])
    va = seg[:, C_VA - C_KA:]
    ka_ref[0] = ka
    va_ref[0] = va
    kb_ref[0] = ka.T.astype(BF16)
    vx_ref[0] = _value_with_ones(va.astype(BF16))
    seg = project(C_QI, C_QG)
    for j in range((C_MISC - C_QI) // LANES):
        qi_ref[0, :, j * LANES:(j + 1) * LANES] = rope(seg[:, j * LANES:(j + 1) * LANES]).astype(BF16)
    m = seg[:, C_MISC - C_QI:]
    mr = rope(m)
    ki_ref[0] = mr[:, :IDX_DIM]
    kib_ref[0] = mr.T[:IDX_DIM].astype(BF16)
    misc_ref[0] = jnp.where(lane < IDX_DIM, mr, m * (IDX_HEADS ** -0.5))
    xg = jnp.dot(m, wg_ref[...], preferred_element_type=F32, precision=HI) + bg_ref[...]
    gg_ref[0] = (jnp.minimum(xg, 0.0) - jnp.log(1.0 + jnp.exp(-jnp.abs(xg)))) * (1.0 / GATE_TAU)
    qg_ref[0] = project(C_QG, C_KG) * (B_DK ** -0.5)
    kg_ref[0] = project(C_KG, C_VG)
    vg_ref[0] = project(C_VG, C_RG)
    rg_ref[0] = project(C_RG, C_END)


def _premix(x, mod, g_mix, w_r, wg_pad, bg, cos_t, sin_t, tm):
    b, t, d = x.shape
    rm, fm = False, True
    widths = [(512, BF16, rm), (128, F32, rm), (128, F32, rm), (256, BF16, rm), (64, F32, rm), (128, F32, rm),
              (256, F32, rm), (256, F32, rm), (512, F32, rm), (256, F32, rm), (512, F32, rm),
              (128, BF16, fm), (2 * LANES, BF16, rm), (IDX_DIM, BF16, fm)]
    return pl.pallas_call(
        _premix_kernel,
        grid=(b, t // tm),
        in_specs=[pl.BlockSpec((1, tm, d), lambda i, j: (i, j, 0)),
                  pl.BlockSpec((1, 6, d), lambda i, j: (i, 0, 0)),
                  pl.BlockSpec((1, d), lambda i, j: (0, 0)),
                  pl.BlockSpec((d, C_END), lambda i, j: (0, 0)),
                  pl.BlockSpec((LANES, 256), lambda i, j: (0, 0)),
                  pl.BlockSpec((1, 256), lambda i, j: (0, 0)),
                  pl.BlockSpec((tm, LANES), lambda i, j: (j, 0)),
                  pl.BlockSpec((tm, LANES), lambda i, j: (j, 0))],
        out_specs=[pl.BlockSpec((1, tm, w), lambda i, j: (i, j, 0)) if not fm else
                   pl.BlockSpec((1, w, tm), lambda i, j: (i, 0, j)) for w, _, fm in widths],
        out_shape=[jax.ShapeDtypeStruct((b, w, t) if fm else (b, t, w), dt) for w, dt, fm in widths],
        name="premix",
        compiler_params=_params("arbitrary", "arbitrary"),
    )(x, mod, g_mix, w_r, wg_pad, bg, cos_t, sin_t)


def _attn_kernel(*refs, tq, n_ctx, top_k, causal, n_keys, q_off, n_groups, n_past, keep_all):
    if n_past:
        q_ref, qi_ref, misc_ref, k_ref, v_ref, ki_ref, pk_ref, pv_ref, pki_ref, o_ref = refs
        assert n_ctx - n_past == LANES and k_ref.shape[1] <= LANES

        def new_columns(rows_ref):
            new = rows_ref[0]
            n_new, width = new.shape
            if width < LANES:
                new = jnp.concatenate([new, jnp.zeros((n_new, LANES - width), F32)], axis=1)
            square = jnp.concatenate([new, jnp.zeros((LANES - n_new, LANES), F32)], axis=0)
            return square.T[:width].astype(BF16)

        k = jnp.concatenate([pk_ref[0].astype(BF16), new_columns(k_ref)], axis=1)
        v = jnp.concatenate([pv_ref[0].astype(BF16), new_columns(v_ref)], axis=1)
        vx = jnp.concatenate([v, jnp.ones_like(v)], axis=0)
        kib = jnp.concatenate([pki_ref[0].astype(BF16), new_columns(ki_ref)], axis=1)
    else:
        q_ref, qi_ref, misc_ref, k_ref, vx_ref, ki_ref, o_ref = refs
        k = k_ref[0]
        vx = vx_ref[0]
        kib = ki_ref[0]
    values_feature_major = bool(n_past)
    rows = tq // n_groups
    row0 = q_off + pl.program_id(1) * tq
    keys, kks = [], []
    idx_dots = {}
    for g in range(n_groups):
        qi = qi_ref[0, g * rows:(g + 1) * rows, :]
        for h in range(IDX_HEADS):
            qh = qi[:, h * IDX_DIM:(h + 1) * IDX_DIM]
            idx_dots[g, h] = jnp.dot(qh, kib, preferred_element_type=F32)
    for g in range(n_groups):
        misc = misc_ref[0, g * rows:(g + 1) * rows, :]
        isc = jnp.zeros((rows, n_ctx), F32)
        for h in range(IDX_HEADS):
            isc = isc + misc[:, MISC_WI + h:MISC_WI + h + 1] * jnp.maximum(idx_dots[g, h], 0.0)
        kpos = lax.broadcasted_iota(I32, (rows, n_ctx), 1)
        if causal:
            row = lax.broadcasted_iota(I32, (rows, 1), 0) + (row0 + g * rows)
            key_lim = (lax.shift_right_logical(row, CHUNK_SHIFT) + 1) * CHUNK
        else:
            key_lim = jnp.full((rows, 1), n_keys, I32)
        bits = pltpu.bitcast(isc, I32)
        key = jnp.where(bits < 0, INT_MIN - bits, bits)
        keys.append(jnp.where(kpos < key_lim, key, INT_MIN))
        kks.append(jnp.minimum(key_lim, top_k).astype(F32))

    def count_at_least(g, cand):
        return jnp.sum(jnp.where(keys[g] >= (cand ^ INT_MIN), 1.0, 0.0), axis=1, keepdims=True)

    bits_per_pass = 2 if n_ctx <= SEARCH_TWO_BIT_MAX_KEYS else 1

    def search(i, ans):
        out = []
        for g in range(n_groups):
            if bits_per_pass == 1:
                cand = ans[g] | lax.shift_left(jnp.int32(1), 31 - i)
                out.append(jnp.where(count_at_least(g, cand) >= kks[g], cand, ans[g]))
            else:
                a1 = ans[g] | lax.shift_left(jnp.int32(1), 31 - 2 * i)
                a2 = ans[g] | lax.shift_left(jnp.int32(1), 30 - 2 * i)
                a3 = a1 | a2
                c1, c2, c3 = count_at_least(g, a1), count_at_least(g, a2), count_at_least(g, a3)
                kk = kks[g]
                out.append(jnp.where(c3 >= kk, a3, jnp.where(c1 >= kk, a1, jnp.where(c2 >= kk, a2, ans[g]))))
        return tuple(out)

    ans = tuple(jnp.zeros((rows, 1), I32) for _ in range(n_groups))
    if not keep_all:
        ans = lax.fori_loop(0, 32 // bits_per_pass, search, ans)
    r = lax.broadcasted_iota(I32, (LANES, LANES), 0)
    c = lax.broadcasted_iota(I32, (LANES, LANES), 1)
    upper = jnp.where(r < c, 1.0, 0.0).astype(BF16)
    n_blk = n_ctx // LANES
    thrs = [ans[g] ^ INT_MIN for g in range(n_groups)]
    ties = {(g, j): jnp.where(keys[g][:, j * LANES:(j + 1) * LANES] == thrs[g], 1.0, 0.0)
            for g in range(n_groups) for j in range(n_blk)}
    within = {gj: jnp.dot(t.astype(BF16), upper, preferred_element_type=F32) for gj, t in ties.items()}
    bias_groups = []
    for g in range(n_groups):
        gt = keys[g] > thrs[g]
        need = kks[g] - jnp.sum(jnp.where(gt, 1.0, 0.0), axis=1, keepdims=True)
        carry = jnp.zeros((rows, 1), F32)
        blocks = []
        for j in range(n_blk):
            sl = slice(j * LANES, (j + 1) * LANES)
            take_tie = jnp.where(within[g, j] + carry < need, ties[g, j], 0.0)
            blocks.append(jnp.where(gt[:, sl], 0.0, jnp.where(take_tie > 0.0, 0.0, NEG_BIG)))
            carry = carry + jnp.sum(ties[g, j], axis=1, keepdims=True)
        bias_groups.append(jnp.concatenate(blocks, axis=1))
    bias = jnp.concatenate(bias_groups, axis=0)

    q = q_ref[0]
    rep = A_HEADS // A_KV
    if values_feature_major and tq * A_HEADS <= 2 * LANES:
        q_groups = [jnp.concatenate([q[:, hh * A_HD:(hh + 1) * A_HD] for hh in range(g * rep, (g + 1) * rep)], axis=0)
                    for g in range(A_KV)]
        s_all = jnp.concatenate([jnp.dot(q_groups[g], k[g * A_HD:(g + 1) * A_HD, :], preferred_element_type=F32)
                                 for g in range(A_KV)], axis=0) + jnp.concatenate([bias] * A_HEADS, axis=0)
        p_all = jnp.exp((s_all - jnp.max(s_all, axis=1, keepdims=True)).astype(BF16))
        ox = _nt(p_all, vx)
        outs = [ox[hh * tq:(hh + 1) * tq, (hh // rep) * A_HD:(hh // rep + 1) * A_HD]
                / ox[hh * tq:(hh + 1) * tq, LANES:LANES + 1] for hh in range(A_HEADS)]
        o_ref[0] = jnp.concatenate(outs, axis=1).astype(BF16)
        return
    kgs = [k[g * A_HD:(g + 1) * A_HD, :] for g in range(A_KV)]
    qs = [q[:, hh * A_HD:(hh + 1) * A_HD] for hh in range(A_HEADS)]
    ss = [jnp.dot(qh, kgs[hh // rep], preferred_element_type=F32) + bias for hh, qh in enumerate(qs)]
    ps = [jnp.exp((s - jnp.max(s, axis=1, keepdims=True)).astype(BF16)) for s in ss]
    oxs = [_nt(p, vx) if values_feature_major else jnp.dot(p, vx, preferred_element_type=F32) for p in ps]
    outs = [ox[:, (hh // rep) * A_HD:(hh // rep + 1) * A_HD] / ox[:, LANES:LANES + 1] for hh, ox in enumerate(oxs)]
    o_ref[0] = jnp.concatenate(outs, axis=1).astype(BF16)


def _attn_call(q, qi, misc, kb, vx, kib, past, *, tq, q_off, n_tiles, n_ctx, causal, n_keys, n_groups):
    b = q.shape[0]
    t0 = q_off // tq
    top_k = min(INDEX_TOPK, n_keys // 4)
    n_past = 0 if past is None else past[0].shape[2]
    n_own = n_ctx if past is None else kb.shape[1]
    kern = functools.partial(_attn_kernel, tq=tq, n_ctx=n_ctx, top_k=top_k, causal=causal, n_keys=n_keys,
                             q_off=q_off, n_groups=n_groups, n_past=n_past,
                             keep_all=causal and q_off + n_tiles * tq <= top_k)

    def tok(w):
        return pl.BlockSpec((1, tq, w), lambda i, j: (i, t0 + j, 0))

    def ctx(rows, w):
        return pl.BlockSpec((1, rows, w), lambda i, j: (i, 0, 0))

    if past is None:
        own_specs = [ctx(kb.shape[1], n_ctx), ctx(n_ctx, vx.shape[2]), ctx(kib.shape[1], n_ctx)]
    else:
        own_specs = [ctx(n_own, a.shape[2]) for a in (kb, vx, kib)]
    in_specs = [tok(512), tok(256), tok(LANES)] + own_specs
    args = [q, qi, misc, kb, vx, kib]
    if past is not None:
        in_specs += [ctx(a.shape[1], n_past) for a in past]
        args += list(past)
    return pl.pallas_call(
        kern,
        grid=(b, n_tiles),
        in_specs=in_specs,
        out_specs=pl.BlockSpec((1, tq, 512), lambda i, j: (i, j, 0)),
        out_shape=jax.ShapeDtypeStruct((b, n_tiles * tq, 512), BF16),
        name="attn",
        compiler_params=_params("arbitrary", "arbitrary"),
    )(*args)


def _attn_causal(q, qi, misc, kb, vx, kib, tq):
    t = q.shape[1]
    return jnp.stack([
        _attn_call(q, qi, misc, kb, vx, kib, None, tq=tq, q_off=c * tq, n_tiles=1, n_ctx=(c + 1) * tq,
                   causal=True, n_keys=t, n_groups=2)
        for c in range(t // tq)])


def _gla_kernel(q_ref, k_ref, v_ref, g_ref, rg_ref, s0_ref, go_ref, ob_ref, st_ref, st_scr, *, cs, nc, bb):
    j = pl.program_id(1)
    hk, hv, hc = B_HEADS * B_DK, B_HEADS * B_DV, B_HEADS * cs

    def head_of(idx, width):
        return lax.shift_right_logical(idx, width.bit_length() - 1)

    def same_head(rows, row_w, cols, col_w):
        r = head_of(lax.broadcasted_iota(I32, (rows, cols), 0), row_w)
        c = head_of(lax.broadcasted_iota(I32, (rows, cols), 1), col_w)
        return r == c

    keep_kb = same_head(hc, cs, hk, B_DK)
    keep_vb = same_head(hc, cs, hv, B_DV)
    keep_st = same_head(hv, B_DV, hk, B_DK)
    t_idx = lax.broadcasted_iota(I32, (cs, hc), 0)
    s_idx = lax.broadcasted_iota(I32, (cs, hc), 1) & (cs - 1)
    keep_a = t_idx >= s_idx
    r = lax.broadcasted_iota(I32, (cs, cs), 0)
    c = lax.broadcasted_iota(I32, (cs, cs), 1)
    tri = jnp.where(r >= c, 1.0, 0.0)
    go = go_ref[...]

    @pl.when(j == 0)
    def _():
        for bi in range(bb):
            blocks = []
            for h in range(B_HEADS):
                parts = [jnp.zeros((B_DV, B_DK), F32)] * B_HEADS
                parts[h] = s0_ref[bi, h]
                blocks.append(jnp.concatenate(parts, axis=1))
            st_scr[bi] = jnp.concatenate(blocks, axis=0)

    inst = [(ci, bi) for ci in range(nc) for bi in range(bb)]
    sls = {ci: slice(ci * cs, (ci + 1) * cs) for ci in range(nc)}
    bcum = {t: jnp.dot(tri, g_ref[t[1], sls[t[0]], :], preferred_element_type=F32, precision=HI) for t in inst}
    n_sub = cs // GLA_SUB
    v_bf, qdec, kdec, qts, kts = {}, {}, {}, {}, {}
    for t in inst:
        ci, bi = t
        q = q_ref[bi, sls[ci], :]
        k = k_ref[bi, sls[ci], :]
        bc = bcum[t]
        blast = bc[cs - 1:cs, :]
        qdec[t] = (q * jnp.exp(bc)).astype(BF16)
        kdec[t] = (k * jnp.exp(blast - bc)).astype(BF16)
        v_bf[t] = v_ref[bi, sls[ci], :].astype(BF16)
        for i in range(n_sub):
            rs = slice(i * GLA_SUB, (i + 1) * GLA_SUB)
            ref = bc[i * GLA_SUB:i * GLA_SUB + 1, :]
            qts[t, i] = (q[rs, :] * jnp.exp(bc[rs, :] - ref)).astype(BF16)
            kt = (k * jnp.exp(jnp.minimum(ref - bc, GLA_EXP_CLAMP))).astype(BF16)
            kts[t, i] = jnp.where(keep_kb, jnp.concatenate([kt] * B_HEADS, axis=0), 0.0)
    a_rows = {(t, i): _nt(qts[t, i], kts[t, i]) for t in inst for i in range(n_sub)}
    a_mat = {t: jnp.where(keep_a, jnp.concatenate([a_rows[t, i] for i in range(n_sub)], axis=0), 0.0).astype(BF16)
             for t in inst}
    o_intra = {t: jnp.dot(a_mat[t], jnp.where(keep_vb, jnp.concatenate([v_bf[t]] * B_HEADS, axis=0), 0.0),
                          preferred_element_type=F32) for t in inst}
    kv = {t: jnp.where(keep_st, _tn(v_bf[t], kdec[t]), 0.0) for t in inst}
    for t in inst:
        ci, bi = t
        st = st_scr[bi]
        o = o_intra[t] + _nt(qdec[t], st.astype(BF16))
        st_scr[bi] = st * jnp.exp(bcum[t][cs - 1:cs, :]) + kv[t]
        rg = rg_ref[bi, sls[ci], :]
        for h in range(B_HEADS):
            vs = slice(h * B_DV, (h + 1) * B_DV)
            ob_ref[bi, sls[ci], vs] = (_rms(o[:, vs]) * go * _silu(rg[:, vs])).astype(BF16)

    @pl.when(j == pl.num_programs(1) - 1)
    def _():
        for bi in range(bb):
            st = st_scr[bi]
            for h in range(B_HEADS):
                st_ref[bi, h] = st[h * B_DV:(h + 1) * B_DV, h * B_DK:(h + 1) * B_DK]


def _gla(qg, kg, vg, gg, rg, state_t, g_out, *, cs, nc, bb):
    b, t, _ = qg.shape
    assert b % bb == 0
    tt = cs * nc
    kern = functools.partial(_gla_kernel, cs=cs, nc=nc, bb=bb)

    def tok(w):
        return pl.BlockSpec((bb, tt, w), lambda i, j: (i, j, 0))

    st_spec = pl.BlockSpec((bb, B_HEADS, B_DV, B_DK), lambda i, j: (i, 0, 0, 0))
    return pl.pallas_call(
        kern,
        grid=(b // bb, t // tt),
        in_specs=[tok(256), tok(256), tok(512), tok(256), tok(512), st_spec,
                  pl.BlockSpec((1, B_DV), lambda i, j: (0, 0))],
        out_specs=[tok(512), st_spec],
        out_shape=[jax.ShapeDtypeStruct((b, t, 512), BF16),
                   jax.ShapeDtypeStruct((b, B_HEADS, B_DV, B_DK), F32)],
        scratch_shapes=[pltpu.VMEM((bb, B_HEADS * B_DV, B_HEADS * B_DK), F32)],
        name="gla",
        compiler_params=_params("arbitrary", "arbitrary"),
    )(qg, kg, vg, gg, rg, state_t, g_out)


def _merge_kernel(oa_ref, ob_ref, x_ref, mod_ref, wo_ref, gf_ref, wr_ref, br_ref, cnt0_ref,
                  x1_ref, h2_ref, ri_ref, rgate_ref, cnt_ref, carry_scr):
    @pl.when((pl.program_id(0) == 0) & (pl.program_id(1) == 0))
    def _():
        carry_scr[...] = cnt0_ref[...]

    mod = mod_ref[0]
    oa = oa_ref[:, 0].reshape(-1, oa_ref.shape[-1])
    cat = jnp.concatenate([oa, ob_ref[0]], axis=1)
    x1 = x_ref[0] + mod[2:3] * jnp.dot(cat, wo_ref[...], preferred_element_type=F32)
    x1_ref[0] = x1
    h2 = _rms(x1) * gf_ref[...] * (1.0 + mod[4:5]) + mod[3:4]
    _store_row_tiles(h2_ref.at[0], h2)
    tm = x1.shape[0]
    lane = lax.broadcasted_iota(I32, (tm, LANES), 1).astype(F32)
    hi = h2.astype(BF16)
    lo = (h2 - hi.astype(F32)).astype(BF16)
    wr = wr_ref[...]
    a = jnp.dot(hi, wr, preferred_element_type=F32)
    left = (a[:, :LANES] + a[:, LANES:]) + jnp.dot(lo, wr[:, :LANES], preferred_element_type=F32) + br_ref[...]
    idx, val = [], []
    for _ in range(TOP_K):
        m = jnp.max(left, axis=1, keepdims=True)
        e = jnp.argmax(left, axis=1, keepdims=True).astype(F32)
        idx.append(e)
        val.append(m)
        left = jnp.where(lane == e, -jnp.inf, left)
    ex = [jnp.exp(vv - val[0]) for vv in val]
    den = ex[0] + ex[1] + ex[2] + ex[3]
    onehot = jnp.zeros((tm, LANES), F32)
    for e in idx:
        onehot = onehot + jnp.where(lane == e, 1.0, 0.0)
    r = lax.broadcasted_iota(I32, (tm, tm), 0)
    c = lax.broadcasted_iota(I32, (tm, tm), 1)
    earlier = jnp.where(r > c, 1.0, 0.0).astype(BF16)
    before = jnp.dot(earlier, onehot.astype(BF16), preferred_element_type=F32) + carry_scr[...]
    ri = jnp.zeros((tm, LANES), F32)
    rgate = jnp.zeros((tm, LANES), F32)
    for kq in range(TOP_K):
        rank = jnp.sum(jnp.where(lane == idx[kq], before, 0.0), axis=1, keepdims=True)
        ri = jnp.where(lane == kq, idx[kq], ri)
        ri = jnp.where(lane == TOP_K + kq, rank, ri)
        rgate = jnp.where(lane == kq, ex[kq] / den, rgate)
    square = ri if tm % LANES == 0 else jnp.concatenate([ri, jnp.zeros((LANES - tm, LANES), F32)], axis=0)
    ri_ref[0] = square.T[:2 * TOP_K, :tm].astype(I32)
    rgate_ref[0] = rgate
    carry_scr[...] = carry_scr[...] + jnp.sum(onehot, axis=0, keepdims=True)
    cnt_ref[...] = carry_scr[...]


def _merge(oa, ob, x, mod, w_out, g_ffn, wr_pad, br_pad, cnt0, tm):
    b, t, d = x.shape

    def tok(w):
        return pl.BlockSpec((1, tm, w), lambda i, j: (i, j, 0))

    def const(s):
        return pl.BlockSpec(s, lambda i, j: (0, 0))

    return pl.pallas_call(
        _merge_kernel,
        grid=(b, t // tm),
        in_specs=[pl.BlockSpec((tm // oa.shape[2], 1, oa.shape[2], 512), lambda i, j: (j, i, 0, 0)),
                  tok(512), tok(d), pl.BlockSpec((1, 6, d), lambda i, j: (i, 0, 0)),
                  const((d, d)), const((1, d)), const((d, 2 * LANES)), const((1, LANES)), const((1, LANES))],
        out_specs=[tok(d), pl.BlockSpec((1, tm * SUBLANES, LANES), lambda i, j: (i, j, 0)),
                   pl.BlockSpec((1, 2 * TOP_K, tm), lambda i, j: (i * (t // tm) + j, 0, 0)), tok(LANES),
                   const((1, LANES))],
        out_shape=[jax.ShapeDtypeStruct((b, t, d), F32), jax.ShapeDtypeStruct((b, t * SUBLANES, LANES), F32),
                   jax.ShapeDtypeStruct((b * (t // tm), 2 * TOP_K, tm), I32), jax.ShapeDtypeStruct((b, t, LANES), F32),
                   jax.ShapeDtypeStruct((1, LANES), F32)],
        scratch_shapes=[pltpu.VMEM((1, LANES), F32)],
        name="merge",
        compiler_params=_params("arbitrary", "arbitrary"),
    )(oa, ob, x, mod, w_out, g_ffn, wr_pad, br_pad, cnt0)


def _dispatch_kernel(pe_ref, hp_ref, dest_ref, h_hbm, *rest, tm, mt, zero_init):
    if zero_init:
        xs_ref, ring, zbuf, in_sem, sem, zsem = rest

        @pl.when(pl.program_id(0) == 0)
        def _():
            zbuf[...] = jnp.zeros_like(zbuf)

            def zero_copy(e):
                start = pl.multiple_of(pe_ref[e] - SLOT_BLOCK, SLOT_BLOCK)
                return pltpu.make_async_copy(zbuf, xs_ref.at[pl.ds(start, SLOT_BLOCK)], zsem)

            for e in range(N_EXPERTS):
                @pl.when(hp_ref[e] > 0)
                def _():
                    zero_copy(e).start()
            for e in range(N_EXPERTS):
                @pl.when(hp_ref[e] > 0)
                def _():
                    zero_copy(e).wait()
    else:
        _, xs_ref, ring, in_sem, sem = rest

    step = pl.program_id(0)
    n_steps = pl.num_programs(0)
    slot_in = lax.rem(step, 3)
    cur = lax.rem(step, 2)

    def load(tile_idx, ring_slot):
        rows = pl.ds(pl.multiple_of(tile_idx * tm, tm), tm)
        return pltpu.make_async_copy(h_hbm.at[rows], ring.at[ring_slot], in_sem.at[ring_slot])

    @pl.when(step == 0)
    def _():
        load(0, 0).start()

    @pl.when(step + 1 < n_steps)
    def _():
        load(step + 1, lax.rem(step + 1, 3)).start()

    load(step, slot_in).wait()

    def row_copy(r, slot, which):
        return pltpu.make_async_copy(ring.at[slot_in, r], xs_ref.at[slot], sem.at[which])

    def issue(i, carry):
        r0 = i * ROW_UNROLL
        base = _slot_index_base(r0, mt)
        for j in range(ROW_UNROLL):
            for kq in range(TOP_K):
                row_copy(r0 + j, dest_ref[base + kq * mt + j], cur).start(priority=kq % 2)
        return carry

    lax.fori_loop(0, tm // ROW_UNROLL, issue, 0)

    def drain(which):
        def body(i, carry):
            for _ in range(ROW_UNROLL * TOP_K):
                row_copy(0, 0, which).wait()
            return carry

        lax.fori_loop(0, tm // ROW_UNROLL, body, 0)

    @pl.when(step > 0)
    def _():
        drain(1 - cur)

    @pl.when(step == n_steps - 1)
    def _():
        drain(cur)


def _dispatch(pad_end, has_pad, dest_flat, h2_tiles, xs, n_slots, tm, mt):
    n = h2_tiles.shape[0]
    tile = h2_tiles.shape[1:]
    zero_init = xs is None
    assert tm % mt == 0
    kern = functools.partial(_dispatch_kernel, tm=tm, mt=mt, zero_init=zero_init)
    in_specs = [pl.BlockSpec((tm * TOP_K,), lambda i, pe, hp: (i,), memory_space=pltpu.SMEM),
                pl.BlockSpec(memory_space=pl.ANY)]
    args = [pad_end, has_pad, dest_flat, h2_tiles]
    ring = pltpu.VMEM((3, tm) + tile, F32)
    scratch = [ring, pltpu.SemaphoreType.DMA((3,)), pltpu.SemaphoreType.DMA((2,))]
    aliases = {}
    if zero_init:
        scratch = [ring, pltpu.VMEM((SLOT_BLOCK,) + tile, F32), pltpu.SemaphoreType.DMA((3,)),
                   pltpu.SemaphoreType.DMA((2,)), pltpu.SemaphoreType.DMA(())]
    else:
        in_specs.append(pl.BlockSpec(memory_space=pl.ANY))
        args.append(xs)
        aliases = {4: 0}
    grid_spec = pltpu.PrefetchScalarGridSpec(
        num_scalar_prefetch=2,
        grid=(n // tm,),
        in_specs=in_specs,
        out_specs=pl.BlockSpec(memory_space=pl.ANY),
        scratch_shapes=scratch,
    )
    return pl.pallas_call(
        kern,
        grid_spec=grid_spec,
        out_shape=jax.ShapeDtypeStruct((n_slots,) + tile, F32),
        input_output_aliases=aliases,
        name="dispatch",
        compiler_params=_params("arbitrary"),
    )(*args)


def _expert_kernel(be_ref, nx_ref, nu_ref, xs_ref, wgu_hbm, bgu_ref, wdn_hbm, bdn_ref, ys_ref,
                   gu_f32, dn_f32, wgu_b, wdn_b, sem):
    j = pl.program_id(0)

    def fetch(e):
        return (pltpu.make_async_copy(wgu_hbm.at[e], gu_f32, sem.at[0]),
                pltpu.make_async_copy(wdn_hbm.at[e], dn_f32, sem.at[1]))

    @pl.when(j < nu_ref[0])
    def _():
        @pl.when(j == 0)
        def _():
            for cp in fetch(be_ref[0]):
                cp.start()

        @pl.when((j == 0) | (be_ref[j] != be_ref[jnp.maximum(j - 1, 0)]))
        def _():
            for cp in fetch(be_ref[j]):
                cp.wait()
            wgu_b[...] = gu_f32[...].astype(BF16)
            wdn_b[...] = dn_f32[...].astype(BF16)

            @pl.when(nx_ref[j] >= 0)
            def _():
                for cp in fetch(nx_ref[j]):
                    cp.start()

        f = wdn_b.shape[0]
        x = _load_row_tiles(xs_ref, SLOT_BLOCK).astype(BF16)
        gu = jnp.dot(x, wgu_b[...], preferred_element_type=F32) + bgu_ref[0]
        gate = jnp.minimum(gu[:, :f], SWIGLU_LIMIT)
        up = jnp.clip(gu[:, f:], -SWIGLU_LIMIT, SWIGLU_LIMIT)
        glu = gate / (1.0 + jnp.exp(-SWIGLU_ALPHA * gate))
        act = ((up + 1.0) * glu).astype(BF16)
        _store_row_tiles(ys_ref, jnp.dot(act, wdn_b[...], preferred_element_type=F32) + bdn_ref[0])


def _experts(block_expert, next_expert, n_used, xs, wgu, bgu, wdn, bdn):
    n_slots = xs.shape[0] // SUBLANES
    nb = n_slots // SLOT_BLOCK
    f, d = wdn.shape[1:]
    rows = SLOT_BLOCK * SUBLANES

    def blk(i, be, nx, nu):
        return (jnp.minimum(i, nu[0] - 1), 0)

    def bsel(i, be, nx, nu):
        return (be[i], 0, 0)

    grid_spec = pltpu.PrefetchScalarGridSpec(
        num_scalar_prefetch=3,
        grid=(nb,),
        in_specs=[pl.BlockSpec((rows, LANES), blk),
                  pl.BlockSpec(memory_space=pl.ANY),
                  pl.BlockSpec((1, 1, 2 * f), bsel),
                  pl.BlockSpec(memory_space=pl.ANY),
                  pl.BlockSpec((1, 1, d), bsel)],
        out_specs=pl.BlockSpec((rows, LANES), blk),
        scratch_shapes=[pltpu.VMEM((d, 2 * f), F32), pltpu.VMEM((f, d), F32),
                        pltpu.VMEM((d, 2 * f), BF16), pltpu.VMEM((f, d), BF16), pltpu.SemaphoreType.DMA((2,))],
    )
    return pl.pallas_call(
        _expert_kernel,
        grid_spec=grid_spec,
        out_shape=jax.ShapeDtypeStruct(xs.shape, F32),
        name="experts",
        compiler_params=_params("arbitrary"),
    )(block_expert, next_expert, n_used, xs, wgu, bgu, wdn, bdn)


def _combine_kernel(dest_ref, dnext_ref, x1_ref, gate_ref, mod_ref, modf_ref, gfin_ref, ys_ref, y_ref,
                    buf, sem, *, tm, mt):
    nt = pl.num_programs(1)
    step = pl.program_id(0) * nt + pl.program_id(1)
    n_steps = pl.num_programs(0) * nt
    half = TOP_K * tm * SUBLANES
    cur = lax.rem(step, 2)
    per_tile = mt // tm

    def row_copy(which, kq, r, slot):
        off = pl.multiple_of(which * half + (kq * tm + r) * SUBLANES, SUBLANES)
        return pltpu.make_async_copy(ys_ref.at[slot], buf.at[pl.ds(off, SUBLANES), :], sem.at[which])

    def gather(slots_ref, which, at_step):
        first_row = lax.rem(at_step, per_tile) * tm

        def issue(i, carry):
            r0 = i * ROW_UNROLL
            for j in range(ROW_UNROLL):
                for kq in range(TOP_K):
                    row_copy(which, kq, r0 + j, slots_ref[first_row + r0 + kq * mt + j]).start(priority=kq % 2)
            return carry

        lax.fori_loop(0, tm // ROW_UNROLL, issue, 0)

    @pl.when(step == 0)
    def _():
        gather(dest_ref, cur, step)

    @pl.when(step + 1 < n_steps)
    def _():
        gather(dnext_ref, 1 - cur, step + 1)

    def drain(i, carry):
        for _ in range(ROW_UNROLL * TOP_K):
            row_copy(cur, 0, 0, 0).wait()
        return carry

    lax.fori_loop(0, tm // ROW_UNROLL, drain, 0)
    gates = gate_ref[0]
    moe = None
    for kq in range(TOP_K):
        start = pl.multiple_of(cur * half + kq * tm * SUBLANES, SUBLANES)
        rows = _load_row_tiles(buf.at[pl.ds(start, tm * SUBLANES), :], tm)
        term = gates[:, kq:kq + 1] * rows
        moe = term if moe is None else moe + term
    xo = x1_ref[0] + mod_ref[0][5:6] * moe
    modf = modf_ref[0]
    y_ref[0] = _rms(xo) * gfin_ref[...] * (1.0 + modf[1:2]) + modf[0:1]


def _combine(dest_flat, x1, gates, mod, modf, g_final, ys, tm, mt):
    b, t, d = x1.shape
    nt = t // tm
    last = b * nt - 1
    assert mt % tm == 0
    per_tile = mt // tm
    kern = functools.partial(_combine_kernel, tm=tm, mt=mt)
    return pl.pallas_call(
        kern,
        grid=(b, nt),
        in_specs=[pl.BlockSpec((mt * TOP_K,), lambda i, j: ((i * nt + j) // per_tile,), memory_space=pltpu.SMEM),
                  pl.BlockSpec((mt * TOP_K,), lambda i, j: (jnp.minimum(i * nt + j + 1, last) // per_tile,),
                               memory_space=pltpu.SMEM),
                  pl.BlockSpec((1, tm, d), lambda i, j: (i, j, 0)),
                  pl.BlockSpec((1, tm, LANES), lambda i, j: (i, j, 0)),
                  pl.BlockSpec((1, 6, d), lambda i, j: (i, 0, 0)),
                  pl.BlockSpec((1, 2, d), lambda i, j: (i, 0, 0)),
                  pl.BlockSpec((1, d), lambda i, j: (0, 0)),
                  pl.BlockSpec(memory_space=pl.ANY)],
        out_specs=pl.BlockSpec((1, tm, d), lambda i, j: (i, j, 0)),
        out_shape=jax.ShapeDtypeStruct((b, t, d), F32),
        scratch_shapes=[pltpu.VMEM((2 * TOP_K * tm * SUBLANES, LANES), F32), pltpu.SemaphoreType.DMA((2,))],
        name="combine",
        compiler_params=_params("arbitrary", "arbitrary"),
    )(dest_flat, dest_flat, x1, gates, mod, modf, g_final, ys)


def _rope_tables(pos):
    half = A_HD // 2
    inv_freq = ROPE_THETA ** (-jnp.arange(half, dtype=F32) / half)
    ang = pos.astype(F32)[:, None] * inv_freq[None, :]
    cos = jnp.cos(ang)
    sin = jnp.sin(ang)
    reps = LANES // A_HD
    return jnp.tile(jnp.concatenate([cos, cos], axis=1), (1, reps)), jnp.tile(jnp.concatenate([-sin, sin], axis=1), (1, reps))


def _reorder_w_in(w_in):
    offs = [0]
    for w in (512, 128, 128, 256, 64, 4, 256, 256, 512, 16, 512):
        offs.append(offs[-1] + w)
    seg = [w_in[:, offs[i]:offs[i + 1]] for i in range(11)]
    qa, ka, va, qi, ki, wi, qg, kg, vg, lr, rg = seg
    pad = jnp.zeros((w_in.shape[0], LANES - IDX_DIM - IDX_HEADS - GATE_RANK), w_in.dtype)
    return jnp.concatenate([qa, ka, va, qi, ki, wi, lr, pad, qg, kg, vg, rg], axis=1).astype(BF16)


def kernel(x_prompt, x_sample, cache_k, cache_v, cache_kidx, state_gla, c_prompt, c_sample,
           w_mod, b_mod, g_mix, g_ffn, w_in, gla_w_gate, gla_b_gate, gla_g_out, w_out,
           w_router, b_router, w_gate_up, b_gate_up, w_down, b_down,
           w_mod_final, b_mod_final, g_final):
    depth = w_in.shape[0]
    assert depth == 1
    bp, sp, d = x_prompt.shape
    bs, ts, _ = x_sample.shape
    past = cache_k.shape[2]
    n_exp = w_router.shape[2]
    f = w_down.shape[2]

    c_all = jnp.concatenate([c_prompt, c_sample], axis=0)
    w_r = _reorder_w_in(w_in[0])
    wg_pad = jnp.zeros((LANES, B_HEADS * B_DK), F32).at[MISC_LR:MISC_LR + GATE_RANK].set(gla_w_gate[0])
    bg = gla_b_gate[0].reshape(1, -1)
    wr_pad = jnp.zeros((d, LANES), F32).at[:, :n_exp].set(w_router[0])
    wr_hi = wr_pad.astype(BF16)
    wr_cat = jnp.concatenate([wr_hi, (wr_pad - wr_hi.astype(F32)).astype(BF16)], axis=1)
    br_pad = jnp.full((1, LANES), NEG_BIG, F32).at[0, :n_exp].set(b_router[0])
    w_out_b = w_out[0].astype(BF16)
    bgu = b_gate_up[0].reshape(n_exp, 1, 2 * f)
    bdn = b_down[0].reshape(n_exp, 1, d)
    g_out = gla_g_out[0].reshape(1, B_DV)

    mod_all = _adaln(c_all, w_mod[0], b_mod[0]).reshape(bp + bs, 6, d)
    modf_all = _adaln(c_all, w_mod_final, b_mod_final).reshape(bp + bs, 2, d)
    mod_p, mod_s = mod_all[:bp], mod_all[bp:]
    modf_p, modf_s = modf_all[:bp], modf_all[bp:]

    def mixer(x, mod, pos, tm, past_kv, state_t, cs, nc, tq):
        cos_t, sin_t = _rope_tables(pos)
        qa, ka, va, qi, ki, misc, qg, kg, vg, gg, rg, kb, vx, kib = _premix(
            x, mod, g_mix[0].reshape(1, d), w_r, wg_pad, bg, cos_t, sin_t, tm)
        if past_kv is None:
            oa = _attn_causal(qa, qi, misc, kb, vx, kib, tq)
        else:
            n_past = past_kv[0].shape[2]
            assert tq == x.shape[1] and n_past % LANES == 0
            oa = _attn_call(qa, qi, misc, ka, va, ki, past_kv, tq=tq, q_off=0, n_tiles=1,
                            n_ctx=n_past + LANES, causal=False, n_keys=n_past + ka.shape[1], n_groups=1)[None]
        ob, st = _gla(qg, kg, vg, gg, rg, state_t, g_out, cs=cs, nc=nc, bb=GLA_BATCH)
        return oa, ob, (ka, va, ki, st)

    state0_p = jnp.zeros((bp, B_HEADS, B_DV, B_DK), F32)
    oa_p, ob_p, (ka_p, va_p, ki_p, st_p) = mixer(
        x_prompt, mod_p, jnp.arange(sp), min(TOKEN_TILE, sp), None, state0_p, CHUNK,
        min(GLA_CHUNKS_PER_STEP, sp // CHUNK), min(TOKEN_TILE, sp))
    state0_s = jnp.swapaxes(state_gla[0], -1, -2)
    past_kv = (jnp.transpose(cache_k[0], (0, 2, 3, 1)).reshape(bs, A_KV * A_HD, past),
               jnp.transpose(cache_v[0], (0, 2, 3, 1)).reshape(bs, A_KV * A_HD, past),
               jnp.swapaxes(cache_kidx[0], 1, 2))
    oa_s, ob_s, (ka_s, va_s, ki_s, st_s) = mixer(
        x_sample, mod_s, past + jnp.arange(ts), ts, past_kv, state0_s, ts, 1, ts)

    cnt0 = jnp.zeros((1, LANES), F32)
    x1_p, h2_p, ri_p, gate_p, cnt1 = _merge(oa_p, ob_p, x_prompt, mod_p, w_out_b, g_ffn[0].reshape(1, d),
                                            wr_cat, br_pad, cnt0, min(MERGE_TILE, sp))
    x1_s, h2_s, ri_s, gate_s, cnt2 = _merge(oa_s, ob_s, x_sample, mod_s, w_out_b, g_ffn[0].reshape(1, d),
                                            wr_cat, br_pad, cnt1, ts)
    counts = cnt2[0, :n_exp].astype(I32)
    padded = (counts + SLOT_BLOCK - 1) // SLOT_BLOCK * SLOT_BLOCK
    pad_end = jnp.cumsum(padded)
    pad_start = pad_end - padded
    n_asg = (bp * sp + bs * ts) * TOP_K
    nb = -(-n_asg // SLOT_BLOCK) + n_exp
    n_slots = nb * SLOT_BLOCK
    block_start = jnp.arange(nb, dtype=I32) * SLOT_BLOCK
    block_expert = jnp.minimum(jnp.sum((pad_end[None, :] <= block_start[:, None]).astype(I32), axis=1), n_exp - 1)
    n_used = (pad_end[-1:] // SLOT_BLOCK).astype(I32)
    following = pad_end.astype(I32)[block_expert] // SLOT_BLOCK
    next_expert = jnp.where(following < n_used[0], block_expert[jnp.minimum(following, nb - 1)], -1).astype(I32)
    has_pad = (padded > counts).astype(I32)
    expert_ids = jnp.arange(n_exp, dtype=I32)

    def dests(route):
        e, rank = route[:, :TOP_K, :], route[:, TOP_K:, :]
        start = jnp.sum(jnp.where(e[..., None] == expert_ids, pad_start.astype(I32), 0), axis=-1)
        return (start + rank).reshape(-1)

    dest_p, dest_s = dests(ri_p), dests(ri_s)
    pad_end_i = pad_end.astype(I32)
    tile = (SUBLANES, LANES)
    xs = _dispatch(pad_end_i, has_pad, dest_p, h2_p.reshape((bp * sp,) + tile), None, n_slots, min(ROW_DMA_TILE, sp),
                   min(MERGE_TILE, sp))
    xs = _dispatch(pad_end_i, has_pad, dest_s, h2_s.reshape((bs * ts,) + tile), xs, n_slots, min(TOKEN_TILE, bs * ts),
                   ts)
    ys = _experts(block_expert, next_expert, n_used, xs.reshape(n_slots * SUBLANES, LANES), w_gate_up[0], bgu,
                  w_down[0], bdn)
    ys = ys.reshape((n_slots,) + tile)
    y_p = _combine(dest_p, x1_p, gate_p, mod_p, modf_p, g_final.reshape(1, d), ys, min(TOKEN_TILE, sp),
                   min(MERGE_TILE, sp))
    y_s = _combine(dest_s, x1_s, gate_s, mod_s, modf_s, g_final.reshape(1, d), ys, ts, ts)

    def kv(a, b, t):
        return a.reshape(1, b, t, A_KV, A_HD)

    return (y_p, y_s,
            kv(ka_p, bp, sp), kv(va_p, bp, sp), ki_p[None], jnp.swapaxes(st_p, -1, -2)[None],
            kv(ka_s, bs, ts), kv(va_s, bs, ts), ki_s[None], jnp.swapaxes(st_s, -1, -2)[None])
```

```python
import functools

import jax
import jax.numpy as jnp
from jax import lax
from jax.experimental import pallas as pl
from jax.experimental.pallas import tpu as pltpu

F32 = jnp.float32
BF16 = jnp.bfloat16
I32 = jnp.int32
HI = lax.Precision.HIGHEST

CHUNK = 64
CHUNK_SHIFT = 6
EPS = 1e-6
ROPE_THETA = 10000.0
A_HD = 64
A_HEADS = 8
A_KV = 2
IDX_HEADS = 4
IDX_DIM = 64
INDEX_TOPK = 256
B_HEADS = 4
B_DK = 64
B_DV = 128
GATE_RANK = 16
GATE_TAU = 16.0
N_EXPERTS = 32
TOP_K = 4
SWIGLU_LIMIT = 7.0
SWIGLU_ALPHA = 1.702

LANES = 128
SUBLANES = 8
TOKEN_TILE = 256
MERGE_TILE = 512
ROW_UNROLL = 8
ROW_DMA_TILE = 512
GLA_SUB = 16
GLA_BATCH = 2
GLA_CHUNKS_PER_STEP = 8
GLA_EXP_CLAMP = 80.0
SLOT_BLOCK = 512
VMEM_LIMIT = 56 * 1024 * 1024
INT_MIN = -2147483648
NEG_BIG = -1e30
SEARCH_TWO_BIT_MAX_KEYS = 768

C_QA, C_KA, C_VA, C_QI, C_MISC, C_QG, C_KG, C_VG, C_RG, C_END = 0, 512, 640, 768, 1024, 1152, 1408, 1664, 2176, 2688
MISC_WI = 64
MISC_LR = 68


def _params(*sem):
    return pltpu.CompilerParams(dimension_semantics=sem, vmem_limit_bytes=VMEM_LIMIT)


def _nt(a, b):
    return lax.dot_general(a, b, (((1,), (1,)), ((), ())), preferred_element_type=F32)


def _tn(a, b):
    return lax.dot_general(a, b, (((0,), (0,)), ((), ())), preferred_element_type=F32)


def _rms(x):
    return x * lax.rsqrt(jnp.mean(x * x, axis=-1, keepdims=True) + EPS)


def _silu(x):
    return x / (1.0 + jnp.exp(-x))


def _slot_index_base(r0, mt):
    shift = mt.bit_length() - 1
    assert mt == 1 << shift and mt % ROW_UNROLL == 0
    return lax.shift_left(lax.shift_right_logical(r0, shift), shift + 2) + (r0 & (mt - 1))


def _store_row_tiles(ref, val):
    rows, width = val.shape
    assert width == SUBLANES * LANES
    for s in range(SUBLANES):
        ref[pl.ds(s, rows, stride=SUBLANES), :] = val[:, s * LANES:(s + 1) * LANES]


def _load_row_tiles(ref, rows):
    return jnp.concatenate([ref[pl.ds(s, rows, stride=SUBLANES), :] for s in range(SUBLANES)], axis=1)


def _value_with_ones(vb):
    return jnp.concatenate([vb, jnp.ones_like(vb)], axis=-1)


def _adaln_kernel(c_ref, w_ref, b_ref, o_ref):
    a = _silu(c_ref[...])
    o_ref[...] = jnp.dot(a, w_ref[...], preferred_element_type=F32, precision=HI) + b_ref[...]


def _adaln(c, w, b):
    r, d = c.shape
    n = w.shape[1]
    tn = 512
    return pl.pallas_call(
        _adaln_kernel,
        grid=(n // tn,),
        in_specs=[pl.BlockSpec((r, d), lambda j: (0, 0)),
                  pl.BlockSpec((d, tn), lambda j: (0, j)),
                  pl.BlockSpec((1, tn), lambda j: (0, j))],
        out_specs=pl.BlockSpec((r, tn), lambda j: (0, j)),
        out_shape=jax.ShapeDtypeStruct((r, n), F32),
        name="adaln",
        compiler_params=_params("arbitrary"),
    )(c, w, b.reshape(1, n))


def _premix_kernel(x_ref, mod_ref, g_ref, w_ref, wg_ref, bg_ref, cos_ref, sin_ref,
                   qa_ref, ka_ref, va_ref, qi_ref, ki_ref, misc_ref, qg_ref, kg_ref, vg_ref, gg_ref, rg_ref,
                   kb_ref, vx_ref, kib_ref):
    x = x_ref[0]
    mod = mod_ref[0]
    hb = (_rms(x) * g_ref[...] * (1.0 + mod[1:2]) + mod[0:1]).astype(BF16)
    tm = x.shape[0]

    def project(c0, c1):
        return jnp.dot(hb, w_ref[:, c0:c1], preferred_element_type=F32)

    cos = cos_ref[...]
    sin = sin_ref[...]
    lane = lax.broadcasted_iota(I32, (tm, LANES), 1)
    lower_half = (lane & (A_HD - 1)) < (A_HD // 2)

    def rope(xc):
        rot = jnp.where(lower_half, pltpu.roll(xc, LANES - A_HD // 2, 1), pltpu.roll(xc, A_HD // 2, 1))
        return xc * cos + rot * sin

    seg = project(C_QA, C_KA)
    for j in range((C_KA - C_QA) // LANES):
        qa_ref[0, :, j * LANES:(j + 1) * LANES] = (rope(seg[:, j * LANES:(j + 1) * LANES]) * (A_HD ** -0.5)).astype(BF16)
    seg = project(C_KA, C_QI)
    ka = rope(seg[:, :C_VA - C_KA])
    va = seg[:, C_VA - C_KA:]
    ka_ref[0] = ka
    va_ref[0] = va
    kb_ref[0] = ka.T.astype(BF16)
    vx_ref[0] = _value_with_ones(va.astype(BF16))
    seg = project(C_QI, C_QG)
    for j in range((C_MISC - C_QI) // LANES):
        qi_ref[0, :, j * LANES:(j + 1) * LANES] = rope(seg[:, j * LANES:(j + 1) * LANES]).astype(BF16)
    m = seg[:, C_MISC - C_QI:]
    mr = rope(m)
    ki_ref[0] = mr[:, :IDX_DIM]
    kib_ref[0] = mr.T[:IDX_DIM].astype(BF16)
    misc_ref[0] = jnp.where(lane < IDX_DIM, mr, m * (IDX_HEADS ** -0.5))
    xg = jnp.dot(m, wg_ref[...], preferred_element_type=F32, precision=HI) + bg_ref[...]
    gg_ref[0] = (jnp.minimum(xg, 0.0) - jnp.log(1.0 + jnp.exp(-jnp.abs(xg)))) * (1.0 / GATE_TAU)
    qg_ref[0] = project(C_QG, C_KG) * (B_DK ** -0.5)
    kg_ref[0] = project(C_KG, C_VG)
    vg_ref[0] = project(C_VG, C_RG)
    rg_ref[0] = project(C_RG, C_END)


def _premix(x, mod, g_mix, w_r, wg_pad, bg, cos_t, sin_t, tm):
    b, t, d = x.shape
    rm, fm = False, True
    widths = [(512, BF16, rm), (128, F32, rm), (128, F32, rm), (256, BF16, rm), (64, F32, rm), (128, F32, rm),
              (256, F32, rm), (256, F32, rm), (512, F32, rm), (256, F32, rm), (512, F32, rm),
              (128, BF16, fm), (2 * LANES, BF16, rm), (IDX_DIM, BF16, fm)]
    return pl.pallas_call(
        _premix_kernel,
        grid=(b, t // tm),
        in_specs=[pl.BlockSpec((1, tm, d), lambda i, j: (i, j, 0)),
                  pl.BlockSpec((1, 6, d), lambda i, j: (i, 0, 0)),
                  pl.BlockSpec((1, d), lambda i, j: (0, 0)),
                  pl.BlockSpec((d, C_END), lambda i, j: (0, 0)),
                  pl.BlockSpec((LANES, 256), lambda i, j: (0, 0)),
                  pl.BlockSpec((1, 256), lambda i, j: (0, 0)),
                  pl.BlockSpec((tm, LANES), lambda i, j: (j, 0)),
                  pl.BlockSpec((tm, LANES), lambda i, j: (j, 0))],
        out_specs=[pl.BlockSpec((1, tm, w), lambda i, j: (i, j, 0)) if not fm else
                   pl.BlockSpec((1, w, tm), lambda i, j: (i, 0, j)) for w, _, fm in widths],
        out_shape=[jax.ShapeDtypeStruct((b, w, t) if fm else (b, t, w), dt) for w, dt, fm in widths],
        name="premix",
        compiler_params=_params("arbitrary", "arbitrary"),
    )(x, mod, g_mix, w_r, wg_pad, bg, cos_t, sin_t)


def _attn_kernel(*refs, tq, n_ctx, top_k, causal, n_keys, q_off, n_groups, n_past, keep_all):
    if n_past:
        q_ref, qi_ref, misc_ref, k_ref, v_ref, ki_ref, pk_ref, pv_ref, pki_ref, o_ref = refs
        assert n_ctx - n_past == LANES and k_ref.shape[1] <= LANES

        def new_columns(rows_ref):
            new = rows_ref[0]
            n_new, width = new.shape
            if width < LANES:
                new = jnp.concatenate([new, jnp.zeros((n_new, LANES - width), F32)], axis=1)
            square = jnp.concatenate([new, jnp.zeros((LANES - n_new, LANES), F32)], axis=0)
            return square.T[:width].astype(BF16)

        k = jnp.concatenate([pk_ref[0].astype(BF16), new_columns(k_ref)], axis=1)
        v = jnp.concatenate([pv_ref[0].astype(BF16), new_columns(v_ref)], axis=1)
        vx = jnp.concatenate([v, jnp.ones_like(v)], axis=0)
        kib = jnp.concatenate([pki_ref[0].astype(BF16), new_columns(ki_ref)], axis=1)
    else:
        q_ref, qi_ref, misc_ref, k_ref, vx_ref, ki_ref, o_ref = refs
        k = k_ref[0]
        vx = vx_ref[0]
        kib = ki_ref[0]
    values_feature_major = bool(n_past)
    rows = tq // n_groups
    row0 = q_off + pl.program_id(1) * tq
    keys, kks = [], []
    idx_dots = {}
    for g in range(n_groups):
        qi = qi_ref[0, g * rows:(g + 1) * rows, :]
        for h in range(IDX_HEADS):
            qh = qi[:, h * IDX_DIM:(h + 1) * IDX_DIM]
            idx_dots[g, h] = jnp.dot(qh, kib, preferred_element_type=F32)
    for g in range(n_groups):
        misc = misc_ref[0, g * rows:(g + 1) * rows, :]
        isc = jnp.zeros((rows, n_ctx), F32)
        for h in range(IDX_HEADS):
            isc = isc + misc[:, MISC_WI + h:MISC_WI + h + 1] * jnp.maximum(idx_dots[g, h], 0.0)
        kpos = lax.broadcasted_iota(I32, (rows, n_ctx), 1)
        if causal:
            row = lax.broadcasted_iota(I32, (rows, 1), 0) + (row0 + g * rows)
            key_lim = (lax.shift_right_logical(row, CHUNK_SHIFT) + 1) * CHUNK
        else:
            key_lim = jnp.full((rows, 1), n_keys, I32)
        bits = pltpu.bitcast(isc, I32)
        key = jnp.where(bits < 0, INT_MIN - bits, bits)
        keys.append(jnp.where(kpos < key_lim, key, INT_MIN))
        kks.append(jnp.minimum(key_lim, top_k).astype(F32))

    def count_at_least(g, cand):
        return jnp.sum(jnp.where(keys[g] >= (cand ^ INT_MIN), 1.0, 0.0), axis=1, keepdims=True)

    bits_per_pass = 2 if n_ctx <= SEARCH_TWO_BIT_MAX_KEYS else 1

    def search(i, ans):
        out = []
        for g in range(n_groups):
            if bits_per_pass == 1:
                cand = ans[g] | lax.shift_left(jnp.int32(1), 31 - i)
                out.append(jnp.where(count_at_least(g, cand) >= kks[g], cand, ans[g]))
            else:
                a1 = ans[g] | lax.shift_left(jnp.int32(1), 31 - 2 * i)
                a2 = ans[g] | lax.shift_left(jnp.int32(1), 30 - 2 * i)
                a3 = a1 | a2
                c1, c2, c3 = count_at_least(g, a1), count_at_least(g, a2), count_at_least(g, a3)
                kk = kks[g]
                out.append(jnp.where(c3 >= kk, a3, jnp.where(c1 >= kk, a1, jnp.where(c2 >= kk, a2, ans[g]))))
        return tuple(out)

    ans = tuple(jnp.zeros((rows, 1), I32) for _ in range(n_groups))
    if not keep_all:
        ans = lax.fori_loop(0, 32 // bits_per_pass, search, ans)
    r = lax.broadcasted_iota(I32, (LANES, LANES), 0)
    c = lax.broadcasted_iota(I32, (LANES, LANES), 1)
    upper = jnp.where(r < c, 1.0, 0.0).astype(BF16)
    n_blk = n_ctx // LANES
    thrs = [ans[g] ^ INT_MIN for g in range(n_groups)]
    ties = {(g, j): jnp.where(keys[g][:, j * LANES:(j + 1) * LANES] == thrs[g], 1.0, 0.0)
            for g in range(n_groups) for j in range(n_blk)}
    within = {gj: jnp.dot(t.astype(BF16), upper, preferred_element_type=F32) for gj, t in ties.items()}
    bias_groups = []
    for g in range(n_groups):
        gt = keys[g] > thrs[g]
        need = kks[g] - jnp.sum(jnp.where(gt, 1.0, 0.0), axis=1, keepdims=True)
        carry = jnp.zeros((rows, 1), F32)
        blocks = []
        for j in range(n_blk):
            sl = slice(j * LANES, (j + 1) * LANES)
            take_tie = jnp.where(within[g, j] + carry < need, ties[g, j], 0.0)
            blocks.append(jnp.where(gt[:, sl], 0.0, jnp.where(take_tie > 0.0, 0.0, NEG_BIG)))
            carry = carry + jnp.sum(ties[g, j], axis=1, keepdims=True)
        bias_groups.append(jnp.concatenate(blocks, axis=1))
    bias = jnp.concatenate(bias_groups, axis=0)

    q = q_ref[0]
    rep = A_HEADS // A_KV
    if values_feature_major and tq * A_HEADS <= 2 * LANES:
        q_groups = [jnp.concatenate([q[:, hh * A_HD:(hh + 1) * A_HD] for hh in range(g * rep, (g + 1) * rep)], axis=0)
                    for g in range(A_KV)]
        s_all = jnp.concatenate([jnp.dot(q_groups[g], k[g * A_HD:(g + 1) * A_HD, :], preferred_element_type=F32)
                                 for g in range(A_KV)], axis=0) + jnp.concatenate([bias] * A_HEADS, axis=0)
        p_all = jnp.exp((s_all - jnp.max(s_all, axis=1, keepdims=True)).astype(BF16))
        ox = _nt(p_all, vx)
        outs = [ox[hh * tq:(hh + 1) * tq, (hh // rep) * A_HD:(hh // rep + 1) * A_HD]
                / ox[hh * tq:(hh + 1) * tq, LANES:LANES + 1] for hh in range(A_HEADS)]
        o_ref[0] = jnp.concatenate(outs, axis=1).astype(BF16)
        return
    kgs = [k[g * A_HD:(g + 1) * A_HD, :] for g in range(A_KV)]
    qs = [q[:, hh * A_HD:(hh + 1) * A_HD] for hh in range(A_HEADS)]
    ss = [jnp.dot(qh, kgs[hh // rep], preferred_element_type=F32) + bias for hh, qh in enumerate(qs)]
    ps = [jnp.exp((s - jnp.max(s, axis=1, keepdims=True)).astype(BF16)) for s in ss]
    oxs = [_nt(p, vx) if values_feature_major else jnp.dot(p, vx, preferred_element_type=F32) for p in ps]
    outs = [ox[:, (hh // rep) * A_HD:(hh // rep + 1) * A_HD] / ox[:, LANES:LANES + 1] for hh, ox in enumerate(oxs)]
    o_ref[0] = jnp.concatenate(outs, axis=1).astype(BF16)


def _attn_call(q, qi, misc, kb, vx, kib, past, *, tq, q_off, n_tiles, n_ctx, causal, n_keys, n_groups):
    b = q.shape[0]
    t0 = q_off // tq
    top_k = min(INDEX_TOPK, n_keys // 4)
    n_past = 0 if past is None else past[0].shape[2]
    n_own = n_ctx if past is None else kb.shape[1]
    kern = functools.partial(_attn_kernel, tq=tq, n_ctx=n_ctx, top_k=top_k, causal=causal, n_keys=n_keys,
                             q_off=q_off, n_groups=n_groups, n_past=n_past,
                             keep_all=causal and q_off + n_tiles * tq <= top_k)

    def tok(w):
        return pl.BlockSpec((1, tq, w), lambda i, j: (i, t0 + j, 0))

    def ctx(rows, w):
        return pl.BlockSpec((1, rows, w), lambda i, j: (i, 0, 0))

    if past is None:
        own_specs = [ctx(kb.shape[1], n_ctx), ctx(n_ctx, vx.shape[2]), ctx(kib.shape[1], n_ctx)]
    else:
        own_specs = [ctx(n_own, a.shape[2]) for a in (kb, vx, kib)]
    in_specs = [tok(512), tok(256), tok(LANES)] + own_specs
    args = [q, qi, misc, kb, vx, kib]
    if past is not None:
        in_specs += [ctx(a.shape[1], n_past) for a in past]
        args += list(past)
    return pl.pallas_call(
        kern,
        grid=(b, n_tiles),
        in_specs=in_specs,
        out_specs=pl.BlockSpec((1, tq, 512), lambda i, j: (i, j, 0)),
        out_shape=jax.ShapeDtypeStruct((b, n_tiles * tq, 512), BF16),
        name="attn",
        compiler_params=_params("arbitrary", "arbitrary"),
    )(*args)


def _attn_causal(q, qi, misc, kb, vx, kib, tq):
    t = q.shape[1]
    return jnp.stack([
        _attn_call(q, qi, misc, kb, vx, kib, None, tq=tq, q_off=c * tq, n_tiles=1, n_ctx=(c + 1) * tq,
                   causal=True, n_keys=t, n_groups=2)
        for c in range(t // tq)])


def _gla_kernel(q_ref, k_ref, v_ref, g_ref, rg_ref, s0_ref, go_ref, ob_ref, st_ref, st_scr, *, cs, nc, bb):
    j = pl.program_id(1)
    hk, hv, hc = B_HEADS * B_DK, B_HEADS * B_DV, B_HEADS * cs

    def head_of(idx, width):
        return lax.shift_right_logical(idx, width.bit_length() - 1)

    def same_head(rows, row_w, cols, col_w):
        r = head_of(lax.broadcasted_iota(I32, (rows, cols), 0), row_w)
        c = head_of(lax.broadcasted_iota(I32, (rows, cols), 1), col_w)
        return r == c

    keep_kb = same_head(hc, cs, hk, B_DK)
    keep_vb = same_head(hc, cs, hv, B_DV)
    keep_st = same_head(hv, B_DV, hk, B_DK)
    t_idx = lax.broadcasted_iota(I32, (cs, hc), 0)
    s_idx = lax.broadcasted_iota(I32, (cs, hc), 1) & (cs - 1)
    keep_a = t_idx >= s_idx
    r = lax.broadcasted_iota(I32, (cs, cs), 0)
    c = lax.broadcasted_iota(I32, (cs, cs), 1)
    tri = jnp.where(r >= c, 1.0, 0.0)
    go = go_ref[...]

    @pl.when(j == 0)
    def _():
        for bi in range(bb):
            blocks = []
            for h in range(B_HEADS):
                parts = [jnp.zeros((B_DV, B_DK), F32)] * B_HEADS
                parts[h] = s0_ref[bi, h].T
                blocks.append(jnp.concatenate(parts, axis=1))
            st_scr[bi] = jnp.concatenate(blocks, axis=0)

    inst = [(ci, bi) for ci in range(nc) for bi in range(bb)]
    sls = {ci: slice(ci * cs, (ci + 1) * cs) for ci in range(nc)}
    bcum = {t: jnp.dot(tri, g_ref[t[1], sls[t[0]], :], preferred_element_type=F32, precision=HI) for t in inst}
    n_sub = cs // GLA_SUB
    v_bf, qdec, kdec, qts, kts = {}, {}, {}, {}, {}
    for t in inst:
        ci, bi = t
        q = q_ref[bi, sls[ci], :]
        k = k_ref[bi, sls[ci], :]
        bc = bcum[t]
        blast = bc[cs - 1:cs, :]
        qdec[t] = (q * jnp.exp(bc)).astype(BF16)
        kdec[t] = (k * jnp.exp(blast - bc)).astype(BF16)
        v_bf[t] = v_ref[bi, sls[ci], :].astype(BF16)
        for i in range(n_sub):
            rs = slice(i * GLA_SUB, (i + 1) * GLA_SUB)
            ref = bc[i * GLA_SUB:i * GLA_SUB + 1, :]
            qts[t, i] = (q[rs, :] * jnp.exp(bc[rs, :] - ref)).astype(BF16)
            kt = (k * jnp.exp(jnp.minimum(ref - bc, GLA_EXP_CLAMP))).astype(BF16)
            kts[t, i] = jnp.where(keep_kb, jnp.concatenate([kt] * B_HEADS, axis=0), 0.0)
    a_rows = {(t, i): _nt(qts[t, i], kts[t, i]) for t in inst for i in range(n_sub)}
    a_mat = {t: jnp.where(keep_a, jnp.concatenate([a_rows[t, i] for i in range(n_sub)], axis=0), 0.0).astype(BF16)
             for t in inst}
    o_intra = {t: jnp.dot(a_mat[t], jnp.where(keep_vb, jnp.concatenate([v_bf[t]] * B_HEADS, axis=0), 0.0),
                          preferred_element_type=F32) for t in inst}
    kv = {t: jnp.where(keep_st, _tn(v_bf[t], kdec[t]), 0.0) for t in inst}
    for t in inst:
        ci, bi = t
        st = st_scr[bi]
        o = o_intra[t] + _nt(qdec[t], st.astype(BF16))
        st_scr[bi] = st * jnp.exp(bcum[t][cs - 1:cs, :]) + kv[t]
        rg = rg_ref[bi, sls[ci], :]
        for h in range(B_HEADS):
            vs = slice(h * B_DV, (h + 1) * B_DV)
            ob_ref[bi, sls[ci], vs] = (_rms(o[:, vs]) * go * _silu(rg[:, vs])).astype(BF16)

    @pl.when(j == pl.num_programs(1) - 1)
    def _():
        for bi in range(bb):
            st = st_scr[bi]
            for h in range(B_HEADS):
                st_ref[bi, h] = st[h * B_DV:(h + 1) * B_DV, h * B_DK:(h + 1) * B_DK].T


def _gla(qg, kg, vg, gg, rg, state_t, g_out, *, cs, nc, bb):
    b, t, _ = qg.shape
    assert b % bb == 0
    tt = cs * nc
    kern = functools.partial(_gla_kernel, cs=cs, nc=nc, bb=bb)

    def tok(w):
        return pl.BlockSpec((bb, tt, w), lambda i, j: (i, j, 0))

    st_spec = pl.BlockSpec((bb, B_HEADS, B_DK, B_DV), lambda i, j: (i, 0, 0, 0))
    return pl.pallas_call(
        kern,
        grid=(b // bb, t // tt),
        in_specs=[tok(256), tok(256), tok(512), tok(256), tok(512), st_spec,
                  pl.BlockSpec((1, B_DV), lambda i, j: (0, 0))],
        out_specs=[tok(512), st_spec],
        out_shape=[jax.ShapeDtypeStruct((b, t, 512), BF16),
                   jax.ShapeDtypeStruct((b, B_HEADS, B_DK, B_DV), F32)],
        scratch_shapes=[pltpu.VMEM((bb, B_HEADS * B_DV, B_HEADS * B_DK), F32)],
        name="gla",
        compiler_params=_params("arbitrary", "arbitrary"),
    )(qg, kg, vg, gg, rg, state_t, g_out)


def _merge_kernel(oa_ref, ob_ref, x_ref, mod_ref, wo_ref, gf_ref, wr_ref, br_ref, cnt0_ref,
                  x1_ref, h2_ref, ri_ref, rgate_ref, cnt_ref, carry_scr):
    @pl.when((pl.program_id(0) == 0) & (pl.program_id(1) == 0))
    def _():
        carry_scr[...] = cnt0_ref[...]

    mod = mod_ref[0]
    oa = oa_ref[:, 0].reshape(-1, oa_ref.shape[-1])
    cat = jnp.concatenate([oa, ob_ref[0]], axis=1)
    x1 = x_ref[0] + mod[2:3] * jnp.dot(cat, wo_ref[...], preferred_element_type=F32)
    x1_ref[0] = x1
    h2 = _rms(x1) * gf_ref[...] * (1.0 + mod[4:5]) + mod[3:4]
    _store_row_tiles(h2_ref.at[0], h2)
    tm = x1.shape[0]
    lane = lax.broadcasted_iota(I32, (tm, LANES), 1).astype(F32)
    hi = h2.astype(BF16)
    lo = (h2 - hi.astype(F32)).astype(BF16)
    wr = wr_ref[...]
    a = jnp.dot(hi, wr, preferred_element_type=F32)
    left = (a[:, :LANES] + a[:, LANES:]) + jnp.dot(lo, wr[:, :LANES], preferred_element_type=F32) + br_ref[...]
    idx, val = [], []
    for _ in range(TOP_K):
        m = jnp.max(left, axis=1, keepdims=True)
        e = jnp.argmax(left, axis=1, keepdims=True).astype(F32)
        idx.append(e)
        val.append(m)
        left = jnp.where(lane == e, -jnp.inf, left)
    ex = [jnp.exp(vv - val[0]) for vv in val]
    den = ex[0] + ex[1] + ex[2] + ex[3]
    onehot = jnp.zeros((tm, LANES), F32)
    for e in idx:
        onehot = onehot + jnp.where(lane == e, 1.0, 0.0)
    r = lax.broadcasted_iota(I32, (tm, tm), 0)
    c = lax.broadcasted_iota(I32, (tm, tm), 1)
    earlier = jnp.where(r > c, 1.0, 0.0).astype(BF16)
    before = jnp.dot(earlier, onehot.astype(BF16), preferred_element_type=F32) + carry_scr[...]
    ri = jnp.zeros((tm, LANES), F32)
    rgate = jnp.zeros((tm, LANES), F32)
    for kq in range(TOP_K):
        rank = jnp.sum(jnp.where(lane == idx[kq], before, 0.0), axis=1, keepdims=True)
        ri = jnp.where(lane == kq, idx[kq], ri)
        ri = jnp.where(lane == TOP_K + kq, rank, ri)
        rgate = jnp.where(lane == kq, ex[kq] / den, rgate)
    square = ri if tm % LANES == 0 else jnp.concatenate([ri, jnp.zeros((LANES - tm, LANES), F32)], axis=0)
    ri_ref[0] = square.T[:2 * TOP_K, :tm].astype(I32)
    rgate_ref[0] = rgate
    carry_scr[...] = carry_scr[...] + jnp.sum(onehot, axis=0, keepdims=True)
    cnt_ref[...] = carry_scr[...]


def _merge(oa, ob, x, mod, w_out, g_ffn, wr_pad, br_pad, cnt0, tm):
    b, t, d = x.shape

    def tok(w):
        return pl.BlockSpec((1, tm, w), lambda i, j: (i, j, 0))

    def const(s):
        return pl.BlockSpec(s, lambda i, j: (0, 0))

    return pl.pallas_call(
        _merge_kernel,
        grid=(b, t // tm),
        in_specs=[pl.BlockSpec((tm // oa.shape[2], 1, oa.shape[2], 512), lambda i, j: (j, i, 0, 0)),
                  tok(512), tok(d), pl.BlockSpec((1, 6, d), lambda i, j: (i, 0, 0)),
                  const((d, d)), const((1, d)), const((d, 2 * LANES)), const((1, LANES)), const((1, LANES))],
        out_specs=[tok(d), pl.BlockSpec((1, tm * SUBLANES, LANES), lambda i, j: (i, j, 0)),
                   pl.BlockSpec((1, 2 * TOP_K, tm), lambda i, j: (i * (t // tm) + j, 0, 0)), tok(LANES),
                   const((1, LANES))],
        out_shape=[jax.ShapeDtypeStruct((b, t, d), F32), jax.ShapeDtypeStruct((b, t * SUBLANES, LANES), F32),
                   jax.ShapeDtypeStruct((b * (t // tm), 2 * TOP_K, tm), I32), jax.ShapeDtypeStruct((b, t, LANES), F32),
                   jax.ShapeDtypeStruct((1, LANES), F32)],
        scratch_shapes=[pltpu.VMEM((1, LANES), F32)],
        name="merge",
        compiler_params=_params("arbitrary", "arbitrary"),
    )(oa, ob, x, mod, w_out, g_ffn, wr_pad, br_pad, cnt0)


def _dispatch_kernel(pe_ref, hp_ref, dest_ref, h_hbm, *rest, tm, mt, zero_init):
    if zero_init:
        xs_ref, ring, zbuf, in_sem, sem, zsem = rest

        @pl.when(pl.program_id(0) == 0)
        def _():
            zbuf[...] = jnp.zeros_like(zbuf)

            def zero_copy(e):
                start = pl.multiple_of(pe_ref[e] - SLOT_BLOCK, SLOT_BLOCK)
                return pltpu.make_async_copy(zbuf, xs_ref.at[pl.ds(start, SLOT_BLOCK)], zsem)

            for e in range(N_EXPERTS):
                @pl.when(hp_ref[e] > 0)
                def _():
                    zero_copy(e).start()
            for e in range(N_EXPERTS):
                @pl.when(hp_ref[e] > 0)
                def _():
                    zero_copy(e).wait()
    else:
        _, xs_ref, ring, in_sem, sem = rest

    step = pl.program_id(0)
    n_steps = pl.num_programs(0)
    slot_in = lax.rem(step, 3)
    cur = lax.rem(step, 2)

    def load(tile_idx, ring_slot):
        rows = pl.ds(pl.multiple_of(tile_idx * tm, tm), tm)
        return pltpu.make_async_copy(h_hbm.at[rows], ring.at[ring_slot], in_sem.at[ring_slot])

    @pl.when(step == 0)
    def _():
        load(0, 0).start()

    @pl.when(step + 1 < n_steps)
    def _():
        load(step + 1, lax.rem(step + 1, 3)).start()

    load(step, slot_in).wait()

    def row_copy(r, slot, which):
        return pltpu.make_async_copy(ring.at[slot_in, r], xs_ref.at[slot], sem.at[which])

    def issue(i, carry):
        r0 = i * ROW_UNROLL
        base = _slot_index_base(r0, mt)
        for j in range(ROW_UNROLL):
            for kq in range(TOP_K):
                row_copy(r0 + j, dest_ref[base + kq * mt + j], cur).start(priority=kq % 2)
        return carry

    lax.fori_loop(0, tm // ROW_UNROLL, issue, 0)

    def drain(which):
        def body(i, carry):
            for _ in range(ROW_UNROLL * TOP_K):
                row_copy(0, 0, which).wait()
            return carry

        lax.fori_loop(0, tm // ROW_UNROLL, body, 0)

    @pl.when(step > 0)
    def _():
        drain(1 - cur)

    @pl.when(step == n_steps - 1)
    def _():
        drain(cur)


def _dispatch(pad_end, has_pad, dest_flat, h2_tiles, xs, n_slots, tm, mt):
    n = h2_tiles.shape[0]
    tile = h2_tiles.shape[1:]
    zero_init = xs is None
    assert tm % mt == 0
    kern = functools.partial(_dispatch_kernel, tm=tm, mt=mt, zero_init=zero_init)
    in_specs = [pl.BlockSpec((tm * TOP_K,), lambda i, pe, hp: (i,), memory_space=pltpu.SMEM),
                pl.BlockSpec(memory_space=pl.ANY)]
    args = [pad_end, has_pad, dest_flat, h2_tiles]
    ring = pltpu.VMEM((3, tm) + tile, F32)
    scratch = [ring, pltpu.SemaphoreType.DMA((3,)), pltpu.SemaphoreType.DMA((2,))]
    aliases = {}
    if zero_init:
        scratch = [ring, pltpu.VMEM((SLOT_BLOCK,) + tile, F32), pltpu.SemaphoreType.DMA((3,)),
                   pltpu.SemaphoreType.DMA((2,)), pltpu.SemaphoreType.DMA(())]
    else:
        in_specs.append(pl.BlockSpec(memory_space=pl.ANY))
        args.append(xs)
        aliases = {4: 0}
    grid_spec = pltpu.PrefetchScalarGridSpec(
        num_scalar_prefetch=2,
        grid=(n // tm,),
        in_specs=in_specs,
        out_specs=pl.BlockSpec(memory_space=pl.ANY),
        scratch_shapes=scratch,
    )
    return pl.pallas_call(
        kern,
        grid_spec=grid_spec,
        out_shape=jax.ShapeDtypeStruct((n_slots,) + tile, F32),
        input_output_aliases=aliases,
        name="dispatch",
        compiler_params=_params("arbitrary"),
    )(*args)


def _expert_kernel(be_ref, nx_ref, nu_ref, xs_ref, wgu_hbm, bgu_ref, wdn_hbm, bdn_ref, ys_ref,
                   gu_f32, dn_f32, wgu_b, wdn_b, sem):
    j = pl.program_id(0)

    def fetch(e):
        return (pltpu.make_async_copy(wgu_hbm.at[e], gu_f32, sem.at[0]),
                pltpu.make_async_copy(wdn_hbm.at[e], dn_f32, sem.at[1]))

    @pl.when(j < nu_ref[0])
    def _():
        @pl.when(j == 0)
        def _():
            for cp in fetch(be_ref[0]):
                cp.start()

        @pl.when((j == 0) | (be_ref[j] != be_ref[jnp.maximum(j - 1, 0)]))
        def _():
            for cp in fetch(be_ref[j]):
                cp.wait()
            wgu_b[...] = gu_f32[...].astype(BF16)
            wdn_b[...] = dn_f32[...].astype(BF16)

            @pl.when(nx_ref[j] >= 0)
            def _():
                for cp in fetch(nx_ref[j]):
                    cp.start()

        f = wdn_b.shape[0]
        x = _load_row_tiles(xs_ref, SLOT_BLOCK).astype(BF16)
        gu = jnp.dot(x, wgu_b[...], preferred_element_type=F32) + bgu_ref[0]
        gate = jnp.minimum(gu[:, :f], SWIGLU_LIMIT)
        up = jnp.clip(gu[:, f:], -SWIGLU_LIMIT, SWIGLU_LIMIT)
        glu = gate / (1.0 + jnp.exp(-SWIGLU_ALPHA * gate))
        act = ((up + 1.0) * glu).astype(BF16)
        _store_row_tiles(ys_ref, jnp.dot(act, wdn_b[...], preferred_element_type=F32) + bdn_ref[0])


def _experts(block_expert, next_expert, n_used, xs, wgu, bgu, wdn, bdn):
    n_slots = xs.shape[0] // SUBLANES
    nb = n_slots // SLOT_BLOCK
    f, d = wdn.shape[1:]
    rows = SLOT_BLOCK * SUBLANES

    def blk(i, be, nx, nu):
        return (jnp.minimum(i, nu[0] - 1), 0)

    def bsel(i, be, nx, nu):
        return (be[i], 0, 0)

    grid_spec = pltpu.PrefetchScalarGridSpec(
        num_scalar_prefetch=3,
        grid=(nb,),
        in_specs=[pl.BlockSpec((rows, LANES), blk),
                  pl.BlockSpec(memory_space=pl.ANY),
                  pl.BlockSpec((1, 1, 2 * f), bsel),
                  pl.BlockSpec(memory_space=pl.ANY),
                  pl.BlockSpec((1, 1, d), bsel)],
        out_specs=pl.BlockSpec((rows, LANES), blk),
        scratch_shapes=[pltpu.VMEM((d, 2 * f), F32), pltpu.VMEM((f, d), F32),
                        pltpu.VMEM((d, 2 * f), BF16), pltpu.VMEM((f, d), BF16), pltpu.SemaphoreType.DMA((2,))],
    )
    return pl.pallas_call(
        _expert_kernel,
        grid_spec=grid_spec,
        out_shape=jax.ShapeDtypeStruct(xs.shape, F32),
        name="experts",
        compiler_params=_params("arbitrary"),
    )(block_expert, next_expert, n_used, xs, wgu, bgu, wdn, bdn)


def _combine_kernel(dest_ref, dnext_ref, x1_ref, gate_ref, mod_ref, modf_ref, gfin_ref, ys_ref, y_ref,
                    buf, sem, *, tm, mt):
    nt = pl.num_programs(1)
    step = pl.program_id(0) * nt + pl.program_id(1)
    n_steps = pl.num_programs(0) * nt
    half = TOP_K * tm * SUBLANES
    cur = lax.rem(step, 2)
    per_tile = mt // tm

    def row_copy(which, kq, r, slot):
        off = pl.multiple_of(which * half + (kq * tm + r) * SUBLANES, SUBLANES)
        return pltpu.make_async_copy(ys_ref.at[slot], buf.at[pl.ds(off, SUBLANES), :], sem.at[which])

    def gather(slots_ref, which, at_step):
        first_row = lax.rem(at_step, per_tile) * tm

        def issue(i, carry):
            r0 = i * ROW_UNROLL
            for j in range(ROW_UNROLL):
                for kq in range(TOP_K):
                    row_copy(which, kq, r0 + j, slots_ref[first_row + r0 + kq * mt + j]).start(priority=kq % 2)
            return carry

        lax.fori_loop(0, tm // ROW_UNROLL, issue, 0)

    @pl.when(step == 0)
    def _():
        gather(dest_ref, cur, step)

    @pl.when(step + 1 < n_steps)
    def _():
        gather(dnext_ref, 1 - cur, step + 1)

    def drain(i, carry):
        for _ in range(ROW_UNROLL * TOP_K):
            row_copy(cur, 0, 0, 0).wait()
        return carry

    lax.fori_loop(0, tm // ROW_UNROLL, drain, 0)
    gates = gate_ref[0]
    moe = None
    for kq in range(TOP_K):
        start = pl.multiple_of(cur * half + kq * tm * SUBLANES, SUBLANES)
        rows = _load_row_tiles(buf.at[pl.ds(start, tm * SUBLANES), :], tm)
        term = gates[:, kq:kq + 1] * rows
        moe = term if moe is None else moe + term
    xo = x1_ref[0] + mod_ref[0][5:6] * moe
    modf = modf_ref[0]
    y_ref[0] = _rms(xo) * gfin_ref[...] * (1.0 + modf[1:2]) + modf[0:1]


def _combine(dest_flat, x1, gates, mod, modf, g_final, ys, tm, mt):
    b, t, d = x1.shape
    nt = t // tm
    last = b * nt - 1
    assert mt % tm == 0
    per_tile = mt // tm
    kern = functools.partial(_combine_kernel, tm=tm, mt=mt)
    return pl.pallas_call(
        kern,
        grid=(b, nt),
        in_specs=[pl.BlockSpec((mt * TOP_K,), lambda i, j: ((i * nt + j) // per_tile,), memory_space=pltpu.SMEM),
                  pl.BlockSpec((mt * TOP_K,), lambda i, j: (jnp.minimum(i * nt + j + 1, last) // per_tile,),
                               memory_space=pltpu.SMEM),
                  pl.BlockSpec((1, tm, d), lambda i, j: (i, j, 0)),
                  pl.BlockSpec((1, tm, LANES), lambda i, j: (i, j, 0)),
                  pl.BlockSpec((1, 6, d), lambda i, j: (i, 0, 0)),
                  pl.BlockSpec((1, 2, d), lambda i, j: (i, 0, 0)),
                  pl.BlockSpec((1, d), lambda i, j: (0, 0)),
                  pl.BlockSpec(memory_space=pl.ANY)],
        out_specs=pl.BlockSpec((1, tm, d), lambda i, j: (i, j, 0)),
        out_shape=jax.ShapeDtypeStruct((b, t, d), F32),
        scratch_shapes=[pltpu.VMEM((2 * TOP_K * tm * SUBLANES, LANES), F32), pltpu.SemaphoreType.DMA((2,))],
        name="combine",
        compiler_params=_params("arbitrary", "arbitrary"),
    )(dest_flat, dest_flat, x1, gates, mod, modf, g_final, ys)


def _rope_tables(pos):
    half = A_HD // 2
    inv_freq = ROPE_THETA ** (-jnp.arange(half, dtype=F32) / half)
    ang = pos.astype(F32)[:, None] * inv_freq[None, :]
    cos = jnp.cos(ang)
    sin = jnp.sin(ang)
    reps = LANES // A_HD
    return jnp.tile(jnp.concatenate([cos, cos], axis=1), (1, reps)), jnp.tile(jnp.concatenate([-sin, sin], axis=1), (1, reps))


def _reorder_w_in(w_in):
    offs = [0]
    for w in (512, 128, 128, 256, 64, 4, 256, 256, 512, 16, 512):
        offs.append(offs[-1] + w)
    seg = [w_in[:, offs[i]:offs[i + 1]] for i in range(11)]
    qa, ka, va, qi, ki, wi, qg, kg, vg, lr, rg = seg
    pad = jnp.zeros((w_in.shape[0], LANES - IDX_DIM - IDX_HEADS - GATE_RANK), w_in.dtype)
    return jnp.concatenate([qa, ka, va, qi, ki, wi, lr, pad, qg, kg, vg, rg], axis=1).astype(BF16)


def kernel(x_prompt, x_sample, cache_k, cache_v, cache_kidx, state_gla, c_prompt, c_sample,
           w_mod, b_mod, g_mix, g_ffn, w_in, gla_w_gate, gla_b_gate, gla_g_out, w_out,
           w_router, b_router, w_gate_up, b_gate_up, w_down, b_down,
           w_mod_final, b_mod_final, g_final):
    depth = w_in.shape[0]
    assert depth == 1
    bp, sp, d = x_prompt.shape
    bs, ts, _ = x_sample.shape
    past = cache_k.shape[2]
    n_exp = w_router.shape[2]
    f = w_down.shape[2]

    c_all = jnp.concatenate([c_prompt, c_sample], axis=0)
    w_r = _reorder_w_in(w_in[0])
    wg_pad = jnp.zeros((LANES, B_HEADS * B_DK), F32).at[MISC_LR:MISC_LR + GATE_RANK].set(gla_w_gate[0])
    bg = gla_b_gate[0].reshape(1, -1)
    wr_pad = jnp.zeros((d, LANES), F32).at[:, :n_exp].set(w_router[0])
    wr_hi = wr_pad.astype(BF16)
    wr_cat = jnp.concatenate([wr_hi, (wr_pad - wr_hi.astype(F32)).astype(BF16)], axis=1)
    br_pad = jnp.full((1, LANES), NEG_BIG, F32).at[0, :n_exp].set(b_router[0])
    w_out_b = w_out[0].astype(BF16)
    bgu = b_gate_up[0].reshape(n_exp, 1, 2 * f)
    bdn = b_down[0].reshape(n_exp, 1, d)
    g_out = gla_g_out[0].reshape(1, B_DV)

    mod_all = _adaln(c_all, w_mod[0], b_mod[0]).reshape(bp + bs, 6, d)
    modf_all = _adaln(c_all, w_mod_final, b_mod_final).reshape(bp + bs, 2, d)
    mod_p, mod_s = mod_all[:bp], mod_all[bp:]
    modf_p, modf_s = modf_all[:bp], modf_all[bp:]

    def mixer(x, mod, pos, tm, past_kv, state_t, cs, nc, tq):
        cos_t, sin_t = _rope_tables(pos)
        qa, ka, va, qi, ki, misc, qg, kg, vg, gg, rg, kb, vx, kib = _premix(
            x, mod, g_mix[0].reshape(1, d), w_r, wg_pad, bg, cos_t, sin_t, tm)
        if past_kv is None:
            oa = _attn_causal(qa, qi, misc, kb, vx, kib, tq)
        else:
            n_past = past_kv[0].shape[2]
            assert tq == x.shape[1] and n_past % LANES == 0
            oa = _attn_call(qa, qi, misc, ka, va, ki, past_kv, tq=tq, q_off=0, n_tiles=1,
                            n_ctx=n_past + LANES, causal=False, n_keys=n_past + ka.shape[1], n_groups=1)[None]
        ob, st = _gla(qg, kg, vg, gg, rg, state_t, g_out, cs=cs, nc=nc, bb=GLA_BATCH)
        return oa, ob, (ka, va, ki, st)

    state0_p = jnp.zeros((bp, B_HEADS, B_DK, B_DV), F32)
    oa_p, ob_p, (ka_p, va_p, ki_p, st_p) = mixer(
        x_prompt, mod_p, jnp.arange(sp), min(TOKEN_TILE, sp), None, state0_p, CHUNK,
        min(GLA_CHUNKS_PER_STEP, sp // CHUNK), min(TOKEN_TILE, sp))
    past_kv = (jnp.transpose(cache_k[0], (0, 2, 3, 1)).reshape(bs, A_KV * A_HD, past),
               jnp.transpose(cache_v[0], (0, 2, 3, 1)).reshape(bs, A_KV * A_HD, past),
               jnp.swapaxes(cache_kidx[0], 1, 2))
    oa_s, ob_s, (ka_s, va_s, ki_s, st_s) = mixer(
        x_sample, mod_s, past + jnp.arange(ts), ts, past_kv, state_gla[0], ts, 1, ts)

    cnt0 = jnp.zeros((1, LANES), F32)
    x1_p, h2_p, ri_p, gate_p, cnt1 = _merge(oa_p, ob_p, x_prompt, mod_p, w_out_b, g_ffn[0].reshape(1, d),
                                            wr_cat, br_pad, cnt0, min(MERGE_TILE, sp))
    x1_s, h2_s, ri_s, gate_s, cnt2 = _merge(oa_s, ob_s, x_sample, mod_s, w_out_b, g_ffn[0].reshape(1, d),
                                            wr_cat, br_pad, cnt1, ts)
    counts = cnt2[0, :n_exp].astype(I32)
    padded = (counts + SLOT_BLOCK - 1) // SLOT_BLOCK * SLOT_BLOCK
    pad_end = jnp.cumsum(padded)
    pad_start = pad_end - padded
    n_asg = (bp * sp + bs * ts) * TOP_K
    nb = -(-n_asg // SLOT_BLOCK) + n_exp
    n_slots = nb * SLOT_BLOCK
    block_start = jnp.arange(nb, dtype=I32) * SLOT_BLOCK
    block_expert = jnp.minimum(jnp.sum((pad_end[None, :] <= block_start[:, None]).astype(I32), axis=1), n_exp - 1)
    n_used = (pad_end[-1:] // SLOT_BLOCK).astype(I32)
    following = pad_end.astype(I32)[block_expert] // SLOT_BLOCK
    next_expert = jnp.where(following < n_used[0], block_expert[jnp.minimum(following, nb - 1)], -1).astype(I32)
    has_pad = (padded > counts).astype(I32)
    expert_ids = jnp.arange(n_exp, dtype=I32)

    def dests(route):
        e, rank = route[:, :TOP_K, :], route[:, TOP_K:, :]
        start = jnp.sum(jnp.where(e[..., None] == expert_ids, pad_start.astype(I32), 0), axis=-1)
        return (start + rank).reshape(-1)

    dest_p, dest_s = dests(ri_p), dests(ri_s)
    pad_end_i = pad_end.astype(I32)
    tile = (SUBLANES, LANES)
    xs = _dispatch(pad_end_i, has_pad, dest_p, h2_p.reshape((bp * sp,) + tile), None, n_slots, min(ROW_DMA_TILE, sp),
                   min(MERGE_TILE, sp))
    xs = _dispatch(pad_end_i, has_pad, dest_s, h2_s.reshape((bs * ts,) + tile), xs, n_slots, min(TOKEN_TILE, bs * ts),
                   ts)
    ys = _experts(block_expert, next_expert, n_used, xs.reshape(n_slots * SUBLANES, LANES), w_gate_up[0], bgu,
                  w_down[0], bdn)
    ys = ys.reshape((n_slots,) + tile)
    y_p = _combine(dest_p, x1_p, gate_p, mod_p, modf_p, g_final.reshape(1, d), ys, min(TOKEN_TILE, sp),
                   min(MERGE_TILE, sp))
    y_s = _combine(dest_s, x1_s, gate_s, mod_s, modf_s, g_final.reshape(1, d), ys, ts, ts)

    def kv(a, b, t):
        return a.reshape(1, b, t, A_KV, A_HD)

    return (y_p, y_s,
            kv(ka_p, bp, sp), kv(va_p, bp, sp), ki_p[None], st_p[None],
            kv(ka_s, bs, ts), kv(va_s, bs, ts), ki_s[None], st_s[None])
```

```python
import functools

import jax
import jax.numpy as jnp
from jax import lax
from jax.experimental import pallas as pl
from jax.experimental.pallas import tpu as pltpu

F32 = jnp.float32
BF16 = jnp.bfloat16
I32 = jnp.int32
HI = lax.Precision.HIGHEST

CHUNK = 64
CHUNK_SHIFT = 6
EPS = 1e-6
ROPE_THETA = 10000.0
A_HD = 64
A_HEADS = 8
A_KV = 2
IDX_HEADS = 4
IDX_DIM = 64
INDEX_TOPK = 256
B_HEADS = 4
B_DK = 64
B_DV = 128
GATE_RANK = 16
GATE_TAU = 16.0
N_EXPERTS = 32
TOP_K = 4
SWIGLU_LIMIT = 7.0
SWIGLU_ALPHA = 1.702

LANES = 128
SUBLANES = 8
TOKEN_TILE = 256
MERGE_TILE = 512
ROW_UNROLL = 8
ROW_DMA_TILE = 512
GLA_SUB = 16
GLA_BATCH = 2
GLA_CHUNKS_PER_STEP = 8
GLA_EXP_CLAMP = 80.0
SLOT_BLOCK = 512
VMEM_LIMIT = 56 * 1024 * 1024
INT_MIN = -2147483648
NEG_BIG = -1e30
SEARCH_TWO_BIT_MAX_SCORES = 256 * 768

C_QA, C_KA, C_VA, C_QI, C_MISC, C_QG, C_KG, C_VG, C_RG, C_END = 0, 512, 640, 768, 1024, 1152, 1408, 1664, 2176, 2688
MISC_WI = 64
MISC_LR = 68


def _params(*sem):
    return pltpu.CompilerParams(dimension_semantics=sem, vmem_limit_bytes=VMEM_LIMIT)


def _nt(a, b):
    return lax.dot_general(a, b, (((1,), (1,)), ((), ())), preferred_element_type=F32)


def _tn(a, b):
    return lax.dot_general(a, b, (((0,), (0,)), ((), ())), preferred_element_type=F32)


def _rms(x):
    return x * lax.rsqrt(jnp.mean(x * x, axis=-1, keepdims=True) + EPS)


def _silu(x):
    return x / (1.0 + jnp.exp(-x))


def _slot_index_base(r0, mt):
    shift = mt.bit_length() - 1
    assert mt == 1 << shift and mt % ROW_UNROLL == 0
    return lax.shift_left(lax.shift_right_logical(r0, shift), shift + 2) + (r0 & (mt - 1))


def _store_row_tiles(ref, val):
    rows, width = val.shape
    assert width == SUBLANES * LANES
    for s in range(SUBLANES):
        ref[pl.ds(s, rows, stride=SUBLANES), :] = val[:, s * LANES:(s + 1) * LANES]


def _load_row_tiles(ref, rows):
    return jnp.concatenate([ref[pl.ds(s, rows, stride=SUBLANES), :] for s in range(SUBLANES)], axis=1)


def _value_with_ones(vb):
    return jnp.concatenate([vb, jnp.ones_like(vb)], axis=-1)


def _adaln_kernel(c_ref, w_ref, b_ref, o_ref):
    a = _silu(c_ref[...])
    o_ref[...] = jnp.dot(a, w_ref[...], preferred_element_type=F32, precision=HI) + b_ref[...]


def _adaln(c, w, b):
    r, d = c.shape
    n = w.shape[1]
    tn = 512
    return pl.pallas_call(
        _adaln_kernel,
        grid=(n // tn,),
        in_specs=[pl.BlockSpec((r, d), lambda j: (0, 0)),
                  pl.BlockSpec((d, tn), lambda j: (0, j)),
                  pl.BlockSpec((1, tn), lambda j: (0, j))],
        out_specs=pl.BlockSpec((r, tn), lambda j: (0, j)),
        out_shape=jax.ShapeDtypeStruct((r, n), F32),
        name="adaln",
        compiler_params=_params("arbitrary"),
    )(c, w, b.reshape(1, n))


def _premix_kernel(x_ref, mod_ref, g_ref, w_ref, wg_ref, bg_ref, cos_ref, sin_ref,
                   qa_ref, ka_ref, va_ref, qi_ref, ki_ref, misc_ref, qg_ref, kg_ref, vg_ref, gg_ref, rg_ref,
                   kb_ref, vx_ref, kib_ref):
    x = x_ref[0]
    mod = mod_ref[0]
    hb = (_rms(x) * g_ref[...] * (1.0 + mod[1:2]) + mod[0:1]).astype(BF16)
    tm = x.shape[0]

    def project(c0, c1):
        return jnp.dot(hb, w_ref[:, c0:c1], preferred_element_type=F32)

    cos = cos_ref[...]
    sin = sin_ref[...]
    lane = lax.broadcasted_iota(I32, (tm, LANES), 1)
    lower_half = (lane & (A_HD - 1)) < (A_HD // 2)

    def rope(xc):
        rot = jnp.where(lower_half, pltpu.roll(xc, LANES - A_HD // 2, 1), pltpu.roll(xc, A_HD // 2, 1))
        return xc * cos + rot * sin

    seg = project(C_QA, C_KA)
    for j in range((C_KA - C_QA) // LANES):
        qa_ref[0, :, j * LANES:(j + 1) * LANES] = (rope(seg[:, j * LANES:(j + 1) * LANES]) * (A_HD ** -0.5)).astype(BF16)
    seg = project(C_KA, C_QI)
    ka = rope(seg[:, :C_VA - C_KA])
    va = seg[:, C_VA - C_KA:]
    ka_ref[0] = ka
    va_ref[0] = va
    kb_ref[0] = ka.T.astype(BF16)
    vx_ref[0] = _value_with_ones(va.astype(BF16))
    seg = project(C_QI, C_QG)
    for j in range((C_MISC - C_QI) // LANES):
        qi_ref[0, :, j * LANES:(j + 1) * LANES] = rope(seg[:, j * LANES:(j + 1) * LANES]).astype(BF16)
    m = seg[:, C_MISC - C_QI:]
    mr = rope(m)
    ki_ref[0] = mr[:, :IDX_DIM]
    kib_ref[0] = mr.T[:IDX_DIM].astype(BF16)
    misc_ref[0] = jnp.where(lane < IDX_DIM, mr, m * (IDX_HEADS ** -0.5))
    xg = jnp.dot(m, wg_ref[...], preferred_element_type=F32, precision=HI) + bg_ref[...]
    gg_ref[0] = (jnp.minimum(xg, 0.0) - jnp.log(1.0 + jnp.exp(-jnp.abs(xg)))) * (1.0 / GATE_TAU)
    qg_ref[0] = project(C_QG, C_KG) * (B_DK ** -0.5)
    kg_ref[0] = project(C_KG, C_VG)
    vg_ref[0] = project(C_VG, C_RG)
    rg_ref[0] = project(C_RG, C_END)


def _premix(x, mod, g_mix, w_r, wg_pad, bg, cos_t, sin_t, tm):
    b, t, d = x.shape
    rm, fm = False, True
    widths = [(512, BF16, rm), (128, F32, rm), (128, F32, rm), (256, BF16, rm), (64, F32, rm), (128, F32, rm),
              (256, F32, rm), (256, F32, rm), (512, F32, rm), (256, F32, rm), (512, F32, rm),
              (128, BF16, fm), (2 * LANES, BF16, rm), (IDX_DIM, BF16, fm)]
    return pl.pallas_call(
        _premix_kernel,
        grid=(b, t // tm),
        in_specs=[pl.BlockSpec((1, tm, d), lambda i, j: (i, j, 0)),
                  pl.BlockSpec((1, 6, d), lambda i, j: (i, 0, 0)),
                  pl.BlockSpec((1, d), lambda i, j: (0, 0)),
                  pl.BlockSpec((d, C_END), lambda i, j: (0, 0)),
                  pl.BlockSpec((LANES, 256), lambda i, j: (0, 0)),
                  pl.BlockSpec((1, 256), lambda i, j: (0, 0)),
                  pl.BlockSpec((tm, LANES), lambda i, j: (j, 0)),
                  pl.BlockSpec((tm, LANES), lambda i, j: (j, 0))],
        out_specs=[pl.BlockSpec((1, tm, w), lambda i, j: (i, j, 0)) if not fm else
                   pl.BlockSpec((1, w, tm), lambda i, j: (i, 0, j)) for w, _, fm in widths],
        out_shape=[jax.ShapeDtypeStruct((b, w, t) if fm else (b, t, w), dt) for w, dt, fm in widths],
        name="premix",
        compiler_params=_params("arbitrary", "arbitrary"),
    )(x, mod, g_mix, w_r, wg_pad, bg, cos_t, sin_t)


def _attn_kernel(*refs, tq, n_ctx, top_k, causal, n_keys, q_off, n_groups, n_past, keep_all):
    if n_past:
        q_ref, qi_ref, misc_ref, k_ref, v_ref, ki_ref, pk_ref, pv_ref, pki_ref, o_ref = refs
        assert n_ctx - n_past == LANES and k_ref.shape[1] <= LANES

        def new_columns(rows_ref):
            new = rows_ref[0]
            n_new, width = new.shape
            if width < LANES:
                new = jnp.concatenate([new, jnp.zeros((n_new, LANES - width), F32)], axis=1)
            square = jnp.concatenate([new, jnp.zeros((LANES - n_new, LANES), F32)], axis=0)
            return square.T[:width].astype(BF16)

        k = jnp.concatenate([pk_ref[0].astype(BF16), new_columns(k_ref)], axis=1)
        v = jnp.concatenate([pv_ref[0].astype(BF16), new_columns(v_ref)], axis=1)
        vx = jnp.concatenate([v, jnp.ones_like(v)], axis=0)
        kib = jnp.concatenate([pki_ref[0].astype(BF16), new_columns(ki_ref)], axis=1)
    else:
        q_ref, qi_ref, misc_ref, k_ref, vx_ref, ki_ref, o_ref = refs
        k = k_ref[0]
        vx = vx_ref[0]
        kib = ki_ref[0]
    values_feature_major = bool(n_past)
    rows = tq // n_groups
    row0 = q_off + pl.program_id(1) * tq
    keys, kks = [], []
    idx_dots = {}
    for g in range(n_groups):
        qi = qi_ref[0, g * rows:(g + 1) * rows, :]
        for h in range(IDX_HEADS):
            qh = qi[:, h * IDX_DIM:(h + 1) * IDX_DIM]
            idx_dots[g, h] = jnp.dot(qh, kib, preferred_element_type=F32)
    for g in range(n_groups):
        misc = misc_ref[0, g * rows:(g + 1) * rows, :]
        isc = jnp.zeros((rows, n_ctx), F32)
        for h in range(IDX_HEADS):
            isc = isc + misc[:, MISC_WI + h:MISC_WI + h + 1] * jnp.maximum(idx_dots[g, h], 0.0)
        kpos = lax.broadcasted_iota(I32, (rows, n_ctx), 1)
        if causal:
            row = lax.broadcasted_iota(I32, (rows, 1), 0) + (row0 + g * rows)
            key_lim = (lax.shift_right_logical(row, CHUNK_SHIFT) + 1) * CHUNK
        else:
            key_lim = jnp.full((rows, 1), n_keys, I32)
        bits = pltpu.bitcast(isc, I32)
        key = jnp.where(bits < 0, INT_MIN - bits, bits)
        keys.append(jnp.where(kpos < key_lim, key, INT_MIN))
        kks.append(jnp.minimum(key_lim, top_k).astype(F32))

    def count_at_least(g, cand):
        return jnp.sum(jnp.where(keys[g] >= (cand ^ INT_MIN), 1.0, 0.0), axis=1, keepdims=True)

    bits_per_pass = 2 if tq * n_ctx <= SEARCH_TWO_BIT_MAX_SCORES else 1

    def search(i, ans):
        out = []
        for g in range(n_groups):
            if bits_per_pass == 1:
                cand = ans[g] | lax.shift_left(jnp.int32(1), 31 - i)
                out.append(jnp.where(count_at_least(g, cand) >= kks[g], cand, ans[g]))
            else:
                a1 = ans[g] | lax.shift_left(jnp.int32(1), 31 - 2 * i)
                a2 = ans[g] | lax.shift_left(jnp.int32(1), 30 - 2 * i)
                a3 = a1 | a2
                c1, c2, c3 = count_at_least(g, a1), count_at_least(g, a2), count_at_least(g, a3)
                kk = kks[g]
                out.append(jnp.where(c3 >= kk, a3, jnp.where(c1 >= kk, a1, jnp.where(c2 >= kk, a2, ans[g]))))
        return tuple(out)

    ans = tuple(jnp.zeros((rows, 1), I32) for _ in range(n_groups))
    if not keep_all:
        ans = lax.fori_loop(0, 32 // bits_per_pass, search, ans)
    r = lax.broadcasted_iota(I32, (LANES, LANES), 0)
    c = lax.broadcasted_iota(I32, (LANES, LANES), 1)
    upper = jnp.where(r < c, 1.0, 0.0).astype(BF16)
    n_blk = n_ctx // LANES
    thrs = [ans[g] ^ INT_MIN for g in range(n_groups)]
    ties = {(g, j): jnp.where(keys[g][:, j * LANES:(j + 1) * LANES] == thrs[g], 1.0, 0.0)
            for g in range(n_groups) for j in range(n_blk)}
    within = {gj: jnp.dot(t.astype(BF16), upper, preferred_element_type=F32) for gj, t in ties.items()}
    bias_groups = []
    for g in range(n_groups):
        gt = keys[g] > thrs[g]
        need = kks[g] - jnp.sum(jnp.where(gt, 1.0, 0.0), axis=1, keepdims=True)
        carry = jnp.zeros((rows, 1), F32)
        blocks = []
        for j in range(n_blk):
            sl = slice(j * LANES, (j + 1) * LANES)
            take_tie = jnp.where(within[g, j] + carry < need, ties[g, j], 0.0)
            blocks.append(jnp.where(gt[:, sl], 0.0, jnp.where(take_tie > 0.0, 0.0, NEG_BIG)))
            carry = carry + jnp.sum(ties[g, j], axis=1, keepdims=True)
        bias_groups.append(jnp.concatenate(blocks, axis=1))
    bias = jnp.concatenate(bias_groups, axis=0)

    q = q_ref[0]
    rep = A_HEADS // A_KV
    if values_feature_major and tq * A_HEADS <= 2 * LANES:
        q_groups = [jnp.concatenate([q[:, hh * A_HD:(hh + 1) * A_HD] for hh in range(g * rep, (g + 1) * rep)], axis=0)
                    for g in range(A_KV)]
        s_all = jnp.concatenate([jnp.dot(q_groups[g], k[g * A_HD:(g + 1) * A_HD, :], preferred_element_type=F32)
                                 for g in range(A_KV)], axis=0) + jnp.concatenate([bias] * A_HEADS, axis=0)
        p_all = jnp.exp((s_all - jnp.max(s_all, axis=1, keepdims=True)).astype(BF16))
        ox = _nt(p_all, vx)
        outs = [ox[hh * tq:(hh + 1) * tq, (hh // rep) * A_HD:(hh // rep + 1) * A_HD]
                / ox[hh * tq:(hh + 1) * tq, LANES:LANES + 1] for hh in range(A_HEADS)]
        o_ref[0] = jnp.concatenate(outs, axis=1).astype(BF16)
        return
    kgs = [k[g * A_HD:(g + 1) * A_HD, :] for g in range(A_KV)]
    qs = [q[:, hh * A_HD:(hh + 1) * A_HD] for hh in range(A_HEADS)]
    ss = [jnp.dot(qh, kgs[hh // rep], preferred_element_type=F32) + bias for hh, qh in enumerate(qs)]
    ps = [jnp.exp((s - jnp.max(s, axis=1, keepdims=True)).astype(BF16)) for s in ss]
    oxs = [_nt(p, vx) if values_feature_major else jnp.dot(p, vx, preferred_element_type=F32) for p in ps]
    outs = [ox[:, (hh // rep) * A_HD:(hh // rep + 1) * A_HD] / ox[:, LANES:LANES + 1] for hh, ox in enumerate(oxs)]
    o_ref[0] = jnp.concatenate(outs, axis=1).astype(BF16)


def _attn_call(q, qi, misc, kb, vx, kib, past, *, tq, q_off, n_tiles, n_ctx, causal, n_keys, n_groups):
    b = q.shape[0]
    t0 = q_off // tq
    top_k = min(INDEX_TOPK, n_keys // 4)
    n_past = 0 if past is None else past[0].shape[2]
    n_own = n_ctx if past is None else kb.shape[1]
    kern = functools.partial(_attn_kernel, tq=tq, n_ctx=n_ctx, top_k=top_k, causal=causal, n_keys=n_keys,
                             q_off=q_off, n_groups=n_groups, n_past=n_past,
                             keep_all=causal and q_off + n_tiles * tq <= top_k)

    def tok(w):
        return pl.BlockSpec((1, tq, w), lambda i, j: (i, t0 + j, 0))

    def ctx(rows, w):
        return pl.BlockSpec((1, rows, w), lambda i, j: (i, 0, 0))

    if past is None:
        own_specs = [ctx(kb.shape[1], n_ctx), ctx(n_ctx, vx.shape[2]), ctx(kib.shape[1], n_ctx)]
    else:
        own_specs = [ctx(n_own, a.shape[2]) for a in (kb, vx, kib)]
    in_specs = [tok(512), tok(256), tok(LANES)] + own_specs
    args = [q, qi, misc, kb, vx, kib]
    if past is not None:
        in_specs += [ctx(a.shape[1], n_past) for a in past]
        args += list(past)
    return pl.pallas_call(
        kern,
        grid=(b, n_tiles),
        in_specs=in_specs,
        out_specs=pl.BlockSpec((1, tq, 512), lambda i, j: (i, j, 0)),
        out_shape=jax.ShapeDtypeStruct((b, n_tiles * tq, 512), BF16),
        name="attn",
        compiler_params=_params("arbitrary", "arbitrary"),
    )(*args)


def _attn_causal(q, qi, misc, kb, vx, kib, tq):
    t = q.shape[1]
    return jnp.stack([
        _attn_call(q, qi, misc, kb, vx, kib, None, tq=tq, q_off=c * tq, n_tiles=1, n_ctx=(c + 1) * tq,
                   causal=True, n_keys=t, n_groups=2)
        for c in range(t // tq)])


def _gla_kernel(q_ref, k_ref, v_ref, g_ref, rg_ref, s0_ref, go_ref, ob_ref, st_ref, st_scr, *, cs, nc, bb):
    j = pl.program_id(1)
    hk, hv, hc = B_HEADS * B_DK, B_HEADS * B_DV, B_HEADS * cs

    def head_of(idx, width):
        return lax.shift_right_logical(idx, width.bit_length() - 1)

    def same_head(rows, row_w, cols, col_w):
        r = head_of(lax.broadcasted_iota(I32, (rows, cols), 0), row_w)
        c = head_of(lax.broadcasted_iota(I32, (rows, cols), 1), col_w)
        return r == c

    keep_kb = same_head(hc, cs, hk, B_DK)
    keep_vb = same_head(hc, cs, hv, B_DV)
    keep_st = same_head(hv, B_DV, hk, B_DK)
    t_idx = lax.broadcasted_iota(I32, (cs, hc), 0)
    s_idx = lax.broadcasted_iota(I32, (cs, hc), 1) & (cs - 1)
    keep_a = t_idx >= s_idx
    r = lax.broadcasted_iota(I32, (cs, cs), 0)
    c = lax.broadcasted_iota(I32, (cs, cs), 1)
    tri = jnp.where(r >= c, 1.0, 0.0)
    go = go_ref[...]

    @pl.when(j == 0)
    def _():
        for bi in range(bb):
            blocks = []
            for h in range(B_HEADS):
                parts = [jnp.zeros((B_DV, B_DK), F32)] * B_HEADS
                parts[h] = s0_ref[bi, h].T
                blocks.append(jnp.concatenate(parts, axis=1))
            st_scr[bi] = jnp.concatenate(blocks, axis=0)

    inst = [(ci, bi) for ci in range(nc) for bi in range(bb)]
    sls = {ci: slice(ci * cs, (ci + 1) * cs) for ci in range(nc)}
    bcum = {t: jnp.dot(tri, g_ref[t[1], sls[t[0]], :], preferred_element_type=F32, precision=HI) for t in inst}
    n_sub = cs // GLA_SUB
    v_bf, qdec, kdec, qts, kts = {}, {}, {}, {}, {}
    for t in inst:
        ci, bi = t
        q = q_ref[bi, sls[ci], :]
        k = k_ref[bi, sls[ci], :]
        bc = bcum[t]
        blast = bc[cs - 1:cs, :]
        qdec[t] = (q * jnp.exp(bc)).astype(BF16)
        kdec[t] = (k * jnp.exp(blast - bc)).astype(BF16)
        v_bf[t] = v_ref[bi, sls[ci], :].astype(BF16)
        for i in range(n_sub):
            rs = slice(i * GLA_SUB, (i + 1) * GLA_SUB)
            ref = bc[i * GLA_SUB:i * GLA_SUB + 1, :]
            qts[t, i] = (q[rs, :] * jnp.exp(bc[rs, :] - ref)).astype(BF16)
            kt = (k * jnp.exp(jnp.minimum(ref - bc, GLA_EXP_CLAMP))).astype(BF16)
            kts[t, i] = jnp.where(keep_kb, jnp.concatenate([kt] * B_HEADS, axis=0), 0.0)
    a_rows = {(t, i): _nt(qts[t, i], kts[t, i]) for t in inst for i in range(n_sub)}
    a_mat = {t: jnp.where(keep_a, jnp.concatenate([a_rows[t, i] for i in range(n_sub)], axis=0), 0.0).astype(BF16)
             for t in inst}
    o_intra = {t: jnp.dot(a_mat[t], jnp.where(keep_vb, jnp.concatenate([v_bf[t]] * B_HEADS, axis=0), 0.0),
                          preferred_element_type=F32) for t in inst}
    kv = {t: jnp.where(keep_st, _tn(v_bf[t], kdec[t]), 0.0) for t in inst}
    for t in inst:
        ci, bi = t
        st = st_scr[bi]
        o = o_intra[t] + _nt(qdec[t], st.astype(BF16))
        st_scr[bi] = st * jnp.exp(bcum[t][cs - 1:cs, :]) + kv[t]
        rg = rg_ref[bi, sls[ci], :]
        for h in range(B_HEADS):
            vs = slice(h * B_DV, (h + 1) * B_DV)
            ob_ref[bi, sls[ci], vs] = (_rms(o[:, vs]) * go * _silu(rg[:, vs])).astype(BF16)

    @pl.when(j == pl.num_programs(1) - 1)
    def _():
        for bi in range(bb):
            st = st_scr[bi]
            for h in range(B_HEADS):
                st_ref[bi, h] = st[h * B_DV:(h + 1) * B_DV, h * B_DK:(h + 1) * B_DK].T


def _gla(qg, kg, vg, gg, rg, state_t, g_out, *, cs, nc, bb):
    b, t, _ = qg.shape
    assert b % bb == 0
    tt = cs * nc
    kern = functools.partial(_gla_kernel, cs=cs, nc=nc, bb=bb)

    def tok(w):
        return pl.BlockSpec((bb, tt, w), lambda i, j: (i, j, 0))

    st_spec = pl.BlockSpec((bb, B_HEADS, B_DK, B_DV), lambda i, j: (i, 0, 0, 0))
    return pl.pallas_call(
        kern,
        grid=(b // bb, t // tt),
        in_specs=[tok(256), tok(256), tok(512), tok(256), tok(512), st_spec,
                  pl.BlockSpec((1, B_DV), lambda i, j: (0, 0))],
        out_specs=[tok(512), st_spec],
        out_shape=[jax.ShapeDtypeStruct((b, t, 512), BF16),
                   jax.ShapeDtypeStruct((b, B_HEADS, B_DK, B_DV), F32)],
        scratch_shapes=[pltpu.VMEM((bb, B_HEADS * B_DV, B_HEADS * B_DK), F32)],
        name="gla",
        compiler_params=_params("arbitrary", "arbitrary"),
    )(qg, kg, vg, gg, rg, state_t, g_out)


def _merge_kernel(oa_ref, ob_ref, x_ref, mod_ref, wo_ref, gf_ref, wr_ref, br_ref, cnt0_ref,
                  x1_ref, h2_ref, ri_ref, rgate_ref, cnt_ref, carry_scr):
    @pl.when((pl.program_id(0) == 0) & (pl.program_id(1) == 0))
    def _():
        carry_scr[...] = cnt0_ref[...]

    mod = mod_ref[0]
    oa = oa_ref[:, 0].reshape(-1, oa_ref.shape[-1])
    cat = jnp.concatenate([oa, ob_ref[0]], axis=1)
    x1 = x_ref[0] + mod[2:3] * jnp.dot(cat, wo_ref[...], preferred_element_type=F32)
    x1_ref[0] = x1
    h2 = _rms(x1) * gf_ref[...] * (1.0 + mod[4:5]) + mod[3:4]
    _store_row_tiles(h2_ref.at[0], h2)
    tm = x1.shape[0]
    lane = lax.broadcasted_iota(I32, (tm, LANES), 1).astype(F32)
    hi = h2.astype(BF16)
    lo = (h2 - hi.astype(F32)).astype(BF16)
    wr = wr_ref[...]
    a = jnp.dot(hi, wr, preferred_element_type=F32)
    left = (a[:, :LANES] + a[:, LANES:]) + jnp.dot(lo, wr[:, :LANES], preferred_element_type=F32) + br_ref[...]
    idx, val = [], []
    for _ in range(TOP_K):
        m = jnp.max(left, axis=1, keepdims=True)
        e = jnp.argmax(left, axis=1, keepdims=True).astype(F32)
        idx.append(e)
        val.append(m)
        left = jnp.where(lane == e, -jnp.inf, left)
    ex = [jnp.exp(vv - val[0]) for vv in val]
    den = ex[0] + ex[1] + ex[2] + ex[3]
    onehot = jnp.zeros((tm, LANES), F32)
    for e in idx:
        onehot = onehot + jnp.where(lane == e, 1.0, 0.0)
    r = lax.broadcasted_iota(I32, (tm, tm), 0)
    c = lax.broadcasted_iota(I32, (tm, tm), 1)
    earlier = jnp.where(r > c, 1.0, 0.0).astype(BF16)
    before = jnp.dot(earlier, onehot.astype(BF16), preferred_element_type=F32) + carry_scr[...]
    ri = jnp.zeros((tm, LANES), F32)
    rgate = jnp.zeros((tm, LANES), F32)
    for kq in range(TOP_K):
        rank = jnp.sum(jnp.where(lane == idx[kq], before, 0.0), axis=1, keepdims=True)
        ri = jnp.where(lane == kq, idx[kq], ri)
        ri = jnp.where(lane == TOP_K + kq, rank, ri)
        rgate = jnp.where(lane == kq, ex[kq] / den, rgate)
    square = ri if tm % LANES == 0 else jnp.concatenate([ri, jnp.zeros((LANES - tm, LANES), F32)], axis=0)
    ri_ref[0] = square.T[:2 * TOP_K, :tm].astype(I32)
    rgate_ref[0] = rgate
    carry_scr[...] = carry_scr[...] + jnp.sum(onehot, axis=0, keepdims=True)
    cnt_ref[...] = carry_scr[...]


def _merge(oa, ob, x, mod, w_out, g_ffn, wr_pad, br_pad, cnt0, tm):
    b, t, d = x.shape

    def tok(w):
        return pl.BlockSpec((1, tm, w), lambda i, j: (i, j, 0))

    def const(s):
        return pl.BlockSpec(s, lambda i, j: (0, 0))

    return pl.pallas_call(
        _merge_kernel,
        grid=(b, t // tm),
        in_specs=[pl.BlockSpec((tm // oa.shape[2], 1, oa.shape[2], 512), lambda i, j: (j, i, 0, 0)),
                  tok(512), tok(d), pl.BlockSpec((1, 6, d), lambda i, j: (i, 0, 0)),
                  const((d, d)), const((1, d)), const((d, 2 * LANES)), const((1, LANES)), const((1, LANES))],
        out_specs=[tok(d), pl.BlockSpec((1, tm * SUBLANES, LANES), lambda i, j: (i, j, 0)),
                   pl.BlockSpec((1, 2 * TOP_K, tm), lambda i, j: (i * (t // tm) + j, 0, 0)), tok(LANES),
                   const((1, LANES))],
        out_shape=[jax.ShapeDtypeStruct((b, t, d), F32), jax.ShapeDtypeStruct((b, t * SUBLANES, LANES), F32),
                   jax.ShapeDtypeStruct((b * (t // tm), 2 * TOP_K, tm), I32), jax.ShapeDtypeStruct((b, t, LANES), F32),
                   jax.ShapeDtypeStruct((1, LANES), F32)],
        scratch_shapes=[pltpu.VMEM((1, LANES), F32)],
        name="merge",
        compiler_params=_params("arbitrary", "arbitrary"),
    )(oa, ob, x, mod, w_out, g_ffn, wr_pad, br_pad, cnt0)


def _dispatch_kernel(pe_ref, hp_ref, dest_ref, h_hbm, *rest, tm, mt, zero_init):
    if zero_init:
        xs_ref, ring, zbuf, in_sem, sem, zsem = rest

        @pl.when(pl.program_id(0) == 0)
        def _():
            zbuf[...] = jnp.zeros_like(zbuf)

            def zero_copy(e):
                start = pl.multiple_of(pe_ref[e] - SLOT_BLOCK, SLOT_BLOCK)
                return pltpu.make_async_copy(zbuf, xs_ref.at[pl.ds(start, SLOT_BLOCK)], zsem)

            for e in range(N_EXPERTS):
                @pl.when(hp_ref[e] > 0)
                def _():
                    zero_copy(e).start()
            for e in range(N_EXPERTS):
                @pl.when(hp_ref[e] > 0)
                def _():
                    zero_copy(e).wait()
    else:
        _, xs_ref, ring, in_sem, sem = rest

    step = pl.program_id(0)
    n_steps = pl.num_programs(0)
    slot_in = lax.rem(step, 3)
    cur = lax.rem(step, 2)

    def load(tile_idx, ring_slot):
        rows = pl.ds(pl.multiple_of(tile_idx * tm, tm), tm)
        return pltpu.make_async_copy(h_hbm.at[rows], ring.at[ring_slot], in_sem.at[ring_slot])

    @pl.when(step == 0)
    def _():
        load(0, 0).start()

    @pl.when(step + 1 < n_steps)
    def _():
        load(step + 1, lax.rem(step + 1, 3)).start()

    load(step, slot_in).wait()

    def row_copy(r, slot, which):
        return pltpu.make_async_copy(ring.at[slot_in, r], xs_ref.at[slot], sem.at[which])

    def issue(i, carry):
        r0 = i * ROW_UNROLL
        base = _slot_index_base(r0, mt)
        for j in range(ROW_UNROLL):
            for kq in range(TOP_K):
                row_copy(r0 + j, dest_ref[base + kq * mt + j], cur).start(priority=kq % 2)
        return carry

    lax.fori_loop(0, tm // ROW_UNROLL, issue, 0)

    def drain(which):
        def body(i, carry):
            for _ in range(ROW_UNROLL * TOP_K):
                row_copy(0, 0, which).wait()
            return carry

        lax.fori_loop(0, tm // ROW_UNROLL, body, 0)

    @pl.when(step > 0)
    def _():
        drain(1 - cur)

    @pl.when(step == n_steps - 1)
    def _():
        drain(cur)


def _dispatch(pad_end, has_pad, dest_flat, h2_tiles, xs, n_slots, tm, mt):
    n = h2_tiles.shape[0]
    tile = h2_tiles.shape[1:]
    zero_init = xs is None
    assert tm % mt == 0
    kern = functools.partial(_dispatch_kernel, tm=tm, mt=mt, zero_init=zero_init)
    in_specs = [pl.BlockSpec((tm * TOP_K,), lambda i, pe, hp: (i,), memory_space=pltpu.SMEM),
                pl.BlockSpec(memory_space=pl.ANY)]
    args = [pad_end, has_pad, dest_flat, h2_tiles]
    ring = pltpu.VMEM((3, tm) + tile, F32)
    scratch = [ring, pltpu.SemaphoreType.DMA((3,)), pltpu.SemaphoreType.DMA((2,))]
    aliases = {}
    if zero_init:
        scratch = [ring, pltpu.VMEM((SLOT_BLOCK,) + tile, F32), pltpu.SemaphoreType.DMA((3,)),
                   pltpu.SemaphoreType.DMA((2,)), pltpu.SemaphoreType.DMA(())]
    else:
        in_specs.append(pl.BlockSpec(memory_space=pl.ANY))
        args.append(xs)
        aliases = {4: 0}
    grid_spec = pltpu.PrefetchScalarGridSpec(
        num_scalar_prefetch=2,
        grid=(n // tm,),
        in_specs=in_specs,
        out_specs=pl.BlockSpec(memory_space=pl.ANY),
        scratch_shapes=scratch,
    )
    return pl.pallas_call(
        kern,
        grid_spec=grid_spec,
        out_shape=jax.ShapeDtypeStruct((n_slots,) + tile, F32),
        input_output_aliases=aliases,
        name="dispatch",
        compiler_params=_params("arbitrary"),
    )(*args)


def _expert_kernel(be_ref, nx_ref, nu_ref, xs_ref, wgu_hbm, bgu_ref, wdn_hbm, bdn_ref, ys_ref,
                   gu_f32, dn_f32, wgu_b, wdn_b, sem):
    j = pl.program_id(0)

    def fetch(e):
        return (pltpu.make_async_copy(wgu_hbm.at[e], gu_f32, sem.at[0]),
                pltpu.make_async_copy(wdn_hbm.at[e], dn_f32, sem.at[1]))

    @pl.when(j < nu_ref[0])
    def _():
        @pl.when(j == 0)
        def _():
            for cp in fetch(be_ref[0]):
                cp.start()

        @pl.when((j == 0) | (be_ref[j] != be_ref[jnp.maximum(j - 1, 0)]))
        def _():
            for cp in fetch(be_ref[j]):
                cp.wait()
            wgu_b[...] = gu_f32[...].astype(BF16)
            wdn_b[...] = dn_f32[...].astype(BF16)

            @pl.when(nx_ref[j] >= 0)
            def _():
                for cp in fetch(nx_ref[j]):
                    cp.start()

        f = wdn_b.shape[0]
        x = _load_row_tiles(xs_ref, SLOT_BLOCK).astype(BF16)
        gu = jnp.dot(x, wgu_b[...], preferred_element_type=F32) + bgu_ref[0]
        gate = jnp.minimum(gu[:, :f], SWIGLU_LIMIT)
        up = jnp.clip(gu[:, f:], -SWIGLU_LIMIT, SWIGLU_LIMIT)
        glu = gate / (1.0 + jnp.exp(-SWIGLU_ALPHA * gate))
        act = ((up + 1.0) * glu).astype(BF16)
        _store_row_tiles(ys_ref, jnp.dot(act, wdn_b[...], preferred_element_type=F32) + bdn_ref[0])


def _experts(block_expert, next_expert, n_used, xs, wgu, bgu, wdn, bdn):
    n_slots = xs.shape[0] // SUBLANES
    nb = n_slots // SLOT_BLOCK
    f, d = wdn.shape[1:]
    rows = SLOT_BLOCK * SUBLANES

    def blk(i, be, nx, nu):
        return (jnp.minimum(i, nu[0] - 1), 0)

    def bsel(i, be, nx, nu):
        return (be[i], 0, 0)

    grid_spec = pltpu.PrefetchScalarGridSpec(
        num_scalar_prefetch=3,
        grid=(nb,),
        in_specs=[pl.BlockSpec((rows, LANES), blk),
                  pl.BlockSpec(memory_space=pl.ANY),
                  pl.BlockSpec((1, 1, 2 * f), bsel),
                  pl.BlockSpec(memory_space=pl.ANY),
                  pl.BlockSpec((1, 1, d), bsel)],
        out_specs=pl.BlockSpec((rows, LANES), blk),
        scratch_shapes=[pltpu.VMEM((d, 2 * f), F32), pltpu.VMEM((f, d), F32),
                        pltpu.VMEM((d, 2 * f), BF16), pltpu.VMEM((f, d), BF16), pltpu.SemaphoreType.DMA((2,))],
    )
    return pl.pallas_call(
        _expert_kernel,
        grid_spec=grid_spec,
        out_shape=jax.ShapeDtypeStruct(xs.shape, F32),
        name="experts",
        compiler_params=_params("arbitrary"),
    )(block_expert, next_expert, n_used, xs, wgu, bgu, wdn, bdn)


def _combine_kernel(dest_ref, dnext_ref, x1_ref, gate_ref, mod_ref, modf_ref, gfin_ref, ys_ref, y_ref,
                    buf, sem, *, tm, mt):
    nt = pl.num_programs(1)
    step = pl.program_id(0) * nt + pl.program_id(1)
    n_steps = pl.num_programs(0) * nt
    half = TOP_K * tm * SUBLANES
    cur = lax.rem(step, 2)
    per_tile = mt // tm

    def row_copy(which, kq, r, slot):
        off = pl.multiple_of(which * half + (kq * tm + r) * SUBLANES, SUBLANES)
        return pltpu.make_async_copy(ys_ref.at[slot], buf.at[pl.ds(off, SUBLANES), :], sem.at[which])

    def gather(slots_ref, which, at_step):
        first_row = lax.rem(at_step, per_tile) * tm

        def issue(i, carry):
            r0 = i * ROW_UNROLL
            for j in range(ROW_UNROLL):
                for kq in range(TOP_K):
                    row_copy(which, kq, r0 + j, slots_ref[first_row + r0 + kq * mt + j]).start(priority=kq % 2)
            return carry

        lax.fori_loop(0, tm // ROW_UNROLL, issue, 0)

    @pl.when(step == 0)
    def _():
        gather(dest_ref, cur, step)

    @pl.when(step + 1 < n_steps)
    def _():
        gather(dnext_ref, 1 - cur, step + 1)

    def drain(i, carry):
        for _ in range(ROW_UNROLL * TOP_K):
            row_copy(cur, 0, 0, 0).wait()
        return carry

    lax.fori_loop(0, tm // ROW_UNROLL, drain, 0)
    gates = gate_ref[0]
    moe = None
    for kq in range(TOP_K):
        start = pl.multiple_of(cur * half + kq * tm * SUBLANES, SUBLANES)
        rows = _load_row_tiles(buf.at[pl.ds(start, tm * SUBLANES), :], tm)
        term = gates[:, kq:kq + 1] * rows
        moe = term if moe is None else moe + term
    xo = x1_ref[0] + mod_ref[0][5:6] * moe
    modf = modf_ref[0]
    y_ref[0] = _rms(xo) * gfin_ref[...] * (1.0 + modf[1:2]) + modf[0:1]


def _combine(dest_flat, x1, gates, mod, modf, g_final, ys, tm, mt):
    b, t, d = x1.shape
    nt = t // tm
    last = b * nt - 1
    assert mt % tm == 0
    per_tile = mt // tm
    kern = functools.partial(_combine_kernel, tm=tm, mt=mt)
    return pl.pallas_call(
        kern,
        grid=(b, nt),
        in_specs=[pl.BlockSpec((mt * TOP_K,), lambda i, j: ((i * nt + j) // per_tile,), memory_space=pltpu.SMEM),
                  pl.BlockSpec((mt * TOP_K,), lambda i, j: (jnp.minimum(i * nt + j + 1, last) // per_tile,),
                               memory_space=pltpu.SMEM),
                  pl.BlockSpec((1, tm, d), lambda i, j: (i, j, 0)),
                  pl.BlockSpec((1, tm, LANES), lambda i, j: (i, j, 0)),
                  pl.BlockSpec((1, 6, d), lambda i, j: (i, 0, 0)),
                  pl.BlockSpec((1, 2, d), lambda i, j: (i, 0, 0)),
                  pl.BlockSpec((1, d), lambda i, j: (0, 0)),
                  pl.BlockSpec(memory_space=pl.ANY)],
        out_specs=pl.BlockSpec((1, tm, d), lambda i, j: (i, j, 0)),
        out_shape=jax.ShapeDtypeStruct((b, t, d), F32),
        scratch_shapes=[pltpu.VMEM((2 * TOP_K * tm * SUBLANES, LANES), F32), pltpu.SemaphoreType.DMA((2,))],
        name="combine",
        compiler_params=_params("arbitrary", "arbitrary"),
    )(dest_flat, dest_flat, x1, gates, mod, modf, g_final, ys)


def _rope_tables(pos):
    half = A_HD // 2
    inv_freq = ROPE_THETA ** (-jnp.arange(half, dtype=F32) / half)
    ang = pos.astype(F32)[:, None] * inv_freq[None, :]
    cos = jnp.cos(ang)
    sin = jnp.sin(ang)
    reps = LANES // A_HD
    return jnp.tile(jnp.concatenate([cos, cos], axis=1), (1, reps)), jnp.tile(jnp.concatenate([-sin, sin], axis=1), (1, reps))


def _reorder_w_in(w_in):
    offs = [0]
    for w in (512, 128, 128, 256, 64, 4, 256, 256, 512, 16, 512):
        offs.append(offs[-1] + w)
    seg = [w_in[:, offs[i]:offs[i + 1]] for i in range(11)]
    qa, ka, va, qi, ki, wi, qg, kg, vg, lr, rg = seg
    pad = jnp.zeros((w_in.shape[0], LANES - IDX_DIM - IDX_HEADS - GATE_RANK), w_in.dtype)
    return jnp.concatenate([qa, ka, va, qi, ki, wi, lr, pad, qg, kg, vg, rg], axis=1).astype(BF16)


def kernel(x_prompt, x_sample, cache_k, cache_v, cache_kidx, state_gla, c_prompt, c_sample,
           w_mod, b_mod, g_mix, g_ffn, w_in, gla_w_gate, gla_b_gate, gla_g_out, w_out,
           w_router, b_router, w_gate_up, b_gate_up, w_down, b_down,
           w_mod_final, b_mod_final, g_final):
    depth = w_in.shape[0]
    assert depth == 1
    bp, sp, d = x_prompt.shape
    bs, ts, _ = x_sample.shape
    past = cache_k.shape[2]
    n_exp = w_router.shape[2]
    f = w_down.shape[2]

    c_all = jnp.concatenate([c_prompt, c_sample], axis=0)
    w_r = _reorder_w_in(w_in[0])
    wg_pad = jnp.zeros((LANES, B_HEADS * B_DK), F32).at[MISC_LR:MISC_LR + GATE_RANK].set(gla_w_gate[0])
    bg = gla_b_gate[0].reshape(1, -1)
    wr_pad = jnp.zeros((d, LANES), F32).at[:, :n_exp].set(w_router[0])
    wr_hi = wr_pad.astype(BF16)
    wr_cat = jnp.concatenate([wr_hi, (wr_pad - wr_hi.astype(F32)).astype(BF16)], axis=1)
    br_pad = jnp.full((1, LANES), NEG_BIG, F32).at[0, :n_exp].set(b_router[0])
    w_out_b = w_out[0].astype(BF16)
    bgu = b_gate_up[0].reshape(n_exp, 1, 2 * f)
    bdn = b_down[0].reshape(n_exp, 1, d)
    g_out = gla_g_out[0].reshape(1, B_DV)

    mod_all = _adaln(c_all, w_mod[0], b_mod[0]).reshape(bp + bs, 6, d)
    modf_all = _adaln(c_all, w_mod_final, b_mod_final).reshape(bp + bs, 2, d)
    mod_p, mod_s = mod_all[:bp], mod_all[bp:]
    modf_p, modf_s = modf_all[:bp], modf_all[bp:]

    def mixer(x, mod, pos, tm, past_kv, state_t, cs, nc, tq):
        cos_t, sin_t = _rope_tables(pos)
        qa, ka, va, qi, ki, misc, qg, kg, vg, gg, rg, kb, vx, kib = _premix(
            x, mod, g_mix[0].reshape(1, d), w_r, wg_pad, bg, cos_t, sin_t, tm)
        if past_kv is None:
            oa = _attn_causal(qa, qi, misc, kb, vx, kib, tq)
        else:
            n_past = past_kv[0].shape[2]
            assert tq == x.shape[1] and n_past % LANES == 0
            oa = _attn_call(qa, qi, misc, ka, va, ki, past_kv, tq=tq, q_off=0, n_tiles=1,
                            n_ctx=n_past + LANES, causal=False, n_keys=n_past + ka.shape[1], n_groups=1)[None]
        ob, st = _gla(qg, kg, vg, gg, rg, state_t, g_out, cs=cs, nc=nc, bb=GLA_BATCH)
        return oa, ob, (ka, va, ki, st)

    state0_p = jnp.zeros((bp, B_HEADS, B_DK, B_DV), F32)
    oa_p, ob_p, (ka_p, va_p, ki_p, st_p) = mixer(
        x_prompt, mod_p, jnp.arange(sp), min(TOKEN_TILE, sp), None, state0_p, CHUNK,
        min(GLA_CHUNKS_PER_STEP, sp // CHUNK), min(TOKEN_TILE, sp))
    past_kv = (jnp.transpose(cache_k[0], (0, 2, 3, 1)).reshape(bs, A_KV * A_HD, past),
               jnp.transpose(cache_v[0], (0, 2, 3, 1)).reshape(bs, A_KV * A_HD, past),
               jnp.swapaxes(cache_kidx[0], 1, 2))
    oa_s, ob_s, (ka_s, va_s, ki_s, st_s) = mixer(
        x_sample, mod_s, past + jnp.arange(ts), ts, past_kv, state_gla[0], ts, 1, ts)

    cnt0 = jnp.zeros((1, LANES), F32)
    x1_p, h2_p, ri_p, gate_p, cnt1 = _merge(oa_p, ob_p, x_prompt, mod_p, w_out_b, g_ffn[0].reshape(1, d),
                                            wr_cat, br_pad, cnt0, min(MERGE_TILE, sp))
    x1_s, h2_s, ri_s, gate_s, cnt2 = _merge(oa_s, ob_s, x_sample, mod_s, w_out_b, g_ffn[0].reshape(1, d),
                                            wr_cat, br_pad, cnt1, ts)
    counts = cnt2[0, :n_exp].astype(I32)
    padded = (counts + SLOT_BLOCK - 1) // SLOT_BLOCK * SLOT_BLOCK
    pad_end = jnp.cumsum(padded)
    pad_start = pad_end - padded
    n_asg = (bp * sp + bs * ts) * TOP_K
    nb = -(-n_asg // SLOT_BLOCK) + n_exp
    n_slots = nb * SLOT_BLOCK
    block_start = jnp.arange(nb, dtype=I32) * SLOT_BLOCK
    block_expert = jnp.minimum(jnp.sum((pad_end[None, :] <= block_start[:, None]).astype(I32), axis=1), n_exp - 1)
    n_used = (pad_end[-1:] // SLOT_BLOCK).astype(I32)
    following = pad_end.astype(I32)[block_expert] // SLOT_BLOCK
    next_expert = jnp.where(following < n_used[0], block_expert[jnp.minimum(following, nb - 1)], -1).astype(I32)
    has_pad = (padded > counts).astype(I32)
    expert_ids = jnp.arange(n_exp, dtype=I32)

    def dests(route):
        e, rank = route[:, :TOP_K, :], route[:, TOP_K:, :]
        start = jnp.sum(jnp.where(e[..., None] == expert_ids, pad_start.astype(I32), 0), axis=-1)
        return (start + rank).reshape(-1)

    dest_p, dest_s = dests(ri_p), dests(ri_s)
    pad_end_i = pad_end.astype(I32)
    tile = (SUBLANES, LANES)
    xs = _dispatch(pad_end_i, has_pad, dest_p, h2_p.reshape((bp * sp,) + tile), None, n_slots, min(ROW_DMA_TILE, sp),
                   min(MERGE_TILE, sp))
    xs = _dispatch(pad_end_i, has_pad, dest_s, h2_s.reshape((bs * ts,) + tile), xs, n_slots, min(TOKEN_TILE, bs * ts),
                   ts)
    ys = _experts(block_expert, next_expert, n_used, xs.reshape(n_slots * SUBLANES, LANES), w_gate_up[0], bgu,
                  w_down[0], bdn)
    ys = ys.reshape((n_slots,) + tile)
    y_p = _combine(dest_p, x1_p, gate_p, mod_p, modf_p, g_final.reshape(1, d), ys, min(TOKEN_TILE, sp),
                   min(MERGE_TILE, sp))
    y_s = _combine(dest_s, x1_s, gate_s, mod_s, modf_s, g_final.reshape(1, d), ys, ts, ts)

    def kv(a, b, t):
        return a.reshape(1, b, t, A_KV, A_HD)

    return (y_p, y_s,
            kv(ka_p, bp, sp), kv(va_p, bp, sp), ki_p[None], st_p[None],
            kv(ka_s, bs, ts), kv(va_s, bs, ts), ki_s[None], st_s[None])
```

```python
import functools

import jax
import jax.numpy as jnp
from jax import lax
from jax.experimental import pallas as pl
from jax.experimental.pallas import tpu as pltpu

F32 = jnp.float32
BF16 = jnp.bfloat16
I32 = jnp.int32
HI = lax.Precision.HIGHEST

CHUNK = 64
CHUNK_SHIFT = 6
EPS = 1e-6
ROPE_THETA = 10000.0
A_HD = 64
A_HEADS = 8
A_KV = 2
IDX_HEADS = 4
IDX_DIM = 64
INDEX_TOPK = 256
B_HEADS = 4
B_DK = 64
B_DV = 128
GATE_RANK = 16
GATE_TAU = 16.0
N_EXPERTS = 32
TOP_K = 4
SWIGLU_LIMIT = 7.0
SWIGLU_ALPHA = 1.702

LANES = 128
SUBLANES = 8
TOKEN_TILE = 256
MERGE_TILE = 512
ROW_UNROLL = 8
ROW_DMA_TILE = 1024
GLA_SUB = 16
GLA_BATCH = 2
GLA_CHUNKS_PER_STEP = 8
GLA_EXP_CLAMP = 80.0
SLOT_BLOCK = 512
VMEM_LIMIT = 56 * 1024 * 1024
INT_MIN = -2147483648
NEG_BIG = -1e30
SEARCH_TWO_BIT_MAX_SCORES = 256 * 768

C_QA, C_KA, C_VA, C_QI, C_MISC, C_QG, C_KG, C_VG, C_RG, C_END = 0, 512, 640, 768, 1024, 1152, 1408, 1664, 2176, 2688
MISC_WI = 64
MISC_LR = 68


def _params(*sem):
    return pltpu.CompilerParams(dimension_semantics=sem, vmem_limit_bytes=VMEM_LIMIT)


def _nt(a, b):
    return lax.dot_general(a, b, (((1,), (1,)), ((), ())), preferred_element_type=F32)


def _tn(a, b):
    return lax.dot_general(a, b, (((0,), (0,)), ((), ())), preferred_element_type=F32)


def _rms(x):
    return x * lax.rsqrt(jnp.mean(x * x, axis=-1, keepdims=True) + EPS)


def _silu(x):
    return x / (1.0 + jnp.exp(-x))


def _slot_index_base(r0, mt):
    shift = mt.bit_length() - 1
    assert mt == 1 << shift and mt % ROW_UNROLL == 0
    return lax.shift_left(lax.shift_right_logical(r0, shift), shift + 2) + (r0 & (mt - 1))


def _store_row_tiles(ref, val):
    rows, width = val.shape
    assert width == SUBLANES * LANES
    for s in range(SUBLANES):
        ref[pl.ds(s, rows, stride=SUBLANES), :] = val[:, s * LANES:(s + 1) * LANES]


def _load_row_tiles(ref, rows):
    return jnp.concatenate([ref[pl.ds(s, rows, stride=SUBLANES), :] for s in range(SUBLANES)], axis=1)


def _value_with_ones(vb):
    return jnp.concatenate([vb, jnp.ones_like(vb)], axis=-1)


def _adaln_kernel(c_ref, w_ref, b_ref, o_ref):
    a = _silu(c_ref[...])
    o_ref[...] = jnp.dot(a, w_ref[...], preferred_element_type=F32, precision=HI) + b_ref[...]


def _adaln(c, w, b):
    r, d = c.shape
    n = w.shape[1]
    tn = 512
    return pl.pallas_call(
        _adaln_kernel,
        grid=(n // tn,),
        in_specs=[pl.BlockSpec((r, d), lambda j: (0, 0)),
                  pl.BlockSpec((d, tn), lambda j: (0, j)),
                  pl.BlockSpec((1, tn), lambda j: (0, j))],
        out_specs=pl.BlockSpec((r, tn), lambda j: (0, j)),
        out_shape=jax.ShapeDtypeStruct((r, n), F32),
        name="adaln",
        compiler_params=_params("arbitrary"),
    )(c, w, b.reshape(1, n))


def _premix_kernel(x_ref, mod_ref, g_ref, w_ref, wg_ref, bg_ref, cos_ref, sin_ref,
                   qa_ref, ka_ref, va_ref, qi_ref, ki_ref, misc_ref, qg_ref, kg_ref, vg_ref, gg_ref, rg_ref,
                   kb_ref, vx_ref, kib_ref):
    x = x_ref[0]
    mod = mod_ref[0]
    hb = (_rms(x) * g_ref[...] * (1.0 + mod[1:2]) + mod[0:1]).astype(BF16)
    tm = x.shape[0]

    def project(c0, c1):
        return jnp.dot(hb, w_ref[:, c0:c1], preferred_element_type=F32)

    cos = cos_ref[...]
    sin = sin_ref[...]
    lane = lax.broadcasted_iota(I32, (tm, LANES), 1)
    lower_half = (lane & (A_HD - 1)) < (A_HD // 2)

    def rope(xc):
        rot = jnp.where(lower_half, pltpu.roll(xc, LANES - A_HD // 2, 1), pltpu.roll(xc, A_HD // 2, 1))
        return xc * cos + rot * sin

    seg = project(C_QA, C_KA)
    for j in range((C_KA - C_QA) // LANES):
        qa_ref[0, :, j * LANES:(j + 1) * LANES] = (rope(seg[:, j * LANES:(j + 1) * LANES]) * (A_HD ** -0.5)).astype(BF16)
    seg = project(C_KA, C_QI)
    ka = rope(seg[:, :C_VA - C_KA])
    va = seg[:, C_VA - C_KA:]
    ka_ref[0] = ka
    va_ref[0] = va
    kb_ref[0] = ka.T.astype(BF16)
    vx_ref[0] = _value_with_ones(va.astype(BF16))
    seg = project(C_QI, C_QG)
    for j in range((C_MISC - C_QI) // LANES):
        qi_ref[0, :, j * LANES:(j + 1) * LANES] = rope(seg[:, j * LANES:(j + 1) * LANES]).astype(BF16)
    m = seg[:, C_MISC - C_QI:]
    mr = rope(m)
    ki_ref[0] = mr[:, :IDX_DIM]
    kib_ref[0] = mr.T[:IDX_DIM].astype(BF16)
    misc_ref[0] = jnp.where(lane < IDX_DIM, mr, m * (IDX_HEADS ** -0.5))
    xg = jnp.dot(m, wg_ref[...], preferred_element_type=F32, precision=HI) + bg_ref[...]
    gg_ref[0] = (jnp.minimum(xg, 0.0) - jnp.log(1.0 + jnp.exp(-jnp.abs(xg)))) * (1.0 / GATE_TAU)
    qg_ref[0] = project(C_QG, C_KG) * (B_DK ** -0.5)
    kg_ref[0] = project(C_KG, C_VG)
    vg_ref[0] = project(C_VG, C_RG)
    rg_ref[0] = project(C_RG, C_END)


def _premix(x, mod, g_mix, w_r, wg_pad, bg, cos_t, sin_t, tm):
    b, t, d = x.shape
    rm, fm = False, True
    widths = [(512, BF16, rm), (128, F32, rm), (128, F32, rm), (256, BF16, rm), (64, F32, rm), (128, F32, rm),
              (256, F32, rm), (256, F32, rm), (512, F32, rm), (256, F32, rm), (512, F32, rm),
              (128, BF16, fm), (2 * LANES, BF16, rm), (IDX_DIM, BF16, fm)]
    return pl.pallas_call(
        _premix_kernel,
        grid=(b, t // tm),
        in_specs=[pl.BlockSpec((1, tm, d), lambda i, j: (i, j, 0)),
                  pl.BlockSpec((1, 6, d), lambda i, j: (i, 0, 0)),
                  pl.BlockSpec((1, d), lambda i, j: (0, 0)),
                  pl.BlockSpec((d, C_END), lambda i, j: (0, 0)),
                  pl.BlockSpec((LANES, 256), lambda i, j: (0, 0)),
                  pl.BlockSpec((1, 256), lambda i, j: (0, 0)),
                  pl.BlockSpec((tm, LANES), lambda i, j: (j, 0)),
                  pl.BlockSpec((tm, LANES), lambda i, j: (j, 0))],
        out_specs=[pl.BlockSpec((1, tm, w), lambda i, j: (i, j, 0)) if not fm else
                   pl.BlockSpec((1, w, tm), lambda i, j: (i, 0, j)) for w, _, fm in widths],
        out_shape=[jax.ShapeDtypeStruct((b, w, t) if fm else (b, t, w), dt) for w, dt, fm in widths],
        name="premix",
        compiler_params=_params("arbitrary", "arbitrary"),
    )(x, mod, g_mix, w_r, wg_pad, bg, cos_t, sin_t)


def _attn_kernel(*refs, tq, n_ctx, top_k, causal, n_keys, q_off, n_groups, n_past, keep_all):
    if n_past:
        q_ref, qi_ref, misc_ref, k_ref, v_ref, ki_ref, pk_ref, pv_ref, pki_ref, o_ref = refs
        assert n_ctx - n_past == LANES and k_ref.shape[1] <= LANES

        def new_columns(rows_ref):
            new = rows_ref[0]
            n_new, width = new.shape
            if width < LANES:
                new = jnp.concatenate([new, jnp.zeros((n_new, LANES - width), F32)], axis=1)
            square = jnp.concatenate([new, jnp.zeros((LANES - n_new, LANES), F32)], axis=0)
            return square.T[:width].astype(BF16)

        k = jnp.concatenate([pk_ref[0].astype(BF16), new_columns(k_ref)], axis=1)
        v = jnp.concatenate([pv_ref[0].astype(BF16), new_columns(v_ref)], axis=1)
        vx = jnp.concatenate([v, jnp.ones_like(v)], axis=0)
        kib = jnp.concatenate([pki_ref[0].astype(BF16), new_columns(ki_ref)], axis=1)
    else:
        q_ref, qi_ref, misc_ref, k_ref, vx_ref, ki_ref, o_ref = refs
        k = k_ref[0]
        vx = vx_ref[0]
        kib = ki_ref[0]
    values_feature_major = bool(n_past)
    rows = tq // n_groups
    row0 = q_off + pl.program_id(1) * tq
    keys, kks = [], []
    idx_dots = {}
    for g in range(n_groups):
        qi = qi_ref[0, g * rows:(g + 1) * rows, :]
        for h in range(IDX_HEADS):
            qh = qi[:, h * IDX_DIM:(h + 1) * IDX_DIM]
            idx_dots[g, h] = jnp.dot(qh, kib, preferred_element_type=F32)
    for g in range(n_groups):
        misc = misc_ref[0, g * rows:(g + 1) * rows, :]
        isc = jnp.zeros((rows, n_ctx), F32)
        for h in range(IDX_HEADS):
            isc = isc + misc[:, MISC_WI + h:MISC_WI + h + 1] * jnp.maximum(idx_dots[g, h], 0.0)
        kpos = lax.broadcasted_iota(I32, (rows, n_ctx), 1)
        if causal:
            row = lax.broadcasted_iota(I32, (rows, 1), 0) + (row0 + g * rows)
            key_lim = (lax.shift_right_logical(row, CHUNK_SHIFT) + 1) * CHUNK
        else:
            key_lim = jnp.full((rows, 1), n_keys, I32)
        bits = pltpu.bitcast(isc, I32)
        key = jnp.where(bits < 0, INT_MIN - bits, bits)
        keys.append(jnp.where(kpos < key_lim, key, INT_MIN))
        kks.append(jnp.minimum(key_lim, top_k).astype(F32))

    def count_at_least(g, cand):
        return jnp.sum(jnp.where(keys[g] >= (cand ^ INT_MIN), 1.0, 0.0), axis=1, keepdims=True)

    bits_per_pass = 2 if tq * n_ctx <= SEARCH_TWO_BIT_MAX_SCORES else 1

    def search(i, ans):
        out = []
        for g in range(n_groups):
            if bits_per_pass == 1:
                cand = ans[g] | lax.shift_left(jnp.int32(1), 31 - i)
                out.append(jnp.where(count_at_least(g, cand) >= kks[g], cand, ans[g]))
            else:
                a1 = ans[g] | lax.shift_left(jnp.int32(1), 31 - 2 * i)
                a2 = ans[g] | lax.shift_left(jnp.int32(1), 30 - 2 * i)
                a3 = a1 | a2
                c1, c2, c3 = count_at_least(g, a1), count_at_least(g, a2), count_at_least(g, a3)
                kk = kks[g]
                out.append(jnp.where(c3 >= kk, a3, jnp.where(c1 >= kk, a1, jnp.where(c2 >= kk, a2, ans[g]))))
        return tuple(out)

    ans = tuple(jnp.zeros((rows, 1), I32) for _ in range(n_groups))
    if not keep_all:
        ans = lax.fori_loop(0, 32 // bits_per_pass, search, ans)
    r = lax.broadcasted_iota(I32, (LANES, LANES), 0)
    c = lax.broadcasted_iota(I32, (LANES, LANES), 1)
    upper = jnp.where(r < c, 1.0, 0.0).astype(BF16)
    n_blk = n_ctx // LANES
    thrs = [ans[g] ^ INT_MIN for g in range(n_groups)]
    ties = {(g, j): jnp.where(keys[g][:, j * LANES:(j + 1) * LANES] == thrs[g], 1.0, 0.0)
            for g in range(n_groups) for j in range(n_blk)}
    within = {gj: jnp.dot(t.astype(BF16), upper, preferred_element_type=F32) for gj, t in ties.items()}
    bias_groups = []
    for g in range(n_groups):
        gt = keys[g] > thrs[g]
        need = kks[g] - jnp.sum(jnp.where(gt, 1.0, 0.0), axis=1, keepdims=True)
        carry = jnp.zeros((rows, 1), F32)
        blocks = []
        for j in range(n_blk):
            sl = slice(j * LANES, (j + 1) * LANES)
            take_tie = jnp.where(within[g, j] + carry < need, ties[g, j], 0.0)
            blocks.append(jnp.where(gt[:, sl], 0.0, jnp.where(take_tie > 0.0, 0.0, NEG_BIG)))
            carry = carry + jnp.sum(ties[g, j], axis=1, keepdims=True)
        bias_groups.append(jnp.concatenate(blocks, axis=1))
    bias = jnp.concatenate(bias_groups, axis=0)

    q = q_ref[0]
    rep = A_HEADS // A_KV
    if values_feature_major and tq * A_HEADS <= 2 * LANES:
        q_groups = [jnp.concatenate([q[:, hh * A_HD:(hh + 1) * A_HD] for hh in range(g * rep, (g + 1) * rep)], axis=0)
                    for g in range(A_KV)]
        s_all = jnp.concatenate([jnp.dot(q_groups[g], k[g * A_HD:(g + 1) * A_HD, :], preferred_element_type=F32)
                                 for g in range(A_KV)], axis=0) + jnp.concatenate([bias] * A_HEADS, axis=0)
        p_all = jnp.exp((s_all - jnp.max(s_all, axis=1, keepdims=True)).astype(BF16))
        ox = _nt(p_all, vx)
        outs = [ox[hh * tq:(hh + 1) * tq, (hh // rep) * A_HD:(hh // rep + 1) * A_HD]
                / ox[hh * tq:(hh + 1) * tq, LANES:LANES + 1] for hh in range(A_HEADS)]
        o_ref[0] = jnp.concatenate(outs, axis=1).astype(BF16)
        return
    kgs = [k[g * A_HD:(g + 1) * A_HD, :] for g in range(A_KV)]
    qs = [q[:, hh * A_HD:(hh + 1) * A_HD] for hh in range(A_HEADS)]
    ss = [jnp.dot(qh, kgs[hh // rep], preferred_element_type=F32) + bias for hh, qh in enumerate(qs)]
    ps = [jnp.exp((s - jnp.max(s, axis=1, keepdims=True)).astype(BF16)) for s in ss]
    oxs = [_nt(p, vx) if values_feature_major else jnp.dot(p, vx, preferred_element_type=F32) for p in ps]
    outs = [ox[:, (hh // rep) * A_HD:(hh // rep + 1) * A_HD] / ox[:, LANES:LANES + 1] for hh, ox in enumerate(oxs)]
    o_ref[0] = jnp.concatenate(outs, axis=1).astype(BF16)


def _attn_call(q, qi, misc, kb, vx, kib, past, *, tq, q_off, n_tiles, n_ctx, causal, n_keys, n_groups):
    b = q.shape[0]
    t0 = q_off // tq
    top_k = min(INDEX_TOPK, n_keys // 4)
    n_past = 0 if past is None else past[0].shape[2]
    n_own = n_ctx if past is None else kb.shape[1]
    kern = functools.partial(_attn_kernel, tq=tq, n_ctx=n_ctx, top_k=top_k, causal=causal, n_keys=n_keys,
                             q_off=q_off, n_groups=n_groups, n_past=n_past,
                             keep_all=causal and q_off + n_tiles * tq <= top_k)

    def tok(w):
        return pl.BlockSpec((1, tq, w), lambda i, j: (i, t0 + j, 0))

    def ctx(rows, w):
        return pl.BlockSpec((1, rows, w), lambda i, j: (i, 0, 0))

    if past is None:
        own_specs = [ctx(kb.shape[1], n_ctx), ctx(n_ctx, vx.shape[2]), ctx(kib.shape[1], n_ctx)]
    else:
        own_specs = [ctx(n_own, a.shape[2]) for a in (kb, vx, kib)]
    in_specs = [tok(512), tok(256), tok(LANES)] + own_specs
    args = [q, qi, misc, kb, vx, kib]
    if past is not None:
        in_specs += [ctx(a.shape[1], n_past) for a in past]
        args += list(past)
    return pl.pallas_call(
        kern,
        grid=(b, n_tiles),
        in_specs=in_specs,
        out_specs=pl.BlockSpec((1, tq, 512), lambda i, j: (i, j, 0)),
        out_shape=jax.ShapeDtypeStruct((b, n_tiles * tq, 512), BF16),
        name="attn",
        compiler_params=_params("arbitrary", "arbitrary"),
    )(*args)


def _attn_causal(q, qi, misc, kb, vx, kib, tq):
    t = q.shape[1]
    return jnp.stack([
        _attn_call(q, qi, misc, kb, vx, kib, None, tq=tq, q_off=c * tq, n_tiles=1, n_ctx=(c + 1) * tq,
                   causal=True, n_keys=t, n_groups=2)
        for c in range(t // tq)])


def _gla_kernel(q_ref, k_ref, v_ref, g_ref, rg_ref, s0_ref, go_ref, ob_ref, st_ref, st_scr, *, cs, nc, bb):
    j = pl.program_id(1)
    hk, hv, hc = B_HEADS * B_DK, B_HEADS * B_DV, B_HEADS * cs

    def head_of(idx, width):
        return lax.shift_right_logical(idx, width.bit_length() - 1)

    def same_head(rows, row_w, cols, col_w):
        r = head_of(lax.broadcasted_iota(I32, (rows, cols), 0), row_w)
        c = head_of(lax.broadcasted_iota(I32, (rows, cols), 1), col_w)
        return r == c

    keep_kb = same_head(hc, cs, hk, B_DK)
    keep_vb = same_head(hc, cs, hv, B_DV)
    keep_st = same_head(hv, B_DV, hk, B_DK)
    t_idx = lax.broadcasted_iota(I32, (cs, hc), 0)
    s_idx = lax.broadcasted_iota(I32, (cs, hc), 1) & (cs - 1)
    keep_a = t_idx >= s_idx
    r = lax.broadcasted_iota(I32, (cs, cs), 0)
    c = lax.broadcasted_iota(I32, (cs, cs), 1)
    tri = jnp.where(r >= c, 1.0, 0.0)
    go = go_ref[...]

    @pl.when(j == 0)
    def _():
        for bi in range(bb):
            blocks = []
            for h in range(B_HEADS):
                parts = [jnp.zeros((B_DV, B_DK), F32)] * B_HEADS
                parts[h] = s0_ref[bi, h].T
                blocks.append(jnp.concatenate(parts, axis=1))
            st_scr[bi] = jnp.concatenate(blocks, axis=0)

    inst = [(ci, bi) for ci in range(nc) for bi in range(bb)]
    sls = {ci: slice(ci * cs, (ci + 1) * cs) for ci in range(nc)}
    bcum = {t: jnp.dot(tri, g_ref[t[1], sls[t[0]], :], preferred_element_type=F32, precision=HI) for t in inst}
    n_sub = cs // GLA_SUB
    v_bf, qdec, kdec, qts, kts = {}, {}, {}, {}, {}
    for t in inst:
        ci, bi = t
        q = q_ref[bi, sls[ci], :]
        k = k_ref[bi, sls[ci], :]
        bc = bcum[t]
        blast = bc[cs - 1:cs, :]
        qdec[t] = (q * jnp.exp(bc)).astype(BF16)
        kdec[t] = (k * jnp.exp(blast - bc)).astype(BF16)
        v_bf[t] = v_ref[bi, sls[ci], :].astype(BF16)
        for i in range(n_sub):
            rs = slice(i * GLA_SUB, (i + 1) * GLA_SUB)
            ref = bc[i * GLA_SUB:i * GLA_SUB + 1, :]
            qts[t, i] = (q[rs, :] * jnp.exp(bc[rs, :] - ref)).astype(BF16)
            kt = (k * jnp.exp(jnp.minimum(ref - bc, GLA_EXP_CLAMP))).astype(BF16)
            kts[t, i] = jnp.where(keep_kb, jnp.concatenate([kt] * B_HEADS, axis=0), 0.0)
    a_rows = {(t, i): _nt(qts[t, i], kts[t, i]) for t in inst for i in range(n_sub)}
    a_mat = {t: jnp.where(keep_a, jnp.concatenate([a_rows[t, i] for i in range(n_sub)], axis=0), 0.0).astype(BF16)
             for t in inst}
    o_intra = {t: jnp.dot(a_mat[t], jnp.where(keep_vb, jnp.concatenate([v_bf[t]] * B_HEADS, axis=0), 0.0),
                          preferred_element_type=F32) for t in inst}
    kv = {t: jnp.where(keep_st, _tn(v_bf[t], kdec[t]), 0.0) for t in inst}
    for t in inst:
        ci, bi = t
        st = st_scr[bi]
        o = o_intra[t] + _nt(qdec[t], st.astype(BF16))
        st_scr[bi] = st * jnp.exp(bcum[t][cs - 1:cs, :]) + kv[t]
        rg = rg_ref[bi, sls[ci], :]
        for h in range(B_HEADS):
            vs = slice(h * B_DV, (h + 1) * B_DV)
            ob_ref[bi, sls[ci], vs] = (_rms(o[:, vs]) * go * _silu(rg[:, vs])).astype(BF16)

    @pl.when(j == pl.num_programs(1) - 1)
    def _():
        for bi in range(bb):
            st = st_scr[bi]
            for h in range(B_HEADS):
                st_ref[bi, h] = st[h * B_DV:(h + 1) * B_DV, h * B_DK:(h + 1) * B_DK].T


def _gla(qg, kg, vg, gg, rg, state_t, g_out, *, cs, nc, bb):
    b, t, _ = qg.shape
    assert b % bb == 0
    tt = cs * nc
    kern = functools.partial(_gla_kernel, cs=cs, nc=nc, bb=bb)

    def tok(w):
        return pl.BlockSpec((bb, tt, w), lambda i, j: (i, j, 0))

    st_spec = pl.BlockSpec((bb, B_HEADS, B_DK, B_DV), lambda i, j: (i, 0, 0, 0))
    return pl.pallas_call(
        kern,
        grid=(b // bb, t // tt),
        in_specs=[tok(256), tok(256), tok(512), tok(256), tok(512), st_spec,
                  pl.BlockSpec((1, B_DV), lambda i, j: (0, 0))],
        out_specs=[tok(512), st_spec],
        out_shape=[jax.ShapeDtypeStruct((b, t, 512), BF16),
                   jax.ShapeDtypeStruct((b, B_HEADS, B_DK, B_DV), F32)],
        scratch_shapes=[pltpu.VMEM((bb, B_HEADS * B_DV, B_HEADS * B_DK), F32)],
        name="gla",
        compiler_params=_params("arbitrary", "arbitrary"),
    )(qg, kg, vg, gg, rg, state_t, g_out)


def _merge_kernel(oa_ref, ob_ref, x_ref, mod_ref, wo_ref, gf_ref, wr_ref, br_ref, cnt0_ref,
                  x1_ref, h2_ref, ri_ref, rgate_ref, cnt_ref, carry_scr):
    @pl.when((pl.program_id(0) == 0) & (pl.program_id(1) == 0))
    def _():
        carry_scr[...] = cnt0_ref[...]

    mod = mod_ref[0]
    oa = oa_ref[:, 0].reshape(-1, oa_ref.shape[-1])
    cat = jnp.concatenate([oa, ob_ref[0]], axis=1)
    x1 = x_ref[0] + mod[2:3] * jnp.dot(cat, wo_ref[...], preferred_element_type=F32)
    x1_ref[0] = x1
    h2 = _rms(x1) * gf_ref[...] * (1.0 + mod[4:5]) + mod[3:4]
    _store_row_tiles(h2_ref.at[0], h2)
    tm = x1.shape[0]
    lane = lax.broadcasted_iota(I32, (tm, LANES), 1).astype(F32)
    hi = h2.astype(BF16)
    lo = (h2 - hi.astype(F32)).astype(BF16)
    wr = wr_ref[...]
    a = jnp.dot(hi, wr, preferred_element_type=F32)
    left = (a[:, :LANES] + a[:, LANES:]) + jnp.dot(lo, wr[:, :LANES], preferred_element_type=F32) + br_ref[...]
    idx, val = [], []
    for _ in range(TOP_K):
        m = jnp.max(left, axis=1, keepdims=True)
        e = jnp.argmax(left, axis=1, keepdims=True).astype(F32)
        idx.append(e)
        val.append(m)
        left = jnp.where(lane == e, -jnp.inf, left)
    ex = [jnp.exp(vv - val[0]) for vv in val]
    den = ex[0] + ex[1] + ex[2] + ex[3]
    onehot = jnp.zeros((tm, LANES), F32)
    for e in idx:
        onehot = onehot + jnp.where(lane == e, 1.0, 0.0)
    r = lax.broadcasted_iota(I32, (tm, tm), 0)
    c = lax.broadcasted_iota(I32, (tm, tm), 1)
    earlier = jnp.where(r > c, 1.0, 0.0).astype(BF16)
    before = jnp.dot(earlier, onehot.astype(BF16), preferred_element_type=F32) + carry_scr[...]
    ri = jnp.zeros((tm, LANES), F32)
    rgate = jnp.zeros((tm, LANES), F32)
    for kq in range(TOP_K):
        rank = jnp.sum(jnp.where(lane == idx[kq], before, 0.0), axis=1, keepdims=True)
        ri = jnp.where(lane == kq, idx[kq], ri)
        ri = jnp.where(lane == TOP_K + kq, rank, ri)
        rgate = jnp.where(lane == kq, ex[kq] / den, rgate)
    square = ri if tm % LANES == 0 else jnp.concatenate([ri, jnp.zeros((LANES - tm, LANES), F32)], axis=0)
    ri_ref[0] = square.T[:2 * TOP_K, :tm].astype(I32)
    rgate_ref[0] = rgate
    carry_scr[...] = carry_scr[...] + jnp.sum(onehot, axis=0, keepdims=True)
    cnt_ref[...] = carry_scr[...]


def _merge(oa, ob, x, mod, w_out, g_ffn, wr_pad, br_pad, cnt0, tm):
    b, t, d = x.shape

    def tok(w):
        return pl.BlockSpec((1, tm, w), lambda i, j: (i, j, 0))

    def const(s):
        return pl.BlockSpec(s, lambda i, j: (0, 0))

    return pl.pallas_call(
        _merge_kernel,
        grid=(b, t // tm),
        in_specs=[pl.BlockSpec((tm // oa.shape[2], 1, oa.shape[2], 512), lambda i, j: (j, i, 0, 0)),
                  tok(512), tok(d), pl.BlockSpec((1, 6, d), lambda i, j: (i, 0, 0)),
                  const((d, d)), const((1, d)), const((d, 2 * LANES)), const((1, LANES)), const((1, LANES))],
        out_specs=[tok(d), pl.BlockSpec((1, tm * SUBLANES, LANES), lambda i, j: (i, j, 0)),
                   pl.BlockSpec((1, 2 * TOP_K, tm), lambda i, j: (i * (t // tm) + j, 0, 0)), tok(LANES),
                   const((1, LANES))],
        out_shape=[jax.ShapeDtypeStruct((b, t, d), F32), jax.ShapeDtypeStruct((b, t * SUBLANES, LANES), F32),
                   jax.ShapeDtypeStruct((b * (t // tm), 2 * TOP_K, tm), I32), jax.ShapeDtypeStruct((b, t, LANES), F32),
                   jax.ShapeDtypeStruct((1, LANES), F32)],
        scratch_shapes=[pltpu.VMEM((1, LANES), F32)],
        name="merge",
        compiler_params=_params("arbitrary", "arbitrary"),
    )(oa, ob, x, mod, w_out, g_ffn, wr_pad, br_pad, cnt0)


def _dispatch_kernel(pe_ref, hp_ref, dest_ref, h_hbm, *rest, tm, mt, zero_init):
    if zero_init:
        xs_ref, ring, zbuf, in_sem, sem, zsem = rest

        @pl.when(pl.program_id(0) == 0)
        def _():
            zbuf[...] = jnp.zeros_like(zbuf)

            def zero_copy(e):
                start = pl.multiple_of(pe_ref[e] - SLOT_BLOCK, SLOT_BLOCK)
                return pltpu.make_async_copy(zbuf, xs_ref.at[pl.ds(start, SLOT_BLOCK)], zsem)

            for e in range(N_EXPERTS):
                @pl.when(hp_ref[e] > 0)
                def _():
                    zero_copy(e).start()
            for e in range(N_EXPERTS):
                @pl.when(hp_ref[e] > 0)
                def _():
                    zero_copy(e).wait()
    else:
        _, xs_ref, ring, in_sem, sem = rest

    step = pl.program_id(0)
    n_steps = pl.num_programs(0)
    slot_in = lax.rem(step, 3)
    cur = lax.rem(step, 2)

    def load(tile_idx, ring_slot):
        rows = pl.ds(pl.multiple_of(tile_idx * tm, tm), tm)
        return pltpu.make_async_copy(h_hbm.at[rows], ring.at[ring_slot], in_sem.at[ring_slot])

    @pl.when(step == 0)
    def _():
        load(0, 0).start()

    @pl.when(step + 1 < n_steps)
    def _():
        load(step + 1, lax.rem(step + 1, 3)).start()

    load(step, slot_in).wait()

    def row_copy(r, slot, which):
        return pltpu.make_async_copy(ring.at[slot_in, r], xs_ref.at[slot], sem.at[which])

    def issue(i, carry):
        r0 = i * ROW_UNROLL
        base = _slot_index_base(r0, mt)
        for j in range(ROW_UNROLL):
            for kq in range(TOP_K):
                row_copy(r0 + j, dest_ref[base + kq * mt + j], cur).start(priority=kq % 2)
        return carry

    lax.fori_loop(0, tm // ROW_UNROLL, issue, 0)

    def drain(which):
        def body(i, carry):
            for _ in range(ROW_UNROLL * TOP_K):
                row_copy(0, 0, which).wait()
            return carry

        lax.fori_loop(0, tm // ROW_UNROLL, body, 0)

    @pl.when(step > 0)
    def _():
        drain(1 - cur)

    @pl.when(step == n_steps - 1)
    def _():
        drain(cur)


def _dispatch(pad_end, has_pad, dest_flat, h2_tiles, xs, n_slots, tm, mt):
    n = h2_tiles.shape[0]
    tile = h2_tiles.shape[1:]
    zero_init = xs is None
    assert tm % mt == 0
    kern = functools.partial(_dispatch_kernel, tm=tm, mt=mt, zero_init=zero_init)
    in_specs = [pl.BlockSpec((tm * TOP_K,), lambda i, pe, hp: (i,), memory_space=pltpu.SMEM),
                pl.BlockSpec(memory_space=pl.ANY)]
    args = [pad_end, has_pad, dest_flat, h2_tiles]
    ring = pltpu.VMEM((3, tm) + tile, F32)
    scratch = [ring, pltpu.SemaphoreType.DMA((3,)), pltpu.SemaphoreType.DMA((2,))]
    aliases = {}
    if zero_init:
        scratch = [ring, pltpu.VMEM((SLOT_BLOCK,) + tile, F32), pltpu.SemaphoreType.DMA((3,)),
                   pltpu.SemaphoreType.DMA((2,)), pltpu.SemaphoreType.DMA(())]
    else:
        in_specs.append(pl.BlockSpec(memory_space=pl.ANY))
        args.append(xs)
        aliases = {4: 0}
    grid_spec = pltpu.PrefetchScalarGridSpec(
        num_scalar_prefetch=2,
        grid=(n // tm,),
        in_specs=in_specs,
        out_specs=pl.BlockSpec(memory_space=pl.ANY),
        scratch_shapes=scratch,
    )
    return pl.pallas_call(
        kern,
        grid_spec=grid_spec,
        out_shape=jax.ShapeDtypeStruct((n_slots,) + tile, F32),
        input_output_aliases=aliases,
        name="dispatch",
        compiler_params=_params("arbitrary"),
    )(*args)


def _expert_kernel(be_ref, nx_ref, nu_ref, xs_ref, wgu_hbm, bgu_ref, wdn_hbm, bdn_ref, ys_ref,
                   gu_f32, dn_f32, wgu_b, wdn_b, sem):
    j = pl.program_id(0)

    def fetch(e):
        return (pltpu.make_async_copy(wgu_hbm.at[e], gu_f32, sem.at[0]),
                pltpu.make_async_copy(wdn_hbm.at[e], dn_f32, sem.at[1]))

    @pl.when(j < nu_ref[0])
    def _():
        @pl.when(j == 0)
        def _():
            for cp in fetch(be_ref[0]):
                cp.start()

        @pl.when((j == 0) | (be_ref[j] != be_ref[jnp.maximum(j - 1, 0)]))
        def _():
            for cp in fetch(be_ref[j]):
                cp.wait()
            wgu_b[...] = gu_f32[...].astype(BF16)
            wdn_b[...] = dn_f32[...].astype(BF16)

            @pl.when(nx_ref[j] >= 0)
            def _():
                for cp in fetch(nx_ref[j]):
                    cp.start()

        f = wdn_b.shape[0]
        x = _load_row_tiles(xs_ref, SLOT_BLOCK).astype(BF16)
        gu = jnp.dot(x, wgu_b[...], preferred_element_type=F32) + bgu_ref[0]
        gate = jnp.minimum(gu[:, :f], SWIGLU_LIMIT)
        up = jnp.clip(gu[:, f:], -SWIGLU_LIMIT, SWIGLU_LIMIT)
        glu = gate / (1.0 + jnp.exp(-SWIGLU_ALPHA * gate))
        act = ((up + 1.0) * glu).astype(BF16)
        _store_row_tiles(ys_ref, jnp.dot(act, wdn_b[...], preferred_element_type=F32) + bdn_ref[0])


def _experts(block_expert, next_expert, n_used, xs, wgu, bgu, wdn, bdn):
    n_slots = xs.shape[0] // SUBLANES
    nb = n_slots // SLOT_BLOCK
    f, d = wdn.shape[1:]
    rows = SLOT_BLOCK * SUBLANES

    def blk(i, be, nx, nu):
        return (jnp.minimum(i, nu[0] - 1), 0)

    def bsel(i, be, nx, nu):
        return (be[i], 0, 0)

    grid_spec = pltpu.PrefetchScalarGridSpec(
        num_scalar_prefetch=3,
        grid=(nb,),
        in_specs=[pl.BlockSpec((rows, LANES), blk),
                  pl.BlockSpec(memory_space=pl.ANY),
                  pl.BlockSpec((1, 1, 2 * f), bsel),
                  pl.BlockSpec(memory_space=pl.ANY),
                  pl.BlockSpec((1, 1, d), bsel)],
        out_specs=pl.BlockSpec((rows, LANES), blk),
        scratch_shapes=[pltpu.VMEM((d, 2 * f), F32), pltpu.VMEM((f, d), F32),
                        pltpu.VMEM((d, 2 * f), BF16), pltpu.VMEM((f, d), BF16), pltpu.SemaphoreType.DMA((2,))],
    )
    return pl.pallas_call(
        _expert_kernel,
        grid_spec=grid_spec,
        out_shape=jax.ShapeDtypeStruct(xs.shape, F32),
        name="experts",
        compiler_params=_params("arbitrary"),
    )(block_expert, next_expert, n_used, xs, wgu, bgu, wdn, bdn)


def _combine_kernel(dest_ref, dnext_ref, x1_ref, gate_ref, mod_ref, modf_ref, gfin_ref, ys_ref, y_ref,
                    buf, sem, *, tm, mt):
    nt = pl.num_programs(1)
    step = pl.program_id(0) * nt + pl.program_id(1)
    n_steps = pl.num_programs(0) * nt
    half = TOP_K * tm * SUBLANES
    cur = lax.rem(step, 2)
    per_tile = mt // tm

    def row_copy(which, kq, r, slot):
        off = pl.multiple_of(which * half + (kq * tm + r) * SUBLANES, SUBLANES)
        return pltpu.make_async_copy(ys_ref.at[slot], buf.at[pl.ds(off, SUBLANES), :], sem.at[which])

    def gather(slots_ref, which, at_step):
        first_row = lax.rem(at_step, per_tile) * tm

        def issue(i, carry):
            r0 = i * ROW_UNROLL
            for j in range(ROW_UNROLL):
                for kq in range(TOP_K):
                    row_copy(which, kq, r0 + j, slots_ref[first_row + r0 + kq * mt + j]).start(priority=kq % 2)
            return carry

        lax.fori_loop(0, tm // ROW_UNROLL, issue, 0)

    @pl.when(step == 0)
    def _():
        gather(dest_ref, cur, step)

    @pl.when(step + 1 < n_steps)
    def _():
        gather(dnext_ref, 1 - cur, step + 1)

    def drain(i, carry):
        for _ in range(ROW_UNROLL * TOP_K):
            row_copy(cur, 0, 0, 0).wait()
        return carry

    lax.fori_loop(0, tm // ROW_UNROLL, drain, 0)
    gates = gate_ref[0]
    moe = None
    for kq in range(TOP_K):
        start = pl.multiple_of(cur * half + kq * tm * SUBLANES, SUBLANES)
        rows = _load_row_tiles(buf.at[pl.ds(start, tm * SUBLANES), :], tm)
        term = gates[:, kq:kq + 1] * rows
        moe = term if moe is None else moe + term
    xo = x1_ref[0] + mod_ref[0][5:6] * moe
    modf = modf_ref[0]
    y_ref[0] = _rms(xo) * gfin_ref[...] * (1.0 + modf[1:2]) + modf[0:1]


def _combine(dest_flat, x1, gates, mod, modf, g_final, ys, tm, mt):
    b, t, d = x1.shape
    nt = t // tm
    last = b * nt - 1
    assert mt % tm == 0
    per_tile = mt // tm
    kern = functools.partial(_combine_kernel, tm=tm, mt=mt)
    return pl.pallas_call(
        kern,
        grid=(b, nt),
        in_specs=[pl.BlockSpec((mt * TOP_K,), lambda i, j: ((i * nt + j) // per_tile,), memory_space=pltpu.SMEM),
                  pl.BlockSpec((mt * TOP_K,), lambda i, j: (jnp.minimum(i * nt + j + 1, last) // per_tile,),
                               memory_space=pltpu.SMEM),
                  pl.BlockSpec((1, tm, d), lambda i, j: (i, j, 0)),
                  pl.BlockSpec((1, tm, LANES), lambda i, j: (i, j, 0)),
                  pl.BlockSpec((1, 6, d), lambda i, j: (i, 0, 0)),
                  pl.BlockSpec((1, 2, d), lambda i, j: (i, 0, 0)),
                  pl.BlockSpec((1, d), lambda i, j: (0, 0)),
                  pl.BlockSpec(memory_space=pl.ANY)],
        out_specs=pl.BlockSpec((1, tm, d), lambda i, j: (i, j, 0)),
        out_shape=jax.ShapeDtypeStruct((b, t, d), F32),
        scratch_shapes=[pltpu.VMEM((2 * TOP_K * tm * SUBLANES, LANES), F32), pltpu.SemaphoreType.DMA((2,))],
        name="combine",
        compiler_params=_params("arbitrary", "arbitrary"),
    )(dest_flat, dest_flat, x1, gates, mod, modf, g_final, ys)


def _rope_tables(pos):
    half = A_HD // 2
    inv_freq = ROPE_THETA ** (-jnp.arange(half, dtype=F32) / half)
    ang = pos.astype(F32)[:, None] * inv_freq[None, :]
    cos = jnp.cos(ang)
    sin = jnp.sin(ang)
    reps = LANES // A_HD
    return jnp.tile(jnp.concatenate([cos, cos], axis=1), (1, reps)), jnp.tile(jnp.concatenate([-sin, sin], axis=1), (1, reps))


def _reorder_w_in(w_in):
    offs = [0]
    for w in (512, 128, 128, 256, 64, 4, 256, 256, 512, 16, 512):
        offs.append(offs[-1] + w)
    seg = [w_in[:, offs[i]:offs[i + 1]] for i in range(11)]
    qa, ka, va, qi, ki, wi, qg, kg, vg, lr, rg = seg
    pad = jnp.zeros((w_in.shape[0], LANES - IDX_DIM - IDX_HEADS - GATE_RANK), w_in.dtype)
    return jnp.concatenate([qa, ka, va, qi, ki, wi, lr, pad, qg, kg, vg, rg], axis=1).astype(BF16)


def kernel(x_prompt, x_sample, cache_k, cache_v, cache_kidx, state_gla, c_prompt, c_sample,
           w_mod, b_mod, g_mix, g_ffn, w_in, gla_w_gate, gla_b_gate, gla_g_out, w_out,
           w_router, b_router, w_gate_up, b_gate_up, w_down, b_down,
           w_mod_final, b_mod_final, g_final):
    depth = w_in.shape[0]
    assert depth == 1
    bp, sp, d = x_prompt.shape
    bs, ts, _ = x_sample.shape
    past = cache_k.shape[2]
    n_exp = w_router.shape[2]
    f = w_down.shape[2]

    c_all = jnp.concatenate([c_prompt, c_sample], axis=0)
    w_r = _reorder_w_in(w_in[0])
    wg_pad = jnp.zeros((LANES, B_HEADS * B_DK), F32).at[MISC_LR:MISC_LR + GATE_RANK].set(gla_w_gate[0])
    bg = gla_b_gate[0].reshape(1, -1)
    wr_pad = jnp.zeros((d, LANES), F32).at[:, :n_exp].set(w_router[0])
    wr_hi = wr_pad.astype(BF16)
    wr_cat = jnp.concatenate([wr_hi, (wr_pad - wr_hi.astype(F32)).astype(BF16)], axis=1)
    br_pad = jnp.full((1, LANES), NEG_BIG, F32).at[0, :n_exp].set(b_router[0])
    w_out_b = w_out[0].astype(BF16)
    bgu = b_gate_up[0].reshape(n_exp, 1, 2 * f)
    bdn = b_down[0].reshape(n_exp, 1, d)
    g_out = gla_g_out[0].reshape(1, B_DV)

    mod_all = _adaln(c_all, w_mod[0], b_mod[0]).reshape(bp + bs, 6, d)
    modf_all = _adaln(c_all, w_mod_final, b_mod_final).reshape(bp + bs, 2, d)
    mod_p, mod_s = mod_all[:bp], mod_all[bp:]
    modf_p, modf_s = modf_all[:bp], modf_all[bp:]

    def mixer(x, mod, pos, tm, past_kv, state_t, cs, nc, tq):
        cos_t, sin_t = _rope_tables(pos)
        qa, ka, va, qi, ki, misc, qg, kg, vg, gg, rg, kb, vx, kib = _premix(
            x, mod, g_mix[0].reshape(1, d), w_r, wg_pad, bg, cos_t, sin_t, tm)
        if past_kv is None:
            oa = _attn_causal(qa, qi, misc, kb, vx, kib, tq)
        else:
            n_past = past_kv[0].shape[2]
            assert tq == x.shape[1] and n_past % LANES == 0
            oa = _attn_call(qa, qi, misc, ka, va, ki, past_kv, tq=tq, q_off=0, n_tiles=1,
                            n_ctx=n_past + LANES, causal=False, n_keys=n_past + ka.shape[1], n_groups=1)[None]
        ob, st = _gla(qg, kg, vg, gg, rg, state_t, g_out, cs=cs, nc=nc, bb=GLA_BATCH)
        return oa, ob, (ka, va, ki, st)

    state0_p = jnp.zeros((bp, B_HEADS, B_DK, B_DV), F32)
    oa_p, ob_p, (ka_p, va_p, ki_p, st_p) = mixer(
        x_prompt, mod_p, jnp.arange(sp), min(TOKEN_TILE, sp), None, state0_p, CHUNK,
        min(GLA_CHUNKS_PER_STEP, sp // CHUNK), min(TOKEN_TILE, sp))
    past_kv = (jnp.transpose(cache_k[0], (0, 2, 3, 1)).reshape(bs, A_KV * A_HD, past),
               jnp.transpose(cache_v[0], (0, 2, 3, 1)).reshape(bs, A_KV * A_HD, past),
               jnp.swapaxes(cache_kidx[0], 1, 2))
    oa_s, ob_s, (ka_s, va_s, ki_s, st_s) = mixer(
        x_sample, mod_s, past + jnp.arange(ts), ts, past_kv, state_gla[0], ts, 1, ts)

    cnt0 = jnp.zeros((1, LANES), F32)
    x1_p, h2_p, ri_p, gate_p, cnt1 = _merge(oa_p, ob_p, x_prompt, mod_p, w_out_b, g_ffn[0].reshape(1, d),
                                            wr_cat, br_pad, cnt0, min(MERGE_TILE, sp))
    x1_s, h2_s, ri_s, gate_s, cnt2 = _merge(oa_s, ob_s, x_sample, mod_s, w_out_b, g_ffn[0].reshape(1, d),
                                            wr_cat, br_pad, cnt1, ts)
    counts = cnt2[0, :n_exp].astype(I32)
    padded = (counts + SLOT_BLOCK - 1) // SLOT_BLOCK * SLOT_BLOCK
    pad_end = jnp.cumsum(padded)
    pad_start = pad_end - padded
    n_asg = (bp * sp + bs * ts) * TOP_K
    nb = -(-n_asg // SLOT_BLOCK) + n_exp
    n_slots = nb * SLOT_BLOCK
    block_start = jnp.arange(nb, dtype=I32) * SLOT_BLOCK
    block_expert = jnp.minimum(jnp.sum((pad_end[None, :] <= block_start[:, None]).astype(I32), axis=1), n_exp - 1)
    n_used = (pad_end[-1:] // SLOT_BLOCK).astype(I32)
    following = pad_end.astype(I32)[block_expert] // SLOT_BLOCK
    next_expert = jnp.where(following < n_used[0], block_expert[jnp.minimum(following, nb - 1)], -1).astype(I32)
    has_pad = (padded > counts).astype(I32)
    expert_ids = jnp.arange(n_exp, dtype=I32)

    def dests(route):
        e, rank = route[:, :TOP_K, :], route[:, TOP_K:, :]
        start = jnp.sum(jnp.where(e[..., None] == expert_ids, pad_start.astype(I32), 0), axis=-1)
        return (start + rank).reshape(-1)

    dest_p, dest_s = dests(ri_p), dests(ri_s)
    pad_end_i = pad_end.astype(I32)
    tile = (SUBLANES, LANES)
    xs = _dispatch(pad_end_i, has_pad, dest_p, h2_p.reshape((bp * sp,) + tile), None, n_slots, min(ROW_DMA_TILE, sp),
                   min(MERGE_TILE, sp))
    xs = _dispatch(pad_end_i, has_pad, dest_s, h2_s.reshape((bs * ts,) + tile), xs, n_slots, min(TOKEN_TILE, bs * ts),
                   ts)
    ys = _experts(block_expert, next_expert, n_used, xs.reshape(n_slots * SUBLANES, LANES), w_gate_up[0], bgu,
                  w_down[0], bdn)
    ys = ys.reshape((n_slots,) + tile)
    y_p = _combine(dest_p, x1_p, gate_p, mod_p, modf_p, g_final.reshape(1, d), ys, min(TOKEN_TILE, sp),
                   min(MERGE_TILE, sp))
    y_s = _combine(dest_s, x1_s, gate_s, mod_s, modf_s, g_final.reshape(1, d), ys, ts, ts)

    def kv(a, b, t):
        return a.reshape(1, b, t, A_KV, A_HD)

    return (y_p, y_s,
            kv(ka_p, bp, sp), kv(va_p, bp, sp), ki_p[None], st_p[None],
            kv(ka_s, bs, ts), kv(va_s, bs, ts), ki_s[None], st_s[None])
```
